```python
import jax, jax.numpy as jnp
from jax import lax
import numpy as np

D_MODEL = 1024
BATCH = 2
SEQ = 8192
DEPTH = 4
DEC_BATCH = 32
DEC_SEQ = 16
PAST_LEN = 1024

CHUNK = 64
Q_BLOCK = 128
N_MIXERS = 3
N_LAYERS_A = (DEPTH + 2) // 3
N_LAYERS_B = (DEPTH + 1) // 3
N_LAYERS_C = DEPTH // 3
EPS = 1e-6
MLA_HEADS = 16
MLA_Q_LORA = 512
MLA_KV_LORA = 256
MLA_NOPE = 64
MLA_ROPE = 32
MLA_V = 64
ROPE_BASE = 10000.0
MLA_SCALE = (MLA_NOPE + MLA_ROPE) ** -0.5
CMLP_CHUNK = 128
CMLP_WIDTH = 2048
CMLP_GROUPS = 8
DSA_HEADS = 16
DSA_HEAD_DIM = 64
DSA_SCALE = DSA_HEAD_DIM ** -0.5
IDX_HEADS = 8
IDX_DIM = 64
IDX_SCALE = IDX_DIM ** -0.5
TOPK_MAX = 256
D_FF = -(-8 * D_MODEL // (3 * 256)) * 256

kernel_name = 'hybrid_mla_cmlp_dsa_stream_step'


def rmsnorm(x, g):
    xf = x.astype(jnp.float32)
    y = xf * lax.rsqrt(jnp.mean(xf * xf, axis=-1, keepdims=True) + EPS)
    return (y * g.astype(jnp.float32)).astype(x.dtype)


def layernorm(x, g, b):
    xf = x.astype(jnp.float32)
    mu = jnp.mean(xf, axis=-1, keepdims=True)
    xc = xf - mu
    y = xc * lax.rsqrt(jnp.mean(xc * xc, axis=-1, keepdims=True) + EPS)
    return (y * g.astype(jnp.float32) + b.astype(jnp.float32)).astype(x.dtype)


def rope(x, pos):
    half = x.shape[-1] // 2
    inv = ROPE_BASE ** (-jnp.arange(half, dtype=jnp.float32) / half)
    ang = pos.astype(jnp.float32)[:, None] * inv[None, :]
    ang = ang.reshape(ang.shape[:1] + (1,) * (x.ndim - 3) + (half,))
    cos, sin = jnp.cos(ang), jnp.sin(ang)
    x1, x2 = x[..., :half].astype(jnp.float32), x[..., half:].astype(jnp.float32)
    return jnp.concatenate([x1 * cos - x2 * sin, x1 * sin + x2 * cos], axis=-1).astype(x.dtype)


def chunk_mask(q_pos, k_pos):
    return (k_pos[None, :] // CHUNK) <= (q_pos[:, None] // CHUNK)


def to_blocks(a):
    b, s = a.shape[:2]
    return jnp.moveaxis(a.reshape((b, s // Q_BLOCK, Q_BLOCK) + a.shape[2:]), 1, 0)


def from_blocks(a):
    a = jnp.moveaxis(a, 0, 1)
    return a.reshape((a.shape[0], a.shape[1] * a.shape[2]) + a.shape[3:])


def swiglu(h, w_in, w_out):
    gu = h @ w_in
    return (jax.nn.silu(gu[..., :D_FF]) * gu[..., D_FF:]) @ w_out


def mla_project(h, pos, w_dq, g_q, w_uq, w_dkv, g_kv, w_ukv):
    c_q = rmsnorm(h @ w_dq, g_q)
    q = jnp.einsum('bsc,chd->bshd', c_q, w_uq)
    q_rope = rope(q[..., MLA_NOPE:], pos)
    q_lat = jnp.einsum('bshn,chn->bshc', q[..., :MLA_NOPE], w_ukv[..., :MLA_NOPE])
    kv = h @ w_dkv
    c_kv = rmsnorm(kv[..., :MLA_KV_LORA], g_kv)
    k_rope = rope(kv[..., MLA_KV_LORA:], pos)
    return q_lat, q_rope, c_kv, k_rope


def mla_attend(q_lat, q_rope, c_kv, k_rope, mask):
    s = (jnp.einsum('bqhc,bkc->bhqk', q_lat, c_kv)
         + jnp.einsum('bqhr,bkr->bhqk', q_rope, k_rope)).astype(jnp.float32) * MLA_SCALE
    p = jax.nn.softmax(jnp.where(mask, s, -jnp.inf), axis=-1).astype(c_kv.dtype)
    return jnp.einsum('bhqk,bkc->bqhc', p, c_kv)


def mla_output(out_lat, w_ukv, w_o):
    v = jnp.einsum('bqhc,chv->bqhv', out_lat, w_ukv[..., MLA_NOPE:])
    return v.reshape(v.shape[:2] + (MLA_HEADS * MLA_V,)) @ w_o


def mla_prompt(h, pos, w_dq, g_q, w_uq, w_dkv, g_kv, w_ukv, w_o):
    q_lat, q_rope, c_kv, k_rope = mla_project(h, pos, w_dq, g_q, w_uq, w_dkv, g_kv, w_ukv)
    nb = h.shape[1] // Q_BLOCK

    def blk(args):
        ql, qr, b0 = args
        q_pos = b0 * Q_BLOCK + jnp.arange(Q_BLOCK)
        return mla_attend(ql, qr, c_kv, k_rope, chunk_mask(q_pos, pos))

    out_lat = from_blocks(lax.map(blk, (to_blocks(q_lat), to_blocks(q_rope), jnp.arange(nb))))
    return mla_output(out_lat, w_ukv, w_o), c_kv, k_rope


def mla_sample(h, pos, c_cache, kr_cache, w_dq, g_q, w_uq, w_dkv, g_kv, w_ukv, w_o):
    q_lat, q_rope, c_new, kr_new = mla_project(h, pos, w_dq, g_q, w_uq, w_dkv, g_kv, w_ukv)
    c_all = jnp.concatenate([c_cache, c_new], axis=1)
    kr_all = jnp.concatenate([kr_cache, kr_new], axis=1)
    mask = jnp.ones((h.shape[1], c_all.shape[1]), dtype=bool)
    out_lat = mla_attend(q_lat, q_rope, c_all, kr_all, mask)
    return mla_output(out_lat, w_ukv, w_o), c_new, kr_new


def chunk_mlp_mix(h, w_in, ln_g, ln_b, w_s, b_s, w_out):
    b, s, _ = h.shape
    n = min(s, CMLP_CHUNK)
    uv = jax.nn.gelu(h @ w_in, approximate=False)
    u = uv[..., :CMLP_WIDTH]
    v = layernorm(uv[..., CMLP_WIDTH:], ln_g, ln_b)
    vg = v.reshape(b, s // n, n, CMLP_GROUPS, CMLP_WIDTH // CMLP_GROUPS)
    w = jnp.tril(w_s[:, :n, :n])
    mixed = jnp.einsum('gts,bcsgd->bctgd', w, vg) + b_s[:, :n].T[:, :, None]
    return (u * mixed.reshape(b, s, CMLP_WIDTH)) @ w_out, v


def dsa_project(h, w_qkv, w_qidx, w_kidx, g_kidx, w_widx):
    b, s, _ = h.shape
    qkv = (h @ w_qkv).reshape(b, s, 3, DSA_HEADS, DSA_HEAD_DIM)
    q_idx = (h @ w_qidx).reshape(b, s, IDX_HEADS, IDX_DIM)
    k_idx = rmsnorm(h @ w_kidx, g_kidx)
    w_idx = (h @ w_widx) * (IDX_HEADS ** -0.5)
    return qkv[:, :, 0], qkv[:, :, 1], qkv[:, :, 2], q_idx, k_idx, w_idx


def dsa_attend(q, q_idx, w_idx, k, v, k_idx, mask, topk):
    logits = jnp.einsum('bqhd,bld->bqhl', q_idx, k_idx).astype(jnp.float32) * IDX_SCALE
    score = jnp.einsum('bqh,bqhl->bql', w_idx.astype(jnp.float32), jax.nn.relu(logits))
    score = jnp.where(mask, score, -jnp.inf)
    sel_score, idx = lax.top_k(score, topk)
    valid = jnp.isfinite(sel_score)
    gather = jax.vmap(lambda kb, ib: kb[ib])
    k_sel, v_sel = gather(k, idx), gather(v, idx)
    s = jnp.einsum('bqhd,bqkhd->bhqk', q, k_sel).astype(jnp.float32) * DSA_SCALE
    p = jax.nn.softmax(jnp.where(valid[:, None], s, -jnp.inf), axis=-1).astype(v.dtype)
    return jnp.einsum('bhqk,bqkhd->bqhd', p, v_sel)


def dsa_output(out, w_o):
    return out.reshape(out.shape[:2] + (DSA_HEADS * DSA_HEAD_DIM,)) @ w_o


def dsa_prompt(h, pos, w_qkv, w_o, w_qidx, w_kidx, g_kidx, w_widx):
    q, k, v, q_idx, k_idx, w_idx = dsa_project(h, w_qkv, w_qidx, w_kidx, g_kidx, w_widx)
    seq = h.shape[1]
    topk = min(TOPK_MAX, seq // 4)

    def blk(args):
        qb, qib, wib, b0 = args
        q_pos = b0 * Q_BLOCK + jnp.arange(Q_BLOCK)
        return dsa_attend(qb, qib, wib, k, v, k_idx, chunk_mask(q_pos, pos), topk)

    out = from_blocks(lax.map(blk, (to_blocks(q), to_blocks(q_idx), to_blocks(w_idx),
                                    jnp.arange(seq // Q_BLOCK))))
    return dsa_output(out, w_o), k, v, k_idx


def dsa_sample(h, k_cache, v_cache, kidx_cache, w_qkv, w_o, w_qidx, w_kidx, g_kidx, w_widx):
    q, k_new, v_new, q_idx, kidx_new, w_idx = dsa_project(h, w_qkv, w_qidx, w_kidx, g_kidx, w_widx)
    k_all = jnp.concatenate([k_cache, k_new], axis=1)
    v_all = jnp.concatenate([v_cache, v_new], axis=1)
    kidx_all = jnp.concatenate([kidx_cache, kidx_new], axis=1)
    n_keys = k_all.shape[1]
    mask = jnp.ones((h.shape[1], n_keys), dtype=bool)
    out = dsa_attend(q, q_idx, w_idx, k_all, v_all, kidx_all, mask, min(TOPK_MAX, n_keys // 4))
    return dsa_output(out, w_o), k_new, v_new, kidx_new


def setup_inputs(seed: int = 0) -> dict:
    key = jax.random.key(seed)
    ks = iter(jax.random.split(key, 64))

    def nrm(shape, scale=1.0):
        return jax.random.normal(next(ks), shape, jnp.float32) * scale

    def gain(shape):
        return 1.0 + 0.05 * nrm(shape)

    NA, NB, NC, D = N_LAYERS_A, N_LAYERS_B, N_LAYERS_C, D_MODEL
    return {
        'x_prompt': nrm((BATCH, SEQ, D)),
        'x_sample': nrm((DEC_BATCH, DEC_SEQ, D)),
        'cache_mla_ckv': nrm((NA, DEC_BATCH, PAST_LEN, MLA_KV_LORA)),
        'cache_mla_krope': nrm((NA, DEC_BATCH, PAST_LEN, MLA_ROPE)),
        'cache_dsa_k': nrm((NC, DEC_BATCH, PAST_LEN, DSA_HEADS, DSA_HEAD_DIM)),
        'cache_dsa_v': nrm((NC, DEC_BATCH, PAST_LEN, DSA_HEADS, DSA_HEAD_DIM)),
        'cache_dsa_kidx': nrm((NC, DEC_BATCH, PAST_LEN, IDX_DIM)),
        'norm_mix': gain((DEPTH, D)),
        'norm_ffn': gain((DEPTH, D)),
        'norm_final': gain((D,)),
        'mla_w_dq': nrm((NA, D, MLA_Q_LORA), D ** -0.5),
        'mla_g_q': gain((NA, MLA_Q_LORA)),
        'mla_w_uq': nrm((NA, MLA_Q_LORA, MLA_HEADS, MLA_NOPE + MLA_ROPE), MLA_Q_LORA ** -0.5),
        'mla_w_dkv': nrm((NA, D, MLA_KV_LORA + MLA_ROPE), D ** -0.5),
        'mla_g_kv': gain((NA, MLA_KV_LORA)),
        'mla_w_ukv': nrm((NA, MLA_KV_LORA, MLA_HEADS, MLA_NOPE + MLA_V), MLA_KV_LORA ** -0.5),
        'mla_w_o': nrm((NA, MLA_HEADS * MLA_V, D), (MLA_HEADS * MLA_V) ** -0.5),
        'cmlp_w_in': nrm((NB, D, 2 * CMLP_WIDTH), D ** -0.5),
        'cmlp_ln_g': gain((NB, CMLP_WIDTH)),
        'cmlp_ln_b': nrm((NB, CMLP_WIDTH), 0.02),
        'cmlp_w_s': nrm((NB, CMLP_GROUPS, CMLP_CHUNK, CMLP_CHUNK), CMLP_CHUNK ** -0.5),
        'cmlp_b_s': gain((NB, CMLP_GROUPS, CMLP_CHUNK)),
        'cmlp_w_out': nrm((NB, CMLP_WIDTH, D), CMLP_WIDTH ** -0.5),
        'dsa_w_qkv': nrm((NC, D, 3 * DSA_HEADS * DSA_HEAD_DIM), D ** -0.5),
        'dsa_w_o': nrm((NC, DSA_HEADS * DSA_HEAD_DIM, D), (DSA_HEADS * DSA_HEAD_DIM) ** -0.5),
        'dsa_w_qidx': nrm((NC, D, IDX_HEADS * IDX_DIM), D ** -0.5),
        'dsa_w_kidx': nrm((NC, D, IDX_DIM), D ** -0.5),
        'dsa_g_kidx': gain((NC, IDX_DIM)),
        'dsa_w_widx': nrm((NC, D, IDX_HEADS), D ** -0.5),
        'ffn_w_in': nrm((DEPTH, D, 2 * D_FF), D ** -0.5),
        'ffn_w_out': nrm((DEPTH, D_FF, D), D_FF ** -0.5),
    }


def reference(x_prompt, x_sample, cache_mla_ckv, cache_mla_krope, cache_dsa_k, cache_dsa_v, cache_dsa_kidx,
              norm_mix, norm_ffn, norm_final,
              mla_w_dq, mla_g_q, mla_w_uq, mla_w_dkv, mla_g_kv, mla_w_ukv, mla_w_o,
              cmlp_w_in, cmlp_ln_g, cmlp_ln_b, cmlp_w_s, cmlp_b_s, cmlp_w_out,
              dsa_w_qkv, dsa_w_o, dsa_w_qidx, dsa_w_kidx, dsa_g_kidx, dsa_w_widx,
              ffn_w_in, ffn_w_out):
    xp, xs = x_prompt, x_sample
    pos_p = jnp.arange(xp.shape[1])
    pos_s = PAST_LEN + jnp.arange(xs.shape[1])
    ckv_p, kr_p, ckv_s, kr_s, cv_s = [], [], [], [], []
    dk_p, dv_p, di_p, dk_s, dv_s, di_s = [], [], [], [], [], []
    for i in range(DEPTH):
        kind, j = i % N_MIXERS, i // N_MIXERS
        hp, hs = rmsnorm(xp, norm_mix[i]), rmsnorm(xs, norm_mix[i])
        if kind == 0:
            yp, c1, r1 = mla_prompt(hp, pos_p, mla_w_dq[j], mla_g_q[j], mla_w_uq[j], mla_w_dkv[j],
                                    mla_g_kv[j], mla_w_ukv[j], mla_w_o[j])
            ys, c2, r2 = mla_sample(hs, pos_s, cache_mla_ckv[j], cache_mla_krope[j], mla_w_dq[j], mla_g_q[j],
                                    mla_w_uq[j], mla_w_dkv[j], mla_g_kv[j], mla_w_ukv[j], mla_w_o[j])
            ckv_p.append(c1); kr_p.append(r1); ckv_s.append(c2); kr_s.append(r2)
        elif kind == 1:
            yp, _ = chunk_mlp_mix(hp, cmlp_w_in[j], cmlp_ln_g[j], cmlp_ln_b[j], cmlp_w_s[j], cmlp_b_s[j], cmlp_w_out[j])
            ys, v2 = chunk_mlp_mix(hs, cmlp_w_in[j], cmlp_ln_g[j], cmlp_ln_b[j], cmlp_w_s[j], cmlp_b_s[j], cmlp_w_out[j])
            cv_s.append(v2)
        else:
            yp, k1, v1, i1 = dsa_prompt(hp, pos_p, dsa_w_qkv[j], dsa_w_o[j], dsa_w_qidx[j], dsa_w_kidx[j],
                                        dsa_g_kidx[j], dsa_w_widx[j])
            ys, k2, v2, i2 = dsa_sample(hs, cache_dsa_k[j], cache_dsa_v[j], cache_dsa_kidx[j], dsa_w_qkv[j], dsa_w_o[j],
                                        dsa_w_qidx[j], dsa_w_kidx[j], dsa_g_kidx[j], dsa_w_widx[j])
            dk_p.append(k1); dv_p.append(v1); di_p.append(i1); dk_s.append(k2); dv_s.append(v2); di_s.append(i2)
        xp, xs = xp + yp, xs + ys
        xp = xp + swiglu(rmsnorm(xp, norm_ffn[i]), ffn_w_in[i], ffn_w_out[i])
        xs = xs + swiglu(rmsnorm(xs, norm_ffn[i]), ffn_w_in[i], ffn_w_out[i])
    y_prompt = rmsnorm(xp, norm_final)
    y_sample = rmsnorm(xs, norm_final)
    return (y_prompt, y_sample,
            jnp.stack(ckv_p), jnp.stack(kr_p), jnp.stack(ckv_s), jnp.stack(kr_s),
            jnp.stack(cv_s),
            jnp.stack(dk_p), jnp.stack(dv_p), jnp.stack(di_p),
            jnp.stack(dk_s), jnp.stack(dv_s), jnp.stack(di_s))
```

```python
import functools

import numpy as np
import jax
import jax.numpy as jnp
from jax import lax
from jax.experimental import pallas as pl
from jax.experimental.pallas import tpu as pltpu

F32, BF16, I32 = jnp.float32, jnp.bfloat16, jnp.int32

CHUNK = 64
EPS = 1e-6
MLA_HEADS, MLA_Q_LORA, MLA_KV_LORA, MLA_NOPE, MLA_ROPE, MLA_V = 16, 512, 256, 64, 32, 64
ROPE_BASE = 10000.0
MLA_SCALE = (MLA_NOPE + MLA_ROPE) ** -0.5
CMLP_CHUNK, CMLP_WIDTH, CMLP_GROUPS = 128, 2048, 8
DSA_HEADS, DSA_HEAD_DIM = 16, 64
DSA_SCALE = DSA_HEAD_DIM ** -0.5
IDX_HEADS, IDX_DIM = 8, 64
IDX_SCALE = IDX_DIM ** -0.5
TOPK_MAX = 256

LANES = 128
VMEM_LIMIT = 52 * 1024 * 1024
NEG_INF = float("-inf")
INT_MIN = -2 ** 31
KEY_NEG_INF = -2139095041


def _dot(a, b):
    return jnp.dot(a, b, preferred_element_type=F32)


def _dot_t(a, b):
    return lax.dot_general(a, b, (((1,), (1,)), ((), ())), preferred_element_type=F32)


def _rms(x, g):
    return x * lax.rsqrt(jnp.mean(x * x, axis=-1, keepdims=True) + EPS) * g


def _log2(n):
    assert n > 0 and n & (n - 1) == 0, n
    return n.bit_length() - 1


def _div(x, n):
    return lax.shift_right_logical(x, jnp.int32(_log2(n)))


def _mod(x, n):
    assert n & (n - 1) == 0, n
    return x & (n - 1)


def _params(*sem):
    return pltpu.CompilerParams(dimension_semantics=sem, vmem_limit_bytes=VMEM_LIMIT)


def _whole(shape):
    nd = len(shape)
    return pl.BlockSpec(shape, lambda *_: (0,) * nd)


def _ffn_kernel(x_ref, g_ref, wg_ref, wu_ref, wo_ref, gf_ref, o_ref, h_scr, acc_scr, *, final):
    j = pl.program_id(1)

    @pl.when(j == 0)
    def _():
        h_scr[...] = _rms(x_ref[...], g_ref[...]).astype(BF16)
        acc_scr[...] = jnp.zeros_like(acc_scr)

    h = h_scr[...]
    gate = _dot(h, wg_ref[...])
    up = _dot(h, wu_ref[...])
    act = (jax.nn.silu(gate) * up).astype(BF16)
    acc_scr[...] += _dot(act, wo_ref[...])

    @pl.when(j == pl.num_programs(1) - 1)
    def _():
        y = x_ref[...] + acc_scr[...]
        if final:
            y = _rms(y, gf_ref[...])
        o_ref[...] = y


def _ffn(x, g, w_in, w_out, g_final, final):
    t, d = x.shape
    f = w_out.shape[0]
    tm = min(512, t)
    tf = f // 2 if (f // 2) % LANES == 0 else f
    nj = f // tf
    return pl.pallas_call(
        functools.partial(_ffn_kernel, final=final),
        grid=(t // tm, nj),
        in_specs=[
            pl.BlockSpec((tm, d), lambda i, j: (i, 0)),
            pl.BlockSpec((1, d), lambda i, j: (0, 0)),
            pl.BlockSpec((d, tf), lambda i, j: (0, j)),
            pl.BlockSpec((d, tf), lambda i, j: (0, nj + j)),
            pl.BlockSpec((tf, d), lambda i, j: (j, 0)),
            pl.BlockSpec((1, d), lambda i, j: (0, 0)),
        ],
        out_specs=pl.BlockSpec((tm, d), lambda i, j: (i, 0)),
        out_shape=jax.ShapeDtypeStruct((t, d), F32),
        scratch_shapes=[pltpu.VMEM((tm, d), BF16), pltpu.VMEM((tm, d), F32)],
        compiler_params=_params("parallel", "arbitrary"),
        name="ffn",
    )(x, g.reshape(1, d), w_in, w_in, w_out, g_final.reshape(1, d))


def _gelu(x):
    return 0.5 * x * (1.0 + lax.erf(x * np.float32(np.sqrt(0.5))))


def _layernorm(x, g, b):
    mu = jnp.mean(x, axis=-1, keepdims=True)
    xc = x - mu
    return xc * lax.rsqrt(jnp.mean(xc * xc, axis=-1, keepdims=True) + EPS) * g + b


def _cmlp_kernel(x_ref, g_ref, win_ref, lng_ref, lnb_ref, ws_ref, bs_ref, wout_ref, *rest, n_rows, write_v):
    if write_v:
        o_ref, v_ref, vb_scr, acc_scr = rest
    else:
        o_ref, vb_scr, acc_scr = rest
    tm = x_ref.shape[0]
    w = CMLP_WIDTH
    gw = w // CMLP_GROUPS
    c = CMLP_CHUNK
    x = x_ref[...]
    h = _rms(x, g_ref[...]).astype(BF16)
    v = _layernorm(_gelu(_dot(h, win_ref[:, w:])), lng_ref[...], lnb_ref[...])
    if write_v:
        v_ref[...] = v
    vb_scr[...] = v.astype(BF16)
    r_i = lax.broadcasted_iota(I32, (c, c), 0)
    c_i = lax.broadcasted_iota(I32, (c, c), 1)
    keep = jnp.where(c_i >= r_i - _mod(r_i, n_rows), jnp.where(c_i <= r_i, 1, 0), 0) > 0
    for g in range(CMLP_GROUPS):
        lo, hi = g * gw, (g + 1) * gw
        wg = jnp.where(keep, ws_ref[g], 0.0).astype(BF16)
        u = _gelu(_dot(h, win_ref[:, lo:hi]))
        bias = bs_ref[:, g:g + 1]
        mixed = jnp.concatenate(
            [_dot(wg, vb_scr[k * c:(k + 1) * c, lo:hi]) + bias for k in range(tm // c)], axis=0)
        contrib = _dot((u * mixed).astype(BF16), wout_ref[lo:hi, :])
        if g == 0:
            acc_scr[...] = contrib
        else:
            acc_scr[...] += contrib
    o_ref[...] = x + acc_scr[...]


def _cmlp(x, g, w_in, ln_g, ln_b, w_s, b_s, w_out, n_rows, write_v):
    t, d = x.shape
    w = CMLP_WIDTH
    c = CMLP_CHUNK
    tm = min(512, t)
    rep = c // n_rows
    ws_t = jnp.tile(w_s[:, :n_rows, :n_rows], (1, rep, rep))
    bs_t = jnp.tile(b_s[:, :n_rows].T, (rep, 1))
    out_shape = [jax.ShapeDtypeStruct((t, d), F32)]
    out_specs = [pl.BlockSpec((tm, d), lambda i: (i, 0))]
    if write_v:
        out_shape.append(jax.ShapeDtypeStruct((t, w), F32))
        out_specs.append(pl.BlockSpec((tm, w), lambda i: (i, 0)))
    res = pl.pallas_call(
        functools.partial(_cmlp_kernel, n_rows=n_rows, write_v=write_v),
        grid=(t // tm,),
        in_specs=[
            pl.BlockSpec((tm, d), lambda i: (i, 0)),
            _whole((1, d)), _whole((d, 2 * w)), _whole((1, w)), _whole((1, w)),
            _whole((CMLP_GROUPS, c, c)), _whole((c, CMLP_GROUPS)), _whole((w, d)),
        ],
        out_specs=out_specs,
        out_shape=out_shape,
        scratch_shapes=[pltpu.VMEM((tm, w), BF16), pltpu.VMEM((tm, d), F32)],
        compiler_params=_params("parallel"),
        name="cmlp",
    )(x, g.reshape(1, d), w_in, ln_g.reshape(1, w), ln_b.reshape(1, w), ws_t, bs_t, w_out)
    return res if write_v else (res[0], None)


def _mla_proj_kernel(x_ref, gm_ref, wdq_ref, gq_ref, wqn_ref, wqr_ref, wqs_ref, wkc_ref, wkr_ref, wks_ref,
                     gkv_ref, wuk_ref, cq_ref, sq_ref, ck_ref, sk_ref,
                     ql_ref, qr_ref, ckv_ref, kr_ref, ckvb_ref, krb_ref):
    h = _rms(x_ref[...], gm_ref[...]).astype(BF16)
    cq = _rms(_dot(h, wdq_ref[...]), gq_ref[...]).astype(BF16)
    qn = _dot(cq, wqn_ref[...])
    qr = ((_dot(cq, wqr_ref[...]) * cq_ref[...] + _dot(cq, wqs_ref[...]) * sq_ref[...]) * MLA_SCALE).astype(BF16)
    for hh in range(MLA_HEADS):
        pair = qn[:, (hh // 2) * LANES:(hh // 2 + 1) * LANES].astype(BF16)
        ql_ref[hh] = (_dot(pair, wuk_ref[hh]) * MLA_SCALE).astype(BF16)
        qr_ref[hh] = qr[:, hh * MLA_ROPE:(hh + 1) * MLA_ROPE]
    ckv = _rms(_dot(h, wkc_ref[...]), gkv_ref[...])
    ckv_ref[...] = ckv
    ckvb_ref[...] = ckv.astype(BF16)
    kr = _dot(h, wkr_ref[...]) * ck_ref[...] + _dot(h, wks_ref[...]) * sk_ref[...]
    kr_ref[...] = kr
    krb_ref[...] = kr.astype(BF16)


def _rope_tables(pos):
    half = MLA_ROPE // 2
    inv = ROPE_BASE ** (-jnp.arange(half, dtype=F32) / half)
    ang = pos.astype(F32)[:, None] * inv[None, :]
    cos, sin = jnp.cos(ang), jnp.sin(ang)
    cos_k = jnp.concatenate([cos, cos], axis=1)
    sin_k = jnp.concatenate([-sin, sin], axis=1)
    return jnp.tile(cos_k, (1, MLA_HEADS)), jnp.tile(sin_k, (1, MLA_HEADS)), cos_k, sin_k


def _swap_halves(w, group):
    shp = w.shape
    wr = w.reshape(shp[:-1] + (shp[-1] // group, 2, group // 2))
    return wr[..., ::-1, :].reshape(shp)


def _mla_weights(w_dq, w_uq, w_dkv, w_ukv, w_o):
    hd = MLA_HEADS
    wqn = w_uq[:, :, :MLA_NOPE].reshape(MLA_Q_LORA, hd * MLA_NOPE)
    wqr = w_uq[:, :, MLA_NOPE:].reshape(MLA_Q_LORA, hd * MLA_ROPE)
    wkc, wkr = w_dkv[:, :MLA_KV_LORA], w_dkv[:, MLA_KV_LORA:]
    wuk = jnp.transpose(w_ukv[:, :, :MLA_NOPE], (1, 2, 0))
    zeros = jnp.zeros_like(wuk)
    even = jnp.concatenate([wuk, zeros], axis=1)
    odd = jnp.concatenate([zeros, wuk], axis=1)
    wuk2 = jnp.where((jnp.arange(hd) % 2 == 0)[:, None, None], even, odd)
    wuv = jnp.transpose(w_ukv[:, :, MLA_NOPE:], (1, 0, 2))
    cast = lambda a: a.astype(BF16)
    return dict(wdq=cast(w_dq), wqn=cast(wqn), wqr=cast(wqr), wqs=cast(_swap_halves(wqr, MLA_ROPE)),
                wkc=cast(wkc), wkr=cast(wkr), wks=cast(_swap_halves(wkr, MLA_ROPE)),
                wuk=cast(wuk2), wuv=cast(wuv), wo=cast(w_o))


def _mla_proj(x, g_mix, wts, g_q, g_kv, tables, rows_per_seq):
    t, d = x.shape
    tm = min(256, t)
    cos_q, sin_q, cos_k, sin_k = tables
    nrep = cos_q.shape[0] // tm
    tab = lambda wdt: pl.BlockSpec((tm, wdt), lambda i: (i % nrep, 0))
    hd = MLA_HEADS
    row = lambda wdt: pl.BlockSpec((tm, wdt), lambda i: (i, 0))
    hm = lambda wdt: pl.BlockSpec((hd, tm, wdt), lambda i: (0, i, 0))
    return pl.pallas_call(
        _mla_proj_kernel,
        grid=(t // tm,),
        in_specs=[
            row(d), _whole((1, d)), _whole(wts["wdq"].shape), _whole((1, MLA_Q_LORA)),
            _whole(wts["wqn"].shape), _whole(wts["wqr"].shape), _whole(wts["wqs"].shape),
            _whole(wts["wkc"].shape), _whole(wts["wkr"].shape), _whole(wts["wks"].shape),
            _whole((1, MLA_KV_LORA)), _whole(wts["wuk"].shape),
            tab(hd * MLA_ROPE), tab(hd * MLA_ROPE), tab(MLA_ROPE), tab(MLA_ROPE),
        ],
        out_specs=[hm(MLA_KV_LORA), hm(MLA_ROPE), row(MLA_KV_LORA), row(MLA_ROPE), row(MLA_KV_LORA), row(MLA_ROPE)],
        out_shape=[
            jax.ShapeDtypeStruct((hd, t, MLA_KV_LORA), BF16), jax.ShapeDtypeStruct((hd, t, MLA_ROPE), BF16),
            jax.ShapeDtypeStruct((t, MLA_KV_LORA), F32), jax.ShapeDtypeStruct((t, MLA_ROPE), F32),
            jax.ShapeDtypeStruct((t, MLA_KV_LORA), BF16), jax.ShapeDtypeStruct((t, MLA_ROPE), BF16),
        ],
        compiler_params=_params("parallel"),
        name="mla_proj",
    )(x, g_mix.reshape(1, d), wts["wdq"], g_q.reshape(1, -1), wts["wqn"], wts["wqr"], wts["wqs"],
      wts["wkc"], wts["wkr"], wts["wks"], g_kv.reshape(1, -1), wts["wuk"], cos_q, sin_q, cos_k, sin_k)


def _mla_epilogue(o_lat, x_ref, wuv_ref, wo_ref, o_ref, cat_scr, tq):
    ob = o_lat.astype(BF16)
    for hh in range(MLA_HEADS):
        cat_scr[:, hh * MLA_V:(hh + 1) * MLA_V] = _dot(ob[hh * tq:(hh + 1) * tq], wuv_ref[hh]).astype(BF16)
    o_ref[...] = x_ref[...] + _dot(cat_scr[...], wo_ref[...])


def _mla_attn_kernel(qi_ref, kj_ref, last_ref, ql_ref, qr_ref, ck_ref, kr_ref, x_ref, wuv_ref, wo_ref, o_ref,
                     m_scr, l_scr, acc_scr, cat_scr, *, tq, tk):
    p = pl.program_id(1)
    qi, kj = qi_ref[p], kj_ref[p]
    hd = MLA_HEADS

    @pl.when(kj == 0)
    def _():
        m_scr[...] = jnp.full_like(m_scr, NEG_INF)
        l_scr[...] = jnp.zeros_like(l_scr)
        acc_scr[...] = jnp.zeros_like(acc_scr)

    ck = ck_ref[...]
    s = _dot_t(ql_ref[...].reshape(hd * tq, MLA_KV_LORA), ck) + _dot_t(qr_ref[...].reshape(hd * tq, MLA_ROPE), kr_ref[...])
    q_chunk = _div(qi * tq + lax.broadcasted_iota(I32, (tq, 1), 0), CHUNK)
    k_chunk = _div(kj * tk + lax.broadcasted_iota(I32, (1, tk), 1), CHUNK)
    vis = k_chunk <= q_chunk
    s = jnp.where(vis[None], s.reshape(hd, tq, tk), NEG_INF).reshape(hd * tq, tk)
    m_prev = m_scr[...]
    m_new = jnp.maximum(m_prev, jnp.max(s, axis=1, keepdims=True))
    alpha = jnp.exp(m_prev - m_new)
    pe = jnp.exp(s - m_new)
    l_scr[...] = alpha * l_scr[...] + jnp.sum(pe, axis=1, keepdims=True)
    acc_scr[...] = alpha * acc_scr[...] + _dot(pe.astype(BF16), ck)
    m_scr[...] = m_new

    @pl.when(last_ref[p] == 1)
    def _():
        _mla_epilogue(acc_scr[...] / l_scr[...], x_ref, wuv_ref, wo_ref, o_ref, cat_scr, tq)


def _causal_pairs(nq, tq, tk):
    qi, kj, last = [], [], []
    for q in range(nq):
        nvis = -(-((q + 1) * tq) // tk)
        for k in range(nvis):
            qi.append(q); kj.append(k); last.append(int(k == nvis - 1))
    return (jnp.asarray(np.array(qi, np.int32)), jnp.asarray(np.array(kj, np.int32)),
            jnp.asarray(np.array(last, np.int32)))


def _mla_attn_prompt(x, ql, qr, ckvb, krb, wts, batch, seq):
    t, d = x.shape
    hd = MLA_HEADS
    tq, tk = min(256, seq), min(512, seq)
    nq, nk = seq // tq, seq // tk
    qi, kj, last = _causal_pairs(nq, tq, tk)
    grid_spec = pltpu.PrefetchScalarGridSpec(
        num_scalar_prefetch=3,
        grid=(batch, int(qi.shape[0])),
        in_specs=[
            pl.BlockSpec((hd, tq, MLA_KV_LORA), lambda b, p, qi, kj, la: (0, b * nq + qi[p], 0)),
            pl.BlockSpec((hd, tq, MLA_ROPE), lambda b, p, qi, kj, la: (0, b * nq + qi[p], 0)),
            pl.BlockSpec((tk, MLA_KV_LORA), lambda b, p, qi, kj, la: (b * nk + kj[p], 0)),
            pl.BlockSpec((tk, MLA_ROPE), lambda b, p, qi, kj, la: (b * nk + kj[p], 0)),
            pl.BlockSpec((tq, d), lambda b, p, qi, kj, la: (b * nq + qi[p], 0)),
            pl.BlockSpec(wts["wuv"].shape, lambda b, p, qi, kj, la: (0, 0, 0)),
            pl.BlockSpec(wts["wo"].shape, lambda b, p, qi, kj, la: (0, 0)),
        ],
        out_specs=pl.BlockSpec((tq, d), lambda b, p, qi, kj, la: (b * nq + qi[p], 0)),
        scratch_shapes=[
            pltpu.VMEM((hd * tq, 1), F32), pltpu.VMEM((hd * tq, 1), F32),
            pltpu.VMEM((hd * tq, MLA_KV_LORA), F32), pltpu.VMEM((tq, hd * MLA_V), BF16),
        ],
    )
    return pl.pallas_call(
        functools.partial(_mla_attn_kernel, tq=tq, tk=tk),
        grid_spec=grid_spec,
        out_shape=jax.ShapeDtypeStruct((t, d), F32),
        compiler_params=_params("parallel", "arbitrary"),
        name="mla_attn_prompt",
    )(qi, kj, last, ql, qr, ckvb, krb, x, wts["wuv"], wts["wo"])


def _mla_samp_kernel(ql_ref, qr_ref, cc_ref, cr_ref, cn_ref, rn_ref, x_ref, wuv_ref, wo_ref, o_ref, cat_scr, *, n_q):
    b = pl.program_id(0)
    hd = MLA_HEADS
    ql = ql_ref[...].reshape(hd * n_q, MLA_KV_LORA)
    qr = qr_ref[...].reshape(hd * n_q, MLA_ROPE)
    cc = cc_ref[0].astype(BF16)
    cr = cr_ref[0].astype(BF16)
    cn = cn_ref[...]
    s1 = _dot_t(ql, cc) + _dot_t(qr, cr)
    s2 = _dot_t(ql, cn) + _dot_t(qr, rn_ref[...])
    own = _div(lax.broadcasted_iota(I32, (1, LANES), 1), n_q) == _mod(b, LANES // n_q)
    s2 = jnp.where(own, s2, NEG_INF)
    m = jnp.maximum(jnp.max(s1, axis=1, keepdims=True), jnp.max(s2, axis=1, keepdims=True))
    p1 = jnp.exp(s1 - m)
    p2 = jnp.exp(s2 - m)
    l = jnp.sum(p1, axis=1, keepdims=True) + jnp.sum(p2, axis=1, keepdims=True)
    o_lat = (_dot(p1.astype(BF16), cc) + _dot(p2.astype(BF16), cn)) / l
    _mla_epilogue(o_lat, x_ref, wuv_ref, wo_ref, o_ref, cat_scr, n_q)


def _mla_attn_sample(x, ql, qr, ckvb, krb, cache_c, cache_r, wts, n_b, n_q):
    t, d = x.shape
    hd = MLA_HEADS
    past = cache_c.shape[1]
    per = LANES // n_q
    return pl.pallas_call(
        functools.partial(_mla_samp_kernel, n_q=n_q),
        grid=(n_b,),
        in_specs=[
            pl.BlockSpec((hd, n_q, MLA_KV_LORA), lambda b: (0, b, 0)),
            pl.BlockSpec((hd, n_q, MLA_ROPE), lambda b: (0, b, 0)),
            pl.BlockSpec((1, past, MLA_KV_LORA), lambda b: (b, 0, 0)),
            pl.BlockSpec((1, past, MLA_ROPE), lambda b: (b, 0, 0)),
            pl.BlockSpec((LANES, MLA_KV_LORA), lambda b: (b // per, 0)),
            pl.BlockSpec((LANES, MLA_ROPE), lambda b: (b // per, 0)),
            pl.BlockSpec((n_q, d), lambda b: (b, 0)),
            _whole(wts["wuv"].shape), _whole(wts["wo"].shape),
        ],
        out_specs=pl.BlockSpec((n_q, d), lambda b: (b, 0)),
        out_shape=jax.ShapeDtypeStruct((t, d), F32),
        scratch_shapes=[pltpu.VMEM((n_q, hd * MLA_V), BF16)],
        compiler_params=_params("parallel"),
        name="mla_attn_sample",
    )(ql, qr, cache_c, cache_r, ckvb, krb, x, wts["wuv"], wts["wo"])


def _dsa_proj_kernel(x_ref, gm_ref, wqkv_ref, wqi_ref, wkw_ref, gki_ref,
                     k_ref, v_ref, ki_ref, qb_ref, kb_ref, vb_ref, qib_ref, kib_ref, wi_ref, *, head_major):
    hdm = DSA_HEADS * DSA_HEAD_DIM
    h = _rms(x_ref[...], gm_ref[...]).astype(BF16)
    qkv = _dot(h, wqkv_ref[...])
    q = qkv[:, :hdm] * DSA_SCALE
    k = qkv[:, hdm:2 * hdm]
    v = qkv[:, 2 * hdm:]
    k_ref[...] = k
    v_ref[...] = v
    qi = _dot(h, wqi_ref[...]) * IDX_SCALE
    kw = _dot(h, wkw_ref[...])
    ki = _rms(kw[:, :IDX_DIM], gki_ref[...])
    ki_ref[...] = ki
    kib_ref[...] = ki.astype(BF16)
    wi_ref[...] = kw[:, IDX_DIM:IDX_DIM + IDX_HEADS] * (IDX_HEADS ** -0.5)
    if head_major:
        for hh in range(DSA_HEADS):
            sl = slice(hh * DSA_HEAD_DIM, (hh + 1) * DSA_HEAD_DIM)
            qb_ref[hh] = q[:, sl].astype(BF16)
            kb_ref[hh] = k[:, sl].astype(BF16)
            vb_ref[hh] = v[:, sl].astype(BF16)
    else:
        qb_ref[...] = q.astype(BF16)
        kb_ref[...] = k.astype(BF16)
        vb_ref[...] = v.astype(BF16)
    for hh in range(IDX_HEADS):
        qib_ref[hh] = qi[:, hh * IDX_DIM:(hh + 1) * IDX_DIM].astype(BF16)


def _dsa_proj(x, g_mix, w_qkv, w_qidx, w_kw, g_kidx, head_major):
    t, d = x.shape
    tm = min(512, t)
    hdm = DSA_HEADS * DSA_HEAD_DIM
    row = lambda wdt: pl.BlockSpec((tm, wdt), lambda i: (i, 0))
    if head_major:
        qkv_spec = pl.BlockSpec((DSA_HEADS, tm, DSA_HEAD_DIM), lambda i: (0, i, 0))
        qkv_shape = jax.ShapeDtypeStruct((DSA_HEADS, t, DSA_HEAD_DIM), BF16)
    else:
        qkv_spec = row(hdm)
        qkv_shape = jax.ShapeDtypeStruct((t, hdm), BF16)
    return pl.pallas_call(
        functools.partial(_dsa_proj_kernel, head_major=head_major),
        grid=(t // tm,),
        in_specs=[row(d), _whole((1, d)), _whole(w_qkv.shape), _whole(w_qidx.shape), _whole(w_kw.shape),
                  _whole((1, IDX_DIM))],
        out_specs=[row(hdm), row(hdm), row(IDX_DIM), qkv_spec, qkv_spec, qkv_spec,
                   pl.BlockSpec((IDX_HEADS, tm, IDX_DIM), lambda i: (0, i, 0)), row(IDX_DIM), row(IDX_HEADS)],
        out_shape=[
            jax.ShapeDtypeStruct((t, hdm), F32), jax.ShapeDtypeStruct((t, hdm), F32),
            jax.ShapeDtypeStruct((t, IDX_DIM), F32), qkv_shape, qkv_shape, qkv_shape,
            jax.ShapeDtypeStruct((IDX_HEADS, t, IDX_DIM), BF16), jax.ShapeDtypeStruct((t, IDX_DIM), BF16),
            jax.ShapeDtypeStruct((t, IDX_HEADS), F32),
        ],
        compiler_params=_params("parallel"),
        name="dsa_proj",
    )(x, g_mix.reshape(1, d), w_qkv, w_qidx, w_kw, g_kidx.reshape(1, IDX_DIM))


def _sort_key(score):
    bits = lax.bitcast_convert_type(score, I32)
    return jnp.where(bits < 0, bits ^ jnp.int32(0x7FFFFFFF), bits)


def _kth_largest_key(count, topk):
    c0 = count(lambda key, idx: jnp.where(key >= 0, 1, 0))
    t0 = jnp.where(c0 >= topk, jnp.int32(0), jnp.int32(INT_MIN))

    def bit_body(i, t):
        cand = t + lax.shift_left(jnp.int32(1), jnp.int32(30) - i)
        c = count(lambda key, idx: jnp.where(key >= cand, 1, 0))
        return jnp.where(c >= topk, cand, t)

    return lax.fori_loop(0, 31, bit_body, t0)


def _tie_cutoff(count, thr, rem, nbits):
    def bit_body(i, j):
        cand = j + lax.shift_left(jnp.int32(1), jnp.int32(nbits - 1) - i)
        c = count(lambda key, idx: jnp.where(key == thr, jnp.where(idx < cand, 1, 0), 0))
        return jnp.where(c < rem, cand, j)

    return lax.fori_loop(0, nbits, bit_body, jnp.zeros_like(thr))


def _select(key, idx, thr, cut):
    chosen = jnp.where(key > thr, 1, jnp.where(key == thr, jnp.where(idx <= cut, 1, 0), 0))
    return jnp.where(key > KEY_NEG_INF, chosen, 0) > 0


def _dsa_select_kernel(qi_ref, wi_ref, ki_ref, bias_ref, key_scr, cut_scr, *, tq, tk, topk):
    qb = pl.program_id(1)
    seq = ki_ref.shape[0]
    nk = seq // tk
    nvis = _div((qb + 1) * tq + tk - 1, tk)
    q_chunk = _div(qb * tq + lax.broadcasted_iota(I32, (tq, 1), 0), CHUNK)
    w = wi_ref[...]
    lane_idx = lax.broadcasted_iota(I32, (1, tk), 1)

    def score_body(j, carry):
        off = pl.multiple_of(j * tk, tk)
        kb = ki_ref[pl.ds(off, tk), :]
        acc = jnp.zeros((tq, tk), F32)
        for hh in range(IDX_HEADS):
            acc = acc + w[:, hh:hh + 1] * jnp.maximum(_dot_t(qi_ref[hh], kb), 0.0)
        vis = _div(off + lane_idx, CHUNK) <= q_chunk
        key_scr[:, pl.ds(off, tk)] = _sort_key(jnp.where(vis, acc, NEG_INF))
        return carry

    lax.fori_loop(0, nvis, score_body, 0)

    def count(hit):
        def body(j, c):
            off = pl.multiple_of(j * tk, tk)
            hits = hit(key_scr[:, pl.ds(off, tk)], off + lane_idx)
            for t in range(tk // LANES):
                c = c + hits[:, t * LANES:(t + 1) * LANES]
            return c
        c = lax.fori_loop(0, nvis, body, jnp.zeros((tq, LANES), I32))
        return jnp.sum(c, axis=1, keepdims=True)

    thr = _kth_largest_key(count, topk)
    rem = topk - count(lambda key, idx: jnp.where(key > thr, 1, 0))
    n_eq = count(lambda key, idx: jnp.where(key == thr, 1, 0))
    cut_scr[...] = jnp.full((tq, 1), seq, I32)
    need = jnp.max(jnp.where(n_eq > rem, jnp.where(thr > KEY_NEG_INF, 1, 0), 0))

    @pl.when(need > 0)
    def _():
        cut_scr[...] = _tie_cutoff(count, thr, rem, int(seq - 1).bit_length())

    cut = cut_scr[...]

    def out_body(j, carry):
        off = pl.multiple_of(j * tk, tk)
        sel = _select(key_scr[:, pl.ds(off, tk)], off + lane_idx, thr, cut)
        bias_ref[:, pl.ds(off, tk)] = jnp.where(sel, 0.0, NEG_INF).astype(BF16)
        return carry

    lax.fori_loop(0, nvis, out_body, 0)

    def fill_body(j, carry):
        off = pl.multiple_of(j * tk, tk)
        bias_ref[:, pl.ds(off, tk)] = jnp.full((tq, tk), NEG_INF, BF16)
        return carry

    lax.fori_loop(nvis, nk, fill_body, 0)


def _dsa_select(qib, wi, kib, batch, seq, topk):
    tq, tk = min(128, seq), min(512, seq)
    nq = seq // tq
    t = batch * seq
    return pl.pallas_call(
        functools.partial(_dsa_select_kernel, tq=tq, tk=tk, topk=topk),
        grid=(batch, nq),
        in_specs=[
            pl.BlockSpec((IDX_HEADS, tq, IDX_DIM), lambda b, q: (0, b * nq + q, 0)),
            pl.BlockSpec((tq, IDX_HEADS), lambda b, q: (b * nq + q, 0)),
            pl.BlockSpec((seq, IDX_DIM), lambda b, q: (b, 0)),
        ],
        out_specs=pl.BlockSpec((tq, seq), lambda b, q: (b * nq + q, 0)),
        out_shape=jax.ShapeDtypeStruct((t, seq), BF16),
        scratch_shapes=[pltpu.VMEM((tq, seq), I32), pltpu.VMEM((tq, 1), I32)],
        compiler_params=_params("parallel", "arbitrary"),
        name="dsa_select",
    )(qib, wi, kib)


def _dsa_attn_kernel(qi_ref, kj_ref, last_ref, q_ref, k_ref, v_ref, bias_ref, x_ref, wo_ref, o_ref,
                     m_scr, l_scr, acc_scr, cat_scr):
    p = pl.program_id(1)
    kj = kj_ref[p]

    @pl.when(kj == 0)
    def _():
        m_scr[...] = jnp.full_like(m_scr, NEG_INF)
        l_scr[...] = jnp.zeros_like(l_scr)
        acc_scr[...] = jnp.zeros_like(acc_scr)

    bias = bias_ref[...].astype(F32)
    for hh in range(DSA_HEADS):
        s = _dot_t(q_ref[hh], k_ref[hh]) + bias
        m_prev = m_scr[hh]
        m_new = jnp.maximum(m_prev, jnp.max(s, axis=1, keepdims=True))
        m_safe = jnp.where(m_new == NEG_INF, 0.0, m_new)
        alpha = jnp.exp(m_prev - m_safe)
        pe = jnp.exp(s - m_safe)
        l_scr[hh] = alpha * l_scr[hh] + jnp.sum(pe, axis=1, keepdims=True)
        acc_scr[hh] = alpha * acc_scr[hh] + _dot(pe.astype(BF16), v_ref[hh])
        m_scr[hh] = m_new

    @pl.when(last_ref[p] == 1)
    def _():
        for hh in range(DSA_HEADS):
            cat_scr[:, hh * DSA_HEAD_DIM:(hh + 1) * DSA_HEAD_DIM] = (acc_scr[hh] / l_scr[hh]).astype(BF16)
        o_ref[...] = x_ref[...] + _dot(cat_scr[...], wo_ref[...])


def _dsa_attn_prompt(x, qb, kb, vb, bias, w_o, batch, seq):
    t, d = x.shape
    hd, dh = DSA_HEADS, DSA_HEAD_DIM
    tq, tk = min(256, seq), min(512, seq)
    nq, nk = seq // tq, seq // tk
    qi, kj, last = _causal_pairs(nq, tq, tk)
    grid_spec = pltpu.PrefetchScalarGridSpec(
        num_scalar_prefetch=3,
        grid=(batch, int(qi.shape[0])),
        in_specs=[
            pl.BlockSpec((hd, tq, dh), lambda b, p, qi, kj, la: (0, b * nq + qi[p], 0)),
            pl.BlockSpec((hd, tk, dh), lambda b, p, qi, kj, la: (0, b * nk + kj[p], 0)),
            pl.BlockSpec((hd, tk, dh), lambda b, p, qi, kj, la: (0, b * nk + kj[p], 0)),
            pl.BlockSpec((tq, tk), lambda b, p, qi, kj, la: (b * nq + qi[p], kj[p])),
            pl.BlockSpec((tq, d), lambda b, p, qi, kj, la: (b * nq + qi[p], 0)),
            pl.BlockSpec(w_o.shape, lambda b, p, qi, kj, la: (0, 0)),
        ],
        out_specs=pl.BlockSpec((tq, d), lambda b, p, qi, kj, la: (b * nq + qi[p], 0)),
        scratch_shapes=[
            pltpu.VMEM((hd, tq, 1), F32), pltpu.VMEM((hd, tq, 1), F32),
            pltpu.VMEM((hd, tq, dh), F32), pltpu.VMEM((tq, hd * dh), BF16),
        ],
    )
    return pl.pallas_call(
        _dsa_attn_kernel,
        grid_spec=grid_spec,
        out_shape=jax.ShapeDtypeStruct((t, d), F32),
        compiler_params=_params("parallel", "arbitrary"),
        name="dsa_attn_prompt",
    )(qi, kj, last, qb, kb, vb, bias, x, w_o)


def _dsa_samp_kernel(q_ref, qi_ref, wi_ref, kc_ref, vc_ref, kic_ref, kn_ref, vn_ref, kin_ref, x_ref, wo_ref, o_ref,
                     *, n_q, topk):
    b = pl.program_id(0)
    hd, dh = DSA_HEADS, DSA_HEAD_DIM
    past = kc_ref.shape[1]
    n_keys = past + LANES
    own = _div(lax.broadcasted_iota(I32, (1, LANES), 1), n_q) == _mod(b, LANES // n_q)

    qi = qi_ref[...].reshape(IDX_HEADS * n_q, IDX_DIM)
    lg1 = jnp.maximum(_dot_t(qi, kic_ref[0].astype(BF16)), 0.0)
    lg2 = jnp.maximum(_dot_t(qi, kin_ref[...]), 0.0)
    w = wi_ref[...]
    sc1 = jnp.zeros((n_q, past), F32)
    sc2 = jnp.zeros((n_q, LANES), F32)
    for hh in range(IDX_HEADS):
        sc1 = sc1 + w[:, hh:hh + 1] * lg1[hh * n_q:(hh + 1) * n_q]
        sc2 = sc2 + w[:, hh:hh + 1] * lg2[hh * n_q:(hh + 1) * n_q]
    key = _sort_key(jnp.concatenate([sc1, jnp.where(own, sc2, NEG_INF)], axis=1))
    idx = lax.broadcasted_iota(I32, (1, n_keys), 1)

    count = lambda hit: jnp.sum(hit(key, idx), axis=1, keepdims=True)
    thr = _kth_largest_key(count, topk)
    rem = topk - count(lambda k_, i_: jnp.where(k_ > thr, 1, 0))
    cut = _tie_cutoff(count, thr, rem, int(n_keys - 1).bit_length())
    bias = jnp.where(_select(key, idx, thr, cut), 0.0, NEG_INF)
    bias = jnp.concatenate([bias] * hd, axis=0)

    lane_head = _div(lax.broadcasted_iota(I32, (1, hd * dh), 1), dh)
    qf = q_ref[...].astype(F32)
    qbd = jnp.concatenate([jnp.where(lane_head == hh, qf, 0.0) for hh in range(hd)], axis=0).astype(BF16)
    kc = kc_ref[0].astype(BF16)
    vc = vc_ref[0].astype(BF16)
    s1 = _dot_t(qbd, kc) + bias[:, :past]
    s2 = _dot_t(qbd, kn_ref[...]) + bias[:, past:]
    m = jnp.maximum(jnp.max(s1, axis=1, keepdims=True), jnp.max(s2, axis=1, keepdims=True))
    p1 = jnp.exp(s1 - m)
    p2 = jnp.exp(s2 - m)
    l = jnp.sum(p1, axis=1, keepdims=True) + jnp.sum(p2, axis=1, keepdims=True)
    o_all = (_dot(p1.astype(BF16), vc) + _dot(p2.astype(BF16), vn_ref[...])) / l
    out = jnp.zeros((n_q, hd * dh), F32)
    for hh in range(hd):
        out = out + jnp.where(lane_head == hh, o_all[hh * n_q:(hh + 1) * n_q], 0.0)
    o_ref[...] = x_ref[...] + _dot(out.astype(BF16), wo_ref[...])


def _dsa_attn_sample(x, qb, qib, wi, kb, vb, kib, cache_k, cache_v, cache_ki, w_o, n_b, n_q, topk):
    t, d = x.shape
    hdm = DSA_HEADS * DSA_HEAD_DIM
    past = cache_k.shape[1]
    per = LANES // n_q
    return pl.pallas_call(
        functools.partial(_dsa_samp_kernel, n_q=n_q, topk=topk),
        grid=(n_b,),
        in_specs=[
            pl.BlockSpec((n_q, hdm), lambda b: (b, 0)),
            pl.BlockSpec((IDX_HEADS, n_q, IDX_DIM), lambda b: (0, b, 0)),
            pl.BlockSpec((n_q, IDX_HEADS), lambda b: (b, 0)),
            pl.BlockSpec((1, past, hdm), lambda b: (b, 0, 0)),
            pl.BlockSpec((1, past, hdm), lambda b: (b, 0, 0)),
            pl.BlockSpec((1, past, IDX_DIM), lambda b: (b, 0, 0)),
            pl.BlockSpec((LANES, hdm), lambda b: (b // per, 0)),
            pl.BlockSpec((LANES, hdm), lambda b: (b // per, 0)),
            pl.BlockSpec((LANES, IDX_DIM), lambda b: (b // per, 0)),
            pl.BlockSpec((n_q, d), lambda b: (b, 0)),
            _whole(w_o.shape),
        ],
        out_specs=pl.BlockSpec((n_q, d), lambda b: (b, 0)),
        out_shape=jax.ShapeDtypeStruct((t, d), F32),
        compiler_params=_params("parallel"),
        name="dsa_attn_sample",
    )(qb, qib, wi, cache_k.reshape(n_b, past, hdm), cache_v.reshape(n_b, past, hdm), cache_ki, kb, vb, kib, x, w_o)


def kernel(x_prompt, x_sample, cache_mla_ckv, cache_mla_krope, cache_dsa_k, cache_dsa_v, cache_dsa_kidx, norm_mix, norm_ffn, norm_final, mla_w_dq, mla_g_q, mla_w_uq, mla_w_dkv, mla_g_kv, mla_w_ukv, mla_w_o, cmlp_w_in, cmlp_ln_g, cmlp_ln_b, cmlp_w_s, cmlp_b_s, cmlp_w_out, dsa_w_qkv, dsa_w_o, dsa_w_qidx, dsa_w_kidx, dsa_g_kidx, dsa_w_widx, ffn_w_in, ffn_w_out):
    batch, seq, d = x_prompt.shape
    n_b, n_q, _ = x_sample.shape
    past = cache_mla_ckv.shape[2]
    depth = norm_mix.shape[0]
    xp = x_prompt.reshape(batch * seq, d)
    xs = x_sample.reshape(n_b * n_q, d)
    tab_p = _rope_tables(jnp.arange(seq))
    tab_s = tuple(jnp.tile(a, (n_b, 1)) for a in _rope_tables(past + jnp.arange(n_q)))
    cast = lambda a: a.astype(BF16)
    outs = {k: [] for k in ("ckv_p", "kr_p", "ckv_s", "kr_s", "cv_s", "dk_p", "dv_p", "di_p", "dk_s", "dv_s", "di_s")}
    for i in range(depth):
        kind, j = i % 3, i // 3
        if kind == 0:
            wts = _mla_weights(mla_w_dq[j], mla_w_uq[j], mla_w_dkv[j], mla_w_ukv[j], mla_w_o[j])
            ql, qr, ckv, kr, ckvb, krb = _mla_proj(xp, norm_mix[i], wts, mla_g_q[j], mla_g_kv[j], tab_p, seq)
            xp = _mla_attn_prompt(xp, ql, qr, ckvb, krb, wts, batch, seq)
            outs["ckv_p"].append(ckv.reshape(batch, seq, -1)); outs["kr_p"].append(kr.reshape(batch, seq, -1))
            ql, qr, ckv, kr, ckvb, krb = _mla_proj(xs, norm_mix[i], wts, mla_g_q[j], mla_g_kv[j], tab_s, n_q)
            xs = _mla_attn_sample(xs, ql, qr, ckvb, krb, cache_mla_ckv[j], cache_mla_krope[j], wts, n_b, n_q)
            outs["ckv_s"].append(ckv.reshape(n_b, n_q, -1)); outs["kr_s"].append(kr.reshape(n_b, n_q, -1))
        elif kind == 1:
            w_in, w_out = cast(cmlp_w_in[j]), cast(cmlp_w_out[j])
            xp, _ = _cmlp(xp, norm_mix[i], w_in, cmlp_ln_g[j], cmlp_ln_b[j], cmlp_w_s[j], cmlp_b_s[j], w_out,
                          min(seq, CMLP_CHUNK), False)
            xs, v_s = _cmlp(xs, norm_mix[i], w_in, cmlp_ln_g[j], cmlp_ln_b[j], cmlp_w_s[j], cmlp_b_s[j], w_out,
                            min(n_q, CMLP_CHUNK), True)
            outs["cv_s"].append(v_s.reshape(n_b, n_q, -1))
        else:
            w_qkv, w_qidx, w_o = cast(dsa_w_qkv[j]), cast(dsa_w_qidx[j]), cast(dsa_w_o[j])
            pad = jnp.zeros((d, LANES - IDX_DIM - IDX_HEADS), F32)
            w_kw = cast(jnp.concatenate([dsa_w_kidx[j], dsa_w_widx[j], pad], axis=1))
            hshape = (DSA_HEADS, DSA_HEAD_DIM)
            k, v, ki, qb, kb, vb, qib, kib, wi = _dsa_proj(xp, norm_mix[i], w_qkv, w_qidx, w_kw, dsa_g_kidx[j], True)
            bias = _dsa_select(qib, wi, kib, batch, seq, min(TOPK_MAX, seq // 4))
            xp = _dsa_attn_prompt(xp, qb, kb, vb, bias, w_o, batch, seq)
            outs["dk_p"].append(k.reshape((batch, seq) + hshape)); outs["dv_p"].append(v.reshape((batch, seq) + hshape))
            outs["di_p"].append(ki.reshape(batch, seq, -1))
            k, v, ki, qb, kb, vb, qib, kib, wi = _dsa_proj(xs, norm_mix[i], w_qkv, w_qidx, w_kw, dsa_g_kidx[j], False)
            xs = _dsa_attn_sample(xs, qb, qib, wi, kb, vb, kib, cache_dsa_k[j], cache_dsa_v[j], cache_dsa_kidx[j], w_o,
                                  n_b, n_q, min(TOPK_MAX, (past + n_q) // 4))
            outs["dk_s"].append(k.reshape((n_b, n_q) + hshape)); outs["dv_s"].append(v.reshape((n_b, n_q) + hshape))
            outs["di_s"].append(ki.reshape(n_b, n_q, -1))
        w_in, w_out = cast(ffn_w_in[i]), cast(ffn_w_out[i])
        final = i == depth - 1
        xp = _ffn(xp, norm_ffn[i], w_in, w_out, norm_final, final)
        xs = _ffn(xs, norm_ffn[i], w_in, w_out, norm_final, final)
    st = lambda name: jnp.stack(outs[name])
    return (xp.reshape(batch, seq, d), xs.reshape(n_b, n_q, d),
            st("ckv_p"), st("kr_p"), st("ckv_s"), st("kr_s"), st("cv_s"),
            st("dk_p"), st("dv_p"), st("di_p"), st("dk_s"), st("dv_s"), st("di_s"))
```

```python
import functools

import numpy as np
import jax
import jax.numpy as jnp
from jax import lax
from jax.experimental import pallas as pl
from jax.experimental.pallas import tpu as pltpu

F32, BF16, I32 = jnp.float32, jnp.bfloat16, jnp.int32

CHUNK = 64
EPS = 1e-6
MLA_HEADS, MLA_Q_LORA, MLA_KV_LORA, MLA_NOPE, MLA_ROPE, MLA_V = 16, 512, 256, 64, 32, 64
ROPE_BASE = 10000.0
MLA_SCALE = (MLA_NOPE + MLA_ROPE) ** -0.5
CMLP_CHUNK, CMLP_WIDTH, CMLP_GROUPS = 128, 2048, 8
DSA_HEADS, DSA_HEAD_DIM = 16, 64
DSA_SCALE = DSA_HEAD_DIM ** -0.5
IDX_HEADS, IDX_DIM = 8, 64
IDX_SCALE = IDX_DIM ** -0.5
TOPK_MAX = 256

LANES = 128
BF16_ROWS = 16
VMEM_LIMIT = 52 * 1024 * 1024
NEG_INF = float("-inf")
INT_MIN = -2 ** 31
KEY_NEG_INF = -2139095041
LOG2E = float(np.log2(np.e))
HEAD_V = 64
V_ROWS = HEAD_V + BF16_ROWS


def _dot(a, b):
    return jnp.dot(a, b, preferred_element_type=F32)


def _dot_t(a, b):
    return lax.dot_general(a, b, (((1,), (1,)), ((), ())), preferred_element_type=F32)


def _rms(x, g):
    return x * lax.rsqrt(jnp.mean(x * x, axis=-1, keepdims=True) + EPS) * g


def _log2(n):
    assert n > 0 and n & (n - 1) == 0, n
    return n.bit_length() - 1


def _div(x, n):
    return lax.shift_right_logical(x, jnp.int32(_log2(n)))


def _mod(x, n):
    assert n & (n - 1) == 0, n
    return x & (n - 1)


def _params(*sem):
    return pltpu.CompilerParams(dimension_semantics=sem, vmem_limit_bytes=VMEM_LIMIT)


def _whole(shape):
    nd = len(shape)
    return pl.BlockSpec(shape, lambda *_: (0,) * nd)


def _ffn_kernel(x_ref, g_ref, wg_ref, wu_ref, wo_ref, gf_ref, o_ref, h_scr, acc_scr, *, final):
    j = pl.program_id(1)

    @pl.when(j == 0)
    def _():
        h_scr[...] = _rms(x_ref[...], g_ref[...]).astype(BF16)
        acc_scr[...] = jnp.zeros_like(acc_scr)

    h = h_scr[...]
    gate = _dot(h, wg_ref[...])
    up = _dot(h, wu_ref[...])
    act = (jax.nn.silu(gate) * up).astype(BF16)
    acc_scr[...] += _dot(act, wo_ref[...])

    @pl.when(j == pl.num_programs(1) - 1)
    def _():
        y = x_ref[...] + acc_scr[...]
        if final:
            y = _rms(y, gf_ref[...])
        o_ref[...] = y


def _ffn(x, g, w_in, w_out, g_final, final):
    t, d = x.shape
    f = w_out.shape[0]
    tm = min(512, t)
    tf = f // 2 if (f // 2) % LANES == 0 else f
    nj = f // tf
    return pl.pallas_call(
        functools.partial(_ffn_kernel, final=final),
        grid=(t // tm, nj),
        in_specs=[
            pl.BlockSpec((tm, d), lambda i, j: (i, 0)),
            pl.BlockSpec((1, d), lambda i, j: (0, 0)),
            pl.BlockSpec((d, tf), lambda i, j: (0, j)),
            pl.BlockSpec((d, tf), lambda i, j: (0, nj + j)),
            pl.BlockSpec((tf, d), lambda i, j: (j, 0)),
            pl.BlockSpec((1, d), lambda i, j: (0, 0)),
        ],
        out_specs=pl.BlockSpec((tm, d), lambda i, j: (i, 0)),
        out_shape=jax.ShapeDtypeStruct((t, d), F32),
        scratch_shapes=[pltpu.VMEM((tm, d), BF16), pltpu.VMEM((tm, d), F32)],
        compiler_params=_params("parallel", "arbitrary"),
        name="ffn",
    )(x, g.reshape(1, d), w_in, w_in, w_out, g_final.reshape(1, d))


def _gelu(x):
    return 0.5 * x * (1.0 + lax.erf(x * np.float32(np.sqrt(0.5))))


def _layernorm(x, g, b):
    mu = jnp.mean(x, axis=-1, keepdims=True)
    xc = x - mu
    return xc * lax.rsqrt(jnp.mean(xc * xc, axis=-1, keepdims=True) + EPS) * g + b


def _cmlp_kernel(x_ref, g_ref, win_ref, lng_ref, lnb_ref, ws_ref, bs_ref, wout_ref, *rest, n_rows, write_v):
    if write_v:
        o_ref, v_ref, vb_scr, acc_scr = rest
    else:
        o_ref, vb_scr, acc_scr = rest
    tm = x_ref.shape[0]
    w = CMLP_WIDTH
    gw = w // CMLP_GROUPS
    c = CMLP_CHUNK
    x = x_ref[...]
    h = _rms(x, g_ref[...]).astype(BF16)
    v = _layernorm(_gelu(_dot(h, win_ref[:, w:])), lng_ref[...], lnb_ref[...])
    if write_v:
        v_ref[...] = v
    vb_scr[...] = v.astype(BF16)
    r_i = lax.broadcasted_iota(I32, (c, c), 0)
    c_i = lax.broadcasted_iota(I32, (c, c), 1)
    keep = jnp.where(c_i >= r_i - _mod(r_i, n_rows), jnp.where(c_i <= r_i, 1, 0), 0) > 0
    for g in range(CMLP_GROUPS):
        lo, hi = g * gw, (g + 1) * gw
        wg = jnp.where(keep, ws_ref[g], 0.0).astype(BF16)
        u = _gelu(_dot(h, win_ref[:, lo:hi]))
        bias = bs_ref[:, g:g + 1]
        mixed = jnp.concatenate(
            [_dot(wg, vb_scr[k * c:(k + 1) * c, lo:hi]) + bias for k in range(tm // c)], axis=0)
        contrib = _dot((u * mixed).astype(BF16), wout_ref[lo:hi, :])
        if g == 0:
            acc_scr[...] = contrib
        else:
            acc_scr[...] += contrib
    o_ref[...] = x + acc_scr[...]


def _cmlp(x, g, w_in, ln_g, ln_b, w_s, b_s, w_out, n_rows, write_v):
    t, d = x.shape
    w = CMLP_WIDTH
    c = CMLP_CHUNK
    tm = min(512, t)
    rep = c // n_rows
    ws_t = jnp.tile(w_s[:, :n_rows, :n_rows], (1, rep, rep))
    bs_t = jnp.tile(b_s[:, :n_rows].T, (rep, 1))
    out_shape = [jax.ShapeDtypeStruct((t, d), F32)]
    out_specs = [pl.BlockSpec((tm, d), lambda i: (i, 0))]
    if write_v:
        out_shape.append(jax.ShapeDtypeStruct((t, w), F32))
        out_specs.append(pl.BlockSpec((tm, w), lambda i: (i, 0)))
    res = pl.pallas_call(
        functools.partial(_cmlp_kernel, n_rows=n_rows, write_v=write_v),
        grid=(t // tm,),
        in_specs=[
            pl.BlockSpec((tm, d), lambda i: (i, 0)),
            _whole((1, d)), _whole((d, 2 * w)), _whole((1, w)), _whole((1, w)),
            _whole((CMLP_GROUPS, c, c)), _whole((c, CMLP_GROUPS)), _whole((w, d)),
        ],
        out_specs=out_specs,
        out_shape=out_shape,
        scratch_shapes=[pltpu.VMEM((tm, w), BF16), pltpu.VMEM((tm, d), F32)],
        compiler_params=_params("parallel"),
        name="cmlp",
    )(x, g.reshape(1, d), w_in, ln_g.reshape(1, w), ln_b.reshape(1, w), ws_t, bs_t, w_out)
    return res if write_v else (res[0], None)


def _attn_t_kernel(qi_ref, kj_ref, last_ref, qt_ref, k_ref, vt_ref, *rest, tq, tk, use_bias):
    if use_bias:
        bias_ref, x_ref, wot_ref, o_ref, m_scr, acc_scr, cat_scr = rest
    else:
        x_ref, wot_ref, o_ref, m_scr, acc_scr, cat_scr = rest
    p = pl.program_id(1)
    qi, kj = qi_ref[p], kj_ref[p]
    hd = qt_ref.shape[0]

    @pl.when(kj == 0)
    def _():
        m_scr[...] = jnp.full_like(m_scr, NEG_INF)
        acc_scr[...] = jnp.zeros_like(acc_scr)

    def step(mask):
        s_next = _dot(k_ref[0], qt_ref[0])
        pend = None
        for hh in range(hd):
            s = s_next
            if hh + 1 < hd:
                s_next = _dot(k_ref[hh + 1], qt_ref[hh + 1])
            if mask is not None:
                s = s + mask
            m_prev = m_scr[hh]
            m_new = jnp.maximum(m_prev, jnp.max(s, axis=0, keepdims=True))
            m_safe = jnp.where(m_new == NEG_INF, 0.0, m_new) if use_bias else m_new
            alpha = jnp.exp2(m_prev - m_safe)
            pe = jnp.exp2(s - m_safe).astype(BF16)
            m_scr[hh] = m_new
            if pend is not None:
                ph, pa, pp = pend
                acc_scr[ph] = pa * acc_scr[ph] + _dot(vt_ref[ph], pp)
            pend = (hh, alpha, pe)
        ph, pa, pp = pend
        acc_scr[ph] = pa * acc_scr[ph] + _dot(vt_ref[ph], pp)

    if use_bias:
        step(bias_ref[...].astype(F32))
    else:
        needs_mask = _div(kj * tk + tk - 1, CHUNK) > _div(qi * tq, CHUNK)

        @pl.when(needs_mask)
        def _():
            k_chunk = _div(kj * tk + lax.broadcasted_iota(I32, (tk, 1), 0), CHUNK)
            q_chunk = _div(qi * tq + lax.broadcasted_iota(I32, (1, tq), 1), CHUNK)
            step(jnp.where(k_chunk <= q_chunk, 0.0, NEG_INF))

        @pl.when(jnp.logical_not(needs_mask))
        def _():
            step(None)

    @pl.when(last_ref[p] == 1)
    def _():
        for hh in range(hd):
            a = acc_scr[hh]
            cat_scr[hh * HEAD_V:(hh + 1) * HEAD_V, :] = (a[:HEAD_V] / a[HEAD_V:HEAD_V + 1]).astype(BF16)
        o_ref[...] = x_ref[...] + _dot(wot_ref[...], cat_scr[...]).T


def _causal_pairs(nq, tq, tk):
    qi, kj, last = [], [], []
    for q in range(nq):
        nvis = -(-((q + 1) * tq) // tk)
        for k in range(nvis):
            qi.append(q); kj.append(k); last.append(int(k == nvis - 1))
    return (jnp.asarray(np.array(qi, np.int32)), jnp.asarray(np.array(kj, np.int32)),
            jnp.asarray(np.array(last, np.int32)))


def _attn_t(x, qt, kh, vt, bias, wot, batch, seq, name):
    t, d = x.shape
    hd, dk, _ = qt.shape
    tq, tk = min(512, seq), min(256, seq)
    nq, nk = seq // tq, seq // tk
    qi, kj, last = _causal_pairs(nq, tq, tk)
    use_bias = bias is not None
    in_specs = [
        pl.BlockSpec((hd, dk, tq), lambda b, p, qi, kj, la: (0, 0, b * nq + qi[p])),
        pl.BlockSpec((hd, tk, dk), lambda b, p, qi, kj, la: (0, b * nk + kj[p], 0)),
        pl.BlockSpec((hd, V_ROWS, tk), lambda b, p, qi, kj, la: (0, 0, b * nk + kj[p])),
    ]
    args = [qt, kh, vt]
    if use_bias:
        in_specs.append(pl.BlockSpec((tk, tq), lambda b, p, qi, kj, la: (kj[p], b * nq + qi[p])))
        args.append(bias)
    in_specs += [
        pl.BlockSpec((tq, d), lambda b, p, qi, kj, la: (b * nq + qi[p], 0)),
        pl.BlockSpec(wot.shape, lambda b, p, qi, kj, la: (0, 0)),
    ]
    args += [x, wot]
    grid_spec = pltpu.PrefetchScalarGridSpec(
        num_scalar_prefetch=3,
        grid=(batch, int(qi.shape[0])),
        in_specs=in_specs,
        out_specs=pl.BlockSpec((tq, d), lambda b, p, qi, kj, la: (b * nq + qi[p], 0)),
        scratch_shapes=[
            pltpu.VMEM((hd, 1, tq), F32), pltpu.VMEM((hd, V_ROWS, tq), F32), pltpu.VMEM((hd * HEAD_V, tq), BF16),
        ],
    )
    return pl.pallas_call(
        functools.partial(_attn_t_kernel, tq=tq, tk=tk, use_bias=use_bias),
        grid_spec=grid_spec,
        out_shape=jax.ShapeDtypeStruct((t, d), F32),
        compiler_params=_params("parallel", "arbitrary"),
        name=name,
    )(qi, kj, last, *args)


def _rope_tables(pos):
    half = MLA_ROPE // 2
    inv = ROPE_BASE ** (-jnp.arange(half, dtype=F32) / half)
    ang = pos.astype(F32)[:, None] * inv[None, :]
    cos, sin = jnp.cos(ang), jnp.sin(ang)
    cos_k = jnp.concatenate([cos, cos], axis=1)
    sin_k = jnp.concatenate([-sin, sin], axis=1)
    return jnp.tile(cos_k, (1, MLA_HEADS)), jnp.tile(sin_k, (1, MLA_HEADS)), cos_k, sin_k


def _swap_halves(w, group):
    shp = w.shape
    wr = w.reshape(shp[:-1] + (shp[-1] // group, 2, group // 2))
    return wr[..., ::-1, :].reshape(shp)


def _mla_weights(w_dq, w_uq, w_dkv, w_ukv, w_o):
    hd = MLA_HEADS
    cast = lambda a: a.astype(BF16)
    w_uq, w_dkv, w_ukv = cast(w_uq), cast(w_dkv), cast(w_ukv)
    wqn = w_uq[:, :, :MLA_NOPE].reshape(MLA_Q_LORA, hd * MLA_NOPE)
    wqr = w_uq[:, :, MLA_NOPE:].reshape(MLA_Q_LORA, hd * MLA_ROPE)
    wqs = _swap_halves(wqr, MLA_ROPE)
    wkc, wkr = w_dkv[:, :MLA_KV_LORA], w_dkv[:, MLA_KV_LORA:]
    wks = _swap_halves(wkr, MLA_ROPE)
    wuk = jnp.transpose(w_ukv[:, :, :MLA_NOPE], (1, 2, 0))
    zeros = jnp.zeros_like(wuk)
    even = jnp.concatenate([wuk, zeros], axis=1)
    odd = jnp.concatenate([zeros, wuk], axis=1)
    wuk2 = jnp.where((jnp.arange(hd) % 2 == 0)[:, None, None], even, odd)
    wuv = jnp.transpose(w_ukv[:, :, MLA_NOPE:], (1, 0, 2))
    d = w_dkv.shape[0]
    pad_r = lambda a: jnp.concatenate([a, jnp.zeros((d, LANES - MLA_ROPE), BF16)], axis=1)
    wukp = jnp.concatenate([jnp.zeros((MLA_KV_LORA, hd, MLA_ROPE), BF16), w_ukv[:, :, :MLA_NOPE],
                            jnp.zeros((MLA_KV_LORA, hd, LANES - MLA_ROPE - MLA_NOPE), BF16)], axis=2)
    return dict(wdq=cast(w_dq), wqn=wqn, wqr=wqr, wqs=wqs, wkc=wkc, wkr=wkr, wks=wks, wuk=wuk2, wuv=wuv, wo=cast(w_o),
                wqnt=wqn.T, wqrt=wqr.T, wqst=wqs.T, wkrp=pad_r(wkr), wksp=pad_r(wks),
                wukp=wukp.reshape(MLA_KV_LORA, hd * LANES), wuvt=wuv.transpose(0, 2, 1).reshape(hd * MLA_V, MLA_KV_LORA),
                wot=cast(w_o).T)


def _mla_proj_kernel(x_ref, gm_ref, wdq_ref, gq_ref, wqn_ref, wqr_ref, wqs_ref, wkc_ref, wkr_ref, wks_ref,
                     gkv_ref, wuk_ref, cq_ref, sq_ref, ck_ref, sk_ref,
                     ql_ref, qr_ref, ckv_ref, kr_ref, ckvb_ref, krb_ref):
    h = _rms(x_ref[...], gm_ref[...]).astype(BF16)
    cq = _rms(_dot(h, wdq_ref[...]), gq_ref[...]).astype(BF16)
    qn = _dot(cq, wqn_ref[...])
    qr = ((_dot(cq, wqr_ref[...]) * cq_ref[...] + _dot(cq, wqs_ref[...]) * sq_ref[...]) * MLA_SCALE).astype(BF16)
    for hh in range(MLA_HEADS):
        pair = qn[:, (hh // 2) * LANES:(hh // 2 + 1) * LANES].astype(BF16)
        ql_ref[hh] = (_dot(pair, wuk_ref[hh]) * MLA_SCALE).astype(BF16)
        qr_ref[hh] = qr[:, hh * MLA_ROPE:(hh + 1) * MLA_ROPE]
    ckv = _rms(_dot(h, wkc_ref[...]), gkv_ref[...])
    ckv_ref[...] = ckv
    ckvb_ref[...] = ckv.astype(BF16)
    kr = _dot(h, wkr_ref[...]) * ck_ref[...] + _dot(h, wks_ref[...]) * sk_ref[...]
    kr_ref[...] = kr
    krb_ref[...] = kr.astype(BF16)


def _mla_proj(x, g_mix, wts, g_q, g_kv, tables):
    t, d = x.shape
    tm = min(256, t)
    cos_q, sin_q, cos_k, sin_k = tables
    nrep = cos_q.shape[0] // tm
    tab = lambda wdt: pl.BlockSpec((tm, wdt), lambda i: (i % nrep, 0))
    hd = MLA_HEADS
    row = lambda wdt: pl.BlockSpec((tm, wdt), lambda i: (i, 0))
    hm = lambda wdt: pl.BlockSpec((hd, tm, wdt), lambda i: (0, i, 0))
    return pl.pallas_call(
        _mla_proj_kernel,
        grid=(t // tm,),
        in_specs=[
            row(d), _whole((1, d)), _whole(wts["wdq"].shape), _whole((1, MLA_Q_LORA)),
            _whole(wts["wqn"].shape), _whole(wts["wqr"].shape), _whole(wts["wqs"].shape),
            _whole(wts["wkc"].shape), _whole(wts["wkr"].shape), _whole(wts["wks"].shape),
            _whole((1, MLA_KV_LORA)), _whole(wts["wuk"].shape),
            tab(hd * MLA_ROPE), tab(hd * MLA_ROPE), tab(MLA_ROPE), tab(MLA_ROPE),
        ],
        out_specs=[hm(MLA_KV_LORA), hm(MLA_ROPE), row(MLA_KV_LORA), row(MLA_ROPE), row(MLA_KV_LORA), row(MLA_ROPE)],
        out_shape=[
            jax.ShapeDtypeStruct((hd, t, MLA_KV_LORA), BF16), jax.ShapeDtypeStruct((hd, t, MLA_ROPE), BF16),
            jax.ShapeDtypeStruct((t, MLA_KV_LORA), F32), jax.ShapeDtypeStruct((t, MLA_ROPE), F32),
            jax.ShapeDtypeStruct((t, MLA_KV_LORA), BF16), jax.ShapeDtypeStruct((t, MLA_ROPE), BF16),
        ],
        compiler_params=_params("parallel"),
        name="mla_proj",
    )(x, g_mix.reshape(1, d), wts["wdq"], g_q.reshape(1, -1), wts["wqn"], wts["wqr"], wts["wqs"],
      wts["wkc"], wts["wkr"], wts["wks"], g_kv.reshape(1, -1), wts["wuk"], cos_q, sin_q, cos_k, sin_k)


def _mla_proj_t_kernel(x_ref, gm_ref, wdq_ref, gq_ref, wqnt_ref, wqrt_ref, wqst_ref, wkc_ref, wkrp_ref, wksp_ref,
                       gkv_ref, wukp_ref, wuvt_ref, cqt_ref, sqt_ref, ckp_ref, skp_ref,
                       qt_ref, kh_ref, vt_ref, ckv_ref, kr_ref):
    tm = x_ref.shape[0]
    hd = MLA_HEADS
    qscale = MLA_SCALE * LOG2E
    h = _rms(x_ref[...], gm_ref[...]).astype(BF16)
    cq = _rms(_dot(h, wdq_ref[...]), gq_ref[...]).astype(BF16)
    qnt = (_dot_t(wqnt_ref[...], cq) * qscale).astype(BF16)
    qrt = ((_dot_t(wqrt_ref[...], cq) * cqt_ref[...] + _dot_t(wqst_ref[...], cq) * sqt_ref[...]) * qscale).astype(BF16)
    zeros = jnp.zeros((LANES - MLA_ROPE - MLA_NOPE, tm), BF16)
    ckv = _rms(_dot(h, wkc_ref[...]), gkv_ref[...])
    ckv_ref[...] = ckv
    cb = ckv.astype(BF16)
    krp = _dot(h, wkrp_ref[...]) * ckp_ref[...] + _dot(h, wksp_ref[...]) * skp_ref[...]
    kr_ref[...] = krp[:, :MLA_ROPE]
    kn = _dot(cb, wukp_ref[...])
    vt = _dot_t(wuvt_ref[...], cb).astype(BF16)
    ones = jnp.ones((BF16_ROWS, tm), BF16)
    for hh in range(hd):
        qt_ref[hh, 0:MLA_ROPE, :] = qrt[hh * MLA_ROPE:(hh + 1) * MLA_ROPE]
        qt_ref[hh, MLA_ROPE:MLA_ROPE + MLA_NOPE, :] = qnt[hh * MLA_NOPE:(hh + 1) * MLA_NOPE]
        qt_ref[hh, MLA_ROPE + MLA_NOPE:, :] = zeros
        kh_ref[hh] = (kn[:, hh * LANES:(hh + 1) * LANES] + krp).astype(BF16)
        vt_ref[hh, 0:MLA_V, :] = vt[hh * MLA_V:(hh + 1) * MLA_V]
        vt_ref[hh, MLA_V:, :] = ones


def _mla_proj_t(x, g_mix, wts, g_q, g_kv, tables, seq):
    t, d = x.shape
    tm = min(256, seq)
    hd = MLA_HEADS
    cos_q, sin_q, cos_k, sin_k = tables
    pad = lambda a: jnp.concatenate([a, jnp.zeros((seq, LANES - MLA_ROPE), F32)], axis=1)
    nrep = seq // tm
    row = lambda wdt: pl.BlockSpec((tm, wdt), lambda i: (i, 0))
    names = ("wdq", "wqnt", "wqrt", "wqst", "wkc", "wkrp", "wksp", "wukp", "wuvt")
    w = {n: wts[n] for n in names}
    return pl.pallas_call(
        _mla_proj_t_kernel,
        grid=(t // tm,),
        in_specs=[
            row(d), _whole((1, d)), _whole(w["wdq"].shape), _whole((1, MLA_Q_LORA)),
            _whole(w["wqnt"].shape), _whole(w["wqrt"].shape), _whole(w["wqst"].shape),
            _whole(w["wkc"].shape), _whole(w["wkrp"].shape), _whole(w["wksp"].shape),
            _whole((1, MLA_KV_LORA)), _whole(w["wukp"].shape), _whole(w["wuvt"].shape),
            pl.BlockSpec((hd * MLA_ROPE, tm), lambda i: (0, i % nrep)),
            pl.BlockSpec((hd * MLA_ROPE, tm), lambda i: (0, i % nrep)),
            pl.BlockSpec((tm, LANES), lambda i: (i % nrep, 0)),
            pl.BlockSpec((tm, LANES), lambda i: (i % nrep, 0)),
        ],
        out_specs=[
            pl.BlockSpec((hd, LANES, tm), lambda i: (0, 0, i)),
            pl.BlockSpec((hd, tm, LANES), lambda i: (0, i, 0)),
            pl.BlockSpec((hd, V_ROWS, tm), lambda i: (0, 0, i)),
            row(MLA_KV_LORA), row(MLA_ROPE),
        ],
        out_shape=[
            jax.ShapeDtypeStruct((hd, LANES, t), BF16), jax.ShapeDtypeStruct((hd, t, LANES), BF16),
            jax.ShapeDtypeStruct((hd, V_ROWS, t), BF16),
            jax.ShapeDtypeStruct((t, MLA_KV_LORA), F32), jax.ShapeDtypeStruct((t, MLA_ROPE), F32),
        ],
        compiler_params=_params("parallel"),
        name="mla_proj_t",
    )(x, g_mix.reshape(1, d), w["wdq"], g_q.reshape(1, -1), w["wqnt"], w["wqrt"], w["wqst"],
      w["wkc"], w["wkrp"], w["wksp"], g_kv.reshape(1, -1), w["wukp"], w["wuvt"],
      cos_q.T, sin_q.T, pad(cos_k), pad(sin_k))


def _mla_epilogue(o_lat, x_ref, wuv_ref, wo_ref, o_ref, cat_scr, tq):
    ob = o_lat.astype(BF16)
    for hh in range(MLA_HEADS):
        cat_scr[:, hh * MLA_V:(hh + 1) * MLA_V] = _dot(ob[hh * tq:(hh + 1) * tq], wuv_ref[hh]).astype(BF16)
    o_ref[...] = x_ref[...] + _dot(cat_scr[...], wo_ref[...])


def _mla_samp_kernel(ql_ref, qr_ref, cc_ref, cr_ref, cn_ref, rn_ref, x_ref, wuv_ref, wo_ref, o_ref, cat_scr, *, n_q):
    b = pl.program_id(0)
    hd = MLA_HEADS
    ql = ql_ref[...].reshape(hd * n_q, MLA_KV_LORA)
    qr = qr_ref[...].reshape(hd * n_q, MLA_ROPE)
    cc = cc_ref[0].astype(BF16)
    cr = cr_ref[0].astype(BF16)
    cn = cn_ref[...]
    s1 = _dot_t(ql, cc) + _dot_t(qr, cr)
    s2 = _dot_t(ql, cn) + _dot_t(qr, rn_ref[...])
    own = _div(lax.broadcasted_iota(I32, (1, LANES), 1), n_q) == _mod(b, LANES // n_q)
    s2 = jnp.where(own, s2, NEG_INF)
    m = jnp.maximum(jnp.max(s1, axis=1, keepdims=True), jnp.max(s2, axis=1, keepdims=True))
    p1 = jnp.exp(s1 - m)
    p2 = jnp.exp(s2 - m)
    l = jnp.sum(p1, axis=1, keepdims=True) + jnp.sum(p2, axis=1, keepdims=True)
    o_lat = (_dot(p1.astype(BF16), cc) + _dot(p2.astype(BF16), cn)) / l
    _mla_epilogue(o_lat, x_ref, wuv_ref, wo_ref, o_ref, cat_scr, n_q)


def _mla_attn_sample(x, ql, qr, ckvb, krb, cache_c, cache_r, wts, n_b, n_q):
    t, d = x.shape
    hd = MLA_HEADS
    past = cache_c.shape[1]
    per = LANES // n_q
    return pl.pallas_call(
        functools.partial(_mla_samp_kernel, n_q=n_q),
        grid=(n_b,),
        in_specs=[
            pl.BlockSpec((hd, n_q, MLA_KV_LORA), lambda b: (0, b, 0)),
            pl.BlockSpec((hd, n_q, MLA_ROPE), lambda b: (0, b, 0)),
            pl.BlockSpec((1, past, MLA_KV_LORA), lambda b: (b, 0, 0)),
            pl.BlockSpec((1, past, MLA_ROPE), lambda b: (b, 0, 0)),
            pl.BlockSpec((LANES, MLA_KV_LORA), lambda b: (b // per, 0)),
            pl.BlockSpec((LANES, MLA_ROPE), lambda b: (b // per, 0)),
            pl.BlockSpec((n_q, d), lambda b: (b, 0)),
            _whole(wts["wuv"].shape), _whole(wts["wo"].shape),
        ],
        out_specs=pl.BlockSpec((n_q, d), lambda b: (b, 0)),
        out_shape=jax.ShapeDtypeStruct((t, d), F32),
        scratch_shapes=[pltpu.VMEM((n_q, hd * MLA_V), BF16)],
        compiler_params=_params("parallel"),
        name="mla_attn_sample",
    )(ql, qr, cache_c, cache_r, ckvb, krb, x, wts["wuv"], wts["wo"])


def _dsa_proj_kernel(x_ref, gm_ref, wqkv_ref, wqi_ref, wkw_ref, gki_ref,
                     k_ref, v_ref, ki_ref, qb_ref, kb_ref, vb_ref, qib_ref, kib_ref, wi_ref):
    hdm = DSA_HEADS * DSA_HEAD_DIM
    h = _rms(x_ref[...], gm_ref[...]).astype(BF16)
    qkv = _dot(h, wqkv_ref[...])
    q = qkv[:, :hdm] * DSA_SCALE
    k = qkv[:, hdm:2 * hdm]
    v = qkv[:, 2 * hdm:]
    k_ref[...] = k
    v_ref[...] = v
    qi = _dot(h, wqi_ref[...]) * IDX_SCALE
    kw = _dot(h, wkw_ref[...])
    ki = _rms(kw[:, :IDX_DIM], gki_ref[...])
    ki_ref[...] = ki
    kib_ref[...] = ki.astype(BF16)
    wi_ref[...] = kw[:, IDX_DIM:IDX_DIM + IDX_HEADS] * (IDX_HEADS ** -0.5)
    qb_ref[...] = q.astype(BF16)
    kb_ref[...] = k.astype(BF16)
    vb_ref[...] = v.astype(BF16)
    for hh in range(IDX_HEADS):
        qib_ref[hh] = qi[:, hh * IDX_DIM:(hh + 1) * IDX_DIM].astype(BF16)


def _dsa_proj(x, g_mix, w_qkv, w_qidx, w_kw, g_kidx):
    t, d = x.shape
    tm = min(512, t)
    hdm = DSA_HEADS * DSA_HEAD_DIM
    row = lambda wdt: pl.BlockSpec((tm, wdt), lambda i: (i, 0))
    return pl.pallas_call(
        _dsa_proj_kernel,
        grid=(t // tm,),
        in_specs=[row(d), _whole((1, d)), _whole(w_qkv.shape), _whole(w_qidx.shape), _whole(w_kw.shape),
                  _whole((1, IDX_DIM))],
        out_specs=[row(hdm), row(hdm), row(IDX_DIM), row(hdm), row(hdm), row(hdm),
                   pl.BlockSpec((IDX_HEADS, tm, IDX_DIM), lambda i: (0, i, 0)), row(IDX_DIM), row(IDX_HEADS)],
        out_shape=[
            jax.ShapeDtypeStruct((t, hdm), F32), jax.ShapeDtypeStruct((t, hdm), F32),
            jax.ShapeDtypeStruct((t, IDX_DIM), F32), jax.ShapeDtypeStruct((t, hdm), BF16),
            jax.ShapeDtypeStruct((t, hdm), BF16), jax.ShapeDtypeStruct((t, hdm), BF16),
            jax.ShapeDtypeStruct((IDX_HEADS, t, IDX_DIM), BF16), jax.ShapeDtypeStruct((t, IDX_DIM), BF16),
            jax.ShapeDtypeStruct((t, IDX_HEADS), F32),
        ],
        compiler_params=_params("parallel"),
        name="dsa_proj",
    )(x, g_mix.reshape(1, d), w_qkv, w_qidx, w_kw, g_kidx.reshape(1, IDX_DIM))


def _dsa_proj_t_kernel(x_ref, gm_ref, wkv_ref, wqt_ref, wvt_ref, wqit_ref, wkw_ref, wwt_ref, gki_ref,
                       k_ref, v_ref, ki_ref, qt_ref, kh_ref, vt_ref, qit_ref, kib_ref, wit_ref):
    tm = x_ref.shape[0]
    hdm = DSA_HEADS * DSA_HEAD_DIM
    dh = DSA_HEAD_DIM
    h = _rms(x_ref[...], gm_ref[...]).astype(BF16)
    kv = _dot(h, wkv_ref[...])
    k = kv[:, :hdm]
    k_ref[...] = k
    v_ref[...] = kv[:, hdm:]
    kw = _dot(h, wkw_ref[...])
    ki = _rms(kw[:, :IDX_DIM], gki_ref[...])
    ki_ref[...] = ki
    kib_ref[...] = ki.astype(BF16)
    wit_ref[...] = _dot_t(wwt_ref[...], h)[:IDX_HEADS] * (IDX_HEADS ** -0.5)
    qt = (_dot_t(wqt_ref[...], h) * (DSA_SCALE * LOG2E)).astype(BF16)
    vt = _dot_t(wvt_ref[...], h).astype(BF16)
    qit = (_dot_t(wqit_ref[...], h) * IDX_SCALE).astype(BF16)
    ones = jnp.ones((BF16_ROWS, tm), BF16)
    for hh in range(DSA_HEADS):
        qt_ref[hh] = qt[hh * dh:(hh + 1) * dh]
        kh_ref[hh] = k[:, hh * dh:(hh + 1) * dh].astype(BF16)
        vt_ref[hh, 0:dh, :] = vt[hh * dh:(hh + 1) * dh]
        vt_ref[hh, dh:, :] = ones
    for hh in range(IDX_HEADS):
        qit_ref[hh] = qit[hh * IDX_DIM:(hh + 1) * IDX_DIM]


def _dsa_proj_t(x, g_mix, w, g_kidx):
    t, d = x.shape
    tm = min(256, t)
    hd, dh = DSA_HEADS, DSA_HEAD_DIM
    hdm = hd * dh
    row = lambda wdt: pl.BlockSpec((tm, wdt), lambda i: (i, 0))
    names = ("wkv", "wqt", "wvt", "wqit", "wkw", "wwt")
    return pl.pallas_call(
        _dsa_proj_t_kernel,
        grid=(t // tm,),
        in_specs=[row(d), _whole((1, d))] + [_whole(w[n].shape) for n in names] + [_whole((1, IDX_DIM))],
        out_specs=[
            row(hdm), row(hdm), row(IDX_DIM),
            pl.BlockSpec((hd, dh, tm), lambda i: (0, 0, i)),
            pl.BlockSpec((hd, tm, dh), lambda i: (0, i, 0)),
            pl.BlockSpec((hd, V_ROWS, tm), lambda i: (0, 0, i)),
            pl.BlockSpec((IDX_HEADS, IDX_DIM, tm), lambda i: (0, 0, i)),
            row(IDX_DIM),
            pl.BlockSpec((IDX_HEADS, tm), lambda i: (0, i)),
        ],
        out_shape=[
            jax.ShapeDtypeStruct((t, hdm), F32), jax.ShapeDtypeStruct((t, hdm), F32),
            jax.ShapeDtypeStruct((t, IDX_DIM), F32),
            jax.ShapeDtypeStruct((hd, dh, t), BF16), jax.ShapeDtypeStruct((hd, t, dh), BF16),
            jax.ShapeDtypeStruct((hd, V_ROWS, t), BF16),
            jax.ShapeDtypeStruct((IDX_HEADS, IDX_DIM, t), BF16), jax.ShapeDtypeStruct((t, IDX_DIM), BF16),
            jax.ShapeDtypeStruct((IDX_HEADS, t), F32),
        ],
        compiler_params=_params("parallel"),
        name="dsa_proj_t",
    )(x, g_mix.reshape(1, d), *[w[n] for n in names], g_kidx.reshape(1, IDX_DIM))


def _sort_key(score):
    bits = lax.bitcast_convert_type(score, I32)
    return jnp.where(bits < 0, bits ^ jnp.int32(0x7FFFFFFF), bits)


def _kth_largest_key(count, topk):
    c0 = count(lambda key, idx: jnp.where(key >= 0, 1, 0))
    t0 = jnp.where(c0 >= topk, jnp.int32(0), jnp.int32(INT_MIN))

    def bit_body(i, t):
        cand = t + lax.shift_left(jnp.int32(1), jnp.int32(30) - i)
        c = count(lambda key, idx: jnp.where(key >= cand, 1, 0))
        return jnp.where(c >= topk, cand, t)

    return lax.fori_loop(0, 31, bit_body, t0)


def _tie_cutoff(count, thr, rem, nbits):
    def bit_body(i, j):
        cand = j + lax.shift_left(jnp.int32(1), jnp.int32(nbits - 1) - i)
        c = count(lambda key, idx: jnp.where(key == thr, jnp.where(idx < cand, 1, 0), 0))
        return jnp.where(c < rem, cand, j)

    return lax.fori_loop(0, nbits, bit_body, jnp.zeros_like(thr))


def _select(key, idx, thr, cut):
    chosen = jnp.where(key > thr, 1, jnp.where(key == thr, jnp.where(idx <= cut, 1, 0), 0))
    return jnp.where(key > KEY_NEG_INF, chosen, 0) > 0


def _dsa_select_kernel(qit_ref, wit_ref, ki_ref, bias_ref, key_scr, cut_scr, *, tq, tk, topk):
    qb = pl.program_id(1)
    seq = ki_ref.shape[0]
    nk = seq // tk
    nvis = _div((qb + 1) * tq + tk - 1, tk)
    q_chunk = _div(qb * tq + lax.broadcasted_iota(I32, (1, tq), 1), CHUNK)
    w = wit_ref[...]
    row_idx = lax.broadcasted_iota(I32, (tk, 1), 0)

    def score_body(j, carry):
        off = pl.multiple_of(j * tk, tk)
        kb = ki_ref[pl.ds(off, tk), :]
        acc = jnp.zeros((tk, tq), F32)
        for hh in range(IDX_HEADS):
            acc = acc + w[hh:hh + 1, :] * jnp.maximum(_dot(kb, qit_ref[hh]), 0.0)
        vis = _div(off + row_idx, CHUNK) <= q_chunk
        key_scr[pl.ds(off, tk), :] = _sort_key(jnp.where(vis, acc, NEG_INF))
        return carry

    lax.fori_loop(0, nvis, score_body, 0)

    def count(hit):
        def body(j, c):
            off = pl.multiple_of(j * tk, tk)
            hits = hit(key_scr[pl.ds(off, tk), :], off + row_idx)
            return c + jnp.sum(hits.reshape(tk // 8, 8, tq), axis=0)
        c = lax.fori_loop(0, nvis, body, jnp.zeros((8, tq), I32))
        return jnp.sum(c, axis=0, keepdims=True)

    thr = _kth_largest_key(count, topk)
    rem = topk - count(lambda key, idx: jnp.where(key > thr, 1, 0))
    n_eq = count(lambda key, idx: jnp.where(key == thr, 1, 0))
    cut_scr[...] = jnp.full((1, tq), seq, I32)
    need = jnp.max(jnp.where(n_eq > rem, jnp.where(thr > KEY_NEG_INF, 1, 0), 0))

    @pl.when(need > 0)
    def _():
        cut_scr[...] = _tie_cutoff(count, thr, rem, int(seq - 1).bit_length())

    cut = cut_scr[...]

    def out_body(j, carry):
        off = pl.multiple_of(j * tk, tk)
        sel = _select(key_scr[pl.ds(off, tk), :], off + row_idx, thr, cut)
        bias_ref[pl.ds(off, tk), :] = jnp.where(sel, 0.0, NEG_INF).astype(BF16)
        return carry

    lax.fori_loop(0, nvis, out_body, 0)

    def fill_body(j, carry):
        off = pl.multiple_of(j * tk, tk)
        bias_ref[pl.ds(off, tk), :] = jnp.full((tk, tq), NEG_INF, BF16)
        return carry

    lax.fori_loop(nvis, nk, fill_body, 0)


def _dsa_select(qit, wit, kib, batch, seq, topk):
    tq, tk = min(256, seq), min(512, seq)
    nq = seq // tq
    t = batch * seq
    return pl.pallas_call(
        functools.partial(_dsa_select_kernel, tq=tq, tk=tk, topk=topk),
        grid=(batch, nq),
        in_specs=[
            pl.BlockSpec((IDX_HEADS, IDX_DIM, tq), lambda b, q: (0, 0, b * nq + q)),
            pl.BlockSpec((IDX_HEADS, tq), lambda b, q: (0, b * nq + q)),
            pl.BlockSpec((seq, IDX_DIM), lambda b, q: (b, 0)),
        ],
        out_specs=pl.BlockSpec((seq, tq), lambda b, q: (0, b * nq + q)),
        out_shape=jax.ShapeDtypeStruct((seq, t), BF16),
        scratch_shapes=[pltpu.VMEM((seq, tq), I32), pltpu.VMEM((1, tq), I32)],
        compiler_params=_params("parallel", "arbitrary"),
        name="dsa_select",
    )(qit, wit, kib)


def _dsa_samp_kernel(q_ref, qi_ref, wi_ref, kc_ref, vc_ref, kic_ref, kn_ref, vn_ref, kin_ref, x_ref, wo_ref, o_ref,
                     *, n_q, topk):
    b = pl.program_id(0)
    hd, dh = DSA_HEADS, DSA_HEAD_DIM
    past = kc_ref.shape[1]
    n_keys = past + LANES
    own = _div(lax.broadcasted_iota(I32, (1, LANES), 1), n_q) == _mod(b, LANES // n_q)

    qi = qi_ref[...].reshape(IDX_HEADS * n_q, IDX_DIM)
    lg1 = jnp.maximum(_dot_t(qi, kic_ref[0].astype(BF16)), 0.0)
    lg2 = jnp.maximum(_dot_t(qi, kin_ref[...]), 0.0)
    w = wi_ref[...]
    sc1 = jnp.zeros((n_q, past), F32)
    sc2 = jnp.zeros((n_q, LANES), F32)
    for hh in range(IDX_HEADS):
        sc1 = sc1 + w[:, hh:hh + 1] * lg1[hh * n_q:(hh + 1) * n_q]
        sc2 = sc2 + w[:, hh:hh + 1] * lg2[hh * n_q:(hh + 1) * n_q]
    key = _sort_key(jnp.concatenate([sc1, jnp.where(own, sc2, NEG_INF)], axis=1))
    idx = lax.broadcasted_iota(I32, (1, n_keys), 1)

    count = lambda hit: jnp.sum(hit(key, idx), axis=1, keepdims=True)
    thr = _kth_largest_key(count, topk)
    rem = topk - count(lambda k_, i_: jnp.where(k_ > thr, 1, 0))
    cut = _tie_cutoff(count, thr, rem, int(n_keys - 1).bit_length())
    bias = jnp.where(_select(key, idx, thr, cut), 0.0, NEG_INF)
    bias = jnp.concatenate([bias] * hd, axis=0)

    lane_head = _div(lax.broadcasted_iota(I32, (1, hd * dh), 1), dh)
    qf = q_ref[...].astype(F32)
    qbd = jnp.concatenate([jnp.where(lane_head == hh, qf, 0.0) for hh in range(hd)], axis=0).astype(BF16)
    kc = kc_ref[0].astype(BF16)
    vc = vc_ref[0].astype(BF16)
    s1 = _dot_t(qbd, kc) + bias[:, :past]
    s2 = _dot_t(qbd, kn_ref[...]) + bias[:, past:]
    m = jnp.maximum(jnp.max(s1, axis=1, keepdims=True), jnp.max(s2, axis=1, keepdims=True))
    p1 = jnp.exp(s1 - m)
    p2 = jnp.exp(s2 - m)
    l = jnp.sum(p1, axis=1, keepdims=True) + jnp.sum(p2, axis=1, keepdims=True)
    o_all = (_dot(p1.astype(BF16), vc) + _dot(p2.astype(BF16), vn_ref[...])) / l
    out = jnp.zeros((n_q, hd * dh), F32)
    for hh in range(hd):
        out = out + jnp.where(lane_head == hh, o_all[hh * n_q:(hh + 1) * n_q], 0.0)
    o_ref[...] = x_ref[...] + _dot(out.astype(BF16), wo_ref[...])


def _dsa_attn_sample(x, qb, qib, wi, kb, vb, kib, cache_k, cache_v, cache_ki, w_o, n_b, n_q, topk):
    t, d = x.shape
    hdm = DSA_HEADS * DSA_HEAD_DIM
    past = cache_k.shape[1]
    per = LANES // n_q
    return pl.pallas_call(
        functools.partial(_dsa_samp_kernel, n_q=n_q, topk=topk),
        grid=(n_b,),
        in_specs=[
            pl.BlockSpec((n_q, hdm), lambda b: (b, 0)),
            pl.BlockSpec((IDX_HEADS, n_q, IDX_DIM), lambda b: (0, b, 0)),
            pl.BlockSpec((n_q, IDX_HEADS), lambda b: (b, 0)),
            pl.BlockSpec((1, past, hdm), lambda b: (b, 0, 0)),
            pl.BlockSpec((1, past, hdm), lambda b: (b, 0, 0)),
            pl.BlockSpec((1, past, IDX_DIM), lambda b: (b, 0, 0)),
            pl.BlockSpec((LANES, hdm), lambda b: (b // per, 0)),
            pl.BlockSpec((LANES, hdm), lambda b: (b // per, 0)),
            pl.BlockSpec((LANES, IDX_DIM), lambda b: (b // per, 0)),
            pl.BlockSpec((n_q, d), lambda b: (b, 0)),
            _whole(w_o.shape),
        ],
        out_specs=pl.BlockSpec((n_q, d), lambda b: (b, 0)),
        out_shape=jax.ShapeDtypeStruct((t, d), F32),
        compiler_params=_params("parallel"),
        name="dsa_attn_sample",
    )(qb, qib, wi, cache_k.reshape(n_b, past, hdm), cache_v.reshape(n_b, past, hdm), cache_ki, kb, vb, kib, x, w_o)


def kernel(x_prompt, x_sample, cache_mla_ckv, cache_mla_krope, cache_dsa_k, cache_dsa_v, cache_dsa_kidx, norm_mix, norm_ffn, norm_final, mla_w_dq, mla_g_q, mla_w_uq, mla_w_dkv, mla_g_kv, mla_w_ukv, mla_w_o, cmlp_w_in, cmlp_ln_g, cmlp_ln_b, cmlp_w_s, cmlp_b_s, cmlp_w_out, dsa_w_qkv, dsa_w_o, dsa_w_qidx, dsa_w_kidx, dsa_g_kidx, dsa_w_widx, ffn_w_in, ffn_w_out):
    batch, seq, d = x_prompt.shape
    n_b, n_q, _ = x_sample.shape
    past = cache_mla_ckv.shape[2]
    depth = norm_mix.shape[0]
    xp = x_prompt.reshape(batch * seq, d)
    xs = x_sample.reshape(n_b * n_q, d)
    tab_p = _rope_tables(jnp.arange(seq))
    tab_s = tuple(jnp.tile(a, (n_b, 1)) for a in _rope_tables(past + jnp.arange(n_q)))
    cast = lambda a: a.astype(BF16)
    outs = {k: [] for k in ("ckv_p", "kr_p", "ckv_s", "kr_s", "cv_s", "dk_p", "dv_p", "di_p", "dk_s", "dv_s", "di_s")}
    for i in range(depth):
        kind, j = i % 3, i // 3
        if kind == 0:
            wts = _mla_weights(mla_w_dq[j], mla_w_uq[j], mla_w_dkv[j], mla_w_ukv[j], mla_w_o[j])
            qt, kh, vt, ckv, kr = _mla_proj_t(xp, norm_mix[i], wts, mla_g_q[j], mla_g_kv[j], tab_p, seq)
            xp = _attn_t(xp, qt, kh, vt, None, wts["wot"], batch, seq, "mla_attn_prompt")
            outs["ckv_p"].append(ckv.reshape(batch, seq, -1)); outs["kr_p"].append(kr.reshape(batch, seq, -1))
            ql, qr, ckv, kr, ckvb, krb = _mla_proj(xs, norm_mix[i], wts, mla_g_q[j], mla_g_kv[j], tab_s)
            xs = _mla_attn_sample(xs, ql, qr, ckvb, krb, cache_mla_ckv[j], cache_mla_krope[j], wts, n_b, n_q)
            outs["ckv_s"].append(ckv.reshape(n_b, n_q, -1)); outs["kr_s"].append(kr.reshape(n_b, n_q, -1))
        elif kind == 1:
            w_in, w_out = cast(cmlp_w_in[j]), cast(cmlp_w_out[j])
            xp, _ = _cmlp(xp, norm_mix[i], w_in, cmlp_ln_g[j], cmlp_ln_b[j], cmlp_w_s[j], cmlp_b_s[j], w_out,
                          min(seq, CMLP_CHUNK), False)
            xs, v_s = _cmlp(xs, norm_mix[i], w_in, cmlp_ln_g[j], cmlp_ln_b[j], cmlp_w_s[j], cmlp_b_s[j], w_out,
                            min(n_q, CMLP_CHUNK), True)
            outs["cv_s"].append(v_s.reshape(n_b, n_q, -1))
        else:
            hdm = DSA_HEADS * DSA_HEAD_DIM
            w_qkv, w_qidx, w_o = cast(dsa_w_qkv[j]), cast(dsa_w_qidx[j]), cast(dsa_w_o[j])
            w_kidx, w_widx = cast(dsa_w_kidx[j]), cast(dsa_w_widx[j])
            zpad = lambda n: jnp.zeros((d, n), BF16)
            wt = dict(wkv=w_qkv[:, hdm:], wqt=w_qkv[:, :hdm].T, wvt=w_qkv[:, 2 * hdm:].T, wqit=w_qidx.T,
                      wkw=jnp.concatenate([w_kidx, zpad(LANES - IDX_DIM)], axis=1),
                      wwt=jnp.concatenate([w_widx, zpad(BF16_ROWS - IDX_HEADS)], axis=1).T)
            hshape = (DSA_HEADS, DSA_HEAD_DIM)
            k, v, ki, qt, kh, vt, qit, kib, wit = _dsa_proj_t(xp, norm_mix[i], wt, dsa_g_kidx[j])
            bias = _dsa_select(qit, wit, kib, batch, seq, min(TOPK_MAX, seq // 4))
            xp = _attn_t(xp, qt, kh, vt, bias, w_o.T, batch, seq, "dsa_attn_prompt")
            outs["dk_p"].append(k.reshape((batch, seq) + hshape)); outs["dv_p"].append(v.reshape((batch, seq) + hshape))
            outs["di_p"].append(ki.reshape(batch, seq, -1))
            w_kw = jnp.concatenate([w_kidx, w_widx, zpad(LANES - IDX_DIM - IDX_HEADS)], axis=1)
            k, v, ki, qb, kb, vb, qib, kib, wi = _dsa_proj(xs, norm_mix[i], w_qkv, w_qidx, w_kw, dsa_g_kidx[j])
            xs = _dsa_attn_sample(xs, qb, qib, wi, kb, vb, kib, cache_dsa_k[j], cache_dsa_v[j], cache_dsa_kidx[j], w_o,
                                  n_b, n_q, min(TOPK_MAX, (past + n_q) // 4))
            outs["dk_s"].append(k.reshape((n_b, n_q) + hshape)); outs["dv_s"].append(v.reshape((n_b, n_q) + hshape))
            outs["di_s"].append(ki.reshape(n_b, n_q, -1))
        w_in, w_out = cast(ffn_w_in[i]), cast(ffn_w_out[i])
        final = i == depth - 1
        xp = _ffn(xp, norm_ffn[i], w_in, w_out, norm_final, final)
        xs = _ffn(xs, norm_ffn[i], w_in, w_out, norm_final, final)
    st = lambda name: jnp.stack(outs[name])
    return (xp.reshape(batch, seq, d), xs.reshape(n_b, n_q, d),
            st("ckv_p"), st("kr_p"), st("ckv_s"), st("kr_s"), st("cv_s"),
            st("dk_p"), st("dv_p"), st("di_p"), st("dk_s"), st("dv_s"), st("di_s"))
```

```python
import functools

import numpy as np
import jax
import jax.numpy as jnp
from jax import lax
from jax.experimental import pallas as pl
from jax.experimental.pallas import tpu as pltpu

F32, BF16, I32 = jnp.float32, jnp.bfloat16, jnp.int32

CHUNK = 64
EPS = 1e-6
MLA_HEADS, MLA_Q_LORA, MLA_KV_LORA, MLA_NOPE, MLA_ROPE, MLA_V = 16, 512, 256, 64, 32, 64
ROPE_BASE = 10000.0
MLA_SCALE = (MLA_NOPE + MLA_ROPE) ** -0.5
CMLP_CHUNK, CMLP_WIDTH, CMLP_GROUPS = 128, 2048, 8
DSA_HEADS, DSA_HEAD_DIM = 16, 64
DSA_SCALE = DSA_HEAD_DIM ** -0.5
IDX_HEADS, IDX_DIM = 8, 64
IDX_SCALE = IDX_DIM ** -0.5
TOPK_MAX = 256

LANES = 128
BF16_ROWS = 16
VMEM_LIMIT = 52 * 1024 * 1024
NEG_INF = float("-inf")
INT_MIN = -2 ** 31
KEY_NEG_INF = -2139095041
LOG2E = float(np.log2(np.e))
HEAD_V = 64
V_ROWS = HEAD_V + BF16_ROWS


def _dot(a, b):
    return jnp.dot(a, b, preferred_element_type=F32)


def _dot_t(a, b):
    return lax.dot_general(a, b, (((1,), (1,)), ((), ())), preferred_element_type=F32)


def _rms(x, g):
    return x * lax.rsqrt(jnp.mean(x * x, axis=-1, keepdims=True) + EPS) * g


def _log2(n):
    assert n > 0 and n & (n - 1) == 0, n
    return n.bit_length() - 1


def _div(x, n):
    return lax.shift_right_logical(x, jnp.int32(_log2(n)))


def _mod(x, n):
    assert n & (n - 1) == 0, n
    return x & (n - 1)


def _params(*sem):
    return pltpu.CompilerParams(dimension_semantics=sem, vmem_limit_bytes=VMEM_LIMIT)


def _whole(shape):
    nd = len(shape)
    return pl.BlockSpec(shape, lambda *_: (0,) * nd)


def _ffn_kernel(x_ref, g_ref, wg_ref, wu_ref, wo_ref, gf_ref, o_ref, h_scr, acc_scr, *, final):
    j = pl.program_id(1)

    @pl.when(j == 0)
    def _():
        h_scr[...] = _rms(x_ref[...], g_ref[...]).astype(BF16)
        acc_scr[...] = jnp.zeros_like(acc_scr)

    h = h_scr[...]
    gate = _dot(h, wg_ref[...])
    up = _dot(h, wu_ref[...])
    act = (jax.nn.silu(gate) * up).astype(BF16)
    acc_scr[...] += _dot(act, wo_ref[...])

    @pl.when(j == pl.num_programs(1) - 1)
    def _():
        y = x_ref[...] + acc_scr[...]
        if final:
            y = _rms(y, gf_ref[...])
        o_ref[...] = y


def _ffn(x, g, w_in, w_out, g_final, final):
    t, d = x.shape
    f = w_out.shape[0]
    tm = min(512, t)
    tf = f // 2 if (f // 2) % LANES == 0 else f
    nj = f // tf
    return pl.pallas_call(
        functools.partial(_ffn_kernel, final=final),
        grid=(t // tm, nj),
        in_specs=[
            pl.BlockSpec((tm, d), lambda i, j: (i, 0)),
            pl.BlockSpec((1, d), lambda i, j: (0, 0)),
            pl.BlockSpec((d, tf), lambda i, j: (0, j)),
            pl.BlockSpec((d, tf), lambda i, j: (0, nj + j)),
            pl.BlockSpec((tf, d), lambda i, j: (j, 0)),
            pl.BlockSpec((1, d), lambda i, j: (0, 0)),
        ],
        out_specs=pl.BlockSpec((tm, d), lambda i, j: (i, 0)),
        out_shape=jax.ShapeDtypeStruct((t, d), F32),
        scratch_shapes=[pltpu.VMEM((tm, d), BF16), pltpu.VMEM((tm, d), F32)],
        compiler_params=_params("parallel", "arbitrary"),
        name="ffn",
    )(x, g.reshape(1, d), w_in, w_in, w_out, g_final.reshape(1, d))


def _gelu(x):
    return 0.5 * x * (1.0 + lax.erf(x * np.float32(np.sqrt(0.5))))


def _layernorm(x, g, b):
    mu = jnp.mean(x, axis=-1, keepdims=True)
    xc = x - mu
    return xc * lax.rsqrt(jnp.mean(xc * xc, axis=-1, keepdims=True) + EPS) * g + b


def _cmlp_kernel(x_ref, g_ref, win_ref, lng_ref, lnb_ref, ws_ref, bs_ref, wout_ref, *rest, n_rows, write_v):
    if write_v:
        o_ref, v_ref, vb_scr, acc_scr = rest
    else:
        o_ref, vb_scr, acc_scr = rest
    tm = x_ref.shape[0]
    w = CMLP_WIDTH
    gw = w // CMLP_GROUPS
    c = CMLP_CHUNK
    x = x_ref[...]
    h = _rms(x, g_ref[...]).astype(BF16)
    v = _layernorm(_gelu(_dot(h, win_ref[:, w:])), lng_ref[...], lnb_ref[...])
    if write_v:
        v_ref[...] = v
    vb_scr[...] = v.astype(BF16)
    r_i = lax.broadcasted_iota(I32, (c, c), 0)
    c_i = lax.broadcasted_iota(I32, (c, c), 1)
    keep = jnp.where(c_i >= r_i - _mod(r_i, n_rows), jnp.where(c_i <= r_i, 1, 0), 0) > 0
    for g in range(CMLP_GROUPS):
        lo, hi = g * gw, (g + 1) * gw
        wg = jnp.where(keep, ws_ref[g], 0.0).astype(BF16)
        u = _gelu(_dot(h, win_ref[:, lo:hi]))
        bias = bs_ref[:, g:g + 1]
        mixed = jnp.concatenate(
            [_dot(wg, vb_scr[k * c:(k + 1) * c, lo:hi]) + bias for k in range(tm // c)], axis=0)
        contrib = _dot((u * mixed).astype(BF16), wout_ref[lo:hi, :])
        if g == 0:
            acc_scr[...] = contrib
        else:
            acc_scr[...] += contrib
    o_ref[...] = x + acc_scr[...]


def _cmlp(x, g, w_in, ln_g, ln_b, w_s, b_s, w_out, n_rows, write_v):
    t, d = x.shape
    w = CMLP_WIDTH
    c = CMLP_CHUNK
    tm = min(512, t)
    rep = c // n_rows
    ws_t = jnp.tile(w_s[:, :n_rows, :n_rows], (1, rep, rep))
    bs_t = jnp.tile(b_s[:, :n_rows].T, (rep, 1))
    out_shape = [jax.ShapeDtypeStruct((t, d), F32)]
    out_specs = [pl.BlockSpec((tm, d), lambda i: (i, 0))]
    if write_v:
        out_shape.append(jax.ShapeDtypeStruct((t, w), F32))
        out_specs.append(pl.BlockSpec((tm, w), lambda i: (i, 0)))
    res = pl.pallas_call(
        functools.partial(_cmlp_kernel, n_rows=n_rows, write_v=write_v),
        grid=(t // tm,),
        in_specs=[
            pl.BlockSpec((tm, d), lambda i: (i, 0)),
            _whole((1, d)), _whole((d, 2 * w)), _whole((1, w)), _whole((1, w)),
            _whole((CMLP_GROUPS, c, c)), _whole((c, CMLP_GROUPS)), _whole((w, d)),
        ],
        out_specs=out_specs,
        out_shape=out_shape,
        scratch_shapes=[pltpu.VMEM((tm, w), BF16), pltpu.VMEM((tm, d), F32)],
        compiler_params=_params("parallel"),
        name="cmlp",
    )(x, g.reshape(1, d), w_in, ln_g.reshape(1, w), ln_b.reshape(1, w), ws_t, bs_t, w_out)
    return res if write_v else (res[0], None)


def _attn_t_kernel(qi_ref, kj_ref, last_ref, qt_ref, k_ref, vt_ref, *rest, tq, tk, use_bias):
    if use_bias:
        bias_ref, x_ref, wot_ref, o_ref, m_scr, acc_scr, cat_scr = rest
    else:
        x_ref, wot_ref, o_ref, m_scr, acc_scr, cat_scr = rest
    p = pl.program_id(1)
    qi, kj = qi_ref[p], kj_ref[p]
    hd = qt_ref.shape[0]

    @pl.when(kj == 0)
    def _():
        m_scr[...] = jnp.full_like(m_scr, NEG_INF)
        acc_scr[...] = jnp.zeros_like(acc_scr)

    def step(mask):
        s_next = _dot(k_ref[0], qt_ref[0])
        pend = None
        for hh in range(hd):
            s = s_next
            if hh + 1 < hd:
                s_next = _dot(k_ref[hh + 1], qt_ref[hh + 1])
            if mask is not None:
                s = s + mask
            m_prev = m_scr[hh]
            m_new = jnp.maximum(m_prev, jnp.max(s, axis=0, keepdims=True))
            m_safe = jnp.where(m_new == NEG_INF, 0.0, m_new) if use_bias else m_new
            alpha = jnp.exp2(m_prev - m_safe)
            pe = jnp.exp2(s - m_safe).astype(BF16)
            m_scr[hh] = m_new
            if pend is not None:
                ph, pa, pp = pend
                acc_scr[ph] = pa * acc_scr[ph] + _dot(vt_ref[ph], pp)
            pend = (hh, alpha, pe)
        ph, pa, pp = pend
        acc_scr[ph] = pa * acc_scr[ph] + _dot(vt_ref[ph], pp)

    if use_bias:
        step(bias_ref[...].astype(F32))
    else:
        needs_mask = _div(kj * tk + tk - 1, CHUNK) > _div(qi * tq, CHUNK)

        @pl.when(needs_mask)
        def _():
            k_chunk = _div(kj * tk + lax.broadcasted_iota(I32, (tk, 1), 0), CHUNK)
            q_chunk = _div(qi * tq + lax.broadcasted_iota(I32, (1, tq), 1), CHUNK)
            step(jnp.where(k_chunk <= q_chunk, 0.0, NEG_INF))

        @pl.when(jnp.logical_not(needs_mask))
        def _():
            step(None)

    @pl.when(last_ref[p] == 1)
    def _():
        for hh in range(hd):
            a = acc_scr[hh]
            cat_scr[hh * HEAD_V:(hh + 1) * HEAD_V, :] = (a[:HEAD_V] / a[HEAD_V:HEAD_V + 1]).astype(BF16)
        o_ref[...] = x_ref[...] + _dot(wot_ref[...], cat_scr[...]).T


def _causal_pairs(nq, tq, tk):
    qi, kj, last = [], [], []
    for q in range(nq):
        nvis = -(-((q + 1) * tq) // tk)
        for k in range(nvis):
            qi.append(q); kj.append(k); last.append(int(k == nvis - 1))
    return (jnp.asarray(np.array(qi, np.int32)), jnp.asarray(np.array(kj, np.int32)),
            jnp.asarray(np.array(last, np.int32)))


def _attn_t(x, qt, kh, vt, bias, wot, batch, seq, name):
    t, d = x.shape
    hd, dk, _ = qt.shape
    tq, tk = min(512, seq), min(512, seq)
    nq, nk = seq // tq, seq // tk
    qi, kj, last = _causal_pairs(nq, tq, tk)
    use_bias = bias is not None
    in_specs = [
        pl.BlockSpec((hd, dk, tq), lambda b, p, qi, kj, la: (0, 0, b * nq + qi[p])),
        pl.BlockSpec((hd, tk, dk), lambda b, p, qi, kj, la: (0, b * nk + kj[p], 0)),
        pl.BlockSpec((hd, V_ROWS, tk), lambda b, p, qi, kj, la: (0, 0, b * nk + kj[p])),
    ]
    args = [qt, kh, vt]
    if use_bias:
        in_specs.append(pl.BlockSpec((tk, tq), lambda b, p, qi, kj, la: (kj[p], b * nq + qi[p])))
        args.append(bias)
    in_specs += [
        pl.BlockSpec((tq, d), lambda b, p, qi, kj, la: (b * nq + qi[p], 0)),
        pl.BlockSpec(wot.shape, lambda b, p, qi, kj, la: (0, 0)),
    ]
    args += [x, wot]
    grid_spec = pltpu.PrefetchScalarGridSpec(
        num_scalar_prefetch=3,
        grid=(batch, int(qi.shape[0])),
        in_specs=in_specs,
        out_specs=pl.BlockSpec((tq, d), lambda b, p, qi, kj, la: (b * nq + qi[p], 0)),
        scratch_shapes=[
            pltpu.VMEM((hd, 1, tq), F32), pltpu.VMEM((hd, V_ROWS, tq), F32), pltpu.VMEM((hd * HEAD_V, tq), BF16),
        ],
    )
    return pl.pallas_call(
        functools.partial(_attn_t_kernel, tq=tq, tk=tk, use_bias=use_bias),
        grid_spec=grid_spec,
        out_shape=jax.ShapeDtypeStruct((t, d), F32),
        compiler_params=_params("parallel", "arbitrary"),
        name=name,
    )(qi, kj, last, *args)


def _rope_tables(pos):
    half = MLA_ROPE // 2
    inv = ROPE_BASE ** (-jnp.arange(half, dtype=F32) / half)
    ang = pos.astype(F32)[:, None] * inv[None, :]
    cos, sin = jnp.cos(ang), jnp.sin(ang)
    cos_k = jnp.concatenate([cos, cos], axis=1)
    sin_k = jnp.concatenate([-sin, sin], axis=1)
    return jnp.tile(cos_k, (1, MLA_HEADS)), jnp.tile(sin_k, (1, MLA_HEADS)), cos_k, sin_k


def _swap_halves(w, group):
    shp = w.shape
    wr = w.reshape(shp[:-1] + (shp[-1] // group, 2, group // 2))
    return wr[..., ::-1, :].reshape(shp)


def _mla_weights(w_dq, w_uq, w_dkv, w_ukv, w_o):
    hd = MLA_HEADS
    cast = lambda a: a.astype(BF16)
    w_uq, w_dkv, w_ukv = cast(w_uq), cast(w_dkv), cast(w_ukv)
    wqn = w_uq[:, :, :MLA_NOPE].reshape(MLA_Q_LORA, hd * MLA_NOPE)
    wqr = w_uq[:, :, MLA_NOPE:].reshape(MLA_Q_LORA, hd * MLA_ROPE)
    wqs = _swap_halves(wqr, MLA_ROPE)
    wkc, wkr = w_dkv[:, :MLA_KV_LORA], w_dkv[:, MLA_KV_LORA:]
    wks = _swap_halves(wkr, MLA_ROPE)
    wuk = jnp.transpose(w_ukv[:, :, :MLA_NOPE], (1, 2, 0))
    zeros = jnp.zeros_like(wuk)
    even = jnp.concatenate([wuk, zeros], axis=1)
    odd = jnp.concatenate([zeros, wuk], axis=1)
    wuk2 = jnp.where((jnp.arange(hd) % 2 == 0)[:, None, None], even, odd)
    wuv = jnp.transpose(w_ukv[:, :, MLA_NOPE:], (1, 0, 2))
    d = w_dkv.shape[0]
    pad_r = lambda a: jnp.concatenate([a, jnp.zeros((d, LANES - MLA_ROPE), BF16)], axis=1)
    wukp = jnp.concatenate([jnp.zeros((MLA_KV_LORA, hd, MLA_ROPE), BF16), w_ukv[:, :, :MLA_NOPE],
                            jnp.zeros((MLA_KV_LORA, hd, LANES - MLA_ROPE - MLA_NOPE), BF16)], axis=2)
    return dict(wdq=cast(w_dq), wqn=wqn, wqr=wqr, wqs=wqs, wkc=wkc, wkr=wkr, wks=wks, wuk=wuk2, wuv=wuv, wo=cast(w_o),
                wqnt=wqn.T, wqrt=wqr.T, wqst=wqs.T, wkrp=pad_r(wkr), wksp=pad_r(wks),
                wukp=wukp.reshape(MLA_KV_LORA, hd * LANES), wuvt=wuv.transpose(0, 2, 1).reshape(hd * MLA_V, MLA_KV_LORA),
                wot=cast(w_o).T)


def _mla_proj_kernel(x_ref, gm_ref, wdq_ref, gq_ref, wqn_ref, wqr_ref, wqs_ref, wkc_ref, wkr_ref, wks_ref,
                     gkv_ref, wuk_ref, cq_ref, sq_ref, ck_ref, sk_ref,
                     ql_ref, qr_ref, ckv_ref, kr_ref, ckvb_ref, krb_ref):
    h = _rms(x_ref[...], gm_ref[...]).astype(BF16)
    cq = _rms(_dot(h, wdq_ref[...]), gq_ref[...]).astype(BF16)
    qn = _dot(cq, wqn_ref[...])
    qr = ((_dot(cq, wqr_ref[...]) * cq_ref[...] + _dot(cq, wqs_ref[...]) * sq_ref[...]) * MLA_SCALE).astype(BF16)
    for hh in range(MLA_HEADS):
        pair = qn[:, (hh // 2) * LANES:(hh // 2 + 1) * LANES].astype(BF16)
        ql_ref[hh] = (_dot(pair, wuk_ref[hh]) * MLA_SCALE).astype(BF16)
        qr_ref[hh] = qr[:, hh * MLA_ROPE:(hh + 1) * MLA_ROPE]
    ckv = _rms(_dot(h, wkc_ref[...]), gkv_ref[...])
    ckv_ref[...] = ckv
    ckvb_ref[...] = ckv.astype(BF16)
    kr = _dot(h, wkr_ref[...]) * ck_ref[...] + _dot(h, wks_ref[...]) * sk_ref[...]
    kr_ref[...] = kr
    krb_ref[...] = kr.astype(BF16)


def _mla_proj(x, g_mix, wts, g_q, g_kv, tables):
    t, d = x.shape
    tm = min(256, t)
    cos_q, sin_q, cos_k, sin_k = tables
    nrep = cos_q.shape[0] // tm
    tab = lambda wdt: pl.BlockSpec((tm, wdt), lambda i: (i % nrep, 0))
    hd = MLA_HEADS
    row = lambda wdt: pl.BlockSpec((tm, wdt), lambda i: (i, 0))
    hm = lambda wdt: pl.BlockSpec((hd, tm, wdt), lambda i: (0, i, 0))
    return pl.pallas_call(
        _mla_proj_kernel,
        grid=(t // tm,),
        in_specs=[
            row(d), _whole((1, d)), _whole(wts["wdq"].shape), _whole((1, MLA_Q_LORA)),
            _whole(wts["wqn"].shape), _whole(wts["wqr"].shape), _whole(wts["wqs"].shape),
            _whole(wts["wkc"].shape), _whole(wts["wkr"].shape), _whole(wts["wks"].shape),
            _whole((1, MLA_KV_LORA)), _whole(wts["wuk"].shape),
            tab(hd * MLA_ROPE), tab(hd * MLA_ROPE), tab(MLA_ROPE), tab(MLA_ROPE),
        ],
        out_specs=[hm(MLA_KV_LORA), hm(MLA_ROPE), row(MLA_KV_LORA), row(MLA_ROPE), row(MLA_KV_LORA), row(MLA_ROPE)],
        out_shape=[
            jax.ShapeDtypeStruct((hd, t, MLA_KV_LORA), BF16), jax.ShapeDtypeStruct((hd, t, MLA_ROPE), BF16),
            jax.ShapeDtypeStruct((t, MLA_KV_LORA), F32), jax.ShapeDtypeStruct((t, MLA_ROPE), F32),
            jax.ShapeDtypeStruct((t, MLA_KV_LORA), BF16), jax.ShapeDtypeStruct((t, MLA_ROPE), BF16),
        ],
        compiler_params=_params("parallel"),
        name="mla_proj",
    )(x, g_mix.reshape(1, d), wts["wdq"], g_q.reshape(1, -1), wts["wqn"], wts["wqr"], wts["wqs"],
      wts["wkc"], wts["wkr"], wts["wks"], g_kv.reshape(1, -1), wts["wuk"], cos_q, sin_q, cos_k, sin_k)


def _mla_proj_t_kernel(x_ref, gm_ref, wdq_ref, gq_ref, wqnt_ref, wqrt_ref, wqst_ref, wkc_ref, wkrp_ref, wksp_ref,
                       gkv_ref, wukp_ref, wuvt_ref, cqt_ref, sqt_ref, ckp_ref, skp_ref,
                       qt_ref, kh_ref, vt_ref, ckv_ref, kr_ref):
    tm = x_ref.shape[0]
    hd = MLA_HEADS
    qscale = MLA_SCALE * LOG2E
    h = _rms(x_ref[...], gm_ref[...]).astype(BF16)
    cq = _rms(_dot(h, wdq_ref[...]), gq_ref[...]).astype(BF16)
    qnt = (_dot_t(wqnt_ref[...], cq) * qscale).astype(BF16)
    qrt = ((_dot_t(wqrt_ref[...], cq) * cqt_ref[...] + _dot_t(wqst_ref[...], cq) * sqt_ref[...]) * qscale).astype(BF16)
    zeros = jnp.zeros((LANES - MLA_ROPE - MLA_NOPE, tm), BF16)
    ckv = _rms(_dot(h, wkc_ref[...]), gkv_ref[...])
    ckv_ref[...] = ckv
    cb = ckv.astype(BF16)
    krp = _dot(h, wkrp_ref[...]) * ckp_ref[...] + _dot(h, wksp_ref[...]) * skp_ref[...]
    kr_ref[...] = krp[:, :MLA_ROPE]
    kn = _dot(cb, wukp_ref[...])
    vt = _dot_t(wuvt_ref[...], cb).astype(BF16)
    ones = jnp.ones((BF16_ROWS, tm), BF16)
    for hh in range(hd):
        qt_ref[hh, 0:MLA_ROPE, :] = qrt[hh * MLA_ROPE:(hh + 1) * MLA_ROPE]
        qt_ref[hh, MLA_ROPE:MLA_ROPE + MLA_NOPE, :] = qnt[hh * MLA_NOPE:(hh + 1) * MLA_NOPE]
        qt_ref[hh, MLA_ROPE + MLA_NOPE:, :] = zeros
        kh_ref[hh] = (kn[:, hh * LANES:(hh + 1) * LANES] + krp).astype(BF16)
        vt_ref[hh, 0:MLA_V, :] = vt[hh * MLA_V:(hh + 1) * MLA_V]
        vt_ref[hh, MLA_V:, :] = ones


def _mla_proj_t(x, g_mix, wts, g_q, g_kv, tables, seq):
    t, d = x.shape
    tm = min(256, seq)
    hd = MLA_HEADS
    cos_q, sin_q, cos_k, sin_k = tables
    pad = lambda a: jnp.concatenate([a, jnp.zeros((seq, LANES - MLA_ROPE), F32)], axis=1)
    nrep = seq // tm
    row = lambda wdt: pl.BlockSpec((tm, wdt), lambda i: (i, 0))
    names = ("wdq", "wqnt", "wqrt", "wqst", "wkc", "wkrp", "wksp", "wukp", "wuvt")
    w = {n: wts[n] for n in names}
    return pl.pallas_call(
        _mla_proj_t_kernel,
        grid=(t // tm,),
        in_specs=[
            row(d), _whole((1, d)), _whole(w["wdq"].shape), _whole((1, MLA_Q_LORA)),
            _whole(w["wqnt"].shape), _whole(w["wqrt"].shape), _whole(w["wqst"].shape),
            _whole(w["wkc"].shape), _whole(w["wkrp"].shape), _whole(w["wksp"].shape),
            _whole((1, MLA_KV_LORA)), _whole(w["wukp"].shape), _whole(w["wuvt"].shape),
            pl.BlockSpec((hd * MLA_ROPE, tm), lambda i: (0, i % nrep)),
            pl.BlockSpec((hd * MLA_ROPE, tm), lambda i: (0, i % nrep)),
            pl.BlockSpec((tm, LANES), lambda i: (i % nrep, 0)),
            pl.BlockSpec((tm, LANES), lambda i: (i % nrep, 0)),
        ],
        out_specs=[
            pl.BlockSpec((hd, LANES, tm), lambda i: (0, 0, i)),
            pl.BlockSpec((hd, tm, LANES), lambda i: (0, i, 0)),
            pl.BlockSpec((hd, V_ROWS, tm), lambda i: (0, 0, i)),
            row(MLA_KV_LORA), row(MLA_ROPE),
        ],
        out_shape=[
            jax.ShapeDtypeStruct((hd, LANES, t), BF16), jax.ShapeDtypeStruct((hd, t, LANES), BF16),
            jax.ShapeDtypeStruct((hd, V_ROWS, t), BF16),
            jax.ShapeDtypeStruct((t, MLA_KV_LORA), F32), jax.ShapeDtypeStruct((t, MLA_ROPE), F32),
        ],
        compiler_params=_params("parallel"),
        name="mla_proj_t",
    )(x, g_mix.reshape(1, d), w["wdq"], g_q.reshape(1, -1), w["wqnt"], w["wqrt"], w["wqst"],
      w["wkc"], w["wkrp"], w["wksp"], g_kv.reshape(1, -1), w["wukp"], w["wuvt"],
      cos_q.T, sin_q.T, pad(cos_k), pad(sin_k))


def _mla_epilogue(o_lat, x_ref, wuv_ref, wo_ref, o_ref, cat_scr, tq):
    ob = o_lat.astype(BF16)
    for hh in range(MLA_HEADS):
        cat_scr[:, hh * MLA_V:(hh + 1) * MLA_V] = _dot(ob[hh * tq:(hh + 1) * tq], wuv_ref[hh]).astype(BF16)
    o_ref[...] = x_ref[...] + _dot(cat_scr[...], wo_ref[...])


def _mla_samp_kernel(ql_ref, qr_ref, cc_ref, cr_ref, cn_ref, rn_ref, x_ref, wuv_ref, wo_ref, o_ref, cat_scr, *, n_q):
    b = pl.program_id(0)
    hd = MLA_HEADS
    ql = ql_ref[...].reshape(hd * n_q, MLA_KV_LORA)
    qr = qr_ref[...].reshape(hd * n_q, MLA_ROPE)
    cc = cc_ref[0].astype(BF16)
    cr = cr_ref[0].astype(BF16)
    cn = cn_ref[...]
    s1 = _dot_t(ql, cc) + _dot_t(qr, cr)
    s2 = _dot_t(ql, cn) + _dot_t(qr, rn_ref[...])
    own = _div(lax.broadcasted_iota(I32, (1, LANES), 1), n_q) == _mod(b, LANES // n_q)
    s2 = jnp.where(own, s2, NEG_INF)
    m = jnp.maximum(jnp.max(s1, axis=1, keepdims=True), jnp.max(s2, axis=1, keepdims=True))
    p1 = jnp.exp(s1 - m)
    p2 = jnp.exp(s2 - m)
    l = jnp.sum(p1, axis=1, keepdims=True) + jnp.sum(p2, axis=1, keepdims=True)
    o_lat = (_dot(p1.astype(BF16), cc) + _dot(p2.astype(BF16), cn)) / l
    _mla_epilogue(o_lat, x_ref, wuv_ref, wo_ref, o_ref, cat_scr, n_q)


def _mla_attn_sample(x, ql, qr, ckvb, krb, cache_c, cache_r, wts, n_b, n_q):
    t, d = x.shape
    hd = MLA_HEADS
    past = cache_c.shape[1]
    per = LANES // n_q
    return pl.pallas_call(
        functools.partial(_mla_samp_kernel, n_q=n_q),
        grid=(n_b,),
        in_specs=[
            pl.BlockSpec((hd, n_q, MLA_KV_LORA), lambda b: (0, b, 0)),
            pl.BlockSpec((hd, n_q, MLA_ROPE), lambda b: (0, b, 0)),
            pl.BlockSpec((1, past, MLA_KV_LORA), lambda b: (b, 0, 0)),
            pl.BlockSpec((1, past, MLA_ROPE), lambda b: (b, 0, 0)),
            pl.BlockSpec((LANES, MLA_KV_LORA), lambda b: (b // per, 0)),
            pl.BlockSpec((LANES, MLA_ROPE), lambda b: (b // per, 0)),
            pl.BlockSpec((n_q, d), lambda b: (b, 0)),
            _whole(wts["wuv"].shape), _whole(wts["wo"].shape),
        ],
        out_specs=pl.BlockSpec((n_q, d), lambda b: (b, 0)),
        out_shape=jax.ShapeDtypeStruct((t, d), F32),
        scratch_shapes=[pltpu.VMEM((n_q, hd * MLA_V), BF16)],
        compiler_params=_params("parallel"),
        name="mla_attn_sample",
    )(ql, qr, cache_c, cache_r, ckvb, krb, x, wts["wuv"], wts["wo"])


def _dsa_proj_kernel(x_ref, gm_ref, wqkv_ref, wqi_ref, wkw_ref, gki_ref,
                     k_ref, v_ref, ki_ref, qb_ref, kb_ref, vb_ref, qib_ref, kib_ref, wi_ref):
    hdm = DSA_HEADS * DSA_HEAD_DIM
    h = _rms(x_ref[...], gm_ref[...]).astype(BF16)
    qkv = _dot(h, wqkv_ref[...])
    q = qkv[:, :hdm] * DSA_SCALE
    k = qkv[:, hdm:2 * hdm]
    v = qkv[:, 2 * hdm:]
    k_ref[...] = k
    v_ref[...] = v
    qi = _dot(h, wqi_ref[...]) * IDX_SCALE
    kw = _dot(h, wkw_ref[...])
    ki = _rms(kw[:, :IDX_DIM], gki_ref[...])
    ki_ref[...] = ki
    kib_ref[...] = ki.astype(BF16)
    wi_ref[...] = kw[:, IDX_DIM:IDX_DIM + IDX_HEADS] * (IDX_HEADS ** -0.5)
    qb_ref[...] = q.astype(BF16)
    kb_ref[...] = k.astype(BF16)
    vb_ref[...] = v.astype(BF16)
    for hh in range(IDX_HEADS):
        qib_ref[hh] = qi[:, hh * IDX_DIM:(hh + 1) * IDX_DIM].astype(BF16)


def _dsa_proj(x, g_mix, w_qkv, w_qidx, w_kw, g_kidx):
    t, d = x.shape
    tm = min(512, t)
    hdm = DSA_HEADS * DSA_HEAD_DIM
    row = lambda wdt: pl.BlockSpec((tm, wdt), lambda i: (i, 0))
    return pl.pallas_call(
        _dsa_proj_kernel,
        grid=(t // tm,),
        in_specs=[row(d), _whole((1, d)), _whole(w_qkv.shape), _whole(w_qidx.shape), _whole(w_kw.shape),
                  _whole((1, IDX_DIM))],
        out_specs=[row(hdm), row(hdm), row(IDX_DIM), row(hdm), row(hdm), row(hdm),
                   pl.BlockSpec((IDX_HEADS, tm, IDX_DIM), lambda i: (0, i, 0)), row(IDX_DIM), row(IDX_HEADS)],
        out_shape=[
            jax.ShapeDtypeStruct((t, hdm), F32), jax.ShapeDtypeStruct((t, hdm), F32),
            jax.ShapeDtypeStruct((t, IDX_DIM), F32), jax.ShapeDtypeStruct((t, hdm), BF16),
            jax.ShapeDtypeStruct((t, hdm), BF16), jax.ShapeDtypeStruct((t, hdm), BF16),
            jax.ShapeDtypeStruct((IDX_HEADS, t, IDX_DIM), BF16), jax.ShapeDtypeStruct((t, IDX_DIM), BF16),
            jax.ShapeDtypeStruct((t, IDX_HEADS), F32),
        ],
        compiler_params=_params("parallel"),
        name="dsa_proj",
    )(x, g_mix.reshape(1, d), w_qkv, w_qidx, w_kw, g_kidx.reshape(1, IDX_DIM))


def _dsa_proj_t_kernel(x_ref, gm_ref, wkv_ref, wqt_ref, wvt_ref, wqit_ref, wkw_ref, wwt_ref, gki_ref,
                       k_ref, v_ref, ki_ref, qt_ref, kh_ref, vt_ref, qit_ref, kib_ref, wit_ref):
    tm = x_ref.shape[0]
    hdm = DSA_HEADS * DSA_HEAD_DIM
    dh = DSA_HEAD_DIM
    h = _rms(x_ref[...], gm_ref[...]).astype(BF16)
    kv = _dot(h, wkv_ref[...])
    k = kv[:, :hdm]
    k_ref[...] = k
    v_ref[...] = kv[:, hdm:]
    kw = _dot(h, wkw_ref[...])
    ki = _rms(kw[:, :IDX_DIM], gki_ref[...])
    ki_ref[...] = ki
    kib_ref[...] = ki.astype(BF16)
    wit_ref[...] = _dot_t(wwt_ref[...], h)[:IDX_HEADS] * (IDX_HEADS ** -0.5)
    qt = (_dot_t(wqt_ref[...], h) * (DSA_SCALE * LOG2E)).astype(BF16)
    vt = _dot_t(wvt_ref[...], h).astype(BF16)
    qit = (_dot_t(wqit_ref[...], h) * IDX_SCALE).astype(BF16)
    ones = jnp.ones((BF16_ROWS, tm), BF16)
    for hh in range(DSA_HEADS):
        qt_ref[hh] = qt[hh * dh:(hh + 1) * dh]
        kh_ref[hh] = k[:, hh * dh:(hh + 1) * dh].astype(BF16)
        vt_ref[hh, 0:dh, :] = vt[hh * dh:(hh + 1) * dh]
        vt_ref[hh, dh:, :] = ones
    for hh in range(IDX_HEADS):
        qit_ref[hh] = qit[hh * IDX_DIM:(hh + 1) * IDX_DIM]


def _dsa_proj_t(x, g_mix, w, g_kidx):
    t, d = x.shape
    tm = min(256, t)
    hd, dh = DSA_HEADS, DSA_HEAD_DIM
    hdm = hd * dh
    row = lambda wdt: pl.BlockSpec((tm, wdt), lambda i: (i, 0))
    names = ("wkv", "wqt", "wvt", "wqit", "wkw", "wwt")
    return pl.pallas_call(
        _dsa_proj_t_kernel,
        grid=(t // tm,),
        in_specs=[row(d), _whole((1, d))] + [_whole(w[n].shape) for n in names] + [_whole((1, IDX_DIM))],
        out_specs=[
            row(hdm), row(hdm), row(IDX_DIM),
            pl.BlockSpec((hd, dh, tm), lambda i: (0, 0, i)),
            pl.BlockSpec((hd, tm, dh), lambda i: (0, i, 0)),
            pl.BlockSpec((hd, V_ROWS, tm), lambda i: (0, 0, i)),
            pl.BlockSpec((IDX_HEADS, IDX_DIM, tm), lambda i: (0, 0, i)),
            row(IDX_DIM),
            pl.BlockSpec((IDX_HEADS, tm), lambda i: (0, i)),
        ],
        out_shape=[
            jax.ShapeDtypeStruct((t, hdm), F32), jax.ShapeDtypeStruct((t, hdm), F32),
            jax.ShapeDtypeStruct((t, IDX_DIM), F32),
            jax.ShapeDtypeStruct((hd, dh, t), BF16), jax.ShapeDtypeStruct((hd, t, dh), BF16),
            jax.ShapeDtypeStruct((hd, V_ROWS, t), BF16),
            jax.ShapeDtypeStruct((IDX_HEADS, IDX_DIM, t), BF16), jax.ShapeDtypeStruct((t, IDX_DIM), BF16),
            jax.ShapeDtypeStruct((IDX_HEADS, t), F32),
        ],
        compiler_params=_params("parallel"),
        name="dsa_proj_t",
    )(x, g_mix.reshape(1, d), *[w[n] for n in names], g_kidx.reshape(1, IDX_DIM))


def _sort_key(score):
    bits = lax.bitcast_convert_type(score, I32)
    return jnp.where(bits < 0, bits ^ jnp.int32(0x7FFFFFFF), bits)


def _kth_largest_key(count, topk):
    c0 = count(lambda key, idx: jnp.where(key >= 0, 1, 0))
    t0 = jnp.where(c0 >= topk, jnp.int32(0), jnp.int32(INT_MIN))

    def bit_body(i, t):
        cand = t + lax.shift_left(jnp.int32(1), jnp.int32(30) - i)
        c = count(lambda key, idx: jnp.where(key >= cand, 1, 0))
        return jnp.where(c >= topk, cand, t)

    return lax.fori_loop(0, 31, bit_body, t0)


def _tie_cutoff(count, thr, rem, nbits):
    def bit_body(i, j):
        cand = j + lax.shift_left(jnp.int32(1), jnp.int32(nbits - 1) - i)
        c = count(lambda key, idx: jnp.where(key == thr, jnp.where(idx < cand, 1, 0), 0))
        return jnp.where(c < rem, cand, j)

    return lax.fori_loop(0, nbits, bit_body, jnp.zeros_like(thr))


def _select(key, idx, thr, cut):
    chosen = jnp.where(key > thr, 1, jnp.where(key == thr, jnp.where(idx <= cut, 1, 0), 0))
    return jnp.where(key > KEY_NEG_INF, chosen, 0) > 0


def _dsa_select_kernel(qit_ref, wit_ref, ki_ref, tri_ref, bias_ref, key_scr, *, tq, tk, topk):
    qb = pl.program_id(1)
    seq = ki_ref.shape[0]
    nk = seq // tk
    nvis = _div((qb + 1) * tq + tk - 1, tk)
    q_chunk = _div(qb * tq + lax.broadcasted_iota(I32, (1, tq), 1), CHUNK)
    w = wit_ref[...]
    row_idx = lax.broadcasted_iota(I32, (tk, 1), 0)

    def score_body(j, carry):
        off = pl.multiple_of(j * tk, tk)
        kb = ki_ref[pl.ds(off, tk), :]
        acc = jnp.zeros((tk, tq), F32)
        for hh in range(IDX_HEADS):
            acc = acc + w[hh:hh + 1, :] * jnp.maximum(_dot(kb, qit_ref[hh]), 0.0)
        vis = _div(off + row_idx, CHUNK) <= q_chunk
        key_scr[pl.ds(off, tk), :] = _sort_key(jnp.where(vis, acc, NEG_INF))
        return carry

    lax.fori_loop(0, nvis, score_body, 0)

    def count(hit):
        def body(j, c):
            off = pl.multiple_of(j * tk, tk)
            hits = hit(key_scr[pl.ds(off, tk), :], off + row_idx)
            return c + jnp.sum(hits.reshape(tk // 8, 8, tq), axis=0)
        c = lax.fori_loop(0, nvis, body, jnp.zeros((8, tq), I32))
        return jnp.sum(c, axis=0, keepdims=True)

    thr = _kth_largest_key(count, topk)
    rem = (topk - count(lambda key, idx: jnp.where(key > thr, 1, 0))).astype(F32)
    tri = tri_ref[...]

    def out_body(j, seen):
        off = pl.multiple_of(j * tk, tk)
        key = key_scr[pl.ds(off, tk), :]
        tie = jnp.where(key == thr, 1.0, 0.0)
        rank = seen + _dot(tri, tie.astype(BF16))
        keep = jnp.where(key > thr, 1.0, jnp.where(rank <= rem, tie, 0.0))
        keep = jnp.where(key > KEY_NEG_INF, keep, 0.0)
        bias_ref[pl.ds(off, tk), :] = jnp.where(keep > 0.0, 0.0, NEG_INF).astype(BF16)
        return rank[tk - 1:tk, :]

    lax.fori_loop(0, nvis, out_body, jnp.zeros((1, tq), F32))

    def fill_body(j, carry):
        off = pl.multiple_of(j * tk, tk)
        bias_ref[pl.ds(off, tk), :] = jnp.full((tk, tq), NEG_INF, BF16)
        return carry

    lax.fori_loop(nvis, nk, fill_body, 0)


def _dsa_select(qit, wit, kib, batch, seq, topk):
    tq, tk = min(256, seq), min(512, seq)
    nq = seq // tq
    t = batch * seq
    return pl.pallas_call(
        functools.partial(_dsa_select_kernel, tq=tq, tk=tk, topk=topk),
        grid=(batch, nq),
        in_specs=[
            pl.BlockSpec((IDX_HEADS, IDX_DIM, tq), lambda b, q: (0, 0, b * nq + q)),
            pl.BlockSpec((IDX_HEADS, tq), lambda b, q: (0, b * nq + q)),
            pl.BlockSpec((seq, IDX_DIM), lambda b, q: (b, 0)),
            _whole((tk, tk)),
        ],
        out_specs=pl.BlockSpec((seq, tq), lambda b, q: (0, b * nq + q)),
        out_shape=jax.ShapeDtypeStruct((seq, t), BF16),
        scratch_shapes=[pltpu.VMEM((seq, tq), I32)],
        compiler_params=_params("parallel", "arbitrary"),
        name="dsa_select",
    )(qit, wit, kib, jnp.tril(jnp.ones((tk, tk), BF16)))


def _dsa_samp_kernel(q_ref, qi_ref, wi_ref, kc_ref, vc_ref, kic_ref, kn_ref, vn_ref, kin_ref, x_ref, wo_ref, o_ref,
                     *, n_q, topk):
    b = pl.program_id(0)
    hd, dh = DSA_HEADS, DSA_HEAD_DIM
    past = kc_ref.shape[1]
    n_keys = past + LANES
    own = _div(lax.broadcasted_iota(I32, (1, LANES), 1), n_q) == _mod(b, LANES // n_q)

    qi = qi_ref[...].reshape(IDX_HEADS * n_q, IDX_DIM)
    lg1 = jnp.maximum(_dot_t(qi, kic_ref[0].astype(BF16)), 0.0)
    lg2 = jnp.maximum(_dot_t(qi, kin_ref[...]), 0.0)
    w = wi_ref[...]
    sc1 = jnp.zeros((n_q, past), F32)
    sc2 = jnp.zeros((n_q, LANES), F32)
    for hh in range(IDX_HEADS):
        sc1 = sc1 + w[:, hh:hh + 1] * lg1[hh * n_q:(hh + 1) * n_q]
        sc2 = sc2 + w[:, hh:hh + 1] * lg2[hh * n_q:(hh + 1) * n_q]
    key = _sort_key(jnp.concatenate([sc1, jnp.where(own, sc2, NEG_INF)], axis=1))
    idx = lax.broadcasted_iota(I32, (1, n_keys), 1)

    count = lambda hit: jnp.sum(hit(key, idx), axis=1, keepdims=True)
    thr = _kth_largest_key(count, topk)
    rem = topk - count(lambda k_, i_: jnp.where(k_ > thr, 1, 0))
    cut = _tie_cutoff(count, thr, rem, int(n_keys - 1).bit_length())
    bias = jnp.where(_select(key, idx, thr, cut), 0.0, NEG_INF)
    bias = jnp.concatenate([bias] * hd, axis=0)

    lane_head = _div(lax.broadcasted_iota(I32, (1, hd * dh), 1), dh)
    qf = q_ref[...].astype(F32)
    qbd = jnp.concatenate([jnp.where(lane_head == hh, qf, 0.0) for hh in range(hd)], axis=0).astype(BF16)
    kc = kc_ref[0].astype(BF16)
    vc = vc_ref[0].astype(BF16)
    s1 = _dot_t(qbd, kc) + bias[:, :past]
    s2 = _dot_t(qbd, kn_ref[...]) + bias[:, past:]
    m = jnp.maximum(jnp.max(s1, axis=1, keepdims=True), jnp.max(s2, axis=1, keepdims=True))
    p1 = jnp.exp(s1 - m)
    p2 = jnp.exp(s2 - m)
    l = jnp.sum(p1, axis=1, keepdims=True) + jnp.sum(p2, axis=1, keepdims=True)
    o_all = (_dot(p1.astype(BF16), vc) + _dot(p2.astype(BF16), vn_ref[...])) / l
    out = jnp.zeros((n_q, hd * dh), F32)
    for hh in range(hd):
        out = out + jnp.where(lane_head == hh, o_all[hh * n_q:(hh + 1) * n_q], 0.0)
    o_ref[...] = x_ref[...] + _dot(out.astype(BF16), wo_ref[...])


def _dsa_attn_sample(x, qb, qib, wi, kb, vb, kib, cache_k, cache_v, cache_ki, w_o, n_b, n_q, topk):
    t, d = x.shape
    hdm = DSA_HEADS * DSA_HEAD_DIM
    past = cache_k.shape[1]
    per = LANES // n_q
    return pl.pallas_call(
        functools.partial(_dsa_samp_kernel, n_q=n_q, topk=topk),
        grid=(n_b,),
        in_specs=[
            pl.BlockSpec((n_q, hdm), lambda b: (b, 0)),
            pl.BlockSpec((IDX_HEADS, n_q, IDX_DIM), lambda b: (0, b, 0)),
            pl.BlockSpec((n_q, IDX_HEADS), lambda b: (b, 0)),
            pl.BlockSpec((1, past, hdm), lambda b: (b, 0, 0)),
            pl.BlockSpec((1, past, hdm), lambda b: (b, 0, 0)),
            pl.BlockSpec((1, past, IDX_DIM), lambda b: (b, 0, 0)),
            pl.BlockSpec((LANES, hdm), lambda b: (b // per, 0)),
            pl.BlockSpec((LANES, hdm), lambda b: (b // per, 0)),
            pl.BlockSpec((LANES, IDX_DIM), lambda b: (b // per, 0)),
            pl.BlockSpec((n_q, d), lambda b: (b, 0)),
            _whole(w_o.shape),
        ],
        out_specs=pl.BlockSpec((n_q, d), lambda b: (b, 0)),
        out_shape=jax.ShapeDtypeStruct((t, d), F32),
        compiler_params=_params("parallel"),
        name="dsa_attn_sample",
    )(qb, qib, wi, cache_k.reshape(n_b, past, hdm), cache_v.reshape(n_b, past, hdm), cache_ki, kb, vb, kib, x, w_o)


def kernel(x_prompt, x_sample, cache_mla_ckv, cache_mla_krope, cache_dsa_k, cache_dsa_v, cache_dsa_kidx, norm_mix, norm_ffn, norm_final, mla_w_dq, mla_g_q, mla_w_uq, mla_w_dkv, mla_g_kv, mla_w_ukv, mla_w_o, cmlp_w_in, cmlp_ln_g, cmlp_ln_b, cmlp_w_s, cmlp_b_s, cmlp_w_out, dsa_w_qkv, dsa_w_o, dsa_w_qidx, dsa_w_kidx, dsa_g_kidx, dsa_w_widx, ffn_w_in, ffn_w_out):
    batch, seq, d = x_prompt.shape
    n_b, n_q, _ = x_sample.shape
    past = cache_mla_ckv.shape[2]
    depth = norm_mix.shape[0]
    xp = x_prompt.reshape(batch * seq, d)
    xs = x_sample.reshape(n_b * n_q, d)
    tab_p = _rope_tables(jnp.arange(seq))
    tab_s = tuple(jnp.tile(a, (n_b, 1)) for a in _rope_tables(past + jnp.arange(n_q)))
    cast = lambda a: a.astype(BF16)
    outs = {k: [] for k in ("ckv_p", "kr_p", "ckv_s", "kr_s", "cv_s", "dk_p", "dv_p", "di_p", "dk_s", "dv_s", "di_s")}
    for i in range(depth):
        kind, j = i % 3, i // 3
        if kind == 0:
            wts = _mla_weights(mla_w_dq[j], mla_w_uq[j], mla_w_dkv[j], mla_w_ukv[j], mla_w_o[j])
            qt, kh, vt, ckv, kr = _mla_proj_t(xp, norm_mix[i], wts, mla_g_q[j], mla_g_kv[j], tab_p, seq)
            xp = _attn_t(xp, qt, kh, vt, None, wts["wot"], batch, seq, "mla_attn_prompt")
            outs["ckv_p"].append(ckv.reshape(batch, seq, -1)); outs["kr_p"].append(kr.reshape(batch, seq, -1))
            ql, qr, ckv, kr, ckvb, krb = _mla_proj(xs, norm_mix[i], wts, mla_g_q[j], mla_g_kv[j], tab_s)
            xs = _mla_attn_sample(xs, ql, qr, ckvb, krb, cache_mla_ckv[j], cache_mla_krope[j], wts, n_b, n_q)
            outs["ckv_s"].append(ckv.reshape(n_b, n_q, -1)); outs["kr_s"].append(kr.reshape(n_b, n_q, -1))
        elif kind == 1:
            w_in, w_out = cast(cmlp_w_in[j]), cast(cmlp_w_out[j])
            xp, _ = _cmlp(xp, norm_mix[i], w_in, cmlp_ln_g[j], cmlp_ln_b[j], cmlp_w_s[j], cmlp_b_s[j], w_out,
                          min(seq, CMLP_CHUNK), False)
            xs, v_s = _cmlp(xs, norm_mix[i], w_in, cmlp_ln_g[j], cmlp_ln_b[j], cmlp_w_s[j], cmlp_b_s[j], w_out,
                            min(n_q, CMLP_CHUNK), True)
            outs["cv_s"].append(v_s.reshape(n_b, n_q, -1))
        else:
            hdm = DSA_HEADS * DSA_HEAD_DIM
            w_qkv, w_qidx, w_o = cast(dsa_w_qkv[j]), cast(dsa_w_qidx[j]), cast(dsa_w_o[j])
            w_kidx, w_widx = cast(dsa_w_kidx[j]), cast(dsa_w_widx[j])
            zpad = lambda n: jnp.zeros((d, n), BF16)
            wt = dict(wkv=w_qkv[:, hdm:], wqt=w_qkv[:, :hdm].T, wvt=w_qkv[:, 2 * hdm:].T, wqit=w_qidx.T,
                      wkw=jnp.concatenate([w_kidx, zpad(LANES - IDX_DIM)], axis=1),
                      wwt=jnp.concatenate([w_widx, zpad(BF16_ROWS - IDX_HEADS)], axis=1).T)
            hshape = (DSA_HEADS, DSA_HEAD_DIM)
            k, v, ki, qt, kh, vt, qit, kib, wit = _dsa_proj_t(xp, norm_mix[i], wt, dsa_g_kidx[j])
            bias = _dsa_select(qit, wit, kib, batch, seq, min(TOPK_MAX, seq // 4))
            xp = _attn_t(xp, qt, kh, vt, bias, w_o.T, batch, seq, "dsa_attn_prompt")
            outs["dk_p"].append(k.reshape((batch, seq) + hshape)); outs["dv_p"].append(v.reshape((batch, seq) + hshape))
            outs["di_p"].append(ki.reshape(batch, seq, -1))
            w_kw = jnp.concatenate([w_kidx, w_widx, zpad(LANES - IDX_DIM - IDX_HEADS)], axis=1)
            k, v, ki, qb, kb, vb, qib, kib, wi = _dsa_proj(xs, norm_mix[i], w_qkv, w_qidx, w_kw, dsa_g_kidx[j])
            xs = _dsa_attn_sample(xs, qb, qib, wi, kb, vb, kib, cache_dsa_k[j], cache_dsa_v[j], cache_dsa_kidx[j], w_o,
                                  n_b, n_q, min(TOPK_MAX, (past + n_q) // 4))
            outs["dk_s"].append(k.reshape((n_b, n_q) + hshape)); outs["dv_s"].append(v.reshape((n_b, n_q) + hshape))
            outs["di_s"].append(ki.reshape(n_b, n_q, -1))
        w_in, w_out = cast(ffn_w_in[i]), cast(ffn_w_out[i])
        final = i == depth - 1
        xp = _ffn(xp, norm_ffn[i], w_in, w_out, norm_final, final)
        xs = _ffn(xs, norm_ffn[i], w_in, w_out, norm_final, final)
    st = lambda name: jnp.stack(outs[name])
    return (xp.reshape(batch, seq, d), xs.reshape(n_b, n_q, d),
            st("ckv_p"), st("kr_p"), st("ckv_s"), st("kr_s"), st("cv_s"),
            st("dk_p"), st("dv_p"), st("di_p"), st("dk_s"), st("dv_s"), st("di_s"))
```

```python
import functools

import numpy as np
import jax
import jax.numpy as jnp
from jax import lax
from jax.experimental import pallas as pl
from jax.experimental.pallas import tpu as pltpu

F32, BF16, I32 = jnp.float32, jnp.bfloat16, jnp.int32

CHUNK = 64
EPS = 1e-6
MLA_HEADS, MLA_Q_LORA, MLA_KV_LORA, MLA_NOPE, MLA_ROPE, MLA_V = 16, 512, 256, 64, 32, 64
ROPE_BASE = 10000.0
MLA_SCALE = (MLA_NOPE + MLA_ROPE) ** -0.5
CMLP_CHUNK, CMLP_WIDTH, CMLP_GROUPS = 128, 2048, 8
DSA_HEADS, DSA_HEAD_DIM = 16, 64
DSA_SCALE = DSA_HEAD_DIM ** -0.5
IDX_HEADS, IDX_DIM = 8, 64
IDX_SCALE = IDX_DIM ** -0.5
TOPK_MAX = 256

LANES = 128
MXU_TILE = 256
BF16_ROWS = 16
VMEM_LIMIT = 52 * 1024 * 1024
NEG_INF = float("-inf")
INT_MIN = -2 ** 31
KEY_NEG_INF = -2139095041
LOG2E = float(np.log2(np.e))
HEAD_V = 64
V_ROWS = HEAD_V + BF16_ROWS


def _dot(a, b):
    return jnp.dot(a, b, preferred_element_type=F32)


def _dot_t(a, b):
    return lax.dot_general(a, b, (((1,), (1,)), ((), ())), preferred_element_type=F32)


def _rms(x, g):
    return x * lax.rsqrt(jnp.mean(x * x, axis=-1, keepdims=True) + EPS) * g


def _log2(n):
    assert n > 0 and n & (n - 1) == 0, n
    return n.bit_length() - 1


def _div(x, n):
    return lax.shift_right_logical(x, jnp.int32(_log2(n)))


def _mod(x, n):
    assert n & (n - 1) == 0, n
    return x & (n - 1)


def _params(*sem):
    return pltpu.CompilerParams(dimension_semantics=sem, vmem_limit_bytes=VMEM_LIMIT)


def _whole(shape):
    nd = len(shape)
    return pl.BlockSpec(shape, lambda *_: (0,) * nd)


def _ffn_kernel(x_ref, g_ref, wg_ref, wu_ref, wo_ref, gf_ref, o_ref, act_scr, *, final):
    x = x_ref[...]
    h = _rms(x, g_ref[...]).astype(BF16)
    f = wo_ref.shape[0]
    chunk = MXU_TILE if f % MXU_TILE == 0 else f
    nc = f // chunk
    nxt = (_dot(h, wg_ref[:, 0:chunk]), _dot(h, wu_ref[:, 0:chunk]))
    for c in range(nc):
        gate, up = nxt
        if c + 1 < nc:
            lo = (c + 1) * chunk
            nxt = (_dot(h, wg_ref[:, lo:lo + chunk]), _dot(h, wu_ref[:, lo:lo + chunk]))
        act_scr[:, c * chunk:(c + 1) * chunk] = (jax.nn.silu(gate) * up).astype(BF16)
    y = x + _dot(act_scr[...], wo_ref[...])
    if final:
        y = _rms(y, gf_ref[...])
    o_ref[...] = y


def _ffn(x, g, w_in, w_out, g_final, final):
    t, d = x.shape
    f = w_out.shape[0]
    tm = min(512, t)
    once = pl.Buffered(1)
    return pl.pallas_call(
        functools.partial(_ffn_kernel, final=final),
        grid=(t // tm,),
        in_specs=[
            pl.BlockSpec((tm, d), lambda i: (i, 0)),
            _whole((1, d)),
            pl.BlockSpec((d, f), lambda i: (0, 0), pipeline_mode=once),
            pl.BlockSpec((d, f), lambda i: (0, 1), pipeline_mode=once),
            pl.BlockSpec((f, d), lambda i: (0, 0), pipeline_mode=once),
            _whole((1, d)),
        ],
        out_specs=pl.BlockSpec((tm, d), lambda i: (i, 0)),
        out_shape=jax.ShapeDtypeStruct((t, d), F32),
        scratch_shapes=[pltpu.VMEM((tm, f), BF16)],
        compiler_params=_params("parallel"),
        name="ffn",
    )(x, g.reshape(1, d), w_in, w_in, w_out, g_final.reshape(1, d))


def _gelu(x):
    return 0.5 * x * (1.0 + lax.erf(x * np.float32(np.sqrt(0.5))))


def _layernorm(x, g, b):
    mu = jnp.mean(x, axis=-1, keepdims=True)
    xc = x - mu
    return xc * lax.rsqrt(jnp.mean(xc * xc, axis=-1, keepdims=True) + EPS) * g + b


def _cmlp_kernel(x_ref, g_ref, win_ref, lng_ref, lnb_ref, ws_ref, bs_ref, wout_ref, *rest, n_rows, write_v):
    if write_v:
        o_ref, v_ref, vb_scr, gated_scr = rest
    else:
        o_ref, vb_scr, gated_scr = rest
    tm = x_ref.shape[0]
    w = CMLP_WIDTH
    gw = w // CMLP_GROUPS
    c = CMLP_CHUNK
    x = x_ref[...]
    h = _rms(x, g_ref[...]).astype(BF16)
    v = _layernorm(_gelu(_dot(h, win_ref[:, w:])), lng_ref[...], lnb_ref[...])
    if write_v:
        v_ref[...] = v
    vb_scr[...] = v.astype(BF16)
    r_i = lax.broadcasted_iota(I32, (c, c), 0)
    c_i = lax.broadcasted_iota(I32, (c, c), 1)
    keep = jnp.where(c_i >= r_i - _mod(r_i, n_rows), jnp.where(c_i <= r_i, 1, 0), 0) > 0
    u_next = _dot(h, win_ref[:, 0:gw])
    for g in range(CMLP_GROUPS):
        lo, hi = g * gw, (g + 1) * gw
        u_raw = u_next
        if g + 1 < CMLP_GROUPS:
            u_next = _dot(h, win_ref[:, hi:hi + gw])
        wg = jnp.where(keep, ws_ref[g], 0.0).astype(BF16)
        bias = bs_ref[:, g:g + 1]
        mixed = jnp.concatenate(
            [_dot(wg, vb_scr[k * c:(k + 1) * c, lo:hi]) + bias for k in range(tm // c)], axis=0)
        gated_scr[:, lo:hi] = (_gelu(u_raw) * mixed).astype(BF16)
    o_ref[...] = x + _dot(gated_scr[...], wout_ref[...])


def _cmlp(x, g, w_in, ln_g, ln_b, w_s, b_s, w_out, n_rows, write_v):
    t, d = x.shape
    w = CMLP_WIDTH
    c = CMLP_CHUNK
    tm = min(512, t)
    rep = c // n_rows
    ws_t = jnp.tile(w_s[:, :n_rows, :n_rows], (1, rep, rep))
    bs_t = jnp.tile(b_s[:, :n_rows].T, (rep, 1))
    out_shape = [jax.ShapeDtypeStruct((t, d), F32)]
    out_specs = [pl.BlockSpec((tm, d), lambda i: (i, 0))]
    if write_v:
        out_shape.append(jax.ShapeDtypeStruct((t, w), F32))
        out_specs.append(pl.BlockSpec((tm, w), lambda i: (i, 0)))
    res = pl.pallas_call(
        functools.partial(_cmlp_kernel, n_rows=n_rows, write_v=write_v),
        grid=(t // tm,),
        in_specs=[
            pl.BlockSpec((tm, d), lambda i: (i, 0)),
            _whole((1, d)), _whole((d, 2 * w)), _whole((1, w)), _whole((1, w)),
            _whole((CMLP_GROUPS, c, c)), _whole((c, CMLP_GROUPS)), _whole((w, d)),
        ],
        out_specs=out_specs,
        out_shape=out_shape,
        scratch_shapes=[pltpu.VMEM((tm, w), BF16), pltpu.VMEM((tm, w), BF16)],
        compiler_params=_params("parallel"),
        name="cmlp",
    )(x, g.reshape(1, d), w_in, ln_g.reshape(1, w), ln_b.reshape(1, w), ws_t, bs_t, w_out)
    return res if write_v else (res[0], None)


def _attn_t_kernel(qi_ref, kj_ref, last_ref, qt_ref, k_ref, vt_ref, *rest, tq, tk, use_bias):
    if use_bias:
        bias_ref, x_ref, wot_ref, o_ref, m_scr, acc_scr, cat_scr = rest
    else:
        x_ref, wot_ref, o_ref, m_scr, acc_scr, cat_scr = rest
    p = pl.program_id(1)
    qi, kj = qi_ref[p], kj_ref[p]
    hd = qt_ref.shape[0]

    @pl.when(kj == 0)
    def _():
        m_scr[...] = jnp.full_like(m_scr, NEG_INF)
        acc_scr[...] = jnp.zeros_like(acc_scr)

    def step(mask):
        s_next = _dot(k_ref[0], qt_ref[0])
        pend = None
        for hh in range(hd):
            s = s_next
            if hh + 1 < hd:
                s_next = _dot(k_ref[hh + 1], qt_ref[hh + 1])
            if mask is not None:
                s = s + mask
            m_prev = m_scr[hh]
            m_new = jnp.maximum(m_prev, jnp.max(s, axis=0, keepdims=True))
            m_safe = jnp.where(m_new == NEG_INF, 0.0, m_new) if use_bias else m_new
            alpha = jnp.exp2(m_prev - m_safe)
            pe = jnp.exp2(s - m_safe).astype(BF16)
            m_scr[hh] = m_new
            if pend is not None:
                ph, pa, pp = pend
                acc_scr[ph] = pa * acc_scr[ph] + _dot(vt_ref[ph], pp)
            pend = (hh, alpha, pe)
        ph, pa, pp = pend
        acc_scr[ph] = pa * acc_scr[ph] + _dot(vt_ref[ph], pp)

    if use_bias:
        step(bias_ref[...].astype(F32))
    else:
        needs_mask = _div(kj * tk + tk - 1, CHUNK) > _div(qi * tq, CHUNK)

        @pl.when(needs_mask)
        def _():
            k_chunk = _div(kj * tk + lax.broadcasted_iota(I32, (tk, 1), 0), CHUNK)
            q_chunk = _div(qi * tq + lax.broadcasted_iota(I32, (1, tq), 1), CHUNK)
            step(jnp.where(k_chunk <= q_chunk, 0.0, NEG_INF))

        @pl.when(jnp.logical_not(needs_mask))
        def _():
            step(None)

    @pl.when(last_ref[p] == 1)
    def _():
        for hh in range(hd):
            a = acc_scr[hh]
            cat_scr[hh * HEAD_V:(hh + 1) * HEAD_V, :] = (a[:HEAD_V] / a[HEAD_V:HEAD_V + 1]).astype(BF16)
        o_ref[...] = x_ref[...] + _dot(wot_ref[...], cat_scr[...]).T


def _causal_pairs(nq, tq, tk):
    qi, kj, last = [], [], []
    for q in range(nq):
        nvis = -(-((q + 1) * tq) // tk)
        for k in range(nvis):
            qi.append(q); kj.append(k); last.append(int(k == nvis - 1))
    return (jnp.asarray(np.array(qi, np.int32)), jnp.asarray(np.array(kj, np.int32)),
            jnp.asarray(np.array(last, np.int32)))


def _attn_t(x, qt, kh, vt, bias, wot, batch, seq, tk, name):
    t, d = x.shape
    hd, dk, _ = qt.shape
    tq, tk = min(512, seq), min(tk, seq)
    nq, nk = seq // tq, seq // tk
    qi, kj, last = _causal_pairs(nq, tq, tk)
    use_bias = bias is not None
    in_specs = [
        pl.BlockSpec((hd, dk, tq), lambda b, p, qi, kj, la: (0, 0, b * nq + qi[p])),
        pl.BlockSpec((hd, tk, dk), lambda b, p, qi, kj, la: (0, b * nk + kj[p], 0)),
        pl.BlockSpec((hd, V_ROWS, tk), lambda b, p, qi, kj, la: (0, 0, b * nk + kj[p])),
    ]
    args = [qt, kh, vt]
    if use_bias:
        in_specs.append(pl.BlockSpec((tk, tq), lambda b, p, qi, kj, la: (kj[p], b * nq + qi[p])))
        args.append(bias)
    in_specs += [
        pl.BlockSpec((tq, d), lambda b, p, qi, kj, la: (b * nq + qi[p], 0)),
        pl.BlockSpec(wot.shape, lambda b, p, qi, kj, la: (0, 0)),
    ]
    args += [x, wot]
    grid_spec = pltpu.PrefetchScalarGridSpec(
        num_scalar_prefetch=3,
        grid=(batch, int(qi.shape[0])),
        in_specs=in_specs,
        out_specs=pl.BlockSpec((tq, d), lambda b, p, qi, kj, la: (b * nq + qi[p], 0)),
        scratch_shapes=[
            pltpu.VMEM((hd, 1, tq), F32), pltpu.VMEM((hd, V_ROWS, tq), F32), pltpu.VMEM((hd * HEAD_V, tq), BF16),
        ],
    )
    return pl.pallas_call(
        functools.partial(_attn_t_kernel, tq=tq, tk=tk, use_bias=use_bias),
        grid_spec=grid_spec,
        out_shape=jax.ShapeDtypeStruct((t, d), F32),
        compiler_params=_params("parallel", "arbitrary"),
        name=name,
    )(qi, kj, last, *args)


def _rope_tables(pos):
    half = MLA_ROPE // 2
    inv = ROPE_BASE ** (-jnp.arange(half, dtype=F32) / half)
    ang = pos.astype(F32)[:, None] * inv[None, :]
    cos, sin = jnp.cos(ang), jnp.sin(ang)
    cos_k = jnp.concatenate([cos, cos], axis=1)
    sin_k = jnp.concatenate([-sin, sin], axis=1)
    return jnp.tile(cos_k, (1, MLA_HEADS)), jnp.tile(sin_k, (1, MLA_HEADS)), cos_k, sin_k


def _swap_halves(w, group):
    shp = w.shape
    wr = w.reshape(shp[:-1] + (shp[-1] // group, 2, group // 2))
    return wr[..., ::-1, :].reshape(shp)


def _mla_weights(w_dq, w_uq, w_dkv, w_ukv, w_o):
    hd = MLA_HEADS
    cast = lambda a: a.astype(BF16)
    w_uq, w_dkv, w_ukv = cast(w_uq), cast(w_dkv), cast(w_ukv)
    wqn = w_uq[:, :, :MLA_NOPE].reshape(MLA_Q_LORA, hd * MLA_NOPE)
    wqr = w_uq[:, :, MLA_NOPE:].reshape(MLA_Q_LORA, hd * MLA_ROPE)
    wqs = _swap_halves(wqr, MLA_ROPE)
    wkc, wkr = w_dkv[:, :MLA_KV_LORA], w_dkv[:, MLA_KV_LORA:]
    wks = _swap_halves(wkr, MLA_ROPE)
    wuk = jnp.transpose(w_ukv[:, :, :MLA_NOPE], (1, 2, 0))
    zeros = jnp.zeros_like(wuk)
    even = jnp.concatenate([wuk, zeros], axis=1)
    odd = jnp.concatenate([zeros, wuk], axis=1)
    wuk2 = jnp.where((jnp.arange(hd) % 2 == 0)[:, None, None], even, odd)
    wuv = jnp.transpose(w_ukv[:, :, MLA_NOPE:], (1, 0, 2))
    d = w_dkv.shape[0]
    pad_r = lambda a: jnp.concatenate([a, jnp.zeros((d, LANES - MLA_ROPE), BF16)], axis=1)
    wukp = jnp.concatenate([jnp.zeros((MLA_KV_LORA, hd, MLA_ROPE), BF16), w_ukv[:, :, :MLA_NOPE],
                            jnp.zeros((MLA_KV_LORA, hd, LANES - MLA_ROPE - MLA_NOPE), BF16)], axis=2)
    return dict(wdq=cast(w_dq), wqn=wqn, wqr=wqr, wqs=wqs, wkc=wkc, wkr=wkr, wks=wks, wuk=wuk2, wuv=wuv, wo=cast(w_o),
                wqnt=wqn.T, wqrt=wqr.T, wqst=wqs.T, wkrp=pad_r(wkr), wksp=pad_r(wks),
                wukp=wukp.reshape(MLA_KV_LORA, hd * LANES), wuvt=wuv.transpose(0, 2, 1).reshape(hd * MLA_V, MLA_KV_LORA),
                wot=cast(w_o).T)


def _mla_proj_kernel(x_ref, gm_ref, wdq_ref, gq_ref, wqn_ref, wqr_ref, wqs_ref, wkc_ref, wkr_ref, wks_ref,
                     gkv_ref, wuk_ref, cq_ref, sq_ref, ck_ref, sk_ref,
                     ql_ref, qr_ref, ckv_ref, kr_ref, ckvb_ref, krb_ref):
    h = _rms(x_ref[...], gm_ref[...]).astype(BF16)
    cq = _rms(_dot(h, wdq_ref[...]), gq_ref[...]).astype(BF16)
    qn = _dot(cq, wqn_ref[...])
    qr = ((_dot(cq, wqr_ref[...]) * cq_ref[...] + _dot(cq, wqs_ref[...]) * sq_ref[...]) * MLA_SCALE).astype(BF16)
    for hh in range(MLA_HEADS):
        pair = qn[:, (hh // 2) * LANES:(hh // 2 + 1) * LANES].astype(BF16)
        ql_ref[hh] = (_dot(pair, wuk_ref[hh]) * MLA_SCALE).astype(BF16)
        qr_ref[hh] = qr[:, hh * MLA_ROPE:(hh + 1) * MLA_ROPE]
    ckv = _rms(_dot(h, wkc_ref[...]), gkv_ref[...])
    ckv_ref[...] = ckv
    ckvb_ref[...] = ckv.astype(BF16)
    kr = _dot(h, wkr_ref[...]) * ck_ref[...] + _dot(h, wks_ref[...]) * sk_ref[...]
    kr_ref[...] = kr
    krb_ref[...] = kr.astype(BF16)


def _mla_proj(x, g_mix, wts, g_q, g_kv, tables):
    t, d = x.shape
    tm = min(256, t)
    cos_q, sin_q, cos_k, sin_k = tables
    nrep = cos_q.shape[0] // tm
    tab = lambda wdt: pl.BlockSpec((tm, wdt), lambda i: (i % nrep, 0))
    hd = MLA_HEADS
    row = lambda wdt: pl.BlockSpec((tm, wdt), lambda i: (i, 0))
    hm = lambda wdt: pl.BlockSpec((hd, tm, wdt), lambda i: (0, i, 0))
    return pl.pallas_call(
        _mla_proj_kernel,
        grid=(t // tm,),
        in_specs=[
            row(d), _whole((1, d)), _whole(wts["wdq"].shape), _whole((1, MLA_Q_LORA)),
            _whole(wts["wqn"].shape), _whole(wts["wqr"].shape), _whole(wts["wqs"].shape),
            _whole(wts["wkc"].shape), _whole(wts["wkr"].shape), _whole(wts["wks"].shape),
            _whole((1, MLA_KV_LORA)), _whole(wts["wuk"].shape),
            tab(hd * MLA_ROPE), tab(hd * MLA_ROPE), tab(MLA_ROPE), tab(MLA_ROPE),
        ],
        out_specs=[hm(MLA_KV_LORA), hm(MLA_ROPE), row(MLA_KV_LORA), row(MLA_ROPE), row(MLA_KV_LORA), row(MLA_ROPE)],
        out_shape=[
            jax.ShapeDtypeStruct((hd, t, MLA_KV_LORA), BF16), jax.ShapeDtypeStruct((hd, t, MLA_ROPE), BF16),
            jax.ShapeDtypeStruct((t, MLA_KV_LORA), F32), jax.ShapeDtypeStruct((t, MLA_ROPE), F32),
            jax.ShapeDtypeStruct((t, MLA_KV_LORA), BF16), jax.ShapeDtypeStruct((t, MLA_ROPE), BF16),
        ],
        compiler_params=_params("parallel"),
        name="mla_proj",
    )(x, g_mix.reshape(1, d), wts["wdq"], g_q.reshape(1, -1), wts["wqn"], wts["wqr"], wts["wqs"],
      wts["wkc"], wts["wkr"], wts["wks"], g_kv.reshape(1, -1), wts["wuk"], cos_q, sin_q, cos_k, sin_k)


def _mla_proj_t_kernel(x_ref, gm_ref, wdq_ref, gq_ref, wqnt_ref, wqrt_ref, wqst_ref, wkc_ref, wkrp_ref, wksp_ref,
                       gkv_ref, wukp_ref, wuvt_ref, cqt_ref, sqt_ref, ckp_ref, skp_ref,
                       qt_ref, kh_ref, vt_ref, ckv_ref, kr_ref):
    tm = x_ref.shape[0]
    hd = MLA_HEADS
    qscale = MLA_SCALE * LOG2E
    h = _rms(x_ref[...], gm_ref[...]).astype(BF16)
    cq_raw = _dot(h, wdq_ref[...])
    ckv_raw = _dot(h, wkc_ref[...])
    kr_a = _dot(h, wkrp_ref[...])
    kr_b = _dot(h, wksp_ref[...])
    cq = _rms(cq_raw, gq_ref[...]).astype(BF16)
    qnt_raw = _dot_t(wqnt_ref[...], cq)
    qrt_a = _dot_t(wqrt_ref[...], cq)
    qrt_b = _dot_t(wqst_ref[...], cq)
    ckv = _rms(ckv_raw, gkv_ref[...])
    ckv_ref[...] = ckv
    cb = ckv.astype(BF16)
    kn = _dot(cb, wukp_ref[...])
    vt = _dot_t(wuvt_ref[...], cb).astype(BF16)
    qnt = (qnt_raw * qscale).astype(BF16)
    qrt = ((qrt_a * cqt_ref[...] + qrt_b * sqt_ref[...]) * qscale).astype(BF16)
    zeros = jnp.zeros((LANES - MLA_ROPE - MLA_NOPE, tm), BF16)
    krp = kr_a * ckp_ref[...] + kr_b * skp_ref[...]
    kr_ref[...] = krp[:, :MLA_ROPE]
    ones = jnp.ones((BF16_ROWS, tm), BF16)
    for hh in range(hd):
        qt_ref[hh, 0:MLA_ROPE, :] = qrt[hh * MLA_ROPE:(hh + 1) * MLA_ROPE]
        qt_ref[hh, MLA_ROPE:MLA_ROPE + MLA_NOPE, :] = qnt[hh * MLA_NOPE:(hh + 1) * MLA_NOPE]
        qt_ref[hh, MLA_ROPE + MLA_NOPE:, :] = zeros
        kh_ref[hh] = (kn[:, hh * LANES:(hh + 1) * LANES] + krp).astype(BF16)
        vt_ref[hh, 0:MLA_V, :] = vt[hh * MLA_V:(hh + 1) * MLA_V]
        vt_ref[hh, MLA_V:, :] = ones


def _mla_proj_t(x, g_mix, wts, g_q, g_kv, tables, seq):
    t, d = x.shape
    tm = min(256, seq)
    hd = MLA_HEADS
    cos_q, sin_q, cos_k, sin_k = tables
    pad = lambda a: jnp.concatenate([a, jnp.zeros((seq, LANES - MLA_ROPE), F32)], axis=1)
    nrep = seq // tm
    row = lambda wdt: pl.BlockSpec((tm, wdt), lambda i: (i, 0))
    names = ("wdq", "wqnt", "wqrt", "wqst", "wkc", "wkrp", "wksp", "wukp", "wuvt")
    w = {n: wts[n] for n in names}
    return pl.pallas_call(
        _mla_proj_t_kernel,
        grid=(t // tm,),
        in_specs=[
            row(d), _whole((1, d)), _whole(w["wdq"].shape), _whole((1, MLA_Q_LORA)),
            _whole(w["wqnt"].shape), _whole(w["wqrt"].shape), _whole(w["wqst"].shape),
            _whole(w["wkc"].shape), _whole(w["wkrp"].shape), _whole(w["wksp"].shape),
            _whole((1, MLA_KV_LORA)), _whole(w["wukp"].shape), _whole(w["wuvt"].shape),
            pl.BlockSpec((hd * MLA_ROPE, tm), lambda i: (0, i % nrep)),
            pl.BlockSpec((hd * MLA_ROPE, tm), lambda i: (0, i % nrep)),
            pl.BlockSpec((tm, LANES), lambda i: (i % nrep, 0)),
            pl.BlockSpec((tm, LANES), lambda i: (i % nrep, 0)),
        ],
        out_specs=[
            pl.BlockSpec((hd, LANES, tm), lambda i: (0, 0, i)),
            pl.BlockSpec((hd, tm, LANES), lambda i: (0, i, 0)),
            pl.BlockSpec((hd, V_ROWS, tm), lambda i: (0, 0, i)),
            row(MLA_KV_LORA), row(MLA_ROPE),
        ],
        out_shape=[
            jax.ShapeDtypeStruct((hd, LANES, t), BF16), jax.ShapeDtypeStruct((hd, t, LANES), BF16),
            jax.ShapeDtypeStruct((hd, V_ROWS, t), BF16),
            jax.ShapeDtypeStruct((t, MLA_KV_LORA), F32), jax.ShapeDtypeStruct((t, MLA_ROPE), F32),
        ],
        compiler_params=_params("parallel"),
        name="mla_proj_t",
    )(x, g_mix.reshape(1, d), w["wdq"], g_q.reshape(1, -1), w["wqnt"], w["wqrt"], w["wqst"],
      w["wkc"], w["wkrp"], w["wksp"], g_kv.reshape(1, -1), w["wukp"], w["wuvt"],
      cos_q.T, sin_q.T, pad(cos_k), pad(sin_k))


def _mla_epilogue(o_lat, x_ref, wuv_ref, wo_ref, o_ref, cat_scr, tq):
    ob = o_lat.astype(BF16)
    for hh in range(MLA_HEADS):
        cat_scr[:, hh * MLA_V:(hh + 1) * MLA_V] = _dot(ob[hh * tq:(hh + 1) * tq], wuv_ref[hh]).astype(BF16)
    o_ref[...] = x_ref[...] + _dot(cat_scr[...], wo_ref[...])


def _mla_samp_kernel(ql_ref, qr_ref, cc_ref, cr_ref, cn_ref, rn_ref, x_ref, wuv_ref, wo_ref, o_ref, cat_scr, *, n_q):
    b = pl.program_id(0)
    hd = MLA_HEADS
    ql = ql_ref[...].reshape(hd * n_q, MLA_KV_LORA)
    qr = qr_ref[...].reshape(hd * n_q, MLA_ROPE)
    cc = cc_ref[0].astype(BF16)
    cr = cr_ref[0].astype(BF16)
    cn = cn_ref[...]
    s1 = _dot_t(ql, cc) + _dot_t(qr, cr)
    s2 = _dot_t(ql, cn) + _dot_t(qr, rn_ref[...])
    own = _div(lax.broadcasted_iota(I32, (1, LANES), 1), n_q) == _mod(b, LANES // n_q)
    s2 = jnp.where(own, s2, NEG_INF)
    m = jnp.maximum(jnp.max(s1, axis=1, keepdims=True), jnp.max(s2, axis=1, keepdims=True))
    p1 = jnp.exp(s1 - m)
    p2 = jnp.exp(s2 - m)
    l = jnp.sum(p1, axis=1, keepdims=True) + jnp.sum(p2, axis=1, keepdims=True)
    o_lat = (_dot(p1.astype(BF16), cc) + _dot(p2.astype(BF16), cn)) / l
    _mla_epilogue(o_lat, x_ref, wuv_ref, wo_ref, o_ref, cat_scr, n_q)


def _mla_attn_sample(x, ql, qr, ckvb, krb, cache_c, cache_r, wts, n_b, n_q):
    t, d = x.shape
    hd = MLA_HEADS
    past = cache_c.shape[1]
    per = LANES // n_q
    return pl.pallas_call(
        functools.partial(_mla_samp_kernel, n_q=n_q),
        grid=(n_b,),
        in_specs=[
            pl.BlockSpec((hd, n_q, MLA_KV_LORA), lambda b: (0, b, 0)),
            pl.BlockSpec((hd, n_q, MLA_ROPE), lambda b: (0, b, 0)),
            pl.BlockSpec((1, past, MLA_KV_LORA), lambda b: (b, 0, 0)),
            pl.BlockSpec((1, past, MLA_ROPE), lambda b: (b, 0, 0)),
            pl.BlockSpec((LANES, MLA_KV_LORA), lambda b: (b // per, 0)),
            pl.BlockSpec((LANES, MLA_ROPE), lambda b: (b // per, 0)),
            pl.BlockSpec((n_q, d), lambda b: (b, 0)),
            _whole(wts["wuv"].shape), _whole(wts["wo"].shape),
        ],
        out_specs=pl.BlockSpec((n_q, d), lambda b: (b, 0)),
        out_shape=jax.ShapeDtypeStruct((t, d), F32),
        scratch_shapes=[pltpu.VMEM((n_q, hd * MLA_V), BF16)],
        compiler_params=_params("parallel"),
        name="mla_attn_sample",
    )(ql, qr, cache_c, cache_r, ckvb, krb, x, wts["wuv"], wts["wo"])


def _dsa_proj_kernel(x_ref, gm_ref, wqkv_ref, wqi_ref, wkw_ref, gki_ref,
                     k_ref, v_ref, ki_ref, qb_ref, kb_ref, vb_ref, qib_ref, kib_ref, wi_ref):
    hdm = DSA_HEADS * DSA_HEAD_DIM
    h = _rms(x_ref[...], gm_ref[...]).astype(BF16)
    qkv = _dot(h, wqkv_ref[...])
    q = qkv[:, :hdm] * DSA_SCALE
    k = qkv[:, hdm:2 * hdm]
    v = qkv[:, 2 * hdm:]
    k_ref[...] = k
    v_ref[...] = v
    qi = _dot(h, wqi_ref[...]) * IDX_SCALE
    kw = _dot(h, wkw_ref[...])
    ki = _rms(kw[:, :IDX_DIM], gki_ref[...])
    ki_ref[...] = ki
    kib_ref[...] = ki.astype(BF16)
    wi_ref[...] = kw[:, IDX_DIM:IDX_DIM + IDX_HEADS] * (IDX_HEADS ** -0.5)
    qb_ref[...] = q.astype(BF16)
    kb_ref[...] = k.astype(BF16)
    vb_ref[...] = v.astype(BF16)
    for hh in range(IDX_HEADS):
        qib_ref[hh] = qi[:, hh * IDX_DIM:(hh + 1) * IDX_DIM].astype(BF16)


def _dsa_proj(x, g_mix, w_qkv, w_qidx, w_kw, g_kidx):
    t, d = x.shape
    tm = min(512, t)
    hdm = DSA_HEADS * DSA_HEAD_DIM
    row = lambda wdt: pl.BlockSpec((tm, wdt), lambda i: (i, 0))
    return pl.pallas_call(
        _dsa_proj_kernel,
        grid=(t // tm,),
        in_specs=[row(d), _whole((1, d)), _whole(w_qkv.shape), _whole(w_qidx.shape), _whole(w_kw.shape),
                  _whole((1, IDX_DIM))],
        out_specs=[row(hdm), row(hdm), row(IDX_DIM), row(hdm), row(hdm), row(hdm),
                   pl.BlockSpec((IDX_HEADS, tm, IDX_DIM), lambda i: (0, i, 0)), row(IDX_DIM), row(IDX_HEADS)],
        out_shape=[
            jax.ShapeDtypeStruct((t, hdm), F32), jax.ShapeDtypeStruct((t, hdm), F32),
            jax.ShapeDtypeStruct((t, IDX_DIM), F32), jax.ShapeDtypeStruct((t, hdm), BF16),
            jax.ShapeDtypeStruct((t, hdm), BF16), jax.ShapeDtypeStruct((t, hdm), BF16),
            jax.ShapeDtypeStruct((IDX_HEADS, t, IDX_DIM), BF16), jax.ShapeDtypeStruct((t, IDX_DIM), BF16),
            jax.ShapeDtypeStruct((t, IDX_HEADS), F32),
        ],
        compiler_params=_params("parallel"),
        name="dsa_proj",
    )(x, g_mix.reshape(1, d), w_qkv, w_qidx, w_kw, g_kidx.reshape(1, IDX_DIM))


def _dsa_proj_t_kernel(x_ref, gm_ref, wkv_ref, wqt_ref, wvt_ref, wqit_ref, wkw_ref, wwt_ref, gki_ref,
                       k_ref, v_ref, ki_ref, qt_ref, kh_ref, vt_ref, qit_ref, kib_ref, wit_ref):
    tm = x_ref.shape[0]
    hdm = DSA_HEADS * DSA_HEAD_DIM
    dh = DSA_HEAD_DIM
    h = _rms(x_ref[...], gm_ref[...]).astype(BF16)
    kv = _dot(h, wkv_ref[...])
    k = kv[:, :hdm]
    k_ref[...] = k
    v_ref[...] = kv[:, hdm:]
    kw = _dot(h, wkw_ref[...])
    ki = _rms(kw[:, :IDX_DIM], gki_ref[...])
    ki_ref[...] = ki
    kib_ref[...] = ki.astype(BF16)
    wit_ref[...] = _dot_t(wwt_ref[...], h)[:IDX_HEADS] * (IDX_HEADS ** -0.5)
    qt = (_dot_t(wqt_ref[...], h) * (DSA_SCALE * LOG2E)).astype(BF16)
    vt = _dot_t(wvt_ref[...], h).astype(BF16)
    qit = (_dot_t(wqit_ref[...], h) * IDX_SCALE).astype(BF16)
    ones = jnp.ones((BF16_ROWS, tm), BF16)
    for hh in range(DSA_HEADS):
        qt_ref[hh] = qt[hh * dh:(hh + 1) * dh]
        kh_ref[hh] = k[:, hh * dh:(hh + 1) * dh].astype(BF16)
        vt_ref[hh, 0:dh, :] = vt[hh * dh:(hh + 1) * dh]
        vt_ref[hh, dh:, :] = ones
    for hh in range(IDX_HEADS):
        qit_ref[hh] = qit[hh * IDX_DIM:(hh + 1) * IDX_DIM]


def _dsa_proj_t(x, g_mix, w, g_kidx):
    t, d = x.shape
    tm = min(256, t)
    hd, dh = DSA_HEADS, DSA_HEAD_DIM
    hdm = hd * dh
    row = lambda wdt: pl.BlockSpec((tm, wdt), lambda i: (i, 0))
    names = ("wkv", "wqt", "wvt", "wqit", "wkw", "wwt")
    return pl.pallas_call(
        _dsa_proj_t_kernel,
        grid=(t // tm,),
        in_specs=[row(d), _whole((1, d))] + [_whole(w[n].shape) for n in names] + [_whole((1, IDX_DIM))],
        out_specs=[
            row(hdm), row(hdm), row(IDX_DIM),
            pl.BlockSpec((hd, dh, tm), lambda i: (0, 0, i)),
            pl.BlockSpec((hd, tm, dh), lambda i: (0, i, 0)),
            pl.BlockSpec((hd, V_ROWS, tm), lambda i: (0, 0, i)),
            pl.BlockSpec((IDX_HEADS, IDX_DIM, tm), lambda i: (0, 0, i)),
            row(IDX_DIM),
            pl.BlockSpec((IDX_HEADS, tm), lambda i: (0, i)),
        ],
        out_shape=[
            jax.ShapeDtypeStruct((t, hdm), F32), jax.ShapeDtypeStruct((t, hdm), F32),
            jax.ShapeDtypeStruct((t, IDX_DIM), F32),
            jax.ShapeDtypeStruct((hd, dh, t), BF16), jax.ShapeDtypeStruct((hd, t, dh), BF16),
            jax.ShapeDtypeStruct((hd, V_ROWS, t), BF16),
            jax.ShapeDtypeStruct((IDX_HEADS, IDX_DIM, t), BF16), jax.ShapeDtypeStruct((t, IDX_DIM), BF16),
            jax.ShapeDtypeStruct((IDX_HEADS, t), F32),
        ],
        compiler_params=_params("parallel"),
        name="dsa_proj_t",
    )(x, g_mix.reshape(1, d), *[w[n] for n in names], g_kidx.reshape(1, IDX_DIM))


def _sort_key(score):
    bits = lax.bitcast_convert_type(score, I32)
    return jnp.where(bits < 0, bits ^ jnp.int32(0x7FFFFFFF), bits)


def _kth_largest_key(count, topk):
    c0 = count(lambda key, idx: jnp.where(key >= 0, 1, 0))
    t0 = jnp.where(c0 >= topk, jnp.int32(0), jnp.int32(INT_MIN))

    def bit_body(i, t):
        cand = t + lax.shift_left(jnp.int32(1), jnp.int32(30) - i)
        c = count(lambda key, idx: jnp.where(key >= cand, 1, 0))
        return jnp.where(c >= topk, cand, t)

    return lax.fori_loop(0, 31, bit_body, t0)


def _tie_cutoff(count, thr, rem, nbits):
    def bit_body(i, j):
        cand = j + lax.shift_left(jnp.int32(1), jnp.int32(nbits - 1) - i)
        c = count(lambda key, idx: jnp.where(key == thr, jnp.where(idx < cand, 1, 0), 0))
        return jnp.where(c < rem, cand, j)

    return lax.fori_loop(0, nbits, bit_body, jnp.zeros_like(thr))


def _select(key, idx, thr, cut):
    chosen = jnp.where(key > thr, 1, jnp.where(key == thr, jnp.where(idx <= cut, 1, 0), 0))
    return jnp.where(key > KEY_NEG_INF, chosen, 0) > 0


def _dsa_select_kernel(qit_ref, wit_ref, ki_ref, tri_ref, bias_ref, key_scr, *, tq, tk, topk):
    qb = pl.program_id(1)
    seq = ki_ref.shape[0]
    nk = seq // tk
    nvis = _div((qb + 1) * tq + tk - 1, tk)
    q_chunk = _div(qb * tq + lax.broadcasted_iota(I32, (1, tq), 1), CHUNK)
    w = wit_ref[...]
    row_idx = lax.broadcasted_iota(I32, (tk, 1), 0)

    def score_body(j, carry):
        off = pl.multiple_of(j * tk, tk)
        kb = ki_ref[pl.ds(off, tk), :]
        acc = jnp.zeros((tk, tq), F32)
        for hh in range(IDX_HEADS):
            acc = acc + w[hh:hh + 1, :] * jnp.maximum(_dot(kb, qit_ref[hh]), 0.0)
        vis = _div(off + row_idx, CHUNK) <= q_chunk
        key_scr[pl.ds(off, tk), :] = _sort_key(jnp.where(vis, acc, NEG_INF))
        return carry

    lax.fori_loop(0, nvis, score_body, 0)

    def count(hit):
        def body(j, c):
            off = pl.multiple_of(j * tk, tk)
            hits = hit(key_scr[pl.ds(off, tk), :], off + row_idx)
            return c + jnp.sum(hits.reshape(tk // 8, 8, tq), axis=0)
        c = lax.fori_loop(0, nvis, body, jnp.zeros((8, tq), I32))
        return jnp.sum(c, axis=0, keepdims=True)

    thr = _kth_largest_key(count, topk)
    rem = (topk - count(lambda key, idx: jnp.where(key > thr, 1, 0))).astype(F32)
    tri = tri_ref[...]

    def out_body(j, seen):
        off = pl.multiple_of(j * tk, tk)
        key = key_scr[pl.ds(off, tk), :]
        tie = jnp.where(key == thr, 1.0, 0.0)
        rank = seen + _dot(tri, tie.astype(BF16))
        keep = jnp.where(key > thr, 1.0, jnp.where(rank <= rem, tie, 0.0))
        keep = jnp.where(key > KEY_NEG_INF, keep, 0.0)
        bias_ref[pl.ds(off, tk), :] = jnp.where(keep > 0.0, 0.0, NEG_INF).astype(BF16)
        return rank[tk - 1:tk, :]

    lax.fori_loop(0, nvis, out_body, jnp.zeros((1, tq), F32))

    def fill_body(j, carry):
        off = pl.multiple_of(j * tk, tk)
        bias_ref[pl.ds(off, tk), :] = jnp.full((tk, tq), NEG_INF, BF16)
        return carry

    lax.fori_loop(nvis, nk, fill_body, 0)


def _dsa_select(qit, wit, kib, batch, seq, topk):
    tq, tk = min(256, seq), min(512, seq)
    nq = seq // tq
    t = batch * seq
    return pl.pallas_call(
        functools.partial(_dsa_select_kernel, tq=tq, tk=tk, topk=topk),
        grid=(batch, nq),
        in_specs=[
            pl.BlockSpec((IDX_HEADS, IDX_DIM, tq), lambda b, q: (0, 0, b * nq + q)),
            pl.BlockSpec((IDX_HEADS, tq), lambda b, q: (0, b * nq + q)),
            pl.BlockSpec((seq, IDX_DIM), lambda b, q: (b, 0)),
            _whole((tk, tk)),
        ],
        out_specs=pl.BlockSpec((seq, tq), lambda b, q: (0, b * nq + q)),
        out_shape=jax.ShapeDtypeStruct((seq, t), BF16),
        scratch_shapes=[pltpu.VMEM((seq, tq), I32)],
        compiler_params=_params("parallel", "arbitrary"),
        name="dsa_select",
    )(qit, wit, kib, jnp.tril(jnp.ones((tk, tk), BF16)))


def _dsa_samp_kernel(q_ref, qi_ref, wi_ref, kc_ref, vc_ref, kic_ref, kn_ref, vn_ref, kin_ref, x_ref, wo_ref, o_ref,
                     *, n_q, topk):
    b = pl.program_id(0)
    hd, dh = DSA_HEADS, DSA_HEAD_DIM
    past = kc_ref.shape[1]
    n_keys = past + LANES
    own = _div(lax.broadcasted_iota(I32, (1, LANES), 1), n_q) == _mod(b, LANES // n_q)

    qi = qi_ref[...].reshape(IDX_HEADS * n_q, IDX_DIM)
    lg1 = jnp.maximum(_dot_t(qi, kic_ref[0].astype(BF16)), 0.0)
    lg2 = jnp.maximum(_dot_t(qi, kin_ref[...]), 0.0)
    w = wi_ref[...]
    sc1 = jnp.zeros((n_q, past), F32)
    sc2 = jnp.zeros((n_q, LANES), F32)
    for hh in range(IDX_HEADS):
        sc1 = sc1 + w[:, hh:hh + 1] * lg1[hh * n_q:(hh + 1) * n_q]
        sc2 = sc2 + w[:, hh:hh + 1] * lg2[hh * n_q:(hh + 1) * n_q]
    key = _sort_key(jnp.concatenate([sc1, jnp.where(own, sc2, NEG_INF)], axis=1))
    idx = lax.broadcasted_iota(I32, (1, n_keys), 1)

    count = lambda hit: jnp.sum(hit(key, idx), axis=1, keepdims=True)
    thr = _kth_largest_key(count, topk)
    rem = topk - count(lambda k_, i_: jnp.where(k_ > thr, 1, 0))
    cut = _tie_cutoff(count, thr, rem, int(n_keys - 1).bit_length())
    bias = jnp.where(_select(key, idx, thr, cut), 0.0, NEG_INF)
    bias = jnp.concatenate([bias] * hd, axis=0)

    lane_head = _div(lax.broadcasted_iota(I32, (1, hd * dh), 1), dh)
    qf = q_ref[...].astype(F32)
    qbd = jnp.concatenate([jnp.where(lane_head == hh, qf, 0.0) for hh in range(hd)], axis=0).astype(BF16)
    kc = kc_ref[0].astype(BF16)
    vc = vc_ref[0].astype(BF16)
    s1 = _dot_t(qbd, kc) + bias[:, :past]
    s2 = _dot_t(qbd, kn_ref[...]) + bias[:, past:]
    m = jnp.maximum(jnp.max(s1, axis=1, keepdims=True), jnp.max(s2, axis=1, keepdims=True))
    p1 = jnp.exp(s1 - m)
    p2 = jnp.exp(s2 - m)
    l = jnp.sum(p1, axis=1, keepdims=True) + jnp.sum(p2, axis=1, keepdims=True)
    o_all = (_dot(p1.astype(BF16), vc) + _dot(p2.astype(BF16), vn_ref[...])) / l
    out = jnp.zeros((n_q, hd * dh), F32)
    for hh in range(hd):
        out = out + jnp.where(lane_head == hh, o_all[hh * n_q:(hh + 1) * n_q], 0.0)
    o_ref[...] = x_ref[...] + _dot(out.astype(BF16), wo_ref[...])


def _dsa_attn_sample(x, qb, qib, wi, kb, vb, kib, cache_k, cache_v, cache_ki, w_o, n_b, n_q, topk):
    t, d = x.shape
    hdm = DSA_HEADS * DSA_HEAD_DIM
    past = cache_k.shape[1]
    per = LANES // n_q
    return pl.pallas_call(
        functools.partial(_dsa_samp_kernel, n_q=n_q, topk=topk),
        grid=(n_b,),
        in_specs=[
            pl.BlockSpec((n_q, hdm), lambda b: (b, 0)),
            pl.BlockSpec((IDX_HEADS, n_q, IDX_DIM), lambda b: (0, b, 0)),
            pl.BlockSpec((n_q, IDX_HEADS), lambda b: (b, 0)),
            pl.BlockSpec((1, past, hdm), lambda b: (b, 0, 0)),
            pl.BlockSpec((1, past, hdm), lambda b: (b, 0, 0)),
            pl.BlockSpec((1, past, IDX_DIM), lambda b: (b, 0, 0)),
            pl.BlockSpec((LANES, hdm), lambda b: (b // per, 0)),
            pl.BlockSpec((LANES, hdm), lambda b: (b // per, 0)),
            pl.BlockSpec((LANES, IDX_DIM), lambda b: (b // per, 0)),
            pl.BlockSpec((n_q, d), lambda b: (b, 0)),
            _whole(w_o.shape),
        ],
        out_specs=pl.BlockSpec((n_q, d), lambda b: (b, 0)),
        out_shape=jax.ShapeDtypeStruct((t, d), F32),
        compiler_params=_params("parallel"),
        name="dsa_attn_sample",
    )(qb, qib, wi, cache_k.reshape(n_b, past, hdm), cache_v.reshape(n_b, past, hdm), cache_ki, kb, vb, kib, x, w_o)


def kernel(x_prompt, x_sample, cache_mla_ckv, cache_mla_krope, cache_dsa_k, cache_dsa_v, cache_dsa_kidx, norm_mix, norm_ffn, norm_final, mla_w_dq, mla_g_q, mla_w_uq, mla_w_dkv, mla_g_kv, mla_w_ukv, mla_w_o, cmlp_w_in, cmlp_ln_g, cmlp_ln_b, cmlp_w_s, cmlp_b_s, cmlp_w_out, dsa_w_qkv, dsa_w_o, dsa_w_qidx, dsa_w_kidx, dsa_g_kidx, dsa_w_widx, ffn_w_in, ffn_w_out):
    batch, seq, d = x_prompt.shape
    n_b, n_q, _ = x_sample.shape
    past = cache_mla_ckv.shape[2]
    depth = norm_mix.shape[0]
    xp = x_prompt.reshape(batch * seq, d)
    xs = x_sample.reshape(n_b * n_q, d)
    tab_p = _rope_tables(jnp.arange(seq))
    tab_s = tuple(jnp.tile(a, (n_b, 1)) for a in _rope_tables(past + jnp.arange(n_q)))
    cast = lambda a: a.astype(BF16)
    outs = {k: [] for k in ("ckv_p", "kr_p", "ckv_s", "kr_s", "cv_s", "dk_p", "dv_p", "di_p", "dk_s", "dv_s", "di_s")}
    for i in range(depth):
        kind, j = i % 3, i // 3
        if kind == 0:
            wts = _mla_weights(mla_w_dq[j], mla_w_uq[j], mla_w_dkv[j], mla_w_ukv[j], mla_w_o[j])
            qt, kh, vt, ckv, kr = _mla_proj_t(xp, norm_mix[i], wts, mla_g_q[j], mla_g_kv[j], tab_p, seq)
            xp = _attn_t(xp, qt, kh, vt, None, wts["wot"], batch, seq, 256, "mla_attn_prompt")
            outs["ckv_p"].append(ckv.reshape(batch, seq, -1)); outs["kr_p"].append(kr.reshape(batch, seq, -1))
            ql, qr, ckv, kr, ckvb, krb = _mla_proj(xs, norm_mix[i], wts, mla_g_q[j], mla_g_kv[j], tab_s)
            xs = _mla_attn_sample(xs, ql, qr, ckvb, krb, cache_mla_ckv[j], cache_mla_krope[j], wts, n_b, n_q)
            outs["ckv_s"].append(ckv.reshape(n_b, n_q, -1)); outs["kr_s"].append(kr.reshape(n_b, n_q, -1))
        elif kind == 1:
            w_in, w_out = cast(cmlp_w_in[j]), cast(cmlp_w_out[j])
            xp, _ = _cmlp(xp, norm_mix[i], w_in, cmlp_ln_g[j], cmlp_ln_b[j], cmlp_w_s[j], cmlp_b_s[j], w_out,
                          min(seq, CMLP_CHUNK), False)
            xs, v_s = _cmlp(xs, norm_mix[i], w_in, cmlp_ln_g[j], cmlp_ln_b[j], cmlp_w_s[j], cmlp_b_s[j], w_out,
                            min(n_q, CMLP_CHUNK), True)
            outs["cv_s"].append(v_s.reshape(n_b, n_q, -1))
        else:
            hdm = DSA_HEADS * DSA_HEAD_DIM
            w_qkv, w_qidx, w_o = cast(dsa_w_qkv[j]), cast(dsa_w_qidx[j]), cast(dsa_w_o[j])
            w_kidx, w_widx = cast(dsa_w_kidx[j]), cast(dsa_w_widx[j])
            zpad = lambda n: jnp.zeros((d, n), BF16)
            wt = dict(wkv=w_qkv[:, hdm:], wqt=w_qkv[:, :hdm].T, wvt=w_qkv[:, 2 * hdm:].T, wqit=w_qidx.T,
                      wkw=jnp.concatenate([w_kidx, zpad(LANES - IDX_DIM)], axis=1),
                      wwt=jnp.concatenate([w_widx, zpad(BF16_ROWS - IDX_HEADS)], axis=1).T)
            hshape = (DSA_HEADS, DSA_HEAD_DIM)
            k, v, ki, qt, kh, vt, qit, kib, wit = _dsa_proj_t(xp, norm_mix[i], wt, dsa_g_kidx[j])
            bias = _dsa_select(qit, wit, kib, batch, seq, min(TOPK_MAX, seq // 4))
            xp = _attn_t(xp, qt, kh, vt, bias, w_o.T, batch, seq, 512, "dsa_attn_prompt")
            outs["dk_p"].append(k.reshape((batch, seq) + hshape)); outs["dv_p"].append(v.reshape((batch, seq) + hshape))
            outs["di_p"].append(ki.reshape(batch, seq, -1))
            w_kw = jnp.concatenate([w_kidx, w_widx, zpad(LANES - IDX_DIM - IDX_HEADS)], axis=1)
            k, v, ki, qb, kb, vb, qib, kib, wi = _dsa_proj(xs, norm_mix[i], w_qkv, w_qidx, w_kw, dsa_g_kidx[j])
            xs = _dsa_attn_sample(xs, qb, qib, wi, kb, vb, kib, cache_dsa_k[j], cache_dsa_v[j], cache_dsa_kidx[j], w_o,
                                  n_b, n_q, min(TOPK_MAX, (past + n_q) // 4))
            outs["dk_s"].append(k.reshape((n_b, n_q) + hshape)); outs["dv_s"].append(v.reshape((n_b, n_q) + hshape))
            outs["di_s"].append(ki.reshape(n_b, n_q, -1))
        w_in, w_out = cast(ffn_w_in[i]), cast(ffn_w_out[i])
        final = i == depth - 1
        xp = _ffn(xp, norm_ffn[i], w_in, w_out, norm_final, final)
        xs = _ffn(xs, norm_ffn[i], w_in, w_out, norm_final, final)
    st = lambda name: jnp.stack(outs[name])
    return (xp.reshape(batch, seq, d), xs.reshape(n_b, n_q, d),
            st("ckv_p"), st("kr_p"), st("ckv_s"), st("kr_s"), st("cv_s"),
            st("dk_p"), st("dv_p"), st("di_p"), st("dk_s"), st("dv_s"), st("di_s"))
```

```python
import functools

import numpy as np
import jax
import jax.numpy as jnp
from jax import lax
from jax.experimental import pallas as pl
from jax.experimental.pallas import tpu as pltpu

F32, BF16, I32 = jnp.float32, jnp.bfloat16, jnp.int32

CHUNK = 64
EPS = 1e-6
MLA_HEADS, MLA_Q_LORA, MLA_KV_LORA, MLA_NOPE, MLA_ROPE, MLA_V = 16, 512, 256, 64, 32, 64
ROPE_BASE = 10000.0
MLA_SCALE = (MLA_NOPE + MLA_ROPE) ** -0.5
CMLP_CHUNK, CMLP_WIDTH, CMLP_GROUPS = 128, 2048, 8
DSA_HEADS, DSA_HEAD_DIM = 16, 64
DSA_SCALE = DSA_HEAD_DIM ** -0.5
IDX_HEADS, IDX_DIM = 8, 64
IDX_SCALE = IDX_DIM ** -0.5
TOPK_MAX = 256

LANES = 128
MXU_TILE = 256
BF16_ROWS = 16
VMEM_LIMIT = 52 * 1024 * 1024
NEG_INF = float("-inf")
INT_MIN = -2 ** 31
KEY_NEG_INF = -2139095041
LOG2E = float(np.log2(np.e))
HEAD_V = 64
V_ROWS = HEAD_V + BF16_ROWS


def _dot(a, b):
    return jnp.dot(a, b, preferred_element_type=F32)


def _dot_t(a, b):
    return lax.dot_general(a, b, (((1,), (1,)), ((), ())), preferred_element_type=F32)


def _rms(x, g):
    return x * lax.rsqrt(jnp.mean(x * x, axis=-1, keepdims=True) + EPS) * g


def _log2(n):
    assert n > 0 and n & (n - 1) == 0, n
    return n.bit_length() - 1


def _div(x, n):
    return lax.shift_right_logical(x, jnp.int32(_log2(n)))


def _mod(x, n):
    assert n & (n - 1) == 0, n
    return x & (n - 1)


def _params(*sem):
    return pltpu.CompilerParams(dimension_semantics=sem, vmem_limit_bytes=VMEM_LIMIT)


def _whole(shape):
    nd = len(shape)
    return pl.BlockSpec(shape, lambda *_: (0,) * nd)


def _ffn_kernel(x_ref, g_ref, wg_ref, wu_ref, wo_ref, gf_ref, o_ref, act_scr, *, final):
    x = x_ref[...]
    h = _rms(x, g_ref[...]).astype(BF16)
    f = wo_ref.shape[0]
    chunk = MXU_TILE if f % MXU_TILE == 0 else f
    nc = f // chunk
    nxt = (_dot(h, wg_ref[:, 0:chunk]), _dot(h, wu_ref[:, 0:chunk]))
    for c in range(nc):
        gate, up = nxt
        if c + 1 < nc:
            lo = (c + 1) * chunk
            nxt = (_dot(h, wg_ref[:, lo:lo + chunk]), _dot(h, wu_ref[:, lo:lo + chunk]))
        act_scr[:, c * chunk:(c + 1) * chunk] = (jax.nn.silu(gate) * up).astype(BF16)
    y = x + _dot(act_scr[...], wo_ref[...])
    if final:
        y = _rms(y, gf_ref[...])
    o_ref[...] = y


def _ffn(x, g, w_in, w_out, g_final, final):
    t, d = x.shape
    f = w_out.shape[0]
    tm = min(512, t)
    once = pl.Buffered(1)
    return pl.pallas_call(
        functools.partial(_ffn_kernel, final=final),
        grid=(t // tm,),
        in_specs=[
            pl.BlockSpec((tm, d), lambda i: (i, 0)),
            _whole((1, d)),
            pl.BlockSpec((d, f), lambda i: (0, 0), pipeline_mode=once),
            pl.BlockSpec((d, f), lambda i: (0, 1), pipeline_mode=once),
            pl.BlockSpec((f, d), lambda i: (0, 0), pipeline_mode=once),
            _whole((1, d)),
        ],
        out_specs=pl.BlockSpec((tm, d), lambda i: (i, 0)),
        out_shape=jax.ShapeDtypeStruct((t, d), F32),
        scratch_shapes=[pltpu.VMEM((tm, f), BF16)],
        compiler_params=_params("parallel"),
        name="ffn",
    )(x, g.reshape(1, d), w_in, w_in, w_out, g_final.reshape(1, d))


def _gelu(x):
    return 0.5 * x * (1.0 + lax.erf(x * np.float32(np.sqrt(0.5))))


def _layernorm(x, g, b):
    mu = jnp.mean(x, axis=-1, keepdims=True)
    xc = x - mu
    return xc * lax.rsqrt(jnp.mean(xc * xc, axis=-1, keepdims=True) + EPS) * g + b


def _cmlp_kernel(x_ref, g_ref, win_ref, lng_ref, lnb_ref, ws_ref, bs_ref, wout_ref, *rest, n_rows, write_v):
    if write_v:
        o_ref, v_ref, vb_scr, gated_scr = rest
    else:
        o_ref, vb_scr, gated_scr = rest
    tm = x_ref.shape[0]
    w = CMLP_WIDTH
    gw = w // CMLP_GROUPS
    c = CMLP_CHUNK
    x = x_ref[...]
    h = _rms(x, g_ref[...]).astype(BF16)
    v = _layernorm(_gelu(_dot(h, win_ref[:, w:])), lng_ref[...], lnb_ref[...])
    if write_v:
        v_ref[...] = v
    vb_scr[...] = v.astype(BF16)
    r_i = lax.broadcasted_iota(I32, (c, c), 0)
    c_i = lax.broadcasted_iota(I32, (c, c), 1)
    keep = jnp.where(c_i >= r_i - _mod(r_i, n_rows), jnp.where(c_i <= r_i, 1, 0), 0) > 0
    u_next = _dot(h, win_ref[:, 0:gw])
    for g in range(CMLP_GROUPS):
        lo, hi = g * gw, (g + 1) * gw
        u_raw = u_next
        if g + 1 < CMLP_GROUPS:
            u_next = _dot(h, win_ref[:, hi:hi + gw])
        wg = jnp.where(keep, ws_ref[g], 0.0).astype(BF16)
        bias = bs_ref[:, g:g + 1]
        mixed = jnp.concatenate(
            [_dot(wg, vb_scr[k * c:(k + 1) * c, lo:hi]) + bias for k in range(tm // c)], axis=0)
        gated_scr[:, lo:hi] = (_gelu(u_raw) * mixed).astype(BF16)
    o_ref[...] = x + _dot(gated_scr[...], wout_ref[...])


def _cmlp(x, g, w_in, ln_g, ln_b, w_s, b_s, w_out, n_rows, write_v):
    t, d = x.shape
    w = CMLP_WIDTH
    c = CMLP_CHUNK
    tm = min(512, t)
    rep = c // n_rows
    ws_t = jnp.tile(w_s[:, :n_rows, :n_rows], (1, rep, rep))
    bs_t = jnp.tile(b_s[:, :n_rows].T, (rep, 1))
    out_shape = [jax.ShapeDtypeStruct((t, d), F32)]
    out_specs = [pl.BlockSpec((tm, d), lambda i: (i, 0))]
    if write_v:
        out_shape.append(jax.ShapeDtypeStruct((t, w), F32))
        out_specs.append(pl.BlockSpec((tm, w), lambda i: (i, 0)))
    res = pl.pallas_call(
        functools.partial(_cmlp_kernel, n_rows=n_rows, write_v=write_v),
        grid=(t // tm,),
        in_specs=[
            pl.BlockSpec((tm, d), lambda i: (i, 0)),
            _whole((1, d)), _whole((d, 2 * w)), _whole((1, w)), _whole((1, w)),
            _whole((CMLP_GROUPS, c, c)), _whole((c, CMLP_GROUPS)), _whole((w, d)),
        ],
        out_specs=out_specs,
        out_shape=out_shape,
        scratch_shapes=[pltpu.VMEM((tm, w), BF16), pltpu.VMEM((tm, w), BF16)],
        compiler_params=_params("parallel"),
        name="cmlp",
    )(x, g.reshape(1, d), w_in, ln_g.reshape(1, w), ln_b.reshape(1, w), ws_t, bs_t, w_out)
    return res if write_v else (res[0], None)


def _attn_t_kernel(qi_ref, kj_ref, last_ref, nsub_ref, qt_ref, k_ref, vt_ref, *rest, tq, tk, sub, use_bias):
    if use_bias:
        bias_ref, x_ref, wot_ref, o_ref, m_scr, acc_scr, cat_scr = rest
    else:
        x_ref, wot_ref, o_ref, m_scr, acc_scr, cat_scr = rest
    p = pl.program_id(1)
    qi, kj = qi_ref[p], kj_ref[p]
    hd = qt_ref.shape[0]

    @pl.when(kj == 0)
    def _():
        m_scr[...] = jnp.full_like(m_scr, NEG_INF)
        acc_scr[...] = jnp.zeros_like(acc_scr)

    def step(off, mask):
        keys = lambda hh: k_ref[hh, pl.ds(off, sub), :]
        vals = lambda hh: vt_ref[hh, :, pl.ds(off, sub)]
        s_next = _dot(keys(0), qt_ref[0])
        pend = None
        for hh in range(hd):
            s = s_next
            if hh + 1 < hd:
                s_next = _dot(keys(hh + 1), qt_ref[hh + 1])
            if mask is not None:
                s = s + mask
            m_prev = m_scr[hh]
            m_new = jnp.maximum(m_prev, jnp.max(s, axis=0, keepdims=True))
            m_safe = jnp.where(m_new == NEG_INF, 0.0, m_new) if use_bias else m_new
            alpha = jnp.exp2(m_prev - m_safe)
            pe = jnp.exp2(s - m_safe).astype(BF16)
            m_scr[hh] = m_new
            if pend is not None:
                ph, pa, pp = pend
                acc_scr[ph] = pa * acc_scr[ph] + _dot(vals(ph), pp)
            pend = (hh, alpha, pe)
        ph, pa, pp = pend
        acc_scr[ph] = pa * acc_scr[ph] + _dot(vals(ph), pp)

    def sub_body(j, carry):
        off = pl.multiple_of(j * sub, sub)
        if use_bias:
            step(off, bias_ref[pl.ds(off, sub), :].astype(F32))
        else:
            base = kj * tk + off
            needs_mask = _div(base + sub - 1, CHUNK) > _div(qi * tq, CHUNK)

            @pl.when(needs_mask)
            def _():
                k_chunk = _div(base + lax.broadcasted_iota(I32, (sub, 1), 0), CHUNK)
                q_chunk = _div(qi * tq + lax.broadcasted_iota(I32, (1, tq), 1), CHUNK)
                step(off, jnp.where(k_chunk <= q_chunk, 0.0, NEG_INF))

            @pl.when(jnp.logical_not(needs_mask))
            def _():
                step(off, None)
        return carry

    lax.fori_loop(0, nsub_ref[p], sub_body, 0)

    @pl.when(last_ref[p] == 1)
    def _():
        for hh in range(hd):
            a = acc_scr[hh]
            cat_scr[hh * HEAD_V:(hh + 1) * HEAD_V, :] = (a[:HEAD_V] / a[HEAD_V:HEAD_V + 1]).astype(BF16)
        o_ref[...] = x_ref[...] + _dot(wot_ref[...], cat_scr[...]).T


def _causal_pairs(nq, tq, tk, sub):
    qi, kj, last, nsub = [], [], [], []
    per = tk // sub
    for q in range(nq):
        vis_sub = -(-((q + 1) * tq) // sub)
        nvis = -(-vis_sub // per)
        for k in range(nvis):
            qi.append(q); kj.append(k); last.append(int(k == nvis - 1)); nsub.append(min(per, vis_sub - k * per))
    as_arr = lambda v: jnp.asarray(np.array(v, np.int32))
    return as_arr(qi), as_arr(kj), as_arr(last), as_arr(nsub)


def _attn_t(x, qt, kh, vt, bias, wot, batch, seq, sub, name):
    t, d = x.shape
    hd, dk, _ = qt.shape
    tq, tk, sub = min(512, seq), min(1024, seq), min(sub, seq)
    nq, nk = seq // tq, seq // tk
    qi, kj, last, nsub = _causal_pairs(nq, tq, tk, sub)
    use_bias = bias is not None
    in_specs = [
        pl.BlockSpec((hd, dk, tq), lambda b, p, qi, kj, la, ns: (0, 0, b * nq + qi[p])),
        pl.BlockSpec((hd, tk, dk), lambda b, p, qi, kj, la, ns: (0, b * nk + kj[p], 0)),
        pl.BlockSpec((hd, V_ROWS, tk), lambda b, p, qi, kj, la, ns: (0, 0, b * nk + kj[p])),
    ]
    args = [qt, kh, vt]
    if use_bias:
        in_specs.append(pl.BlockSpec((tk, tq), lambda b, p, qi, kj, la, ns: (kj[p], b * nq + qi[p])))
        args.append(bias)
    in_specs += [
        pl.BlockSpec((tq, d), lambda b, p, qi, kj, la, ns: (b * nq + qi[p], 0)),
        pl.BlockSpec(wot.shape, lambda b, p, qi, kj, la, ns: (0, 0)),
    ]
    args += [x, wot]
    grid_spec = pltpu.PrefetchScalarGridSpec(
        num_scalar_prefetch=4,
        grid=(batch, int(qi.shape[0])),
        in_specs=in_specs,
        out_specs=pl.BlockSpec((tq, d), lambda b, p, qi, kj, la, ns: (b * nq + qi[p], 0)),
        scratch_shapes=[
            pltpu.VMEM((hd, 1, tq), F32), pltpu.VMEM((hd, V_ROWS, tq), F32), pltpu.VMEM((hd * HEAD_V, tq), BF16),
        ],
    )
    return pl.pallas_call(
        functools.partial(_attn_t_kernel, tq=tq, tk=tk, sub=sub, use_bias=use_bias),
        grid_spec=grid_spec,
        out_shape=jax.ShapeDtypeStruct((t, d), F32),
        compiler_params=_params("parallel", "arbitrary"),
        name=name,
    )(qi, kj, last, nsub, *args)


def _rope_tables(pos):
    half = MLA_ROPE // 2
    inv = ROPE_BASE ** (-jnp.arange(half, dtype=F32) / half)
    ang = pos.astype(F32)[:, None] * inv[None, :]
    cos, sin = jnp.cos(ang), jnp.sin(ang)
    cos_k = jnp.concatenate([cos, cos], axis=1)
    sin_k = jnp.concatenate([-sin, sin], axis=1)
    return jnp.tile(cos_k, (1, MLA_HEADS)), jnp.tile(sin_k, (1, MLA_HEADS)), cos_k, sin_k


def _swap_halves(w, group):
    shp = w.shape
    wr = w.reshape(shp[:-1] + (shp[-1] // group, 2, group // 2))
    return wr[..., ::-1, :].reshape(shp)


def _mla_weights(w_dq, w_uq, w_dkv, w_ukv, w_o):
    hd = MLA_HEADS
    cast = lambda a: a.astype(BF16)
    w_uq, w_dkv, w_ukv = cast(w_uq), cast(w_dkv), cast(w_ukv)
    wqn = w_uq[:, :, :MLA_NOPE].reshape(MLA_Q_LORA, hd * MLA_NOPE)
    wqr = w_uq[:, :, MLA_NOPE:].reshape(MLA_Q_LORA, hd * MLA_ROPE)
    wqs = _swap_halves(wqr, MLA_ROPE)
    wkc, wkr = w_dkv[:, :MLA_KV_LORA], w_dkv[:, MLA_KV_LORA:]
    wks = _swap_halves(wkr, MLA_ROPE)
    wuk = jnp.transpose(w_ukv[:, :, :MLA_NOPE], (1, 2, 0))
    zeros = jnp.zeros_like(wuk)
    even = jnp.concatenate([wuk, zeros], axis=1)
    odd = jnp.concatenate([zeros, wuk], axis=1)
    wuk2 = jnp.where((jnp.arange(hd) % 2 == 0)[:, None, None], even, odd)
    wuv = jnp.transpose(w_ukv[:, :, MLA_NOPE:], (1, 0, 2))
    d = w_dkv.shape[0]
    pad_r = lambda a: jnp.concatenate([a, jnp.zeros((d, LANES - MLA_ROPE), BF16)], axis=1)
    wukp = jnp.concatenate([jnp.zeros((MLA_KV_LORA, hd, MLA_ROPE), BF16), w_ukv[:, :, :MLA_NOPE],
                            jnp.zeros((MLA_KV_LORA, hd, LANES - MLA_ROPE - MLA_NOPE), BF16)], axis=2)
    return dict(wdq=cast(w_dq), wqn=wqn, wqr=wqr, wqs=wqs, wkc=wkc, wkr=wkr, wks=wks, wuk=wuk2, wuv=wuv, wo=cast(w_o),
                wqnt=wqn.T, wqrt=wqr.T, wqst=wqs.T, wkrp=pad_r(wkr), wksp=pad_r(wks),
                wukp=wukp.reshape(MLA_KV_LORA, hd * LANES), wuvt=wuv.transpose(0, 2, 1).reshape(hd * MLA_V, MLA_KV_LORA),
                wot=cast(w_o).T)


def _mla_proj_kernel(x_ref, gm_ref, wdq_ref, gq_ref, wqn_ref, wqr_ref, wqs_ref, wkc_ref, wkr_ref, wks_ref,
                     gkv_ref, wuk_ref, cq_ref, sq_ref, ck_ref, sk_ref,
                     ql_ref, qr_ref, ckv_ref, kr_ref, ckvb_ref, krb_ref):
    h = _rms(x_ref[...], gm_ref[...]).astype(BF16)
    cq = _rms(_dot(h, wdq_ref[...]), gq_ref[...]).astype(BF16)
    qn = _dot(cq, wqn_ref[...])
    qr = ((_dot(cq, wqr_ref[...]) * cq_ref[...] + _dot(cq, wqs_ref[...]) * sq_ref[...]) * MLA_SCALE).astype(BF16)
    for hh in range(MLA_HEADS):
        pair = qn[:, (hh // 2) * LANES:(hh // 2 + 1) * LANES].astype(BF16)
        ql_ref[hh] = (_dot(pair, wuk_ref[hh]) * MLA_SCALE).astype(BF16)
        qr_ref[hh] = qr[:, hh * MLA_ROPE:(hh + 1) * MLA_ROPE]
    ckv = _rms(_dot(h, wkc_ref[...]), gkv_ref[...])
    ckv_ref[...] = ckv
    ckvb_ref[...] = ckv.astype(BF16)
    kr = _dot(h, wkr_ref[...]) * ck_ref[...] + _dot(h, wks_ref[...]) * sk_ref[...]
    kr_ref[...] = kr
    krb_ref[...] = kr.astype(BF16)


def _mla_proj(x, g_mix, wts, g_q, g_kv, tables):
    t, d = x.shape
    tm = min(256, t)
    cos_q, sin_q, cos_k, sin_k = tables
    nrep = cos_q.shape[0] // tm
    tab = lambda wdt: pl.BlockSpec((tm, wdt), lambda i: (i % nrep, 0))
    hd = MLA_HEADS
    row = lambda wdt: pl.BlockSpec((tm, wdt), lambda i: (i, 0))
    hm = lambda wdt: pl.BlockSpec((hd, tm, wdt), lambda i: (0, i, 0))
    return pl.pallas_call(
        _mla_proj_kernel,
        grid=(t // tm,),
        in_specs=[
            row(d), _whole((1, d)), _whole(wts["wdq"].shape), _whole((1, MLA_Q_LORA)),
            _whole(wts["wqn"].shape), _whole(wts["wqr"].shape), _whole(wts["wqs"].shape),
            _whole(wts["wkc"].shape), _whole(wts["wkr"].shape), _whole(wts["wks"].shape),
            _whole((1, MLA_KV_LORA)), _whole(wts["wuk"].shape),
            tab(hd * MLA_ROPE), tab(hd * MLA_ROPE), tab(MLA_ROPE), tab(MLA_ROPE),
        ],
        out_specs=[hm(MLA_KV_LORA), hm(MLA_ROPE), row(MLA_KV_LORA), row(MLA_ROPE), row(MLA_KV_LORA), row(MLA_ROPE)],
        out_shape=[
            jax.ShapeDtypeStruct((hd, t, MLA_KV_LORA), BF16), jax.ShapeDtypeStruct((hd, t, MLA_ROPE), BF16),
            jax.ShapeDtypeStruct((t, MLA_KV_LORA), F32), jax.ShapeDtypeStruct((t, MLA_ROPE), F32),
            jax.ShapeDtypeStruct((t, MLA_KV_LORA), BF16), jax.ShapeDtypeStruct((t, MLA_ROPE), BF16),
        ],
        compiler_params=_params("parallel"),
        name="mla_proj",
    )(x, g_mix.reshape(1, d), wts["wdq"], g_q.reshape(1, -1), wts["wqn"], wts["wqr"], wts["wqs"],
      wts["wkc"], wts["wkr"], wts["wks"], g_kv.reshape(1, -1), wts["wuk"], cos_q, sin_q, cos_k, sin_k)


def _mla_proj_t_kernel(x_ref, gm_ref, wdq_ref, gq_ref, wqnt_ref, wqrt_ref, wqst_ref, wkc_ref, wkrp_ref, wksp_ref,
                       gkv_ref, wukp_ref, wuvt_ref, cqt_ref, sqt_ref, ckp_ref, skp_ref,
                       qt_ref, kh_ref, vt_ref, ckv_ref, kr_ref):
    tm = x_ref.shape[0]
    hd = MLA_HEADS
    qscale = MLA_SCALE * LOG2E
    h = _rms(x_ref[...], gm_ref[...]).astype(BF16)
    cq_raw = _dot(h, wdq_ref[...])
    ckv_raw = _dot(h, wkc_ref[...])
    kr_a = _dot(h, wkrp_ref[...])
    kr_b = _dot(h, wksp_ref[...])
    cq = _rms(cq_raw, gq_ref[...]).astype(BF16)
    qnt_raw = _dot_t(wqnt_ref[...], cq)
    qrt_a = _dot_t(wqrt_ref[...], cq)
    qrt_b = _dot_t(wqst_ref[...], cq)
    ckv = _rms(ckv_raw, gkv_ref[...])
    ckv_ref[...] = ckv
    cb = ckv.astype(BF16)
    kn = _dot(cb, wukp_ref[...])
    vt = _dot_t(wuvt_ref[...], cb).astype(BF16)
    qnt = (qnt_raw * qscale).astype(BF16)
    qrt = ((qrt_a * cqt_ref[...] + qrt_b * sqt_ref[...]) * qscale).astype(BF16)
    zeros = jnp.zeros((LANES - MLA_ROPE - MLA_NOPE, tm), BF16)
    krp = kr_a * ckp_ref[...] + kr_b * skp_ref[...]
    kr_ref[...] = krp[:, :MLA_ROPE]
    ones = jnp.ones((BF16_ROWS, tm), BF16)
    for hh in range(hd):
        qt_ref[hh, 0:MLA_ROPE, :] = qrt[hh * MLA_ROPE:(hh + 1) * MLA_ROPE]
        qt_ref[hh, MLA_ROPE:MLA_ROPE + MLA_NOPE, :] = qnt[hh * MLA_NOPE:(hh + 1) * MLA_NOPE]
        qt_ref[hh, MLA_ROPE + MLA_NOPE:, :] = zeros
        kh_ref[hh] = (kn[:, hh * LANES:(hh + 1) * LANES] + krp).astype(BF16)
        vt_ref[hh, 0:MLA_V, :] = vt[hh * MLA_V:(hh + 1) * MLA_V]
        vt_ref[hh, MLA_V:, :] = ones


def _mla_proj_t(x, g_mix, wts, g_q, g_kv, tables, seq):
    t, d = x.shape
    tm = min(256, seq)
    hd = MLA_HEADS
    cos_q, sin_q, cos_k, sin_k = tables
    pad = lambda a: jnp.concatenate([a, jnp.zeros((seq, LANES - MLA_ROPE), F32)], axis=1)
    nrep = seq // tm
    row = lambda wdt: pl.BlockSpec((tm, wdt), lambda i: (i, 0))
    names = ("wdq", "wqnt", "wqrt", "wqst", "wkc", "wkrp", "wksp", "wukp", "wuvt")
    w = {n: wts[n] for n in names}
    return pl.pallas_call(
        _mla_proj_t_kernel,
        grid=(t // tm,),
        in_specs=[
            row(d), _whole((1, d)), _whole(w["wdq"].shape), _whole((1, MLA_Q_LORA)),
            _whole(w["wqnt"].shape), _whole(w["wqrt"].shape), _whole(w["wqst"].shape),
            _whole(w["wkc"].shape), _whole(w["wkrp"].shape), _whole(w["wksp"].shape),
            _whole((1, MLA_KV_LORA)), _whole(w["wukp"].shape), _whole(w["wuvt"].shape),
            pl.BlockSpec((hd * MLA_ROPE, tm), lambda i: (0, i % nrep)),
            pl.BlockSpec((hd * MLA_ROPE, tm), lambda i: (0, i % nrep)),
            pl.BlockSpec((tm, LANES), lambda i: (i % nrep, 0)),
            pl.BlockSpec((tm, LANES), lambda i: (i % nrep, 0)),
        ],
        out_specs=[
            pl.BlockSpec((hd, LANES, tm), lambda i: (0, 0, i)),
            pl.BlockSpec((hd, tm, LANES), lambda i: (0, i, 0)),
            pl.BlockSpec((hd, V_ROWS, tm), lambda i: (0, 0, i)),
            row(MLA_KV_LORA), row(MLA_ROPE),
        ],
        out_shape=[
            jax.ShapeDtypeStruct((hd, LANES, t), BF16), jax.ShapeDtypeStruct((hd, t, LANES), BF16),
            jax.ShapeDtypeStruct((hd, V_ROWS, t), BF16),
            jax.ShapeDtypeStruct((t, MLA_KV_LORA), F32), jax.ShapeDtypeStruct((t, MLA_ROPE), F32),
        ],
        compiler_params=_params("parallel"),
        name="mla_proj_t",
    )(x, g_mix.reshape(1, d), w["wdq"], g_q.reshape(1, -1), w["wqnt"], w["wqrt"], w["wqst"],
      w["wkc"], w["wkrp"], w["wksp"], g_kv.reshape(1, -1), w["wukp"], w["wuvt"],
      cos_q.T, sin_q.T, pad(cos_k), pad(sin_k))


def _mla_epilogue(o_lat, x_ref, wuv_ref, wo_ref, o_ref, cat_scr, tq):
    ob = o_lat.astype(BF16)
    for hh in range(MLA_HEADS):
        cat_scr[:, hh * MLA_V:(hh + 1) * MLA_V] = _dot(ob[hh * tq:(hh + 1) * tq], wuv_ref[hh]).astype(BF16)
    o_ref[...] = x_ref[...] + _dot(cat_scr[...], wo_ref[...])


def _mla_samp_kernel(ql_ref, qr_ref, cc_ref, cr_ref, cn_ref, rn_ref, x_ref, wuv_ref, wo_ref, o_ref, cat_scr, *, n_q):
    b = pl.program_id(0)
    hd = MLA_HEADS
    ql = ql_ref[...].reshape(hd * n_q, MLA_KV_LORA)
    qr = qr_ref[...].reshape(hd * n_q, MLA_ROPE)
    cc = cc_ref[0].astype(BF16)
    cr = cr_ref[0].astype(BF16)
    cn = cn_ref[...]
    s1 = _dot_t(ql, cc) + _dot_t(qr, cr)
    s2 = _dot_t(ql, cn) + _dot_t(qr, rn_ref[...])
    own = _div(lax.broadcasted_iota(I32, (1, LANES), 1), n_q) == _mod(b, LANES // n_q)
    s2 = jnp.where(own, s2, NEG_INF)
    m = jnp.maximum(jnp.max(s1, axis=1, keepdims=True), jnp.max(s2, axis=1, keepdims=True))
    p1 = jnp.exp(s1 - m)
    p2 = jnp.exp(s2 - m)
    l = jnp.sum(p1, axis=1, keepdims=True) + jnp.sum(p2, axis=1, keepdims=True)
    o_lat = (_dot(p1.astype(BF16), cc) + _dot(p2.astype(BF16), cn)) / l
    _mla_epilogue(o_lat, x_ref, wuv_ref, wo_ref, o_ref, cat_scr, n_q)


def _mla_attn_sample(x, ql, qr, ckvb, krb, cache_c, cache_r, wts, n_b, n_q):
    t, d = x.shape
    hd = MLA_HEADS
    past = cache_c.shape[1]
    per = LANES // n_q
    return pl.pallas_call(
        functools.partial(_mla_samp_kernel, n_q=n_q),
        grid=(n_b,),
        in_specs=[
            pl.BlockSpec((hd, n_q, MLA_KV_LORA), lambda b: (0, b, 0)),
            pl.BlockSpec((hd, n_q, MLA_ROPE), lambda b: (0, b, 0)),
            pl.BlockSpec((1, past, MLA_KV_LORA), lambda b: (b, 0, 0)),
            pl.BlockSpec((1, past, MLA_ROPE), lambda b: (b, 0, 0)),
            pl.BlockSpec((LANES, MLA_KV_LORA), lambda b: (b // per, 0)),
            pl.BlockSpec((LANES, MLA_ROPE), lambda b: (b // per, 0)),
            pl.BlockSpec((n_q, d), lambda b: (b, 0)),
            _whole(wts["wuv"].shape), _whole(wts["wo"].shape),
        ],
        out_specs=pl.BlockSpec((n_q, d), lambda b: (b, 0)),
        out_shape=jax.ShapeDtypeStruct((t, d), F32),
        scratch_shapes=[pltpu.VMEM((n_q, hd * MLA_V), BF16)],
        compiler_params=_params("parallel"),
        name="mla_attn_sample",
    )(ql, qr, cache_c, cache_r, ckvb, krb, x, wts["wuv"], wts["wo"])


def _dsa_proj_kernel(x_ref, gm_ref, wqkv_ref, wqi_ref, wkw_ref, gki_ref,
                     k_ref, v_ref, ki_ref, qb_ref, kb_ref, vb_ref, qib_ref, kib_ref, wi_ref):
    hdm = DSA_HEADS * DSA_HEAD_DIM
    h = _rms(x_ref[...], gm_ref[...]).astype(BF16)
    qkv = _dot(h, wqkv_ref[...])
    q = qkv[:, :hdm] * DSA_SCALE
    k = qkv[:, hdm:2 * hdm]
    v = qkv[:, 2 * hdm:]
    k_ref[...] = k
    v_ref[...] = v
    qi = _dot(h, wqi_ref[...]) * IDX_SCALE
    kw = _dot(h, wkw_ref[...])
    ki = _rms(kw[:, :IDX_DIM], gki_ref[...])
    ki_ref[...] = ki
    kib_ref[...] = ki.astype(BF16)
    wi_ref[...] = kw[:, IDX_DIM:IDX_DIM + IDX_HEADS] * (IDX_HEADS ** -0.5)
    qb_ref[...] = q.astype(BF16)
    kb_ref[...] = k.astype(BF16)
    vb_ref[...] = v.astype(BF16)
    for hh in range(IDX_HEADS):
        qib_ref[hh] = qi[:, hh * IDX_DIM:(hh + 1) * IDX_DIM].astype(BF16)


def _dsa_proj(x, g_mix, w_qkv, w_qidx, w_kw, g_kidx):
    t, d = x.shape
    tm = min(512, t)
    hdm = DSA_HEADS * DSA_HEAD_DIM
    row = lambda wdt: pl.BlockSpec((tm, wdt), lambda i: (i, 0))
    return pl.pallas_call(
        _dsa_proj_kernel,
        grid=(t // tm,),
        in_specs=[row(d), _whole((1, d)), _whole(w_qkv.shape), _whole(w_qidx.shape), _whole(w_kw.shape),
                  _whole((1, IDX_DIM))],
        out_specs=[row(hdm), row(hdm), row(IDX_DIM), row(hdm), row(hdm), row(hdm),
                   pl.BlockSpec((IDX_HEADS, tm, IDX_DIM), lambda i: (0, i, 0)), row(IDX_DIM), row(IDX_HEADS)],
        out_shape=[
            jax.ShapeDtypeStruct((t, hdm), F32), jax.ShapeDtypeStruct((t, hdm), F32),
            jax.ShapeDtypeStruct((t, IDX_DIM), F32), jax.ShapeDtypeStruct((t, hdm), BF16),
            jax.ShapeDtypeStruct((t, hdm), BF16), jax.ShapeDtypeStruct((t, hdm), BF16),
            jax.ShapeDtypeStruct((IDX_HEADS, t, IDX_DIM), BF16), jax.ShapeDtypeStruct((t, IDX_DIM), BF16),
            jax.ShapeDtypeStruct((t, IDX_HEADS), F32),
        ],
        compiler_params=_params("parallel"),
        name="dsa_proj",
    )(x, g_mix.reshape(1, d), w_qkv, w_qidx, w_kw, g_kidx.reshape(1, IDX_DIM))


def _dsa_proj_t_kernel(x_ref, gm_ref, wkv_ref, wqt_ref, wvt_ref, wqit_ref, wkw_ref, wwt_ref, gki_ref,
                       k_ref, v_ref, ki_ref, qt_ref, kh_ref, vt_ref, qit_ref, kib_ref, wit_ref):
    tm = x_ref.shape[0]
    hdm = DSA_HEADS * DSA_HEAD_DIM
    dh = DSA_HEAD_DIM
    h = _rms(x_ref[...], gm_ref[...]).astype(BF16)
    kv = _dot(h, wkv_ref[...])
    k = kv[:, :hdm]
    k_ref[...] = k
    v_ref[...] = kv[:, hdm:]
    kw = _dot(h, wkw_ref[...])
    ki = _rms(kw[:, :IDX_DIM], gki_ref[...])
    ki_ref[...] = ki
    kib_ref[...] = ki.astype(BF16)
    wit_ref[...] = _dot_t(wwt_ref[...], h)[:IDX_HEADS] * (IDX_HEADS ** -0.5)
    qt = (_dot_t(wqt_ref[...], h) * (DSA_SCALE * LOG2E)).astype(BF16)
    vt = _dot_t(wvt_ref[...], h).astype(BF16)
    qit = (_dot_t(wqit_ref[...], h) * IDX_SCALE).astype(BF16)
    ones = jnp.ones((BF16_ROWS, tm), BF16)
    for hh in range(DSA_HEADS):
        qt_ref[hh] = qt[hh * dh:(hh + 1) * dh]
        kh_ref[hh] = k[:, hh * dh:(hh + 1) * dh].astype(BF16)
        vt_ref[hh, 0:dh, :] = vt[hh * dh:(hh + 1) * dh]
        vt_ref[hh, dh:, :] = ones
    for hh in range(IDX_HEADS):
        qit_ref[hh] = qit[hh * IDX_DIM:(hh + 1) * IDX_DIM]


def _dsa_proj_t(x, g_mix, w, g_kidx):
    t, d = x.shape
    tm = min(256, t)
    hd, dh = DSA_HEADS, DSA_HEAD_DIM
    hdm = hd * dh
    row = lambda wdt: pl.BlockSpec((tm, wdt), lambda i: (i, 0))
    names = ("wkv", "wqt", "wvt", "wqit", "wkw", "wwt")
    return pl.pallas_call(
        _dsa_proj_t_kernel,
        grid=(t // tm,),
        in_specs=[row(d), _whole((1, d))] + [_whole(w[n].shape) for n in names] + [_whole((1, IDX_DIM))],
        out_specs=[
            row(hdm), row(hdm), row(IDX_DIM),
            pl.BlockSpec((hd, dh, tm), lambda i: (0, 0, i)),
            pl.BlockSpec((hd, tm, dh), lambda i: (0, i, 0)),
            pl.BlockSpec((hd, V_ROWS, tm), lambda i: (0, 0, i)),
            pl.BlockSpec((IDX_HEADS, IDX_DIM, tm), lambda i: (0, 0, i)),
            row(IDX_DIM),
            pl.BlockSpec((IDX_HEADS, tm), lambda i: (0, i)),
        ],
        out_shape=[
            jax.ShapeDtypeStruct((t, hdm), F32), jax.ShapeDtypeStruct((t, hdm), F32),
            jax.ShapeDtypeStruct((t, IDX_DIM), F32),
            jax.ShapeDtypeStruct((hd, dh, t), BF16), jax.ShapeDtypeStruct((hd, t, dh), BF16),
            jax.ShapeDtypeStruct((hd, V_ROWS, t), BF16),
            jax.ShapeDtypeStruct((IDX_HEADS, IDX_DIM, t), BF16), jax.ShapeDtypeStruct((t, IDX_DIM), BF16),
            jax.ShapeDtypeStruct((IDX_HEADS, t), F32),
        ],
        compiler_params=_params("parallel"),
        name="dsa_proj_t",
    )(x, g_mix.reshape(1, d), *[w[n] for n in names], g_kidx.reshape(1, IDX_DIM))


def _sort_key(score):
    bits = lax.bitcast_convert_type(score, I32)
    return jnp.where(bits < 0, bits ^ jnp.int32(0x7FFFFFFF), bits)


def _kth_largest_key(count, topk):
    c0 = count(lambda key, idx: jnp.where(key >= 0, 1, 0))
    t0 = jnp.where(c0 >= topk, jnp.int32(0), jnp.int32(INT_MIN))

    def bit_body(i, t):
        cand = t + lax.shift_left(jnp.int32(1), jnp.int32(30) - i)
        c = count(lambda key, idx: jnp.where(key >= cand, 1, 0))
        return jnp.where(c >= topk, cand, t)

    return lax.fori_loop(0, 31, bit_body, t0)


def _tie_cutoff(count, thr, rem, nbits):
    def bit_body(i, j):
        cand = j + lax.shift_left(jnp.int32(1), jnp.int32(nbits - 1) - i)
        c = count(lambda key, idx: jnp.where(key == thr, jnp.where(idx < cand, 1, 0), 0))
        return jnp.where(c < rem, cand, j)

    return lax.fori_loop(0, nbits, bit_body, jnp.zeros_like(thr))


def _select(key, idx, thr, cut):
    chosen = jnp.where(key > thr, 1, jnp.where(key == thr, jnp.where(idx <= cut, 1, 0), 0))
    return jnp.where(key > KEY_NEG_INF, chosen, 0) > 0


def _dsa_select_kernel(qit_ref, wit_ref, ki_ref, tri_ref, bias_ref, key_scr, *, tq, tk, topk):
    qb = pl.program_id(1)
    seq = ki_ref.shape[0]
    nk = seq // tk
    nvis = _div((qb + 1) * tq + tk - 1, tk)
    q_chunk = _div(qb * tq + lax.broadcasted_iota(I32, (1, tq), 1), CHUNK)
    w = wit_ref[...]
    row_idx = lax.broadcasted_iota(I32, (tk, 1), 0)

    def score_body(j, carry):
        off = pl.multiple_of(j * tk, tk)
        kb = ki_ref[pl.ds(off, tk), :]
        acc = jnp.zeros((tk, tq), F32)
        for hh in range(IDX_HEADS):
            acc = acc + w[hh:hh + 1, :] * jnp.maximum(_dot(kb, qit_ref[hh]), 0.0)
        vis = _div(off + row_idx, CHUNK) <= q_chunk
        key_scr[pl.ds(off, tk), :] = _sort_key(jnp.where(vis, acc, NEG_INF))
        return carry

    lax.fori_loop(0, nvis, score_body, 0)

    def count(hit):
        def body(j, c):
            off = pl.multiple_of(j * tk, tk)
            hits = hit(key_scr[pl.ds(off, tk), :], off + row_idx)
            return c + jnp.sum(hits.reshape(tk // 8, 8, tq), axis=0)
        c = lax.fori_loop(0, nvis, body, jnp.zeros((8, tq), I32))
        return jnp.sum(c, axis=0, keepdims=True)

    thr = _kth_largest_key(count, topk)
    rem = (topk - count(lambda key, idx: jnp.where(key > thr, 1, 0))).astype(F32)
    tri = tri_ref[...]

    def out_body(j, seen):
        off = pl.multiple_of(j * tk, tk)
        key = key_scr[pl.ds(off, tk), :]
        tie = jnp.where(key == thr, 1.0, 0.0)
        rank = seen + _dot(tri, tie.astype(BF16))
        keep = jnp.where(key > thr, 1.0, jnp.where(rank <= rem, tie, 0.0))
        keep = jnp.where(key > KEY_NEG_INF, keep, 0.0)
        bias_ref[pl.ds(off, tk), :] = jnp.where(keep > 0.0, 0.0, NEG_INF).astype(BF16)
        return rank[tk - 1:tk, :]

    lax.fori_loop(0, nvis, out_body, jnp.zeros((1, tq), F32))

    def fill_body(j, carry):
        off = pl.multiple_of(j * tk, tk)
        bias_ref[pl.ds(off, tk), :] = jnp.full((tk, tq), NEG_INF, BF16)
        return carry

    lax.fori_loop(nvis, nk, fill_body, 0)


def _dsa_select(qit, wit, kib, batch, seq, topk):
    tq, tk = min(256, seq), min(512, seq)
    nq = seq // tq
    t = batch * seq
    return pl.pallas_call(
        functools.partial(_dsa_select_kernel, tq=tq, tk=tk, topk=topk),
        grid=(batch, nq),
        in_specs=[
            pl.BlockSpec((IDX_HEADS, IDX_DIM, tq), lambda b, q: (0, 0, b * nq + q)),
            pl.BlockSpec((IDX_HEADS, tq), lambda b, q: (0, b * nq + q)),
            pl.BlockSpec((seq, IDX_DIM), lambda b, q: (b, 0)),
            _whole((tk, tk)),
        ],
        out_specs=pl.BlockSpec((seq, tq), lambda b, q: (0, b * nq + q)),
        out_shape=jax.ShapeDtypeStruct((seq, t), BF16),
        scratch_shapes=[pltpu.VMEM((seq, tq), I32)],
        compiler_params=_params("parallel", "arbitrary"),
        name="dsa_select",
    )(qit, wit, kib, jnp.tril(jnp.ones((tk, tk), BF16)))


def _dsa_samp_kernel(q_ref, qi_ref, wi_ref, kc_ref, vc_ref, kic_ref, kn_ref, vn_ref, kin_ref, x_ref, wo_ref, o_ref,
                     *, n_q, topk):
    b = pl.program_id(0)
    hd, dh = DSA_HEADS, DSA_HEAD_DIM
    past = kc_ref.shape[1]
    n_keys = past + LANES
    own = _div(lax.broadcasted_iota(I32, (1, LANES), 1), n_q) == _mod(b, LANES // n_q)

    qi = qi_ref[...].reshape(IDX_HEADS * n_q, IDX_DIM)
    lg1 = jnp.maximum(_dot_t(qi, kic_ref[0].astype(BF16)), 0.0)
    lg2 = jnp.maximum(_dot_t(qi, kin_ref[...]), 0.0)
    w = wi_ref[...]
    sc1 = jnp.zeros((n_q, past), F32)
    sc2 = jnp.zeros((n_q, LANES), F32)
    for hh in range(IDX_HEADS):
        sc1 = sc1 + w[:, hh:hh + 1] * lg1[hh * n_q:(hh + 1) * n_q]
        sc2 = sc2 + w[:, hh:hh + 1] * lg2[hh * n_q:(hh + 1) * n_q]
    key = _sort_key(jnp.concatenate([sc1, jnp.where(own, sc2, NEG_INF)], axis=1))
    idx = lax.broadcasted_iota(I32, (1, n_keys), 1)

    count = lambda hit: jnp.sum(hit(key, idx), axis=1, keepdims=True)
    thr = _kth_largest_key(count, topk)
    rem = topk - count(lambda k_, i_: jnp.where(k_ > thr, 1, 0))
    cut = _tie_cutoff(count, thr, rem, int(n_keys - 1).bit_length())
    bias = jnp.where(_select(key, idx, thr, cut), 0.0, NEG_INF)
    bias = jnp.concatenate([bias] * hd, axis=0)

    lane_head = _div(lax.broadcasted_iota(I32, (1, hd * dh), 1), dh)
    qf = q_ref[...].astype(F32)
    qbd = jnp.concatenate([jnp.where(lane_head == hh, qf, 0.0) for hh in range(hd)], axis=0).astype(BF16)
    kc = kc_ref[0].astype(BF16)
    vc = vc_ref[0].astype(BF16)
    s1 = _dot_t(qbd, kc) + bias[:, :past]
    s2 = _dot_t(qbd, kn_ref[...]) + bias[:, past:]
    m = jnp.maximum(jnp.max(s1, axis=1, keepdims=True), jnp.max(s2, axis=1, keepdims=True))
    p1 = jnp.exp(s1 - m)
    p2 = jnp.exp(s2 - m)
    l = jnp.sum(p1, axis=1, keepdims=True) + jnp.sum(p2, axis=1, keepdims=True)
    o_all = (_dot(p1.astype(BF16), vc) + _dot(p2.astype(BF16), vn_ref[...])) / l
    out = jnp.zeros((n_q, hd * dh), F32)
    for hh in range(hd):
        out = out + jnp.where(lane_head == hh, o_all[hh * n_q:(hh + 1) * n_q], 0.0)
    o_ref[...] = x_ref[...] + _dot(out.astype(BF16), wo_ref[...])


def _dsa_attn_sample(x, qb, qib, wi, kb, vb, kib, cache_k, cache_v, cache_ki, w_o, n_b, n_q, topk):
    t, d = x.shape
    hdm = DSA_HEADS * DSA_HEAD_DIM
    past = cache_k.shape[1]
    per = LANES // n_q
    return pl.pallas_call(
        functools.partial(_dsa_samp_kernel, n_q=n_q, topk=topk),
        grid=(n_b,),
        in_specs=[
            pl.BlockSpec((n_q, hdm), lambda b: (b, 0)),
            pl.BlockSpec((IDX_HEADS, n_q, IDX_DIM), lambda b: (0, b, 0)),
            pl.BlockSpec((n_q, IDX_HEADS), lambda b: (b, 0)),
            pl.BlockSpec((1, past, hdm), lambda b: (b, 0, 0)),
            pl.BlockSpec((1, past, hdm), lambda b: (b, 0, 0)),
            pl.BlockSpec((1, past, IDX_DIM), lambda b: (b, 0, 0)),
            pl.BlockSpec((LANES, hdm), lambda b: (b // per, 0)),
            pl.BlockSpec((LANES, hdm), lambda b: (b // per, 0)),
            pl.BlockSpec((LANES, IDX_DIM), lambda b: (b // per, 0)),
            pl.BlockSpec((n_q, d), lambda b: (b, 0)),
            _whole(w_o.shape),
        ],
        out_specs=pl.BlockSpec((n_q, d), lambda b: (b, 0)),
        out_shape=jax.ShapeDtypeStruct((t, d), F32),
        compiler_params=_params("parallel"),
        name="dsa_attn_sample",
    )(qb, qib, wi, cache_k.reshape(n_b, past, hdm), cache_v.reshape(n_b, past, hdm), cache_ki, kb, vb, kib, x, w_o)


def kernel(x_prompt, x_sample, cache_mla_ckv, cache_mla_krope, cache_dsa_k, cache_dsa_v, cache_dsa_kidx, norm_mix, norm_ffn, norm_final, mla_w_dq, mla_g_q, mla_w_uq, mla_w_dkv, mla_g_kv, mla_w_ukv, mla_w_o, cmlp_w_in, cmlp_ln_g, cmlp_ln_b, cmlp_w_s, cmlp_b_s, cmlp_w_out, dsa_w_qkv, dsa_w_o, dsa_w_qidx, dsa_w_kidx, dsa_g_kidx, dsa_w_widx, ffn_w_in, ffn_w_out):
    batch, seq, d = x_prompt.shape
    n_b, n_q, _ = x_sample.shape
    past = cache_mla_ckv.shape[2]
    depth = norm_mix.shape[0]
    xp = x_prompt.reshape(batch * seq, d)
    xs = x_sample.reshape(n_b * n_q, d)
    tab_p = _rope_tables(jnp.arange(seq))
    tab_s = tuple(jnp.tile(a, (n_b, 1)) for a in _rope_tables(past + jnp.arange(n_q)))
    cast = lambda a: a.astype(BF16)
    outs = {k: [] for k in ("ckv_p", "kr_p", "ckv_s", "kr_s", "cv_s", "dk_p", "dv_p", "di_p", "dk_s", "dv_s", "di_s")}
    for i in range(depth):
        kind, j = i % 3, i // 3
        if kind == 0:
            wts = _mla_weights(mla_w_dq[j], mla_w_uq[j], mla_w_dkv[j], mla_w_ukv[j], mla_w_o[j])
            qt, kh, vt, ckv, kr = _mla_proj_t(xp, norm_mix[i], wts, mla_g_q[j], mla_g_kv[j], tab_p, seq)
            xp = _attn_t(xp, qt, kh, vt, None, wts["wot"], batch, seq, 256, "mla_attn_prompt")
            outs["ckv_p"].append(ckv.reshape(batch, seq, -1)); outs["kr_p"].append(kr.reshape(batch, seq, -1))
            ql, qr, ckv, kr, ckvb, krb = _mla_proj(xs, norm_mix[i], wts, mla_g_q[j], mla_g_kv[j], tab_s)
            xs = _mla_attn_sample(xs, ql, qr, ckvb, krb, cache_mla_ckv[j], cache_mla_krope[j], wts, n_b, n_q)
            outs["ckv_s"].append(ckv.reshape(n_b, n_q, -1)); outs["kr_s"].append(kr.reshape(n_b, n_q, -1))
        elif kind == 1:
            w_in, w_out = cast(cmlp_w_in[j]), cast(cmlp_w_out[j])
            xp, _ = _cmlp(xp, norm_mix[i], w_in, cmlp_ln_g[j], cmlp_ln_b[j], cmlp_w_s[j], cmlp_b_s[j], w_out,
                          min(seq, CMLP_CHUNK), False)
            xs, v_s = _cmlp(xs, norm_mix[i], w_in, cmlp_ln_g[j], cmlp_ln_b[j], cmlp_w_s[j], cmlp_b_s[j], w_out,
                            min(n_q, CMLP_CHUNK), True)
            outs["cv_s"].append(v_s.reshape(n_b, n_q, -1))
        else:
            hdm = DSA_HEADS * DSA_HEAD_DIM
            w_qkv, w_qidx, w_o = cast(dsa_w_qkv[j]), cast(dsa_w_qidx[j]), cast(dsa_w_o[j])
            w_kidx, w_widx = cast(dsa_w_kidx[j]), cast(dsa_w_widx[j])
            zpad = lambda n: jnp.zeros((d, n), BF16)
            wt = dict(wkv=w_qkv[:, hdm:], wqt=w_qkv[:, :hdm].T, wvt=w_qkv[:, 2 * hdm:].T, wqit=w_qidx.T,
                      wkw=jnp.concatenate([w_kidx, zpad(LANES - IDX_DIM)], axis=1),
                      wwt=jnp.concatenate([w_widx, zpad(BF16_ROWS - IDX_HEADS)], axis=1).T)
            hshape = (DSA_HEADS, DSA_HEAD_DIM)
            k, v, ki, qt, kh, vt, qit, kib, wit = _dsa_proj_t(xp, norm_mix[i], wt, dsa_g_kidx[j])
            bias = _dsa_select(qit, wit, kib, batch, seq, min(TOPK_MAX, seq // 4))
            xp = _attn_t(xp, qt, kh, vt, bias, w_o.T, batch, seq, 512, "dsa_attn_prompt")
            outs["dk_p"].append(k.reshape((batch, seq) + hshape)); outs["dv_p"].append(v.reshape((batch, seq) + hshape))
            outs["di_p"].append(ki.reshape(batch, seq, -1))
            w_kw = jnp.concatenate([w_kidx, w_widx, zpad(LANES - IDX_DIM - IDX_HEADS)], axis=1)
            k, v, ki, qb, kb, vb, qib, kib, wi = _dsa_proj(xs, norm_mix[i], w_qkv, w_qidx, w_kw, dsa_g_kidx[j])
            xs = _dsa_attn_sample(xs, qb, qib, wi, kb, vb, kib, cache_dsa_k[j], cache_dsa_v[j], cache_dsa_kidx[j], w_o,
                                  n_b, n_q, min(TOPK_MAX, (past + n_q) // 4))
            outs["dk_s"].append(k.reshape((n_b, n_q) + hshape)); outs["dv_s"].append(v.reshape((n_b, n_q) + hshape))
            outs["di_s"].append(ki.reshape(n_b, n_q, -1))
        w_in, w_out = cast(ffn_w_in[i]), cast(ffn_w_out[i])
        final = i == depth - 1
        xp = _ffn(xp, norm_ffn[i], w_in, w_out, norm_final, final)
        xs = _ffn(xs, norm_ffn[i], w_in, w_out, norm_final, final)
    st = lambda name: jnp.stack(outs[name])
    return (xp.reshape(batch, seq, d), xs.reshape(n_b, n_q, d),
            st("ckv_p"), st("kr_p"), st("ckv_s"), st("kr_s"), st("cv_s"),
            st("dk_p"), st("dv_p"), st("di_p"), st("dk_s"), st("dv_s"), st("di_s"))
```

```python
import functools

import numpy as np
import jax
import jax.numpy as jnp
from jax import lax
from jax.experimental import pallas as pl
from jax.experimental.pallas import tpu as pltpu

F32, BF16, I32 = jnp.float32, jnp.bfloat16, jnp.int32

CHUNK = 64
EPS = 1e-6
MLA_HEADS, MLA_Q_LORA, MLA_KV_LORA, MLA_NOPE, MLA_ROPE, MLA_V = 16, 512, 256, 64, 32, 64
ROPE_BASE = 10000.0
MLA_SCALE = (MLA_NOPE + MLA_ROPE) ** -0.5
CMLP_CHUNK, CMLP_WIDTH, CMLP_GROUPS = 128, 2048, 8
DSA_HEADS, DSA_HEAD_DIM = 16, 64
DSA_SCALE = DSA_HEAD_DIM ** -0.5
IDX_HEADS, IDX_DIM = 8, 64
IDX_SCALE = IDX_DIM ** -0.5
TOPK_MAX = 256

LANES = 128
MXU_TILE = 256
BF16_ROWS = 16
VMEM_LIMIT = 52 * 1024 * 1024
NEG_INF = float("-inf")
INT_MIN = -2 ** 31
KEY_NEG_INF = -2139095041
LOG2E = float(np.log2(np.e))
LAG_MARGIN = 100.0
HEAD_V = 64
V_ROWS = HEAD_V + BF16_ROWS


def _dot(a, b):
    return jnp.dot(a, b, preferred_element_type=F32)


def _dot_t(a, b):
    return lax.dot_general(a, b, (((1,), (1,)), ((), ())), preferred_element_type=F32)


def _rms(x, g):
    return x * lax.rsqrt(jnp.mean(x * x, axis=-1, keepdims=True) + EPS) * g


def _log2(n):
    assert n > 0 and n & (n - 1) == 0, n
    return n.bit_length() - 1


def _div(x, n):
    return lax.shift_right_logical(x, jnp.int32(_log2(n)))


def _mod(x, n):
    assert n & (n - 1) == 0, n
    return x & (n - 1)


def _params(*sem):
    return pltpu.CompilerParams(dimension_semantics=sem, vmem_limit_bytes=VMEM_LIMIT)


def _whole(shape):
    nd = len(shape)
    return pl.BlockSpec(shape, lambda *_: (0,) * nd)


def _ffn_kernel(x_ref, g_ref, wg_ref, wu_ref, wo_ref, gf_ref, o_ref, act_scr, *, final):
    x = x_ref[...]
    h = _rms(x, g_ref[...]).astype(BF16)
    f = wo_ref.shape[0]
    chunk = MXU_TILE if f % MXU_TILE == 0 else f
    nc = f // chunk
    nxt = (_dot(h, wg_ref[:, 0:chunk]), _dot(h, wu_ref[:, 0:chunk]))
    for c in range(nc):
        gate, up = nxt
        if c + 1 < nc:
            lo = (c + 1) * chunk
            nxt = (_dot(h, wg_ref[:, lo:lo + chunk]), _dot(h, wu_ref[:, lo:lo + chunk]))
        act_scr[:, c * chunk:(c + 1) * chunk] = (jax.nn.silu(gate) * up).astype(BF16)
    y = x + _dot(act_scr[...], wo_ref[...])
    if final:
        y = _rms(y, gf_ref[...])
    o_ref[...] = y


def _ffn(x, g, w_in, w_out, g_final, final):
    t, d = x.shape
    f = w_out.shape[0]
    tm = min(512, t)
    once = pl.Buffered(1)
    return pl.pallas_call(
        functools.partial(_ffn_kernel, final=final),
        grid=(t // tm,),
        in_specs=[
            pl.BlockSpec((tm, d), lambda i: (i, 0)),
            _whole((1, d)),
            pl.BlockSpec((d, f), lambda i: (0, 0), pipeline_mode=once),
            pl.BlockSpec((d, f), lambda i: (0, 1), pipeline_mode=once),
            pl.BlockSpec((f, d), lambda i: (0, 0), pipeline_mode=once),
            _whole((1, d)),
        ],
        out_specs=pl.BlockSpec((tm, d), lambda i: (i, 0)),
        out_shape=jax.ShapeDtypeStruct((t, d), F32),
        scratch_shapes=[pltpu.VMEM((tm, f), BF16)],
        compiler_params=_params("parallel"),
        name="ffn",
    )(x, g.reshape(1, d), w_in, w_in, w_out, g_final.reshape(1, d))


def _gelu(x):
    return 0.5 * x * (1.0 + lax.erf(x * np.float32(np.sqrt(0.5))))


def _layernorm(x, g, b):
    mu = jnp.mean(x, axis=-1, keepdims=True)
    xc = x - mu
    return xc * lax.rsqrt(jnp.mean(xc * xc, axis=-1, keepdims=True) + EPS) * g + b


def _cmlp_kernel(x_ref, g_ref, win_ref, lng_ref, lnb_ref, ws_ref, bs_ref, wout_ref, *rest, n_rows, write_v):
    if write_v:
        o_ref, v_ref, vb_scr, gated_scr = rest
    else:
        o_ref, vb_scr, gated_scr = rest
    tm = x_ref.shape[0]
    w = CMLP_WIDTH
    gw = w // CMLP_GROUPS
    c = CMLP_CHUNK
    x = x_ref[...]
    h = _rms(x, g_ref[...]).astype(BF16)
    v = _layernorm(_gelu(_dot(h, win_ref[:, w:])), lng_ref[...], lnb_ref[...])
    if write_v:
        v_ref[...] = v
    vb_scr[...] = v.astype(BF16)
    r_i = lax.broadcasted_iota(I32, (c, c), 0)
    c_i = lax.broadcasted_iota(I32, (c, c), 1)
    keep = jnp.where(c_i >= r_i - _mod(r_i, n_rows), jnp.where(c_i <= r_i, 1, 0), 0) > 0
    u_next = _dot(h, win_ref[:, 0:gw])
    for g in range(CMLP_GROUPS):
        lo, hi = g * gw, (g + 1) * gw
        u_raw = u_next
        if g + 1 < CMLP_GROUPS:
            u_next = _dot(h, win_ref[:, hi:hi + gw])
        wg = jnp.where(keep, ws_ref[g], 0.0).astype(BF16)
        bias = bs_ref[:, g:g + 1]
        mixed = jnp.concatenate(
            [_dot(wg, vb_scr[k * c:(k + 1) * c, lo:hi]) + bias for k in range(tm // c)], axis=0)
        gated_scr[:, lo:hi] = (_gelu(u_raw) * mixed).astype(BF16)
    o_ref[...] = x + _dot(gated_scr[...], wout_ref[...])


def _cmlp(x, g, w_in, ln_g, ln_b, w_s, b_s, w_out, n_rows, write_v):
    t, d = x.shape
    w = CMLP_WIDTH
    c = CMLP_CHUNK
    tm = min(512, t)
    rep = c // n_rows
    ws_t = jnp.tile(w_s[:, :n_rows, :n_rows], (1, rep, rep))
    bs_t = jnp.tile(b_s[:, :n_rows].T, (rep, 1))
    out_shape = [jax.ShapeDtypeStruct((t, d), F32)]
    out_specs = [pl.BlockSpec((tm, d), lambda i: (i, 0))]
    if write_v:
        out_shape.append(jax.ShapeDtypeStruct((t, w), F32))
        out_specs.append(pl.BlockSpec((tm, w), lambda i: (i, 0)))
    res = pl.pallas_call(
        functools.partial(_cmlp_kernel, n_rows=n_rows, write_v=write_v),
        grid=(t // tm,),
        in_specs=[
            pl.BlockSpec((tm, d), lambda i: (i, 0)),
            _whole((1, d)), _whole((d, 2 * w)), _whole((1, w)), _whole((1, w)),
            _whole((CMLP_GROUPS, c, c)), _whole((c, CMLP_GROUPS)), _whole((w, d)),
        ],
        out_specs=out_specs,
        out_shape=out_shape,
        scratch_shapes=[pltpu.VMEM((tm, w), BF16), pltpu.VMEM((tm, w), BF16)],
        compiler_params=_params("parallel"),
        name="cmlp",
    )(x, g.reshape(1, d), w_in, ln_g.reshape(1, w), ln_b.reshape(1, w), ws_t, bs_t, w_out)
    return res if write_v else (res[0], None)


def _attn_t_kernel(qi_ref, kj_ref, last_ref, nsub_ref, qt_ref, k_ref, vt_ref, *rest, tq, tk, sub, use_bias):
    if use_bias:
        bias_ref, kn_ref, x_ref, wot_ref, o_ref, m_scr, pend_scr, qn_scr, acc_scr, cat_scr = rest
    else:
        kn_ref, x_ref, wot_ref, o_ref, m_scr, pend_scr, qn_scr, acc_scr, cat_scr = rest
    p = pl.program_id(1)
    qi, kj = qi_ref[p], kj_ref[p]
    hd = qt_ref.shape[0]

    @pl.when(kj == 0)
    def _():
        m_scr[...] = jnp.full_like(m_scr, NEG_INF)
        pend_scr[...] = jnp.ones_like(pend_scr)
        acc_scr[...] = jnp.zeros_like(acc_scr)
        for hh in range(hd):
            q = qt_ref[hh].astype(F32)
            qn_scr[hh] = jnp.sqrt(jnp.sum(q * q, axis=0, keepdims=True)) * kn_ref[0, hh:hh + 1, 0:1]

    def step(off, mask, lagged):
        keys = lambda hh: k_ref[hh, pl.ds(off, sub), :]
        vals = lambda hh: vt_ref[hh, :, pl.ds(off, sub)]
        s_next = _dot(keys(0), qt_ref[0])
        pend = None
        for hh in range(hd):
            s = s_next
            if hh + 1 < hd:
                s_next = _dot(keys(hh + 1), qt_ref[hh + 1])
            if mask is not None:
                s = s + mask
            m_prev = m_scr[hh]
            m_new = jnp.maximum(m_prev, jnp.max(s, axis=0, keepdims=True))
            if lagged:
                scale = pend_scr[hh]
                pe = jnp.exp2(s - m_prev).astype(BF16)
                pend_scr[hh] = jnp.exp2(m_prev - m_new)
            else:
                m_safe = jnp.where(m_new == NEG_INF, 0.0, m_new)
                scale = pend_scr[hh] * jnp.exp2(m_prev - m_safe)
                pe = jnp.exp2(s - m_safe).astype(BF16)
                pend_scr[hh] = jnp.ones_like(m_prev)
            m_scr[hh] = m_new
            if pend is not None:
                ph, pa, pp = pend
                acc_scr[ph] = pa * acc_scr[ph] + _dot(vals(ph), pp)
            pend = (hh, scale, pe)
        ph, pa, pp = pend
        acc_scr[ph] = pa * acc_scr[ph] + _dot(vals(ph), pp)

    def sub_body(j, carry):
        off = pl.multiple_of(j * sub, sub)
        excess = qn_scr[0] - m_scr[0]
        for hh in range(1, hd):
            excess = jnp.maximum(excess, qn_scr[hh] - m_scr[hh])
        lag_ok = jnp.max(excess) <= LAG_MARGIN
        if use_bias:
            bias = bias_ref[pl.ds(off, sub), :].astype(F32)

            @pl.when(lag_ok)
            def _():
                step(off, bias, True)

            @pl.when(jnp.logical_not(lag_ok))
            def _():
                step(off, bias, False)
        else:
            base = kj * tk + off
            needs_mask = _div(base + sub - 1, CHUNK) > _div(qi * tq, CHUNK)

            def chunk_mask():
                k_chunk = _div(base + lax.broadcasted_iota(I32, (sub, 1), 0), CHUNK)
                q_chunk = _div(qi * tq + lax.broadcasted_iota(I32, (1, tq), 1), CHUNK)
                return jnp.where(k_chunk <= q_chunk, 0.0, NEG_INF)

            @pl.when(jnp.logical_not(lag_ok))
            def _():
                step(off, chunk_mask(), False)

            @pl.when(jnp.logical_and(lag_ok, needs_mask))
            def _():
                step(off, chunk_mask(), True)

            @pl.when(jnp.logical_and(lag_ok, jnp.logical_not(needs_mask)))
            def _():
                step(off, None, True)
        return carry

    lax.fori_loop(0, nsub_ref[p], sub_body, 0)

    @pl.when(last_ref[p] == 1)
    def _():
        for hh in range(hd):
            a = acc_scr[hh]
            cat_scr[hh * HEAD_V:(hh + 1) * HEAD_V, :] = (a[:HEAD_V] / a[HEAD_V:HEAD_V + 1]).astype(BF16)
        o_ref[...] = x_ref[...] + _dot(wot_ref[...], cat_scr[...]).T


def _causal_pairs(nq, tq, tk, sub):
    qi, kj, last, nsub = [], [], [], []
    per = tk // sub
    for q in range(nq):
        vis_sub = -(-((q + 1) * tq) // sub)
        nvis = -(-vis_sub // per)
        for k in range(nvis):
            qi.append(q); kj.append(k); last.append(int(k == nvis - 1)); nsub.append(min(per, vis_sub - k * per))
    as_arr = lambda v: jnp.asarray(np.array(v, np.int32))
    return as_arr(qi), as_arr(kj), as_arr(last), as_arr(nsub)


def _attn_t(x, qt, kh, vt, bias, wot, batch, seq, sub, name):
    t, d = x.shape
    hd, dk, _ = qt.shape
    tq, tk, sub = min(512, seq), min(1024, seq), min(sub, seq)
    nq, nk = seq // tq, seq // tk
    qi, kj, last, nsub = _causal_pairs(nq, tq, tk, sub)
    use_bias = bias is not None
    in_specs = [
        pl.BlockSpec((hd, dk, tq), lambda b, p, qi, kj, la, ns: (0, 0, b * nq + qi[p])),
        pl.BlockSpec((hd, tk, dk), lambda b, p, qi, kj, la, ns: (0, b * nk + kj[p], 0)),
        pl.BlockSpec((hd, V_ROWS, tk), lambda b, p, qi, kj, la, ns: (0, 0, b * nk + kj[p])),
    ]
    args = [qt, kh, vt]
    if use_bias:
        in_specs.append(pl.BlockSpec((tk, tq), lambda b, p, qi, kj, la, ns: (kj[p], b * nq + qi[p])))
        args.append(bias)
    kn = jnp.sqrt(jnp.max(jnp.sum(jnp.square(kh.astype(F32)), axis=-1).reshape(hd, batch, seq), axis=-1))
    in_specs.append(pl.BlockSpec((1, hd, LANES), lambda b, p, qi, kj, la, ns: (b, 0, 0)))
    args.append(jnp.broadcast_to(kn.T[:, :, None], (batch, hd, LANES)))
    in_specs += [
        pl.BlockSpec((tq, d), lambda b, p, qi, kj, la, ns: (b * nq + qi[p], 0)),
        pl.BlockSpec(wot.shape, lambda b, p, qi, kj, la, ns: (0, 0)),
    ]
    args += [x, wot]
    grid_spec = pltpu.PrefetchScalarGridSpec(
        num_scalar_prefetch=4,
        grid=(batch, int(qi.shape[0])),
        in_specs=in_specs,
        out_specs=pl.BlockSpec((tq, d), lambda b, p, qi, kj, la, ns: (b * nq + qi[p], 0)),
        scratch_shapes=[
            pltpu.VMEM((hd, 1, tq), F32), pltpu.VMEM((hd, 1, tq), F32), pltpu.VMEM((hd, 1, tq), F32),
            pltpu.VMEM((hd, V_ROWS, tq), F32), pltpu.VMEM((hd * HEAD_V, tq), BF16),
        ],
    )
    return pl.pallas_call(
        functools.partial(_attn_t_kernel, tq=tq, tk=tk, sub=sub, use_bias=use_bias),
        grid_spec=grid_spec,
        out_shape=jax.ShapeDtypeStruct((t, d), F32),
        compiler_params=_params("parallel", "arbitrary"),
        name=name,
    )(qi, kj, last, nsub, *args)


def _rope_tables(pos):
    half = MLA_ROPE // 2
    inv = ROPE_BASE ** (-jnp.arange(half, dtype=F32) / half)
    ang = pos.astype(F32)[:, None] * inv[None, :]
    cos, sin = jnp.cos(ang), jnp.sin(ang)
    cos_k = jnp.concatenate([cos, cos], axis=1)
    sin_k = jnp.concatenate([-sin, sin], axis=1)
    return jnp.tile(cos_k, (1, MLA_HEADS)), jnp.tile(sin_k, (1, MLA_HEADS)), cos_k, sin_k


def _swap_halves(w, group):
    shp = w.shape
    wr = w.reshape(shp[:-1] + (shp[-1] // group, 2, group // 2))
    return wr[..., ::-1, :].reshape(shp)


def _mla_weights(w_dq, w_uq, w_dkv, w_ukv, w_o):
    hd = MLA_HEADS
    cast = lambda a: a.astype(BF16)
    w_uq, w_dkv, w_ukv = cast(w_uq), cast(w_dkv), cast(w_ukv)
    wqn = w_uq[:, :, :MLA_NOPE].reshape(MLA_Q_LORA, hd * MLA_NOPE)
    wqr = w_uq[:, :, MLA_NOPE:].reshape(MLA_Q_LORA, hd * MLA_ROPE)
    wqs = _swap_halves(wqr, MLA_ROPE)
    wkc, wkr = w_dkv[:, :MLA_KV_LORA], w_dkv[:, MLA_KV_LORA:]
    wks = _swap_halves(wkr, MLA_ROPE)
    wuk = jnp.transpose(w_ukv[:, :, :MLA_NOPE], (1, 2, 0))
    zeros = jnp.zeros_like(wuk)
    even = jnp.concatenate([wuk, zeros], axis=1)
    odd = jnp.concatenate([zeros, wuk], axis=1)
    wuk2 = jnp.where((jnp.arange(hd) % 2 == 0)[:, None, None], even, odd)
    wuv = jnp.transpose(w_ukv[:, :, MLA_NOPE:], (1, 0, 2))
    d = w_dkv.shape[0]
    pad_r = lambda a: jnp.concatenate([a, jnp.zeros((d, LANES - MLA_ROPE), BF16)], axis=1)
    wukp = jnp.concatenate([jnp.zeros((MLA_KV_LORA, hd, MLA_ROPE), BF16), w_ukv[:, :, :MLA_NOPE],
                            jnp.zeros((MLA_KV_LORA, hd, LANES - MLA_ROPE - MLA_NOPE), BF16)], axis=2)
    return dict(wdq=cast(w_dq), wqn=wqn, wqr=wqr, wqs=wqs, wkc=wkc, wkr=wkr, wks=wks, wuk=wuk2, wuv=wuv, wo=cast(w_o),
                wqnt=wqn.T, wqrt=wqr.T, wqst=wqs.T, wkrp=pad_r(wkr), wksp=pad_r(wks),
                wukp=wukp.reshape(MLA_KV_LORA, hd * LANES), wuvt=wuv.transpose(0, 2, 1).reshape(hd * MLA_V, MLA_KV_LORA),
                wot=cast(w_o).T)


def _mla_proj_kernel(x_ref, gm_ref, wdq_ref, gq_ref, wqn_ref, wqr_ref, wqs_ref, wkc_ref, wkr_ref, wks_ref,
                     gkv_ref, wuk_ref, cq_ref, sq_ref, ck_ref, sk_ref,
                     ql_ref, qr_ref, ckv_ref, kr_ref, ckvb_ref, krb_ref):
    h = _rms(x_ref[...], gm_ref[...]).astype(BF16)
    cq = _rms(_dot(h, wdq_ref[...]), gq_ref[...]).astype(BF16)
    qn = _dot(cq, wqn_ref[...])
    qr = ((_dot(cq, wqr_ref[...]) * cq_ref[...] + _dot(cq, wqs_ref[...]) * sq_ref[...]) * MLA_SCALE).astype(BF16)
    for hh in range(MLA_HEADS):
        pair = qn[:, (hh // 2) * LANES:(hh // 2 + 1) * LANES].astype(BF16)
        ql_ref[hh] = (_dot(pair, wuk_ref[hh]) * MLA_SCALE).astype(BF16)
        qr_ref[hh] = qr[:, hh * MLA_ROPE:(hh + 1) * MLA_ROPE]
    ckv = _rms(_dot(h, wkc_ref[...]), gkv_ref[...])
    ckv_ref[...] = ckv
    ckvb_ref[...] = ckv.astype(BF16)
    kr = _dot(h, wkr_ref[...]) * ck_ref[...] + _dot(h, wks_ref[...]) * sk_ref[...]
    kr_ref[...] = kr
    krb_ref[...] = kr.astype(BF16)


def _mla_proj(x, g_mix, wts, g_q, g_kv, tables):
    t, d = x.shape
    tm = min(256, t)
    cos_q, sin_q, cos_k, sin_k = tables
    nrep = cos_q.shape[0] // tm
    tab = lambda wdt: pl.BlockSpec((tm, wdt), lambda i: (i % nrep, 0))
    hd = MLA_HEADS
    row = lambda wdt: pl.BlockSpec((tm, wdt), lambda i: (i, 0))
    hm = lambda wdt: pl.BlockSpec((hd, tm, wdt), lambda i: (0, i, 0))
    return pl.pallas_call(
        _mla_proj_kernel,
        grid=(t // tm,),
        in_specs=[
            row(d), _whole((1, d)), _whole(wts["wdq"].shape), _whole((1, MLA_Q_LORA)),
            _whole(wts["wqn"].shape), _whole(wts["wqr"].shape), _whole(wts["wqs"].shape),
            _whole(wts["wkc"].shape), _whole(wts["wkr"].shape), _whole(wts["wks"].shape),
            _whole((1, MLA_KV_LORA)), _whole(wts["wuk"].shape),
            tab(hd * MLA_ROPE), tab(hd * MLA_ROPE), tab(MLA_ROPE), tab(MLA_ROPE),
        ],
        out_specs=[hm(MLA_KV_LORA), hm(MLA_ROPE), row(MLA_KV_LORA), row(MLA_ROPE), row(MLA_KV_LORA), row(MLA_ROPE)],
        out_shape=[
            jax.ShapeDtypeStruct((hd, t, MLA_KV_LORA), BF16), jax.ShapeDtypeStruct((hd, t, MLA_ROPE), BF16),
            jax.ShapeDtypeStruct((t, MLA_KV_LORA), F32), jax.ShapeDtypeStruct((t, MLA_ROPE), F32),
            jax.ShapeDtypeStruct((t, MLA_KV_LORA), BF16), jax.ShapeDtypeStruct((t, MLA_ROPE), BF16),
        ],
        compiler_params=_params("parallel"),
        name="mla_proj",
    )(x, g_mix.reshape(1, d), wts["wdq"], g_q.reshape(1, -1), wts["wqn"], wts["wqr"], wts["wqs"],
      wts["wkc"], wts["wkr"], wts["wks"], g_kv.reshape(1, -1), wts["wuk"], cos_q, sin_q, cos_k, sin_k)


def _mla_proj_t_kernel(x_ref, gm_ref, wdq_ref, gq_ref, wqnt_ref, wqrt_ref, wqst_ref, wkc_ref, wkrp_ref, wksp_ref,
                       gkv_ref, wukp_ref, wuvt_ref, cqt_ref, sqt_ref, ckp_ref, skp_ref,
                       qt_ref, kh_ref, vt_ref, ckv_ref, kr_ref):
    tm = x_ref.shape[0]
    hd = MLA_HEADS
    qscale = MLA_SCALE * LOG2E
    h = _rms(x_ref[...], gm_ref[...]).astype(BF16)
    cq_raw = _dot(h, wdq_ref[...])
    ckv_raw = _dot(h, wkc_ref[...])
    kr_a = _dot(h, wkrp_ref[...])
    kr_b = _dot(h, wksp_ref[...])
    cq = _rms(cq_raw, gq_ref[...]).astype(BF16)
    qnt_raw = _dot_t(wqnt_ref[...], cq)
    qrt_a = _dot_t(wqrt_ref[...], cq)
    qrt_b = _dot_t(wqst_ref[...], cq)
    ckv = _rms(ckv_raw, gkv_ref[...])
    ckv_ref[...] = ckv
    cb = ckv.astype(BF16)
    kn = _dot(cb, wukp_ref[...])
    vt = _dot_t(wuvt_ref[...], cb).astype(BF16)
    qnt = (qnt_raw * qscale).astype(BF16)
    qrt = ((qrt_a * cqt_ref[...] + qrt_b * sqt_ref[...]) * qscale).astype(BF16)
    zeros = jnp.zeros((LANES - MLA_ROPE - MLA_NOPE, tm), BF16)
    krp = kr_a * ckp_ref[...] + kr_b * skp_ref[...]
    kr_ref[...] = krp[:, :MLA_ROPE]
    ones = jnp.ones((BF16_ROWS, tm), BF16)
    for hh in range(hd):
        qt_ref[hh, 0:MLA_ROPE, :] = qrt[hh * MLA_ROPE:(hh + 1) * MLA_ROPE]
        qt_ref[hh, MLA_ROPE:MLA_ROPE + MLA_NOPE, :] = qnt[hh * MLA_NOPE:(hh + 1) * MLA_NOPE]
        qt_ref[hh, MLA_ROPE + MLA_NOPE:, :] = zeros
        kh_ref[hh] = (kn[:, hh * LANES:(hh + 1) * LANES] + krp).astype(BF16)
        vt_ref[hh, 0:MLA_V, :] = vt[hh * MLA_V:(hh + 1) * MLA_V]
        vt_ref[hh, MLA_V:, :] = ones


def _mla_proj_t(x, g_mix, wts, g_q, g_kv, tables, seq):
    t, d = x.shape
    tm = min(256, seq)
    hd = MLA_HEADS
    cos_q, sin_q, cos_k, sin_k = tables
    pad = lambda a: jnp.concatenate([a, jnp.zeros((seq, LANES - MLA_ROPE), F32)], axis=1)
    nrep = seq // tm
    row = lambda wdt: pl.BlockSpec((tm, wdt), lambda i: (i, 0))
    names = ("wdq", "wqnt", "wqrt", "wqst", "wkc", "wkrp", "wksp", "wukp", "wuvt")
    w = {n: wts[n] for n in names}
    return pl.pallas_call(
        _mla_proj_t_kernel,
        grid=(t // tm,),
        in_specs=[
            row(d), _whole((1, d)), _whole(w["wdq"].shape), _whole((1, MLA_Q_LORA)),
            _whole(w["wqnt"].shape), _whole(w["wqrt"].shape), _whole(w["wqst"].shape),
            _whole(w["wkc"].shape), _whole(w["wkrp"].shape), _whole(w["wksp"].shape),
            _whole((1, MLA_KV_LORA)), _whole(w["wukp"].shape), _whole(w["wuvt"].shape),
            pl.BlockSpec((hd * MLA_ROPE, tm), lambda i: (0, i % nrep)),
            pl.BlockSpec((hd * MLA_ROPE, tm), lambda i: (0, i % nrep)),
            pl.BlockSpec((tm, LANES), lambda i: (i % nrep, 0)),
            pl.BlockSpec((tm, LANES), lambda i: (i % nrep, 0)),
        ],
        out_specs=[
            pl.BlockSpec((hd, LANES, tm), lambda i: (0, 0, i)),
            pl.BlockSpec((hd, tm, LANES), lambda i: (0, i, 0)),
            pl.BlockSpec((hd, V_ROWS, tm), lambda i: (0, 0, i)),
            row(MLA_KV_LORA), row(MLA_ROPE),
        ],
        out_shape=[
            jax.ShapeDtypeStruct((hd, LANES, t), BF16), jax.ShapeDtypeStruct((hd, t, LANES), BF16),
            jax.ShapeDtypeStruct((hd, V_ROWS, t), BF16),
            jax.ShapeDtypeStruct((t, MLA_KV_LORA), F32), jax.ShapeDtypeStruct((t, MLA_ROPE), F32),
        ],
        compiler_params=_params("parallel"),
        name="mla_proj_t",
    )(x, g_mix.reshape(1, d), w["wdq"], g_q.reshape(1, -1), w["wqnt"], w["wqrt"], w["wqst"],
      w["wkc"], w["wkrp"], w["wksp"], g_kv.reshape(1, -1), w["wukp"], w["wuvt"],
      cos_q.T, sin_q.T, pad(cos_k), pad(sin_k))


def _mla_epilogue(o_lat, x_ref, wuv_ref, wo_ref, o_ref, cat_scr, tq):
    ob = o_lat.astype(BF16)
    for hh in range(MLA_HEADS):
        cat_scr[:, hh * MLA_V:(hh + 1) * MLA_V] = _dot(ob[hh * tq:(hh + 1) * tq], wuv_ref[hh]).astype(BF16)
    o_ref[...] = x_ref[...] + _dot(cat_scr[...], wo_ref[...])


def _mla_samp_kernel(ql_ref, qr_ref, cc_ref, cr_ref, cn_ref, rn_ref, x_ref, wuv_ref, wo_ref, o_ref, cat_scr, *, n_q):
    b = pl.program_id(0)
    hd = MLA_HEADS
    ql = ql_ref[...].reshape(hd * n_q, MLA_KV_LORA)
    qr = qr_ref[...].reshape(hd * n_q, MLA_ROPE)
    cc = cc_ref[0].astype(BF16)
    cr = cr_ref[0].astype(BF16)
    cn = cn_ref[...]
    s1 = _dot_t(ql, cc) + _dot_t(qr, cr)
    s2 = _dot_t(ql, cn) + _dot_t(qr, rn_ref[...])
    own = _div(lax.broadcasted_iota(I32, (1, LANES), 1), n_q) == _mod(b, LANES // n_q)
    s2 = jnp.where(own, s2, NEG_INF)
    m = jnp.maximum(jnp.max(s1, axis=1, keepdims=True), jnp.max(s2, axis=1, keepdims=True))
    p1 = jnp.exp(s1 - m)
    p2 = jnp.exp(s2 - m)
    l = jnp.sum(p1, axis=1, keepdims=True) + jnp.sum(p2, axis=1, keepdims=True)
    o_lat = (_dot(p1.astype(BF16), cc) + _dot(p2.astype(BF16), cn)) / l
    _mla_epilogue(o_lat, x_ref, wuv_ref, wo_ref, o_ref, cat_scr, n_q)


def _mla_attn_sample(x, ql, qr, ckvb, krb, cache_c, cache_r, wts, n_b, n_q):
    t, d = x.shape
    hd = MLA_HEADS
    past = cache_c.shape[1]
    per = LANES // n_q
    return pl.pallas_call(
        functools.partial(_mla_samp_kernel, n_q=n_q),
        grid=(n_b,),
        in_specs=[
            pl.BlockSpec((hd, n_q, MLA_KV_LORA), lambda b: (0, b, 0)),
            pl.BlockSpec((hd, n_q, MLA_ROPE), lambda b: (0, b, 0)),
            pl.BlockSpec((1, past, MLA_KV_LORA), lambda b: (b, 0, 0)),
            pl.BlockSpec((1, past, MLA_ROPE), lambda b: (b, 0, 0)),
            pl.BlockSpec((LANES, MLA_KV_LORA), lambda b: (b // per, 0)),
            pl.BlockSpec((LANES, MLA_ROPE), lambda b: (b // per, 0)),
            pl.BlockSpec((n_q, d), lambda b: (b, 0)),
            _whole(wts["wuv"].shape), _whole(wts["wo"].shape),
        ],
        out_specs=pl.BlockSpec((n_q, d), lambda b: (b, 0)),
        out_shape=jax.ShapeDtypeStruct((t, d), F32),
        scratch_shapes=[pltpu.VMEM((n_q, hd * MLA_V), BF16)],
        compiler_params=_params("parallel"),
        name="mla_attn_sample",
    )(ql, qr, cache_c, cache_r, ckvb, krb, x, wts["wuv"], wts["wo"])


def _dsa_proj_kernel(x_ref, gm_ref, wqkv_ref, wqi_ref, wkw_ref, gki_ref,
                     k_ref, v_ref, ki_ref, qb_ref, kb_ref, vb_ref, qib_ref, kib_ref, wi_ref):
    hdm = DSA_HEADS * DSA_HEAD_DIM
    h = _rms(x_ref[...], gm_ref[...]).astype(BF16)
    qkv = _dot(h, wqkv_ref[...])
    q = qkv[:, :hdm] * DSA_SCALE
    k = qkv[:, hdm:2 * hdm]
    v = qkv[:, 2 * hdm:]
    k_ref[...] = k
    v_ref[...] = v
    qi = _dot(h, wqi_ref[...]) * IDX_SCALE
    kw = _dot(h, wkw_ref[...])
    ki = _rms(kw[:, :IDX_DIM], gki_ref[...])
    ki_ref[...] = ki
    kib_ref[...] = ki.astype(BF16)
    wi_ref[...] = kw[:, IDX_DIM:IDX_DIM + IDX_HEADS] * (IDX_HEADS ** -0.5)
    qb_ref[...] = q.astype(BF16)
    kb_ref[...] = k.astype(BF16)
    vb_ref[...] = v.astype(BF16)
    for hh in range(IDX_HEADS):
        qib_ref[hh] = qi[:, hh * IDX_DIM:(hh + 1) * IDX_DIM].astype(BF16)


def _dsa_proj(x, g_mix, w_qkv, w_qidx, w_kw, g_kidx):
    t, d = x.shape
    tm = min(512, t)
    hdm = DSA_HEADS * DSA_HEAD_DIM
    row = lambda wdt: pl.BlockSpec((tm, wdt), lambda i: (i, 0))
    return pl.pallas_call(
        _dsa_proj_kernel,
        grid=(t // tm,),
        in_specs=[row(d), _whole((1, d)), _whole(w_qkv.shape), _whole(w_qidx.shape), _whole(w_kw.shape),
                  _whole((1, IDX_DIM))],
        out_specs=[row(hdm), row(hdm), row(IDX_DIM), row(hdm), row(hdm), row(hdm),
                   pl.BlockSpec((IDX_HEADS, tm, IDX_DIM), lambda i: (0, i, 0)), row(IDX_DIM), row(IDX_HEADS)],
        out_shape=[
            jax.ShapeDtypeStruct((t, hdm), F32), jax.ShapeDtypeStruct((t, hdm), F32),
            jax.ShapeDtypeStruct((t, IDX_DIM), F32), jax.ShapeDtypeStruct((t, hdm), BF16),
            jax.ShapeDtypeStruct((t, hdm), BF16), jax.ShapeDtypeStruct((t, hdm), BF16),
            jax.ShapeDtypeStruct((IDX_HEADS, t, IDX_DIM), BF16), jax.ShapeDtypeStruct((t, IDX_DIM), BF16),
            jax.ShapeDtypeStruct((t, IDX_HEADS), F32),
        ],
        compiler_params=_params("parallel"),
        name="dsa_proj",
    )(x, g_mix.reshape(1, d), w_qkv, w_qidx, w_kw, g_kidx.reshape(1, IDX_DIM))


def _dsa_proj_t_kernel(x_ref, gm_ref, wkv_ref, wqt_ref, wvt_ref, wqit_ref, wkw_ref, wwt_ref, gki_ref,
                       k_ref, v_ref, ki_ref, qt_ref, kh_ref, vt_ref, qit_ref, kib_ref, wit_ref):
    tm = x_ref.shape[0]
    hdm = DSA_HEADS * DSA_HEAD_DIM
    dh = DSA_HEAD_DIM
    h = _rms(x_ref[...], gm_ref[...]).astype(BF16)
    kv = _dot(h, wkv_ref[...])
    k = kv[:, :hdm]
    k_ref[...] = k
    v_ref[...] = kv[:, hdm:]
    kw = _dot(h, wkw_ref[...])
    ki = _rms(kw[:, :IDX_DIM], gki_ref[...])
    ki_ref[...] = ki
    kib_ref[...] = ki.astype(BF16)
    wit_ref[...] = _dot_t(wwt_ref[...], h)[:IDX_HEADS] * (IDX_HEADS ** -0.5)
    qt = (_dot_t(wqt_ref[...], h) * (DSA_SCALE * LOG2E)).astype(BF16)
    vt = _dot_t(wvt_ref[...], h).astype(BF16)
    qit = (_dot_t(wqit_ref[...], h) * IDX_SCALE).astype(BF16)
    ones = jnp.ones((BF16_ROWS, tm), BF16)
    for hh in range(DSA_HEADS):
        qt_ref[hh] = qt[hh * dh:(hh + 1) * dh]
        kh_ref[hh] = k[:, hh * dh:(hh + 1) * dh].astype(BF16)
        vt_ref[hh, 0:dh, :] = vt[hh * dh:(hh + 1) * dh]
        vt_ref[hh, dh:, :] = ones
    for hh in range(IDX_HEADS):
        qit_ref[hh] = qit[hh * IDX_DIM:(hh + 1) * IDX_DIM]


def _dsa_proj_t(x, g_mix, w, g_kidx):
    t, d = x.shape
    tm = min(256, t)
    hd, dh = DSA_HEADS, DSA_HEAD_DIM
    hdm = hd * dh
    row = lambda wdt: pl.BlockSpec((tm, wdt), lambda i: (i, 0))
    names = ("wkv", "wqt", "wvt", "wqit", "wkw", "wwt")
    return pl.pallas_call(
        _dsa_proj_t_kernel,
        grid=(t // tm,),
        in_specs=[row(d), _whole((1, d))] + [_whole(w[n].shape) for n in names] + [_whole((1, IDX_DIM))],
        out_specs=[
            row(hdm), row(hdm), row(IDX_DIM),
            pl.BlockSpec((hd, dh, tm), lambda i: (0, 0, i)),
            pl.BlockSpec((hd, tm, dh), lambda i: (0, i, 0)),
            pl.BlockSpec((hd, V_ROWS, tm), lambda i: (0, 0, i)),
            pl.BlockSpec((IDX_HEADS, IDX_DIM, tm), lambda i: (0, 0, i)),
            row(IDX_DIM),
            pl.BlockSpec((IDX_HEADS, tm), lambda i: (0, i)),
        ],
        out_shape=[
            jax.ShapeDtypeStruct((t, hdm), F32), jax.ShapeDtypeStruct((t, hdm), F32),
            jax.ShapeDtypeStruct((t, IDX_DIM), F32),
            jax.ShapeDtypeStruct((hd, dh, t), BF16), jax.ShapeDtypeStruct((hd, t, dh), BF16),
            jax.ShapeDtypeStruct((hd, V_ROWS, t), BF16),
            jax.ShapeDtypeStruct((IDX_HEADS, IDX_DIM, t), BF16), jax.ShapeDtypeStruct((t, IDX_DIM), BF16),
            jax.ShapeDtypeStruct((IDX_HEADS, t), F32),
        ],
        compiler_params=_params("parallel"),
        name="dsa_proj_t",
    )(x, g_mix.reshape(1, d), *[w[n] for n in names], g_kidx.reshape(1, IDX_DIM))


def _sort_key(score):
    bits = lax.bitcast_convert_type(score, I32)
    return jnp.where(bits < 0, bits ^ jnp.int32(0x7FFFFFFF), bits)


def _kth_largest_key(count, topk):
    c0 = count(lambda key, idx: jnp.where(key >= 0, 1, 0))
    t0 = jnp.where(c0 >= topk, jnp.int32(0), jnp.int32(INT_MIN))

    def bit_body(i, t):
        cand = t + lax.shift_left(jnp.int32(1), jnp.int32(30) - i)
        c = count(lambda key, idx: jnp.where(key >= cand, 1, 0))
        return jnp.where(c >= topk, cand, t)

    return lax.fori_loop(0, 31, bit_body, t0)


def _tie_cutoff(count, thr, rem, nbits):
    def bit_body(i, j):
        cand = j + lax.shift_left(jnp.int32(1), jnp.int32(nbits - 1) - i)
        c = count(lambda key, idx: jnp.where(key == thr, jnp.where(idx < cand, 1, 0), 0))
        return jnp.where(c < rem, cand, j)

    return lax.fori_loop(0, nbits, bit_body, jnp.zeros_like(thr))


def _select(key, idx, thr, cut):
    chosen = jnp.where(key > thr, 1, jnp.where(key == thr, jnp.where(idx <= cut, 1, 0), 0))
    return jnp.where(key > KEY_NEG_INF, chosen, 0) > 0


def _dsa_select_kernel(qit_ref, wit_ref, ki_ref, tri_ref, bias_ref, key_scr, *, tq, tk, topk):
    qb = pl.program_id(1)
    seq = ki_ref.shape[0]
    nk = seq // tk
    nvis = _div((qb + 1) * tq + tk - 1, tk)
    q_chunk = _div(qb * tq + lax.broadcasted_iota(I32, (1, tq), 1), CHUNK)
    w = wit_ref[...]
    row_idx = lax.broadcasted_iota(I32, (tk, 1), 0)

    def score_body(j, carry):
        off = pl.multiple_of(j * tk, tk)
        kb = ki_ref[pl.ds(off, tk), :]
        acc = jnp.zeros((tk, tq), F32)
        for hh in range(IDX_HEADS):
            acc = acc + w[hh:hh + 1, :] * jnp.maximum(_dot(kb, qit_ref[hh]), 0.0)
        vis = _div(off + row_idx, CHUNK) <= q_chunk
        key_scr[pl.ds(off, tk), :] = _sort_key(jnp.where(vis, acc, NEG_INF))
        return carry

    lax.fori_loop(0, nvis, score_body, 0)

    def count(hit):
        def body(j, c):
            off = pl.multiple_of(j * tk, tk)
            hits = hit(key_scr[pl.ds(off, tk), :], off + row_idx)
            return c + jnp.sum(hits.reshape(tk // 8, 8, tq), axis=0)
        c = lax.fori_loop(0, nvis, body, jnp.zeros((8, tq), I32))
        return jnp.sum(c, axis=0, keepdims=True)

    thr = _kth_largest_key(count, topk)
    rem = (topk - count(lambda key, idx: jnp.where(key > thr, 1, 0))).astype(F32)
    tri = tri_ref[...]

    def out_body(j, seen):
        off = pl.multiple_of(j * tk, tk)
        key = key_scr[pl.ds(off, tk), :]
        tie = jnp.where(key == thr, 1.0, 0.0)
        rank = seen + _dot(tri, tie.astype(BF16))
        keep = jnp.where(key > thr, 1.0, jnp.where(rank <= rem, tie, 0.0))
        keep = jnp.where(key > KEY_NEG_INF, keep, 0.0)
        bias_ref[pl.ds(off, tk), :] = jnp.where(keep > 0.0, 0.0, NEG_INF).astype(BF16)
        return rank[tk - 1:tk, :]

    lax.fori_loop(0, nvis, out_body, jnp.zeros((1, tq), F32))

    def fill_body(j, carry):
        off = pl.multiple_of(j * tk, tk)
        bias_ref[pl.ds(off, tk), :] = jnp.full((tk, tq), NEG_INF, BF16)
        return carry

    lax.fori_loop(nvis, nk, fill_body, 0)


def _dsa_select(qit, wit, kib, batch, seq, topk):
    tq, tk = min(256, seq), min(512, seq)
    nq = seq // tq
    t = batch * seq
    return pl.pallas_call(
        functools.partial(_dsa_select_kernel, tq=tq, tk=tk, topk=topk),
        grid=(batch, nq),
        in_specs=[
            pl.BlockSpec((IDX_HEADS, IDX_DIM, tq), lambda b, q: (0, 0, b * nq + q)),
            pl.BlockSpec((IDX_HEADS, tq), lambda b, q: (0, b * nq + q)),
            pl.BlockSpec((seq, IDX_DIM), lambda b, q: (b, 0)),
            _whole((tk, tk)),
        ],
        out_specs=pl.BlockSpec((seq, tq), lambda b, q: (0, b * nq + q)),
        out_shape=jax.ShapeDtypeStruct((seq, t), BF16),
        scratch_shapes=[pltpu.VMEM((seq, tq), I32)],
        compiler_params=_params("parallel", "arbitrary"),
        name="dsa_select",
    )(qit, wit, kib, jnp.tril(jnp.ones((tk, tk), BF16)))


def _dsa_samp_kernel(q_ref, qi_ref, wi_ref, kc_ref, vc_ref, kic_ref, kn_ref, vn_ref, kin_ref, x_ref, wo_ref, o_ref,
                     *, n_q, topk):
    b = pl.program_id(0)
    hd, dh = DSA_HEADS, DSA_HEAD_DIM
    past = kc_ref.shape[1]
    n_keys = past + LANES
    own = _div(lax.broadcasted_iota(I32, (1, LANES), 1), n_q) == _mod(b, LANES // n_q)

    qi = qi_ref[...].reshape(IDX_HEADS * n_q, IDX_DIM)
    lg1 = jnp.maximum(_dot_t(qi, kic_ref[0].astype(BF16)), 0.0)
    lg2 = jnp.maximum(_dot_t(qi, kin_ref[...]), 0.0)
    w = wi_ref[...]
    sc1 = jnp.zeros((n_q, past), F32)
    sc2 = jnp.zeros((n_q, LANES), F32)
    for hh in range(IDX_HEADS):
        sc1 = sc1 + w[:, hh:hh + 1] * lg1[hh * n_q:(hh + 1) * n_q]
        sc2 = sc2 + w[:, hh:hh + 1] * lg2[hh * n_q:(hh + 1) * n_q]
    key = _sort_key(jnp.concatenate([sc1, jnp.where(own, sc2, NEG_INF)], axis=1))
    idx = lax.broadcasted_iota(I32, (1, n_keys), 1)

    count = lambda hit: jnp.sum(hit(key, idx), axis=1, keepdims=True)
    thr = _kth_largest_key(count, topk)
    rem = topk - count(lambda k_, i_: jnp.where(k_ > thr, 1, 0))
    cut = _tie_cutoff(count, thr, rem, int(n_keys - 1).bit_length())
    bias = jnp.where(_select(key, idx, thr, cut), 0.0, NEG_INF)
    bias = jnp.concatenate([bias] * hd, axis=0)

    lane_head = _div(lax.broadcasted_iota(I32, (1, hd * dh), 1), dh)
    qf = q_ref[...].astype(F32)
    qbd = jnp.concatenate([jnp.where(lane_head == hh, qf, 0.0) for hh in range(hd)], axis=0).astype(BF16)
    kc = kc_ref[0].astype(BF16)
    vc = vc_ref[0].astype(BF16)
    s1 = _dot_t(qbd, kc) + bias[:, :past]
    s2 = _dot_t(qbd, kn_ref[...]) + bias[:, past:]
    m = jnp.maximum(jnp.max(s1, axis=1, keepdims=True), jnp.max(s2, axis=1, keepdims=True))
    p1 = jnp.exp(s1 - m)
    p2 = jnp.exp(s2 - m)
    l = jnp.sum(p1, axis=1, keepdims=True) + jnp.sum(p2, axis=1, keepdims=True)
    o_all = (_dot(p1.astype(BF16), vc) + _dot(p2.astype(BF16), vn_ref[...])) / l
    out = jnp.zeros((n_q, hd * dh), F32)
    for hh in range(hd):
        out = out + jnp.where(lane_head == hh, o_all[hh * n_q:(hh + 1) * n_q], 0.0)
    o_ref[...] = x_ref[...] + _dot(out.astype(BF16), wo_ref[...])


def _dsa_attn_sample(x, qb, qib, wi, kb, vb, kib, cache_k, cache_v, cache_ki, w_o, n_b, n_q, topk):
    t, d = x.shape
    hdm = DSA_HEADS * DSA_HEAD_DIM
    past = cache_k.shape[1]
    per = LANES // n_q
    return pl.pallas_call(
        functools.partial(_dsa_samp_kernel, n_q=n_q, topk=topk),
        grid=(n_b,),
        in_specs=[
            pl.BlockSpec((n_q, hdm), lambda b: (b, 0)),
            pl.BlockSpec((IDX_HEADS, n_q, IDX_DIM), lambda b: (0, b, 0)),
            pl.BlockSpec((n_q, IDX_HEADS), lambda b: (b, 0)),
            pl.BlockSpec((1, past, hdm), lambda b: (b, 0, 0)),
            pl.BlockSpec((1, past, hdm), lambda b: (b, 0, 0)),
            pl.BlockSpec((1, past, IDX_DIM), lambda b: (b, 0, 0)),
            pl.BlockSpec((LANES, hdm), lambda b: (b // per, 0)),
            pl.BlockSpec((LANES, hdm), lambda b: (b // per, 0)),
            pl.BlockSpec((LANES, IDX_DIM), lambda b: (b // per, 0)),
            pl.BlockSpec((n_q, d), lambda b: (b, 0)),
            _whole(w_o.shape),
        ],
        out_specs=pl.BlockSpec((n_q, d), lambda b: (b, 0)),
        out_shape=jax.ShapeDtypeStruct((t, d), F32),
        compiler_params=_params("parallel"),
        name="dsa_attn_sample",
    )(qb, qib, wi, cache_k.reshape(n_b, past, hdm), cache_v.reshape(n_b, past, hdm), cache_ki, kb, vb, kib, x, w_o)


def kernel(x_prompt, x_sample, cache_mla_ckv, cache_mla_krope, cache_dsa_k, cache_dsa_v, cache_dsa_kidx, norm_mix, norm_ffn, norm_final, mla_w_dq, mla_g_q, mla_w_uq, mla_w_dkv, mla_g_kv, mla_w_ukv, mla_w_o, cmlp_w_in, cmlp_ln_g, cmlp_ln_b, cmlp_w_s, cmlp_b_s, cmlp_w_out, dsa_w_qkv, dsa_w_o, dsa_w_qidx, dsa_w_kidx, dsa_g_kidx, dsa_w_widx, ffn_w_in, ffn_w_out):
    batch, seq, d = x_prompt.shape
    n_b, n_q, _ = x_sample.shape
    past = cache_mla_ckv.shape[2]
    depth = norm_mix.shape[0]
    xp = x_prompt.reshape(batch * seq, d)
    xs = x_sample.reshape(n_b * n_q, d)
    tab_p = _rope_tables(jnp.arange(seq))
    tab_s = tuple(jnp.tile(a, (n_b, 1)) for a in _rope_tables(past + jnp.arange(n_q)))
    cast = lambda a: a.astype(BF16)
    outs = {k: [] for k in ("ckv_p", "kr_p", "ckv_s", "kr_s", "cv_s", "dk_p", "dv_p", "di_p", "dk_s", "dv_s", "di_s")}
    for i in range(depth):
        kind, j = i % 3, i // 3
        if kind == 0:
            wts = _mla_weights(mla_w_dq[j], mla_w_uq[j], mla_w_dkv[j], mla_w_ukv[j], mla_w_o[j])
            qt, kh, vt, ckv, kr = _mla_proj_t(xp, norm_mix[i], wts, mla_g_q[j], mla_g_kv[j], tab_p, seq)
            xp = _attn_t(xp, qt, kh, vt, None, wts["wot"], batch, seq, 256, "mla_attn_prompt")
            outs["ckv_p"].append(ckv.reshape(batch, seq, -1)); outs["kr_p"].append(kr.reshape(batch, seq, -1))
            ql, qr, ckv, kr, ckvb, krb = _mla_proj(xs, norm_mix[i], wts, mla_g_q[j], mla_g_kv[j], tab_s)
            xs = _mla_attn_sample(xs, ql, qr, ckvb, krb, cache_mla_ckv[j], cache_mla_krope[j], wts, n_b, n_q)
            outs["ckv_s"].append(ckv.reshape(n_b, n_q, -1)); outs["kr_s"].append(kr.reshape(n_b, n_q, -1))
        elif kind == 1:
            w_in, w_out = cast(cmlp_w_in[j]), cast(cmlp_w_out[j])
            xp, _ = _cmlp(xp, norm_mix[i], w_in, cmlp_ln_g[j], cmlp_ln_b[j], cmlp_w_s[j], cmlp_b_s[j], w_out,
                          min(seq, CMLP_CHUNK), False)
            xs, v_s = _cmlp(xs, norm_mix[i], w_in, cmlp_ln_g[j], cmlp_ln_b[j], cmlp_w_s[j], cmlp_b_s[j], w_out,
                            min(n_q, CMLP_CHUNK), True)
            outs["cv_s"].append(v_s.reshape(n_b, n_q, -1))
        else:
            hdm = DSA_HEADS * DSA_HEAD_DIM
            w_qkv, w_qidx, w_o = cast(dsa_w_qkv[j]), cast(dsa_w_qidx[j]), cast(dsa_w_o[j])
            w_kidx, w_widx = cast(dsa_w_kidx[j]), cast(dsa_w_widx[j])
            zpad = lambda n: jnp.zeros((d, n), BF16)
            wt = dict(wkv=w_qkv[:, hdm:], wqt=w_qkv[:, :hdm].T, wvt=w_qkv[:, 2 * hdm:].T, wqit=w_qidx.T,
                      wkw=jnp.concatenate([w_kidx, zpad(LANES - IDX_DIM)], axis=1),
                      wwt=jnp.concatenate([w_widx, zpad(BF16_ROWS - IDX_HEADS)], axis=1).T)
            hshape = (DSA_HEADS, DSA_HEAD_DIM)
            k, v, ki, qt, kh, vt, qit, kib, wit = _dsa_proj_t(xp, norm_mix[i], wt, dsa_g_kidx[j])
            bias = _dsa_select(qit, wit, kib, batch, seq, min(TOPK_MAX, seq // 4))
            xp = _attn_t(xp, qt, kh, vt, bias, w_o.T, batch, seq, 512, "dsa_attn_prompt")
            outs["dk_p"].append(k.reshape((batch, seq) + hshape)); outs["dv_p"].append(v.reshape((batch, seq) + hshape))
            outs["di_p"].append(ki.reshape(batch, seq, -1))
            w_kw = jnp.concatenate([w_kidx, w_widx, zpad(LANES - IDX_DIM - IDX_HEADS)], axis=1)
            k, v, ki, qb, kb, vb, qib, kib, wi = _dsa_proj(xs, norm_mix[i], w_qkv, w_qidx, w_kw, dsa_g_kidx[j])
            xs = _dsa_attn_sample(xs, qb, qib, wi, kb, vb, kib, cache_dsa_k[j], cache_dsa_v[j], cache_dsa_kidx[j], w_o,
                                  n_b, n_q, min(TOPK_MAX, (past + n_q) // 4))
            outs["dk_s"].append(k.reshape((n_b, n_q) + hshape)); outs["dv_s"].append(v.reshape((n_b, n_q) + hshape))
            outs["di_s"].append(ki.reshape(n_b, n_q, -1))
        w_in, w_out = cast(ffn_w_in[i]), cast(ffn_w_out[i])
        final = i == depth - 1
        xp = _ffn(xp, norm_ffn[i], w_in, w_out, norm_final, final)
        xs = _ffn(xs, norm_ffn[i], w_in, w_out, norm_final, final)
    st = lambda name: jnp.stack(outs[name])
    return (xp.reshape(batch, seq, d), xs.reshape(n_b, n_q, d),
            st("ckv_p"), st("kr_p"), st("ckv_s"), st("kr_s"), st("cv_s"),
            st("dk_p"), st("dv_p"), st("di_p"), st("dk_s"), st("dv_s"), st("di_s"))
```

```python
import functools

import numpy as np
import jax
import jax.numpy as jnp
from jax import lax
from jax.experimental import pallas as pl
from jax.experimental.pallas import tpu as pltpu

F32, BF16, I32 = jnp.float32, jnp.bfloat16, jnp.int32

CHUNK = 64
EPS = 1e-6
MLA_HEADS, MLA_Q_LORA, MLA_KV_LORA, MLA_NOPE, MLA_ROPE, MLA_V = 16, 512, 256, 64, 32, 64
ROPE_BASE = 10000.0
MLA_SCALE = (MLA_NOPE + MLA_ROPE) ** -0.5
CMLP_CHUNK, CMLP_WIDTH, CMLP_GROUPS = 128, 2048, 8
DSA_HEADS, DSA_HEAD_DIM = 16, 64
DSA_SCALE = DSA_HEAD_DIM ** -0.5
IDX_HEADS, IDX_DIM = 8, 64
IDX_SCALE = IDX_DIM ** -0.5
TOPK_MAX = 256

LANES = 128
MXU_TILE = 256
BF16_ROWS = 16
VMEM_LIMIT = 52 * 1024 * 1024
NEG_INF = float("-inf")
INT_MIN = -2 ** 31
KEY_NEG_INF = -2139095041
LOG2E = float(np.log2(np.e))
KEY_NORM_SLACK = 1.01
LAG_MARGIN = 100.0
HEAD_V = 64
V_ROWS = HEAD_V + BF16_ROWS


def _dot(a, b):
    return jnp.dot(a, b, preferred_element_type=F32)


def _dot_t(a, b):
    return lax.dot_general(a, b, (((1,), (1,)), ((), ())), preferred_element_type=F32)


def _rms(x, g):
    return x * lax.rsqrt(jnp.mean(x * x, axis=-1, keepdims=True) + EPS) * g


def _log2(n):
    assert n > 0 and n & (n - 1) == 0, n
    return n.bit_length() - 1


def _div(x, n):
    return lax.shift_right_logical(x, jnp.int32(_log2(n)))


def _mod(x, n):
    assert n & (n - 1) == 0, n
    return x & (n - 1)


def _params(*sem):
    return pltpu.CompilerParams(dimension_semantics=sem, vmem_limit_bytes=VMEM_LIMIT)


def _whole(shape):
    nd = len(shape)
    return pl.BlockSpec(shape, lambda *_: (0,) * nd)


def _ffn_kernel(x_ref, g_ref, wg_ref, wu_ref, wo_ref, gf_ref, o_ref, act_scr, *, final):
    x = x_ref[...]
    h = _rms(x, g_ref[...]).astype(BF16)
    f = wo_ref.shape[0]
    chunk = MXU_TILE if f % MXU_TILE == 0 else f
    nc = f // chunk
    nxt = (_dot(h, wg_ref[:, 0:chunk]), _dot(h, wu_ref[:, 0:chunk]))
    for c in range(nc):
        gate, up = nxt
        if c + 1 < nc:
            lo = (c + 1) * chunk
            nxt = (_dot(h, wg_ref[:, lo:lo + chunk]), _dot(h, wu_ref[:, lo:lo + chunk]))
        act_scr[:, c * chunk:(c + 1) * chunk] = (jax.nn.silu(gate) * up).astype(BF16)
    y = x + _dot(act_scr[...], wo_ref[...])
    if final:
        y = _rms(y, gf_ref[...])
    o_ref[...] = y


def _ffn(x, g, w_in, w_out, g_final, final):
    t, d = x.shape
    f = w_out.shape[0]
    tm = min(512, t)
    once = pl.Buffered(1)
    return pl.pallas_call(
        functools.partial(_ffn_kernel, final=final),
        grid=(t // tm,),
        in_specs=[
            pl.BlockSpec((tm, d), lambda i: (i, 0)),
            _whole((1, d)),
            pl.BlockSpec((d, f), lambda i: (0, 0), pipeline_mode=once),
            pl.BlockSpec((d, f), lambda i: (0, 1), pipeline_mode=once),
            pl.BlockSpec((f, d), lambda i: (0, 0), pipeline_mode=once),
            _whole((1, d)),
        ],
        out_specs=pl.BlockSpec((tm, d), lambda i: (i, 0)),
        out_shape=jax.ShapeDtypeStruct((t, d), F32),
        scratch_shapes=[pltpu.VMEM((tm, f), BF16)],
        compiler_params=_params("parallel"),
        name="ffn",
    )(x, g.reshape(1, d), w_in, w_in, w_out, g_final.reshape(1, d))


def _gelu(x):
    return 0.5 * x * (1.0 + lax.erf(x * np.float32(np.sqrt(0.5))))


def _layernorm(x, g, b):
    mu = jnp.mean(x, axis=-1, keepdims=True)
    xc = x - mu
    return xc * lax.rsqrt(jnp.mean(xc * xc, axis=-1, keepdims=True) + EPS) * g + b


def _cmlp_kernel(x_ref, g_ref, win_ref, lng_ref, lnb_ref, ws_ref, bs_ref, wout_ref, *rest, n_rows, write_v):
    if write_v:
        o_ref, v_ref, vb_scr, gated_scr = rest
    else:
        o_ref, vb_scr, gated_scr = rest
    tm = x_ref.shape[0]
    w = CMLP_WIDTH
    gw = w // CMLP_GROUPS
    c = CMLP_CHUNK
    x = x_ref[...]
    h = _rms(x, g_ref[...]).astype(BF16)
    v = _layernorm(_gelu(_dot(h, win_ref[:, w:])), lng_ref[...], lnb_ref[...])
    if write_v:
        v_ref[...] = v
    vb_scr[...] = v.astype(BF16)
    r_i = lax.broadcasted_iota(I32, (c, c), 0)
    c_i = lax.broadcasted_iota(I32, (c, c), 1)
    keep = jnp.where(c_i >= r_i - _mod(r_i, n_rows), jnp.where(c_i <= r_i, 1, 0), 0) > 0
    u_next = _dot(h, win_ref[:, 0:gw])
    for g in range(CMLP_GROUPS):
        lo, hi = g * gw, (g + 1) * gw
        u_raw = u_next
        if g + 1 < CMLP_GROUPS:
            u_next = _dot(h, win_ref[:, hi:hi + gw])
        wg = jnp.where(keep, ws_ref[g], 0.0).astype(BF16)
        bias = bs_ref[:, g:g + 1]
        mixed = jnp.concatenate(
            [_dot(wg, vb_scr[k * c:(k + 1) * c, lo:hi]) + bias for k in range(tm // c)], axis=0)
        gated_scr[:, lo:hi] = (_gelu(u_raw) * mixed).astype(BF16)
    o_ref[...] = x + _dot(gated_scr[...], wout_ref[...])


def _cmlp(x, g, w_in, ln_g, ln_b, w_s, b_s, w_out, n_rows, write_v):
    t, d = x.shape
    w = CMLP_WIDTH
    c = CMLP_CHUNK
    tm = min(512, t)
    rep = c // n_rows
    ws_t = jnp.tile(w_s[:, :n_rows, :n_rows], (1, rep, rep))
    bs_t = jnp.tile(b_s[:, :n_rows].T, (rep, 1))
    out_shape = [jax.ShapeDtypeStruct((t, d), F32)]
    out_specs = [pl.BlockSpec((tm, d), lambda i: (i, 0))]
    if write_v:
        out_shape.append(jax.ShapeDtypeStruct((t, w), F32))
        out_specs.append(pl.BlockSpec((tm, w), lambda i: (i, 0)))
    res = pl.pallas_call(
        functools.partial(_cmlp_kernel, n_rows=n_rows, write_v=write_v),
        grid=(t // tm,),
        in_specs=[
            pl.BlockSpec((tm, d), lambda i: (i, 0)),
            _whole((1, d)), _whole((d, 2 * w)), _whole((1, w)), _whole((1, w)),
            _whole((CMLP_GROUPS, c, c)), _whole((c, CMLP_GROUPS)), _whole((w, d)),
        ],
        out_specs=out_specs,
        out_shape=out_shape,
        scratch_shapes=[pltpu.VMEM((tm, w), BF16), pltpu.VMEM((tm, w), BF16)],
        compiler_params=_params("parallel"),
        name="cmlp",
    )(x, g.reshape(1, d), w_in, ln_g.reshape(1, w), ln_b.reshape(1, w), ws_t, bs_t, w_out)
    return res if write_v else (res[0], None)


def _attn_t_kernel(qi_ref, kj_ref, last_ref, nsub_ref, qt_ref, k_ref, vt_ref, *rest, tq, tk, sub, use_bias):
    if use_bias:
        bias_ref, kn_ref, x_ref, wot_ref, o_ref, m_scr, pend_scr, qn_scr, acc_scr, cat_scr = rest
    else:
        kn_ref, x_ref, wot_ref, o_ref, m_scr, pend_scr, qn_scr, acc_scr, cat_scr = rest
    p = pl.program_id(1)
    qi, kj = qi_ref[p], kj_ref[p]
    hd = qt_ref.shape[0]

    @pl.when(kj == 0)
    def _():
        m_scr[...] = jnp.full_like(m_scr, NEG_INF)
        pend_scr[...] = jnp.ones_like(pend_scr)
        acc_scr[...] = jnp.zeros_like(acc_scr)
        for hh in range(hd):
            q = qt_ref[hh].astype(F32)
            qn_scr[hh] = jnp.sqrt(jnp.sum(q * q, axis=0, keepdims=True)) * kn_ref[0, hh:hh + 1, 0:1]

    def step(off, mask, lagged):
        keys = lambda hh: k_ref[hh, pl.ds(off, sub), :]
        vals = lambda hh: vt_ref[hh, :, pl.ds(off, sub)]
        s_next = _dot(keys(0), qt_ref[0])
        pend = None
        for hh in range(hd):
            s = s_next
            if hh + 1 < hd:
                s_next = _dot(keys(hh + 1), qt_ref[hh + 1])
            if mask is not None:
                s = s + mask
            m_prev = m_scr[hh]
            m_new = jnp.maximum(m_prev, jnp.max(s, axis=0, keepdims=True))
            if lagged:
                scale = pend_scr[hh]
                pe = jnp.exp2(s - m_prev).astype(BF16)
                pend_scr[hh] = jnp.exp2(m_prev - m_new)
            else:
                m_safe = jnp.where(m_new == NEG_INF, 0.0, m_new)
                scale = pend_scr[hh] * jnp.exp2(m_prev - m_safe)
                pe = jnp.exp2(s - m_safe).astype(BF16)
                pend_scr[hh] = jnp.ones_like(m_prev)
            m_scr[hh] = m_new
            if pend is not None:
                ph, pa, pp = pend
                acc_scr[ph] = pa * acc_scr[ph] + _dot(vals(ph), pp)
            pend = (hh, scale, pe)
        ph, pa, pp = pend
        acc_scr[ph] = pa * acc_scr[ph] + _dot(vals(ph), pp)

    def sub_body(j, carry):
        off = pl.multiple_of(j * sub, sub)
        excess = qn_scr[0] - m_scr[0]
        for hh in range(1, hd):
            excess = jnp.maximum(excess, qn_scr[hh] - m_scr[hh])
        lag_ok = jnp.max(excess) <= LAG_MARGIN
        if use_bias:
            bias = bias_ref[pl.ds(off, sub), :].astype(F32)

            @pl.when(lag_ok)
            def _():
                step(off, bias, True)

            @pl.when(jnp.logical_not(lag_ok))
            def _():
                step(off, bias, False)
        else:
            base = kj * tk + off
            needs_mask = _div(base + sub - 1, CHUNK) > _div(qi * tq, CHUNK)

            def chunk_mask():
                k_chunk = _div(base + lax.broadcasted_iota(I32, (sub, 1), 0), CHUNK)
                q_chunk = _div(qi * tq + lax.broadcasted_iota(I32, (1, tq), 1), CHUNK)
                return jnp.where(k_chunk <= q_chunk, 0.0, NEG_INF)

            @pl.when(jnp.logical_not(lag_ok))
            def _():
                step(off, chunk_mask(), False)

            @pl.when(jnp.logical_and(lag_ok, needs_mask))
            def _():
                step(off, chunk_mask(), True)

            @pl.when(jnp.logical_and(lag_ok, jnp.logical_not(needs_mask)))
            def _():
                step(off, None, True)
        return carry

    lax.fori_loop(0, nsub_ref[p], sub_body, 0)

    @pl.when(last_ref[p] == 1)
    def _():
        for hh in range(hd):
            a = acc_scr[hh]
            cat_scr[hh * HEAD_V:(hh + 1) * HEAD_V, :] = (a[:HEAD_V] / a[HEAD_V:HEAD_V + 1]).astype(BF16)
        o_ref[...] = x_ref[...] + _dot(wot_ref[...], cat_scr[...]).T


def _causal_pairs(nq, tq, tk, sub):
    qi, kj, last, nsub = [], [], [], []
    per = tk // sub
    for q in range(nq):
        vis_sub = -(-((q + 1) * tq) // sub)
        nvis = -(-vis_sub // per)
        for k in range(nvis):
            qi.append(q); kj.append(k); last.append(int(k == nvis - 1)); nsub.append(min(per, vis_sub - k * per))
    as_arr = lambda v: jnp.asarray(np.array(v, np.int32))
    return as_arr(qi), as_arr(kj), as_arr(last), as_arr(nsub)


def _attn_t(x, qt, kh, vt, kn2, bias, wot, batch, seq, sub, name):
    t, d = x.shape
    hd, dk, _ = qt.shape
    tq, tk, sub = min(512, seq), min(1024, seq), min(sub, seq)
    nq, nk = seq // tq, seq // tk
    qi, kj, last, nsub = _causal_pairs(nq, tq, tk, sub)
    use_bias = bias is not None
    in_specs = [
        pl.BlockSpec((hd, dk, tq), lambda b, p, qi, kj, la, ns: (0, 0, b * nq + qi[p])),
        pl.BlockSpec((hd, tk, dk), lambda b, p, qi, kj, la, ns: (0, b * nk + kj[p], 0)),
        pl.BlockSpec((hd, V_ROWS, tk), lambda b, p, qi, kj, la, ns: (0, 0, b * nk + kj[p])),
    ]
    args = [qt, kh, vt]
    if use_bias:
        in_specs.append(pl.BlockSpec((tk, tq), lambda b, p, qi, kj, la, ns: (kj[p], b * nq + qi[p])))
        args.append(bias)
    kn = jnp.sqrt(jnp.max(kn2[:, :, 0].reshape(batch, -1, hd), axis=1)) * KEY_NORM_SLACK
    in_specs.append(pl.BlockSpec((1, hd, LANES), lambda b, p, qi, kj, la, ns: (b, 0, 0)))
    args.append(jnp.broadcast_to(kn[:, :, None], (batch, hd, LANES)))
    in_specs += [
        pl.BlockSpec((tq, d), lambda b, p, qi, kj, la, ns: (b * nq + qi[p], 0)),
        pl.BlockSpec(wot.shape, lambda b, p, qi, kj, la, ns: (0, 0)),
    ]
    args += [x, wot]
    grid_spec = pltpu.PrefetchScalarGridSpec(
        num_scalar_prefetch=4,
        grid=(batch, int(qi.shape[0])),
        in_specs=in_specs,
        out_specs=pl.BlockSpec((tq, d), lambda b, p, qi, kj, la, ns: (b * nq + qi[p], 0)),
        scratch_shapes=[
            pltpu.VMEM((hd, 1, tq), F32), pltpu.VMEM((hd, 1, tq), F32), pltpu.VMEM((hd, 1, tq), F32),
            pltpu.VMEM((hd, V_ROWS, tq), F32), pltpu.VMEM((hd * HEAD_V, tq), BF16),
        ],
    )
    return pl.pallas_call(
        functools.partial(_attn_t_kernel, tq=tq, tk=tk, sub=sub, use_bias=use_bias),
        grid_spec=grid_spec,
        out_shape=jax.ShapeDtypeStruct((t, d), F32),
        compiler_params=_params("parallel", "arbitrary"),
        name=name,
    )(qi, kj, last, nsub, *args)


def _rope_tables(pos):
    half = MLA_ROPE // 2
    inv = ROPE_BASE ** (-jnp.arange(half, dtype=F32) / half)
    ang = pos.astype(F32)[:, None] * inv[None, :]
    cos, sin = jnp.cos(ang), jnp.sin(ang)
    cos_k = jnp.concatenate([cos, cos], axis=1)
    sin_k = jnp.concatenate([-sin, sin], axis=1)
    return jnp.tile(cos_k, (1, MLA_HEADS)), jnp.tile(sin_k, (1, MLA_HEADS)), cos_k, sin_k


def _swap_halves(w, group):
    shp = w.shape
    wr = w.reshape(shp[:-1] + (shp[-1] // group, 2, group // 2))
    return wr[..., ::-1, :].reshape(shp)


def _mla_weights(w_dq, w_uq, w_dkv, w_ukv, w_o):
    hd = MLA_HEADS
    cast = lambda a: a.astype(BF16)
    w_uq, w_dkv, w_ukv = cast(w_uq), cast(w_dkv), cast(w_ukv)
    wqn = w_uq[:, :, :MLA_NOPE].reshape(MLA_Q_LORA, hd * MLA_NOPE)
    wqr = w_uq[:, :, MLA_NOPE:].reshape(MLA_Q_LORA, hd * MLA_ROPE)
    wqs = _swap_halves(wqr, MLA_ROPE)
    wkc, wkr = w_dkv[:, :MLA_KV_LORA], w_dkv[:, MLA_KV_LORA:]
    wks = _swap_halves(wkr, MLA_ROPE)
    wuk = jnp.transpose(w_ukv[:, :, :MLA_NOPE], (1, 2, 0))
    zeros = jnp.zeros_like(wuk)
    even = jnp.concatenate([wuk, zeros], axis=1)
    odd = jnp.concatenate([zeros, wuk], axis=1)
    wuk2 = jnp.where((jnp.arange(hd) % 2 == 0)[:, None, None], even, odd)
    wuv = jnp.transpose(w_ukv[:, :, MLA_NOPE:], (1, 0, 2))
    d = w_dkv.shape[0]
    pad_r = lambda a: jnp.concatenate([a, jnp.zeros((d, LANES - MLA_ROPE), BF16)], axis=1)
    wukp = jnp.concatenate([jnp.zeros((MLA_KV_LORA, hd, MLA_ROPE), BF16), w_ukv[:, :, :MLA_NOPE],
                            jnp.zeros((MLA_KV_LORA, hd, LANES - MLA_ROPE - MLA_NOPE), BF16)], axis=2)
    return dict(wdq=cast(w_dq), wqn=wqn, wqr=wqr, wqs=wqs, wkc=wkc, wkr=wkr, wks=wks, wuk=wuk2, wuv=wuv, wo=cast(w_o),
                wqnt=wqn.T, wqrt=wqr.T, wqst=wqs.T, wkrp=pad_r(wkr), wksp=pad_r(wks),
                wukp=wukp.reshape(MLA_KV_LORA, hd * LANES), wuvt=wuv.transpose(0, 2, 1).reshape(hd * MLA_V, MLA_KV_LORA),
                wot=cast(w_o).T)


def _mla_proj_kernel(x_ref, gm_ref, wdq_ref, gq_ref, wqn_ref, wqr_ref, wqs_ref, wkc_ref, wkr_ref, wks_ref,
                     gkv_ref, wuk_ref, cq_ref, sq_ref, ck_ref, sk_ref,
                     ql_ref, qr_ref, ckv_ref, kr_ref, ckvb_ref, krb_ref):
    h = _rms(x_ref[...], gm_ref[...]).astype(BF16)
    cq = _rms(_dot(h, wdq_ref[...]), gq_ref[...]).astype(BF16)
    qn = _dot(cq, wqn_ref[...])
    qr = ((_dot(cq, wqr_ref[...]) * cq_ref[...] + _dot(cq, wqs_ref[...]) * sq_ref[...]) * MLA_SCALE).astype(BF16)
    for hh in range(MLA_HEADS):
        pair = qn[:, (hh // 2) * LANES:(hh // 2 + 1) * LANES].astype(BF16)
        ql_ref[hh] = (_dot(pair, wuk_ref[hh]) * MLA_SCALE).astype(BF16)
        qr_ref[hh] = qr[:, hh * MLA_ROPE:(hh + 1) * MLA_ROPE]
    ckv = _rms(_dot(h, wkc_ref[...]), gkv_ref[...])
    ckv_ref[...] = ckv
    ckvb_ref[...] = ckv.astype(BF16)
    kr = _dot(h, wkr_ref[...]) * ck_ref[...] + _dot(h, wks_ref[...]) * sk_ref[...]
    kr_ref[...] = kr
    krb_ref[...] = kr.astype(BF16)


def _mla_proj(x, g_mix, wts, g_q, g_kv, tables):
    t, d = x.shape
    tm = min(256, t)
    cos_q, sin_q, cos_k, sin_k = tables
    nrep = cos_q.shape[0] // tm
    tab = lambda wdt: pl.BlockSpec((tm, wdt), lambda i: (i % nrep, 0))
    hd = MLA_HEADS
    row = lambda wdt: pl.BlockSpec((tm, wdt), lambda i: (i, 0))
    hm = lambda wdt: pl.BlockSpec((hd, tm, wdt), lambda i: (0, i, 0))
    return pl.pallas_call(
        _mla_proj_kernel,
        grid=(t // tm,),
        in_specs=[
            row(d), _whole((1, d)), _whole(wts["wdq"].shape), _whole((1, MLA_Q_LORA)),
            _whole(wts["wqn"].shape), _whole(wts["wqr"].shape), _whole(wts["wqs"].shape),
            _whole(wts["wkc"].shape), _whole(wts["wkr"].shape), _whole(wts["wks"].shape),
            _whole((1, MLA_KV_LORA)), _whole(wts["wuk"].shape),
            tab(hd * MLA_ROPE), tab(hd * MLA_ROPE), tab(MLA_ROPE), tab(MLA_ROPE),
        ],
        out_specs=[hm(MLA_KV_LORA), hm(MLA_ROPE), row(MLA_KV_LORA), row(MLA_ROPE), row(MLA_KV_LORA), row(MLA_ROPE)],
        out_shape=[
            jax.ShapeDtypeStruct((hd, t, MLA_KV_LORA), BF16), jax.ShapeDtypeStruct((hd, t, MLA_ROPE), BF16),
            jax.ShapeDtypeStruct((t, MLA_KV_LORA), F32), jax.ShapeDtypeStruct((t, MLA_ROPE), F32),
            jax.ShapeDtypeStruct((t, MLA_KV_LORA), BF16), jax.ShapeDtypeStruct((t, MLA_ROPE), BF16),
        ],
        compiler_params=_params("parallel"),
        name="mla_proj",
    )(x, g_mix.reshape(1, d), wts["wdq"], g_q.reshape(1, -1), wts["wqn"], wts["wqr"], wts["wqs"],
      wts["wkc"], wts["wkr"], wts["wks"], g_kv.reshape(1, -1), wts["wuk"], cos_q, sin_q, cos_k, sin_k)


def _mla_proj_t_kernel(x_ref, gm_ref, wdq_ref, gq_ref, wqnt_ref, wqrt_ref, wqst_ref, wkc_ref, wkrp_ref, wksp_ref,
                       gkv_ref, wukp_ref, wuvt_ref, cqt_ref, sqt_ref, ckp_ref, skp_ref,
                       qt_ref, kh_ref, vt_ref, ckv_ref, kr_ref, kn2_ref):
    tm = x_ref.shape[0]
    hd = MLA_HEADS
    qscale = MLA_SCALE * LOG2E
    h = _rms(x_ref[...], gm_ref[...]).astype(BF16)
    cq_raw = _dot(h, wdq_ref[...])
    ckv_raw = _dot(h, wkc_ref[...])
    kr_a = _dot(h, wkrp_ref[...])
    kr_b = _dot(h, wksp_ref[...])
    cq = _rms(cq_raw, gq_ref[...]).astype(BF16)
    qnt_raw = _dot_t(wqnt_ref[...], cq)
    qrt_a = _dot_t(wqrt_ref[...], cq)
    qrt_b = _dot_t(wqst_ref[...], cq)
    ckv = _rms(ckv_raw, gkv_ref[...])
    ckv_ref[...] = ckv
    cb = ckv.astype(BF16)
    kn = _dot(cb, wukp_ref[...])
    vt = _dot_t(wuvt_ref[...], cb).astype(BF16)
    qnt = (qnt_raw * qscale).astype(BF16)
    qrt = ((qrt_a * cqt_ref[...] + qrt_b * sqt_ref[...]) * qscale).astype(BF16)
    zeros = jnp.zeros((LANES - MLA_ROPE - MLA_NOPE, tm), BF16)
    krp = kr_a * ckp_ref[...] + kr_b * skp_ref[...]
    kr_ref[...] = krp[:, :MLA_ROPE]
    ones = jnp.ones((BF16_ROWS, tm), BF16)
    for hh in range(hd):
        qt_ref[hh, 0:MLA_ROPE, :] = qrt[hh * MLA_ROPE:(hh + 1) * MLA_ROPE]
        qt_ref[hh, MLA_ROPE:MLA_ROPE + MLA_NOPE, :] = qnt[hh * MLA_NOPE:(hh + 1) * MLA_NOPE]
        qt_ref[hh, MLA_ROPE + MLA_NOPE:, :] = zeros
        key = kn[:, hh * LANES:(hh + 1) * LANES] + krp
        kh_ref[hh] = key.astype(BF16)
        kn2_ref[0, hh:hh + 1, :] = jnp.broadcast_to(
            jnp.max(jnp.sum(key * key, axis=1, keepdims=True), axis=0, keepdims=True), (1, LANES))
        vt_ref[hh, 0:MLA_V, :] = vt[hh * MLA_V:(hh + 1) * MLA_V]
        vt_ref[hh, MLA_V:, :] = ones


def _mla_proj_t(x, g_mix, wts, g_q, g_kv, tables, seq):
    t, d = x.shape
    tm = min(256, seq)
    hd = MLA_HEADS
    cos_q, sin_q, cos_k, sin_k = tables
    pad = lambda a: jnp.concatenate([a, jnp.zeros((seq, LANES - MLA_ROPE), F32)], axis=1)
    nrep = seq // tm
    row = lambda wdt: pl.BlockSpec((tm, wdt), lambda i: (i, 0))
    names = ("wdq", "wqnt", "wqrt", "wqst", "wkc", "wkrp", "wksp", "wukp", "wuvt")
    w = {n: wts[n] for n in names}
    return pl.pallas_call(
        _mla_proj_t_kernel,
        grid=(t // tm,),
        in_specs=[
            row(d), _whole((1, d)), _whole(w["wdq"].shape), _whole((1, MLA_Q_LORA)),
            _whole(w["wqnt"].shape), _whole(w["wqrt"].shape), _whole(w["wqst"].shape),
            _whole(w["wkc"].shape), _whole(w["wkrp"].shape), _whole(w["wksp"].shape),
            _whole((1, MLA_KV_LORA)), _whole(w["wukp"].shape), _whole(w["wuvt"].shape),
            pl.BlockSpec((hd * MLA_ROPE, tm), lambda i: (0, i % nrep)),
            pl.BlockSpec((hd * MLA_ROPE, tm), lambda i: (0, i % nrep)),
            pl.BlockSpec((tm, LANES), lambda i: (i % nrep, 0)),
            pl.BlockSpec((tm, LANES), lambda i: (i % nrep, 0)),
        ],
        out_specs=[
            pl.BlockSpec((hd, LANES, tm), lambda i: (0, 0, i)),
            pl.BlockSpec((hd, tm, LANES), lambda i: (0, i, 0)),
            pl.BlockSpec((hd, V_ROWS, tm), lambda i: (0, 0, i)),
            row(MLA_KV_LORA), row(MLA_ROPE),
            pl.BlockSpec((1, hd, LANES), lambda i: (i, 0, 0)),
        ],
        out_shape=[
            jax.ShapeDtypeStruct((hd, LANES, t), BF16), jax.ShapeDtypeStruct((hd, t, LANES), BF16),
            jax.ShapeDtypeStruct((hd, V_ROWS, t), BF16),
            jax.ShapeDtypeStruct((t, MLA_KV_LORA), F32), jax.ShapeDtypeStruct((t, MLA_ROPE), F32),
            jax.ShapeDtypeStruct((t // tm, hd, LANES), F32),
        ],
        compiler_params=_params("parallel"),
        name="mla_proj_t",
    )(x, g_mix.reshape(1, d), w["wdq"], g_q.reshape(1, -1), w["wqnt"], w["wqrt"], w["wqst"],
      w["wkc"], w["wkrp"], w["wksp"], g_kv.reshape(1, -1), w["wukp"], w["wuvt"],
      cos_q.T, sin_q.T, pad(cos_k), pad(sin_k))


def _mla_epilogue(o_lat, x_ref, wuv_ref, wo_ref, o_ref, cat_scr, tq):
    ob = o_lat.astype(BF16)
    for hh in range(MLA_HEADS):
        cat_scr[:, hh * MLA_V:(hh + 1) * MLA_V] = _dot(ob[hh * tq:(hh + 1) * tq], wuv_ref[hh]).astype(BF16)
    o_ref[...] = x_ref[...] + _dot(cat_scr[...], wo_ref[...])


def _mla_samp_kernel(ql_ref, qr_ref, cc_ref, cr_ref, cn_ref, rn_ref, x_ref, wuv_ref, wo_ref, o_ref, cat_scr, *, n_q):
    b = pl.program_id(0)
    hd = MLA_HEADS
    ql = ql_ref[...].reshape(hd * n_q, MLA_KV_LORA)
    qr = qr_ref[...].reshape(hd * n_q, MLA_ROPE)
    cc = cc_ref[0].astype(BF16)
    cr = cr_ref[0].astype(BF16)
    cn = cn_ref[...]
    s1 = _dot_t(ql, cc) + _dot_t(qr, cr)
    s2 = _dot_t(ql, cn) + _dot_t(qr, rn_ref[...])
    own = _div(lax.broadcasted_iota(I32, (1, LANES), 1), n_q) == _mod(b, LANES // n_q)
    s2 = jnp.where(own, s2, NEG_INF)
    m = jnp.maximum(jnp.max(s1, axis=1, keepdims=True), jnp.max(s2, axis=1, keepdims=True))
    p1 = jnp.exp(s1 - m)
    p2 = jnp.exp(s2 - m)
    l = jnp.sum(p1, axis=1, keepdims=True) + jnp.sum(p2, axis=1, keepdims=True)
    o_lat = (_dot(p1.astype(BF16), cc) + _dot(p2.astype(BF16), cn)) / l
    _mla_epilogue(o_lat, x_ref, wuv_ref, wo_ref, o_ref, cat_scr, n_q)


def _mla_attn_sample(x, ql, qr, ckvb, krb, cache_c, cache_r, wts, n_b, n_q):
    t, d = x.shape
    hd = MLA_HEADS
    past = cache_c.shape[1]
    per = LANES // n_q
    return pl.pallas_call(
        functools.partial(_mla_samp_kernel, n_q=n_q),
        grid=(n_b,),
        in_specs=[
            pl.BlockSpec((hd, n_q, MLA_KV_LORA), lambda b: (0, b, 0)),
            pl.BlockSpec((hd, n_q, MLA_ROPE), lambda b: (0, b, 0)),
            pl.BlockSpec((1, past, MLA_KV_LORA), lambda b: (b, 0, 0)),
            pl.BlockSpec((1, past, MLA_ROPE), lambda b: (b, 0, 0)),
            pl.BlockSpec((LANES, MLA_KV_LORA), lambda b: (b // per, 0)),
            pl.BlockSpec((LANES, MLA_ROPE), lambda b: (b // per, 0)),
            pl.BlockSpec((n_q, d), lambda b: (b, 0)),
            _whole(wts["wuv"].shape), _whole(wts["wo"].shape),
        ],
        out_specs=pl.BlockSpec((n_q, d), lambda b: (b, 0)),
        out_shape=jax.ShapeDtypeStruct((t, d), F32),
        scratch_shapes=[pltpu.VMEM((n_q, hd * MLA_V), BF16)],
        compiler_params=_params("parallel"),
        name="mla_attn_sample",
    )(ql, qr, cache_c, cache_r, ckvb, krb, x, wts["wuv"], wts["wo"])


def _dsa_proj_kernel(x_ref, gm_ref, wqkv_ref, wqi_ref, wkw_ref, gki_ref,
                     k_ref, v_ref, ki_ref, qb_ref, kb_ref, vb_ref, qib_ref, kib_ref, wi_ref):
    hdm = DSA_HEADS * DSA_HEAD_DIM
    h = _rms(x_ref[...], gm_ref[...]).astype(BF16)
    qkv = _dot(h, wqkv_ref[...])
    q = qkv[:, :hdm] * DSA_SCALE
    k = qkv[:, hdm:2 * hdm]
    v = qkv[:, 2 * hdm:]
    k_ref[...] = k
    v_ref[...] = v
    qi = _dot(h, wqi_ref[...]) * IDX_SCALE
    kw = _dot(h, wkw_ref[...])
    ki = _rms(kw[:, :IDX_DIM], gki_ref[...])
    ki_ref[...] = ki
    kib_ref[...] = ki.astype(BF16)
    wi_ref[...] = kw[:, IDX_DIM:IDX_DIM + IDX_HEADS] * (IDX_HEADS ** -0.5)
    qb_ref[...] = q.astype(BF16)
    kb_ref[...] = k.astype(BF16)
    vb_ref[...] = v.astype(BF16)
    for hh in range(IDX_HEADS):
        qib_ref[hh] = qi[:, hh * IDX_DIM:(hh + 1) * IDX_DIM].astype(BF16)


def _dsa_proj(x, g_mix, w_qkv, w_qidx, w_kw, g_kidx):
    t, d = x.shape
    tm = min(512, t)
    hdm = DSA_HEADS * DSA_HEAD_DIM
    row = lambda wdt: pl.BlockSpec((tm, wdt), lambda i: (i, 0))
    return pl.pallas_call(
        _dsa_proj_kernel,
        grid=(t // tm,),
        in_specs=[row(d), _whole((1, d)), _whole(w_qkv.shape), _whole(w_qidx.shape), _whole(w_kw.shape),
                  _whole((1, IDX_DIM))],
        out_specs=[row(hdm), row(hdm), row(IDX_DIM), row(hdm), row(hdm), row(hdm),
                   pl.BlockSpec((IDX_HEADS, tm, IDX_DIM), lambda i: (0, i, 0)), row(IDX_DIM), row(IDX_HEADS)],
        out_shape=[
            jax.ShapeDtypeStruct((t, hdm), F32), jax.ShapeDtypeStruct((t, hdm), F32),
            jax.ShapeDtypeStruct((t, IDX_DIM), F32), jax.ShapeDtypeStruct((t, hdm), BF16),
            jax.ShapeDtypeStruct((t, hdm), BF16), jax.ShapeDtypeStruct((t, hdm), BF16),
            jax.ShapeDtypeStruct((IDX_HEADS, t, IDX_DIM), BF16), jax.ShapeDtypeStruct((t, IDX_DIM), BF16),
            jax.ShapeDtypeStruct((t, IDX_HEADS), F32),
        ],
        compiler_params=_params("parallel"),
        name="dsa_proj",
    )(x, g_mix.reshape(1, d), w_qkv, w_qidx, w_kw, g_kidx.reshape(1, IDX_DIM))


def _dsa_proj_t_kernel(x_ref, gm_ref, wkv_ref, wqt_ref, wvt_ref, wqit_ref, wkw_ref, wwt_ref, gki_ref,
                       k_ref, v_ref, ki_ref, qt_ref, kh_ref, vt_ref, qit_ref, kib_ref, wit_ref, kn2_ref):
    tm = x_ref.shape[0]
    hdm = DSA_HEADS * DSA_HEAD_DIM
    dh = DSA_HEAD_DIM
    h = _rms(x_ref[...], gm_ref[...]).astype(BF16)
    kv = _dot(h, wkv_ref[...])
    k = kv[:, :hdm]
    k_ref[...] = k
    v_ref[...] = kv[:, hdm:]
    kw = _dot(h, wkw_ref[...])
    ki = _rms(kw[:, :IDX_DIM], gki_ref[...])
    ki_ref[...] = ki
    kib_ref[...] = ki.astype(BF16)
    wit_ref[...] = _dot_t(wwt_ref[...], h)[:IDX_HEADS] * (IDX_HEADS ** -0.5)
    qt = (_dot_t(wqt_ref[...], h) * (DSA_SCALE * LOG2E)).astype(BF16)
    vt = _dot_t(wvt_ref[...], h).astype(BF16)
    qit = (_dot_t(wqit_ref[...], h) * IDX_SCALE).astype(BF16)
    ones = jnp.ones((BF16_ROWS, tm), BF16)
    for hh in range(DSA_HEADS):
        qt_ref[hh] = qt[hh * dh:(hh + 1) * dh]
        key = k[:, hh * dh:(hh + 1) * dh]
        kh_ref[hh] = key.astype(BF16)
        kn2_ref[0, hh:hh + 1, :] = jnp.broadcast_to(
            jnp.max(jnp.sum(key * key, axis=1, keepdims=True), axis=0, keepdims=True), (1, LANES))
        vt_ref[hh, 0:dh, :] = vt[hh * dh:(hh + 1) * dh]
        vt_ref[hh, dh:, :] = ones
    for hh in range(IDX_HEADS):
        qit_ref[hh] = qit[hh * IDX_DIM:(hh + 1) * IDX_DIM]


def _dsa_proj_t(x, g_mix, w, g_kidx):
    t, d = x.shape
    tm = min(256, t)
    hd, dh = DSA_HEADS, DSA_HEAD_DIM
    hdm = hd * dh
    row = lambda wdt: pl.BlockSpec((tm, wdt), lambda i: (i, 0))
    names = ("wkv", "wqt", "wvt", "wqit", "wkw", "wwt")
    return pl.pallas_call(
        _dsa_proj_t_kernel,
        grid=(t // tm,),
        in_specs=[row(d), _whole((1, d))] + [_whole(w[n].shape) for n in names] + [_whole((1, IDX_DIM))],
        out_specs=[
            row(hdm), row(hdm), row(IDX_DIM),
            pl.BlockSpec((hd, dh, tm), lambda i: (0, 0, i)),
            pl.BlockSpec((hd, tm, dh), lambda i: (0, i, 0)),
            pl.BlockSpec((hd, V_ROWS, tm), lambda i: (0, 0, i)),
            pl.BlockSpec((IDX_HEADS, IDX_DIM, tm), lambda i: (0, 0, i)),
            row(IDX_DIM),
            pl.BlockSpec((IDX_HEADS, tm), lambda i: (0, i)),
            pl.BlockSpec((1, hd, LANES), lambda i: (i, 0, 0)),
        ],
        out_shape=[
            jax.ShapeDtypeStruct((t, hdm), F32), jax.ShapeDtypeStruct((t, hdm), F32),
            jax.ShapeDtypeStruct((t, IDX_DIM), F32),
            jax.ShapeDtypeStruct((hd, dh, t), BF16), jax.ShapeDtypeStruct((hd, t, dh), BF16),
            jax.ShapeDtypeStruct((hd, V_ROWS, t), BF16),
            jax.ShapeDtypeStruct((IDX_HEADS, IDX_DIM, t), BF16), jax.ShapeDtypeStruct((t, IDX_DIM), BF16),
            jax.ShapeDtypeStruct((IDX_HEADS, t), F32),
            jax.ShapeDtypeStruct((t // tm, hd, LANES), F32),
        ],
        compiler_params=_params("parallel"),
        name="dsa_proj_t",
    )(x, g_mix.reshape(1, d), *[w[n] for n in names], g_kidx.reshape(1, IDX_DIM))


def _sort_key(score):
    bits = lax.bitcast_convert_type(score, I32)
    return jnp.where(bits < 0, bits ^ jnp.int32(0x7FFFFFFF), bits)


def _kth_largest_key(count, topk):
    c0 = count(lambda key, idx: jnp.where(key >= 0, 1, 0))
    t0 = jnp.where(c0 >= topk, jnp.int32(0), jnp.int32(INT_MIN))

    def bit_body(i, t):
        cand = t + lax.shift_left(jnp.int32(1), jnp.int32(30) - i)
        c = count(lambda key, idx: jnp.where(key >= cand, 1, 0))
        return jnp.where(c >= topk, cand, t)

    return lax.fori_loop(0, 31, bit_body, t0)


def _tie_cutoff(count, thr, rem, nbits):
    def bit_body(i, j):
        cand = j + lax.shift_left(jnp.int32(1), jnp.int32(nbits - 1) - i)
        c = count(lambda key, idx: jnp.where(key == thr, jnp.where(idx < cand, 1, 0), 0))
        return jnp.where(c < rem, cand, j)

    return lax.fori_loop(0, nbits, bit_body, jnp.zeros_like(thr))


def _select(key, idx, thr, cut):
    chosen = jnp.where(key > thr, 1, jnp.where(key == thr, jnp.where(idx <= cut, 1, 0), 0))
    return jnp.where(key > KEY_NEG_INF, chosen, 0) > 0


def _dsa_select_kernel(qit_ref, wit_ref, ki_ref, tri_ref, bias_ref, key_scr, *, tq, tk, topk):
    qb = pl.program_id(1)
    seq = ki_ref.shape[0]
    nk = seq // tk
    nvis = _div((qb + 1) * tq + tk - 1, tk)
    q_chunk = _div(qb * tq + lax.broadcasted_iota(I32, (1, tq), 1), CHUNK)
    w = wit_ref[...]
    row_idx = lax.broadcasted_iota(I32, (tk, 1), 0)

    def score_body(j, carry):
        off = pl.multiple_of(j * tk, tk)
        kb = ki_ref[pl.ds(off, tk), :]
        acc = jnp.zeros((tk, tq), F32)
        for hh in range(IDX_HEADS):
            acc = acc + w[hh:hh + 1, :] * jnp.maximum(_dot(kb, qit_ref[hh]), 0.0)
        vis = _div(off + row_idx, CHUNK) <= q_chunk
        key_scr[pl.ds(off, tk), :] = _sort_key(jnp.where(vis, acc, NEG_INF))
        return carry

    lax.fori_loop(0, nvis, score_body, 0)

    pair = 2 * tk if nk % 2 == 0 else tk
    npair = _div(nvis * tk + pair - 1, pair)

    @pl.when(npair * pair > nvis * tk)
    def _():
        key_scr[pl.ds(pl.multiple_of(nvis * tk, tk), tk), :] = jnp.full((tk, tq), KEY_NEG_INF, I32)

    def fold_rows(v):
        parts = [jnp.sum(v[g * (pair // 8):(g + 1) * (pair // 8)].reshape(pair // 64, 8, tq), axis=0) for g in range(8)]
        while len(parts) > 1:
            parts = [parts[i] + parts[i + 1] for i in range(0, len(parts), 2)]
        return parts[0]

    def count(hit):
        def body(j, c):
            off = pl.multiple_of(j * pair, pair)
            hits = hit(key_scr[pl.ds(off, pair), :], None)
            return c + fold_rows(hits)
        c = lax.fori_loop(0, npair, body, jnp.zeros((8, tq), I32))
        return jnp.sum(c, axis=0, keepdims=True)

    thr = _kth_largest_key(count, topk)
    rem = (topk - count(lambda key, idx: jnp.where(key > thr, 1, 0))).astype(F32)
    tri = tri_ref[...]

    def out_body(j, seen):
        off = pl.multiple_of(j * tk, tk)
        key = key_scr[pl.ds(off, tk), :]
        tie = jnp.where(key == thr, 1.0, 0.0)
        rank = seen + _dot(tri, tie.astype(BF16))
        keep = jnp.where(key > thr, 1.0, jnp.where(rank <= rem, tie, 0.0))
        keep = jnp.where(key > KEY_NEG_INF, keep, 0.0)
        bias_ref[pl.ds(off, tk), :] = jnp.where(keep > 0.0, 0.0, NEG_INF).astype(BF16)
        return rank[tk - 1:tk, :]

    lax.fori_loop(0, nvis, out_body, jnp.zeros((1, tq), F32))

    def fill_body(j, carry):
        off = pl.multiple_of(j * tk, tk)
        bias_ref[pl.ds(off, tk), :] = jnp.full((tk, tq), NEG_INF, BF16)
        return carry

    lax.fori_loop(nvis, nk, fill_body, 0)


def _dsa_select(qit, wit, kib, batch, seq, topk):
    tq, tk = min(256, seq), min(512, seq)
    nq = seq // tq
    t = batch * seq
    return pl.pallas_call(
        functools.partial(_dsa_select_kernel, tq=tq, tk=tk, topk=topk),
        grid=(batch, nq),
        in_specs=[
            pl.BlockSpec((IDX_HEADS, IDX_DIM, tq), lambda b, q: (0, 0, b * nq + q)),
            pl.BlockSpec((IDX_HEADS, tq), lambda b, q: (0, b * nq + q)),
            pl.BlockSpec((seq, IDX_DIM), lambda b, q: (b, 0)),
            _whole((tk, tk)),
        ],
        out_specs=pl.BlockSpec((seq, tq), lambda b, q: (0, b * nq + q)),
        out_shape=jax.ShapeDtypeStruct((seq, t), BF16),
        scratch_shapes=[pltpu.VMEM((seq, tq), I32)],
        compiler_params=_params("parallel", "arbitrary"),
        name="dsa_select",
    )(qit, wit, kib, jnp.tril(jnp.ones((tk, tk), BF16)))


def _dsa_samp_kernel(q_ref, qi_ref, wi_ref, kc_ref, vc_ref, kic_ref, kn_ref, vn_ref, kin_ref, x_ref, wo_ref, o_ref,
                     *, n_q, topk):
    b = pl.program_id(0)
    hd, dh = DSA_HEADS, DSA_HEAD_DIM
    past = kc_ref.shape[1]
    n_keys = past + LANES
    own = _div(lax.broadcasted_iota(I32, (1, LANES), 1), n_q) == _mod(b, LANES // n_q)

    qi = qi_ref[...].reshape(IDX_HEADS * n_q, IDX_DIM)
    lg1 = jnp.maximum(_dot_t(qi, kic_ref[0].astype(BF16)), 0.0)
    lg2 = jnp.maximum(_dot_t(qi, kin_ref[...]), 0.0)
    w = wi_ref[...]
    sc1 = jnp.zeros((n_q, past), F32)
    sc2 = jnp.zeros((n_q, LANES), F32)
    for hh in range(IDX_HEADS):
        sc1 = sc1 + w[:, hh:hh + 1] * lg1[hh * n_q:(hh + 1) * n_q]
        sc2 = sc2 + w[:, hh:hh + 1] * lg2[hh * n_q:(hh + 1) * n_q]
    key = _sort_key(jnp.concatenate([sc1, jnp.where(own, sc2, NEG_INF)], axis=1))
    idx = lax.broadcasted_iota(I32, (1, n_keys), 1)

    count = lambda hit: jnp.sum(hit(key, idx), axis=1, keepdims=True)
    thr = _kth_largest_key(count, topk)
    rem = topk - count(lambda k_, i_: jnp.where(k_ > thr, 1, 0))
    cut = _tie_cutoff(count, thr, rem, int(n_keys - 1).bit_length())
    bias = jnp.where(_select(key, idx, thr, cut), 0.0, NEG_INF)
    bias = jnp.concatenate([bias] * hd, axis=0)

    lane_head = _div(lax.broadcasted_iota(I32, (1, hd * dh), 1), dh)
    qf = q_ref[...].astype(F32)
    qbd = jnp.concatenate([jnp.where(lane_head == hh, qf, 0.0) for hh in range(hd)], axis=0).astype(BF16)
    kc = kc_ref[0].astype(BF16)
    vc = vc_ref[0].astype(BF16)
    s1 = _dot_t(qbd, kc) + bias[:, :past]
    s2 = _dot_t(qbd, kn_ref[...]) + bias[:, past:]
    m = jnp.maximum(jnp.max(s1, axis=1, keepdims=True), jnp.max(s2, axis=1, keepdims=True))
    p1 = jnp.exp(s1 - m)
    p2 = jnp.exp(s2 - m)
    l = jnp.sum(p1, axis=1, keepdims=True) + jnp.sum(p2, axis=1, keepdims=True)
    o_all = (_dot(p1.astype(BF16), vc) + _dot(p2.astype(BF16), vn_ref[...])) / l
    out = jnp.zeros((n_q, hd * dh), F32)
    for hh in range(hd):
        out = out + jnp.where(lane_head == hh, o_all[hh * n_q:(hh + 1) * n_q], 0.0)
    o_ref[...] = x_ref[...] + _dot(out.astype(BF16), wo_ref[...])


def _dsa_attn_sample(x, qb, qib, wi, kb, vb, kib, cache_k, cache_v, cache_ki, w_o, n_b, n_q, topk):
    t, d = x.shape
    hdm = DSA_HEADS * DSA_HEAD_DIM
    past = cache_k.shape[1]
    per = LANES // n_q
    return pl.pallas_call(
        functools.partial(_dsa_samp_kernel, n_q=n_q, topk=topk),
        grid=(n_b,),
        in_specs=[
            pl.BlockSpec((n_q, hdm), lambda b: (b, 0)),
            pl.BlockSpec((IDX_HEADS, n_q, IDX_DIM), lambda b: (0, b, 0)),
            pl.BlockSpec((n_q, IDX_HEADS), lambda b: (b, 0)),
            pl.BlockSpec((1, past, hdm), lambda b: (b, 0, 0)),
            pl.BlockSpec((1, past, hdm), lambda b: (b, 0, 0)),
            pl.BlockSpec((1, past, IDX_DIM), lambda b: (b, 0, 0)),
            pl.BlockSpec((LANES, hdm), lambda b: (b // per, 0)),
            pl.BlockSpec((LANES, hdm), lambda b: (b // per, 0)),
            pl.BlockSpec((LANES, IDX_DIM), lambda b: (b // per, 0)),
            pl.BlockSpec((n_q, d), lambda b: (b, 0)),
            _whole(w_o.shape),
        ],
        out_specs=pl.BlockSpec((n_q, d), lambda b: (b, 0)),
        out_shape=jax.ShapeDtypeStruct((t, d), F32),
        compiler_params=_params("parallel"),
        name="dsa_attn_sample",
    )(qb, qib, wi, cache_k.reshape(n_b, past, hdm), cache_v.reshape(n_b, past, hdm), cache_ki, kb, vb, kib, x, w_o)


def kernel(x_prompt, x_sample, cache_mla_ckv, cache_mla_krope, cache_dsa_k, cache_dsa_v, cache_dsa_kidx, norm_mix, norm_ffn, norm_final, mla_w_dq, mla_g_q, mla_w_uq, mla_w_dkv, mla_g_kv, mla_w_ukv, mla_w_o, cmlp_w_in, cmlp_ln_g, cmlp_ln_b, cmlp_w_s, cmlp_b_s, cmlp_w_out, dsa_w_qkv, dsa_w_o, dsa_w_qidx, dsa_w_kidx, dsa_g_kidx, dsa_w_widx, ffn_w_in, ffn_w_out):
    batch, seq, d = x_prompt.shape
    n_b, n_q, _ = x_sample.shape
    past = cache_mla_ckv.shape[2]
    depth = norm_mix.shape[0]
    xp = x_prompt.reshape(batch * seq, d)
    xs = x_sample.reshape(n_b * n_q, d)
    tab_p = _rope_tables(jnp.arange(seq))
    tab_s = tuple(jnp.tile(a, (n_b, 1)) for a in _rope_tables(past + jnp.arange(n_q)))
    cast = lambda a: a.astype(BF16)
    outs = {k: [] for k in ("ckv_p", "kr_p", "ckv_s", "kr_s", "cv_s", "dk_p", "dv_p", "di_p", "dk_s", "dv_s", "di_s")}
    for i in range(depth):
        kind, j = i % 3, i // 3
        if kind == 0:
            wts = _mla_weights(mla_w_dq[j], mla_w_uq[j], mla_w_dkv[j], mla_w_ukv[j], mla_w_o[j])
            qt, kh, vt, ckv, kr, kn2 = _mla_proj_t(xp, norm_mix[i], wts, mla_g_q[j], mla_g_kv[j], tab_p, seq)
            xp = _attn_t(xp, qt, kh, vt, kn2, None, wts["wot"], batch, seq, 256, "mla_attn_prompt")
            outs["ckv_p"].append(ckv.reshape(batch, seq, -1)); outs["kr_p"].append(kr.reshape(batch, seq, -1))
            ql, qr, ckv, kr, ckvb, krb = _mla_proj(xs, norm_mix[i], wts, mla_g_q[j], mla_g_kv[j], tab_s)
            xs = _mla_attn_sample(xs, ql, qr, ckvb, krb, cache_mla_ckv[j], cache_mla_krope[j], wts, n_b, n_q)
            outs["ckv_s"].append(ckv.reshape(n_b, n_q, -1)); outs["kr_s"].append(kr.reshape(n_b, n_q, -1))
        elif kind == 1:
            w_in, w_out = cast(cmlp_w_in[j]), cast(cmlp_w_out[j])
            xp, _ = _cmlp(xp, norm_mix[i], w_in, cmlp_ln_g[j], cmlp_ln_b[j], cmlp_w_s[j], cmlp_b_s[j], w_out,
                          min(seq, CMLP_CHUNK), False)
            xs, v_s = _cmlp(xs, norm_mix[i], w_in, cmlp_ln_g[j], cmlp_ln_b[j], cmlp_w_s[j], cmlp_b_s[j], w_out,
                            min(n_q, CMLP_CHUNK), True)
            outs["cv_s"].append(v_s.reshape(n_b, n_q, -1))
        else:
            hdm = DSA_HEADS * DSA_HEAD_DIM
            w_qkv, w_qidx, w_o = cast(dsa_w_qkv[j]), cast(dsa_w_qidx[j]), cast(dsa_w_o[j])
            w_kidx, w_widx = cast(dsa_w_kidx[j]), cast(dsa_w_widx[j])
            zpad = lambda n: jnp.zeros((d, n), BF16)
            wt = dict(wkv=w_qkv[:, hdm:], wqt=w_qkv[:, :hdm].T, wvt=w_qkv[:, 2 * hdm:].T, wqit=w_qidx.T,
                      wkw=jnp.concatenate([w_kidx, zpad(LANES - IDX_DIM)], axis=1),
                      wwt=jnp.concatenate([w_widx, zpad(BF16_ROWS - IDX_HEADS)], axis=1).T)
            hshape = (DSA_HEADS, DSA_HEAD_DIM)
            k, v, ki, qt, kh, vt, qit, kib, wit, kn2 = _dsa_proj_t(xp, norm_mix[i], wt, dsa_g_kidx[j])
            bias = _dsa_select(qit, wit, kib, batch, seq, min(TOPK_MAX, seq // 4))
            xp = _attn_t(xp, qt, kh, vt, kn2, bias, w_o.T, batch, seq, 512, "dsa_attn_prompt")
            outs["dk_p"].append(k.reshape((batch, seq) + hshape)); outs["dv_p"].append(v.reshape((batch, seq) + hshape))
            outs["di_p"].append(ki.reshape(batch, seq, -1))
            w_kw = jnp.concatenate([w_kidx, w_widx, zpad(LANES - IDX_DIM - IDX_HEADS)], axis=1)
            k, v, ki, qb, kb, vb, qib, kib, wi = _dsa_proj(xs, norm_mix[i], w_qkv, w_qidx, w_kw, dsa_g_kidx[j])
            xs = _dsa_attn_sample(xs, qb, qib, wi, kb, vb, kib, cache_dsa_k[j], cache_dsa_v[j], cache_dsa_kidx[j], w_o,
                                  n_b, n_q, min(TOPK_MAX, (past + n_q) // 4))
            outs["dk_s"].append(k.reshape((n_b, n_q) + hshape)); outs["dv_s"].append(v.reshape((n_b, n_q) + hshape))
            outs["di_s"].append(ki.reshape(n_b, n_q, -1))
        w_in, w_out = cast(ffn_w_in[i]), cast(ffn_w_out[i])
        final = i == depth - 1
        xp = _ffn(xp, norm_ffn[i], w_in, w_out, norm_final, final)
        xs = _ffn(xs, norm_ffn[i], w_in, w_out, norm_final, final)
    st = lambda name: jnp.stack(outs[name])
    return (xp.reshape(batch, seq, d), xs.reshape(n_b, n_q, d),
            st("ckv_p"), st("kr_p"), st("ckv_s"), st("kr_s"), st("cv_s"),
            st("dk_p"), st("dv_p"), st("di_p"), st("dk_s"), st("dv_s"), st("di_s"))
```

```python
import functools

import numpy as np
import jax
import jax.numpy as jnp
from jax import lax
from jax.experimental import pallas as pl
from jax.experimental.pallas import tpu as pltpu

F32, BF16, I32 = jnp.float32, jnp.bfloat16, jnp.int32

CHUNK = 64
EPS = 1e-6
MLA_HEADS, MLA_Q_LORA, MLA_KV_LORA, MLA_NOPE, MLA_ROPE, MLA_V = 16, 512, 256, 64, 32, 64
ROPE_BASE = 10000.0
MLA_SCALE = (MLA_NOPE + MLA_ROPE) ** -0.5
CMLP_CHUNK, CMLP_WIDTH, CMLP_GROUPS = 128, 2048, 8
DSA_HEADS, DSA_HEAD_DIM = 16, 64
DSA_SCALE = DSA_HEAD_DIM ** -0.5
IDX_HEADS, IDX_DIM = 8, 64
IDX_SCALE = IDX_DIM ** -0.5
TOPK_MAX = 256

LANES = 128
MXU_TILE = 256
BF16_ROWS = 16
VMEM_LIMIT = 52 * 1024 * 1024
NEG_INF = float("-inf")
INT_MIN = -2 ** 31
KEY_NEG_INF = -2139095041
LOG2E = float(np.log2(np.e))
KEY_NORM_SLACK = 1.01
LAG_MARGIN = 100.0
HEAD_V = 64
V_ROWS = HEAD_V + BF16_ROWS


def _dot(a, b):
    return jnp.dot(a, b, preferred_element_type=F32)


def _dot_t(a, b):
    return lax.dot_general(a, b, (((1,), (1,)), ((), ())), preferred_element_type=F32)


def _rms(x, g):
    return x * lax.rsqrt(jnp.mean(x * x, axis=-1, keepdims=True) + EPS) * g


def _log2(n):
    assert n > 0 and n & (n - 1) == 0, n
    return n.bit_length() - 1


def _div(x, n):
    return lax.shift_right_logical(x, jnp.int32(_log2(n)))


def _mod(x, n):
    assert n & (n - 1) == 0, n
    return x & (n - 1)


def _params(*sem):
    return pltpu.CompilerParams(dimension_semantics=sem, vmem_limit_bytes=VMEM_LIMIT)


def _whole(shape):
    nd = len(shape)
    return pl.BlockSpec(shape, lambda *_: (0,) * nd)


def _ffn_kernel(x_ref, g_ref, wg_ref, wu_ref, wo_ref, gf_ref, o_ref, act_scr, *, final):
    x = x_ref[...]
    h = _rms(x, g_ref[...]).astype(BF16)
    f = wo_ref.shape[0]
    chunk = MXU_TILE if f % MXU_TILE == 0 else f
    nc = f // chunk
    nxt = (_dot(h, wg_ref[:, 0:chunk]), _dot(h, wu_ref[:, 0:chunk]))
    for c in range(nc):
        gate, up = nxt
        if c + 1 < nc:
            lo = (c + 1) * chunk
            nxt = (_dot(h, wg_ref[:, lo:lo + chunk]), _dot(h, wu_ref[:, lo:lo + chunk]))
        act_scr[:, c * chunk:(c + 1) * chunk] = (jax.nn.silu(gate) * up).astype(BF16)
    y = x + _dot(act_scr[...], wo_ref[...])
    if final:
        y = _rms(y, gf_ref[...])
    o_ref[...] = y


def _ffn(x, g, w_in, w_out, g_final, final):
    t, d = x.shape
    f = w_out.shape[0]
    tm = min(512, t)
    once = pl.Buffered(1)
    return pl.pallas_call(
        functools.partial(_ffn_kernel, final=final),
        grid=(t // tm,),
        in_specs=[
            pl.BlockSpec((tm, d), lambda i: (i, 0)),
            _whole((1, d)),
            pl.BlockSpec((d, f), lambda i: (0, 0), pipeline_mode=once),
            pl.BlockSpec((d, f), lambda i: (0, 1), pipeline_mode=once),
            pl.BlockSpec((f, d), lambda i: (0, 0), pipeline_mode=once),
            _whole((1, d)),
        ],
        out_specs=pl.BlockSpec((tm, d), lambda i: (i, 0)),
        out_shape=jax.ShapeDtypeStruct((t, d), F32),
        scratch_shapes=[pltpu.VMEM((tm, f), BF16)],
        compiler_params=_params("parallel"),
        name="ffn",
    )(x, g.reshape(1, d), w_in, w_in, w_out, g_final.reshape(1, d))


def _gelu(x):
    return 0.5 * x * (1.0 + lax.erf(x * np.float32(np.sqrt(0.5))))


def _layernorm(x, g, b):
    mu = jnp.mean(x, axis=-1, keepdims=True)
    xc = x - mu
    return xc * lax.rsqrt(jnp.mean(xc * xc, axis=-1, keepdims=True) + EPS) * g + b


def _cmlp_kernel(x_ref, g_ref, win_ref, lng_ref, lnb_ref, ws_ref, bs_ref, wout_ref, *rest, n_rows, write_v):
    if write_v:
        o_ref, v_ref, vb_scr, gated_scr = rest
    else:
        o_ref, vb_scr, gated_scr = rest
    tm = x_ref.shape[0]
    w = CMLP_WIDTH
    gw = w // CMLP_GROUPS
    c = CMLP_CHUNK
    x = x_ref[...]
    h = _rms(x, g_ref[...]).astype(BF16)
    v = _layernorm(_gelu(_dot(h, win_ref[:, w:])), lng_ref[...], lnb_ref[...])
    if write_v:
        v_ref[...] = v
    vb_scr[...] = v.astype(BF16)
    r_i = lax.broadcasted_iota(I32, (c, c), 0)
    c_i = lax.broadcasted_iota(I32, (c, c), 1)
    keep = jnp.where(c_i >= r_i - _mod(r_i, n_rows), jnp.where(c_i <= r_i, 1, 0), 0) > 0
    u_next = _dot(h, win_ref[:, 0:gw])
    for g in range(CMLP_GROUPS):
        lo, hi = g * gw, (g + 1) * gw
        u_raw = u_next
        if g + 1 < CMLP_GROUPS:
            u_next = _dot(h, win_ref[:, hi:hi + gw])
        wg = jnp.where(keep, ws_ref[g], 0.0).astype(BF16)
        bias = bs_ref[:, g:g + 1]
        mixed = jnp.concatenate(
            [_dot(wg, vb_scr[k * c:(k + 1) * c, lo:hi]) + bias for k in range(tm // c)], axis=0)
        gated_scr[:, lo:hi] = (_gelu(u_raw) * mixed).astype(BF16)
    o_ref[...] = x + _dot(gated_scr[...], wout_ref[...])


def _cmlp(x, g, w_in, ln_g, ln_b, w_s, b_s, w_out, n_rows, write_v):
    t, d = x.shape
    w = CMLP_WIDTH
    c = CMLP_CHUNK
    tm = min(512, t)
    rep = c // n_rows
    ws_t = jnp.tile(w_s[:, :n_rows, :n_rows], (1, rep, rep))
    bs_t = jnp.tile(b_s[:, :n_rows].T, (rep, 1))
    out_shape = [jax.ShapeDtypeStruct((t, d), F32)]
    out_specs = [pl.BlockSpec((tm, d), lambda i: (i, 0))]
    if write_v:
        out_shape.append(jax.ShapeDtypeStruct((t, w), F32))
        out_specs.append(pl.BlockSpec((tm, w), lambda i: (i, 0)))
    res = pl.pallas_call(
        functools.partial(_cmlp_kernel, n_rows=n_rows, write_v=write_v),
        grid=(t // tm,),
        in_specs=[
            pl.BlockSpec((tm, d), lambda i: (i, 0)),
            _whole((1, d)), _whole((d, 2 * w)), _whole((1, w)), _whole((1, w)),
            _whole((CMLP_GROUPS, c, c)), _whole((c, CMLP_GROUPS)), _whole((w, d)),
        ],
        out_specs=out_specs,
        out_shape=out_shape,
        scratch_shapes=[pltpu.VMEM((tm, w), BF16), pltpu.VMEM((tm, w), BF16)],
        compiler_params=_params("parallel"),
        name="cmlp",
    )(x, g.reshape(1, d), w_in, ln_g.reshape(1, w), ln_b.reshape(1, w), ws_t, bs_t, w_out)
    return res if write_v else (res[0], None)


def _attn_t_kernel(qi_ref, kj_ref, last_ref, nsub_ref, qt_ref, k_ref, vt_ref, *rest, tq, tk, sub, use_bias):
    if use_bias:
        bias_ref, kn_ref, x_ref, wot_ref, o_ref, m_scr, pend_scr, qn_scr, acc_scr, cat_scr = rest
    else:
        kn_ref, x_ref, wot_ref, o_ref, m_scr, pend_scr, qn_scr, acc_scr, cat_scr = rest
    p = pl.program_id(1)
    qi, kj = qi_ref[p], kj_ref[p]
    hd = qt_ref.shape[0]

    @pl.when(kj == 0)
    def _():
        m_scr[...] = jnp.full_like(m_scr, NEG_INF)
        pend_scr[...] = jnp.ones_like(pend_scr)
        acc_scr[...] = jnp.zeros_like(acc_scr)
        for hh in range(hd):
            q = qt_ref[hh].astype(F32)
            qn_scr[hh] = jnp.sqrt(jnp.sum(q * q, axis=0, keepdims=True)) * kn_ref[0, hh:hh + 1, 0:1]

    def step(off, mask, lagged):
        keys = lambda hh: k_ref[hh, pl.ds(off, sub), :]
        vals = lambda hh: vt_ref[hh, :, pl.ds(off, sub)]
        s_next = _dot(keys(0), qt_ref[0])
        pend = None
        for hh in range(hd):
            s = s_next
            if hh + 1 < hd:
                s_next = _dot(keys(hh + 1), qt_ref[hh + 1])
            if mask is not None:
                s = s + mask
            m_prev = m_scr[hh]
            m_new = jnp.maximum(m_prev, jnp.max(s, axis=0, keepdims=True))
            if lagged:
                scale = pend_scr[hh]
                pe = jnp.exp2((s - m_prev).astype(BF16))
                pend_scr[hh] = jnp.exp2(m_prev - m_new)
            else:
                m_safe = jnp.where(m_new == NEG_INF, 0.0, m_new)
                scale = pend_scr[hh] * jnp.exp2(m_prev - m_safe)
                pe = jnp.exp2(s - m_safe).astype(BF16)
                pend_scr[hh] = jnp.ones_like(m_prev)
            m_scr[hh] = m_new
            if pend is not None:
                ph, pa, pp = pend
                acc_scr[ph] = pa * acc_scr[ph] + _dot(vals(ph), pp)
            pend = (hh, scale, pe)
        ph, pa, pp = pend
        acc_scr[ph] = pa * acc_scr[ph] + _dot(vals(ph), pp)

    def sub_body(j, carry):
        off = pl.multiple_of(j * sub, sub)
        excess = qn_scr[0] - m_scr[0]
        for hh in range(1, hd):
            excess = jnp.maximum(excess, qn_scr[hh] - m_scr[hh])
        lag_ok = jnp.max(excess) <= LAG_MARGIN
        if use_bias:
            bias = bias_ref[pl.ds(off, sub), :].astype(F32)

            @pl.when(lag_ok)
            def _():
                step(off, bias, True)

            @pl.when(jnp.logical_not(lag_ok))
            def _():
                step(off, bias, False)
        else:
            base = kj * tk + off
            needs_mask = _div(base + sub - 1, CHUNK) > _div(qi * tq, CHUNK)

            def chunk_mask():
                k_chunk = _div(base + lax.broadcasted_iota(I32, (sub, 1), 0), CHUNK)
                q_chunk = _div(qi * tq + lax.broadcasted_iota(I32, (1, tq), 1), CHUNK)
                return jnp.where(k_chunk <= q_chunk, 0.0, NEG_INF)

            @pl.when(jnp.logical_not(lag_ok))
            def _():
                step(off, chunk_mask(), False)

            @pl.when(jnp.logical_and(lag_ok, needs_mask))
            def _():
                step(off, chunk_mask(), True)

            @pl.when(jnp.logical_and(lag_ok, jnp.logical_not(needs_mask)))
            def _():
                step(off, None, True)
        return carry

    lax.fori_loop(0, nsub_ref[p], sub_body, 0)

    @pl.when(last_ref[p] == 1)
    def _():
        for hh in range(hd):
            a = acc_scr[hh]
            cat_scr[hh * HEAD_V:(hh + 1) * HEAD_V, :] = (a[:HEAD_V] / a[HEAD_V:HEAD_V + 1]).astype(BF16)
        o_ref[...] = x_ref[...] + _dot(wot_ref[...], cat_scr[...]).T


def _causal_pairs(nq, tq, tk, sub):
    qi, kj, last, nsub = [], [], [], []
    per = tk // sub
    for q in range(nq):
        vis_sub = -(-((q + 1) * tq) // sub)
        nvis = -(-vis_sub // per)
        for k in range(nvis):
            qi.append(q); kj.append(k); last.append(int(k == nvis - 1)); nsub.append(min(per, vis_sub - k * per))
    as_arr = lambda v: jnp.asarray(np.array(v, np.int32))
    return as_arr(qi), as_arr(kj), as_arr(last), as_arr(nsub)


def _attn_t(x, qt, kh, vt, kn2, bias, wot, batch, seq, sub, name):
    t, d = x.shape
    hd, dk, _ = qt.shape
    tq, tk, sub = min(512, seq), min(1024, seq), min(sub, seq)
    nq, nk = seq // tq, seq // tk
    qi, kj, last, nsub = _causal_pairs(nq, tq, tk, sub)
    use_bias = bias is not None
    in_specs = [
        pl.BlockSpec((hd, dk, tq), lambda b, p, qi, kj, la, ns: (0, 0, b * nq + qi[p])),
        pl.BlockSpec((hd, tk, dk), lambda b, p, qi, kj, la, ns: (0, b * nk + kj[p], 0)),
        pl.BlockSpec((hd, V_ROWS, tk), lambda b, p, qi, kj, la, ns: (0, 0, b * nk + kj[p])),
    ]
    args = [qt, kh, vt]
    if use_bias:
        in_specs.append(pl.BlockSpec((tk, tq), lambda b, p, qi, kj, la, ns: (kj[p], b * nq + qi[p])))
        args.append(bias)
    kn = jnp.sqrt(jnp.max(kn2[:, :, 0].reshape(batch, -1, hd), axis=1)) * KEY_NORM_SLACK
    in_specs.append(pl.BlockSpec((1, hd, LANES), lambda b, p, qi, kj, la, ns: (b, 0, 0)))
    args.append(jnp.broadcast_to(kn[:, :, None], (batch, hd, LANES)))
    in_specs += [
        pl.BlockSpec((tq, d), lambda b, p, qi, kj, la, ns: (b * nq + qi[p], 0)),
        pl.BlockSpec(wot.shape, lambda b, p, qi, kj, la, ns: (0, 0)),
    ]
    args += [x, wot]
    grid_spec = pltpu.PrefetchScalarGridSpec(
        num_scalar_prefetch=4,
        grid=(batch, int(qi.shape[0])),
        in_specs=in_specs,
        out_specs=pl.BlockSpec((tq, d), lambda b, p, qi, kj, la, ns: (b * nq + qi[p], 0)),
        scratch_shapes=[
            pltpu.VMEM((hd, 1, tq), F32), pltpu.VMEM((hd, 1, tq), F32), pltpu.VMEM((hd, 1, tq), F32),
            pltpu.VMEM((hd, V_ROWS, tq), F32), pltpu.VMEM((hd * HEAD_V, tq), BF16),
        ],
    )
    return pl.pallas_call(
        functools.partial(_attn_t_kernel, tq=tq, tk=tk, sub=sub, use_bias=use_bias),
        grid_spec=grid_spec,
        out_shape=jax.ShapeDtypeStruct((t, d), F32),
        compiler_params=_params("parallel", "arbitrary"),
        name=name,
    )(qi, kj, last, nsub, *args)


def _rope_tables(pos):
    half = MLA_ROPE // 2
    inv = ROPE_BASE ** (-jnp.arange(half, dtype=F32) / half)
    ang = pos.astype(F32)[:, None] * inv[None, :]
    cos, sin = jnp.cos(ang), jnp.sin(ang)
    cos_k = jnp.concatenate([cos, cos], axis=1)
    sin_k = jnp.concatenate([-sin, sin], axis=1)
    return jnp.tile(cos_k, (1, MLA_HEADS)), jnp.tile(sin_k, (1, MLA_HEADS)), cos_k, sin_k


def _swap_halves(w, group):
    shp = w.shape
    wr = w.reshape(shp[:-1] + (shp[-1] // group, 2, group // 2))
    return wr[..., ::-1, :].reshape(shp)


def _mla_weights(w_dq, w_uq, w_dkv, w_ukv, w_o):
    hd = MLA_HEADS
    cast = lambda a: a.astype(BF16)
    w_uq, w_dkv, w_ukv = cast(w_uq), cast(w_dkv), cast(w_ukv)
    wqn = w_uq[:, :, :MLA_NOPE].reshape(MLA_Q_LORA, hd * MLA_NOPE)
    wqr = w_uq[:, :, MLA_NOPE:].reshape(MLA_Q_LORA, hd * MLA_ROPE)
    wqs = _swap_halves(wqr, MLA_ROPE)
    wkc, wkr = w_dkv[:, :MLA_KV_LORA], w_dkv[:, MLA_KV_LORA:]
    wks = _swap_halves(wkr, MLA_ROPE)
    wuk = jnp.transpose(w_ukv[:, :, :MLA_NOPE], (1, 2, 0))
    zeros = jnp.zeros_like(wuk)
    even = jnp.concatenate([wuk, zeros], axis=1)
    odd = jnp.concatenate([zeros, wuk], axis=1)
    wuk2 = jnp.where((jnp.arange(hd) % 2 == 0)[:, None, None], even, odd)
    wuv = jnp.transpose(w_ukv[:, :, MLA_NOPE:], (1, 0, 2))
    d = w_dkv.shape[0]
    pad_r = lambda a: jnp.concatenate([a, jnp.zeros((d, LANES - MLA_ROPE), BF16)], axis=1)
    wukp = jnp.concatenate([jnp.zeros((MLA_KV_LORA, hd, MLA_ROPE), BF16), w_ukv[:, :, :MLA_NOPE],
                            jnp.zeros((MLA_KV_LORA, hd, LANES - MLA_ROPE - MLA_NOPE), BF16)], axis=2)
    return dict(wdq=cast(w_dq), wqn=wqn, wqr=wqr, wqs=wqs, wkc=wkc, wkr=wkr, wks=wks, wuk=wuk2, wuv=wuv, wo=cast(w_o),
                wqnt=wqn.T, wqrt=wqr.T, wqst=wqs.T, wkrp=pad_r(wkr), wksp=pad_r(wks),
                wukp=wukp.reshape(MLA_KV_LORA, hd * LANES), wuvt=wuv.transpose(0, 2, 1).reshape(hd * MLA_V, MLA_KV_LORA),
                wot=cast(w_o).T)


def _mla_proj_kernel(x_ref, gm_ref, wdq_ref, gq_ref, wqn_ref, wqr_ref, wqs_ref, wkc_ref, wkr_ref, wks_ref,
                     gkv_ref, wuk_ref, cq_ref, sq_ref, ck_ref, sk_ref,
                     ql_ref, qr_ref, ckv_ref, kr_ref, ckvb_ref, krb_ref):
    h = _rms(x_ref[...], gm_ref[...]).astype(BF16)
    cq = _rms(_dot(h, wdq_ref[...]), gq_ref[...]).astype(BF16)
    qn = _dot(cq, wqn_ref[...])
    qr = ((_dot(cq, wqr_ref[...]) * cq_ref[...] + _dot(cq, wqs_ref[...]) * sq_ref[...]) * MLA_SCALE).astype(BF16)
    for hh in range(MLA_HEADS):
        pair = qn[:, (hh // 2) * LANES:(hh // 2 + 1) * LANES].astype(BF16)
        ql_ref[hh] = (_dot(pair, wuk_ref[hh]) * MLA_SCALE).astype(BF16)
        qr_ref[hh] = qr[:, hh * MLA_ROPE:(hh + 1) * MLA_ROPE]
    ckv = _rms(_dot(h, wkc_ref[...]), gkv_ref[...])
    ckv_ref[...] = ckv
    ckvb_ref[...] = ckv.astype(BF16)
    kr = _dot(h, wkr_ref[...]) * ck_ref[...] + _dot(h, wks_ref[...]) * sk_ref[...]
    kr_ref[...] = kr
    krb_ref[...] = kr.astype(BF16)


def _mla_proj(x, g_mix, wts, g_q, g_kv, tables):
    t, d = x.shape
    tm = min(256, t)
    cos_q, sin_q, cos_k, sin_k = tables
    nrep = cos_q.shape[0] // tm
    tab = lambda wdt: pl.BlockSpec((tm, wdt), lambda i: (i % nrep, 0))
    hd = MLA_HEADS
    row = lambda wdt: pl.BlockSpec((tm, wdt), lambda i: (i, 0))
    hm = lambda wdt: pl.BlockSpec((hd, tm, wdt), lambda i: (0, i, 0))
    return pl.pallas_call(
        _mla_proj_kernel,
        grid=(t // tm,),
        in_specs=[
            row(d), _whole((1, d)), _whole(wts["wdq"].shape), _whole((1, MLA_Q_LORA)),
            _whole(wts["wqn"].shape), _whole(wts["wqr"].shape), _whole(wts["wqs"].shape),
            _whole(wts["wkc"].shape), _whole(wts["wkr"].shape), _whole(wts["wks"].shape),
            _whole((1, MLA_KV_LORA)), _whole(wts["wuk"].shape),
            tab(hd * MLA_ROPE), tab(hd * MLA_ROPE), tab(MLA_ROPE), tab(MLA_ROPE),
        ],
        out_specs=[hm(MLA_KV_LORA), hm(MLA_ROPE), row(MLA_KV_LORA), row(MLA_ROPE), row(MLA_KV_LORA), row(MLA_ROPE)],
        out_shape=[
            jax.ShapeDtypeStruct((hd, t, MLA_KV_LORA), BF16), jax.ShapeDtypeStruct((hd, t, MLA_ROPE), BF16),
            jax.ShapeDtypeStruct((t, MLA_KV_LORA), F32), jax.ShapeDtypeStruct((t, MLA_ROPE), F32),
            jax.ShapeDtypeStruct((t, MLA_KV_LORA), BF16), jax.ShapeDtypeStruct((t, MLA_ROPE), BF16),
        ],
        compiler_params=_params("parallel"),
        name="mla_proj",
    )(x, g_mix.reshape(1, d), wts["wdq"], g_q.reshape(1, -1), wts["wqn"], wts["wqr"], wts["wqs"],
      wts["wkc"], wts["wkr"], wts["wks"], g_kv.reshape(1, -1), wts["wuk"], cos_q, sin_q, cos_k, sin_k)


def _mla_proj_t_kernel(x_ref, gm_ref, wdq_ref, gq_ref, wqnt_ref, wqrt_ref, wqst_ref, wkc_ref, wkrp_ref, wksp_ref,
                       gkv_ref, wukp_ref, wuvt_ref, cqt_ref, sqt_ref, ckp_ref, skp_ref,
                       qt_ref, kh_ref, vt_ref, ckv_ref, kr_ref, kn2_ref):
    tm = x_ref.shape[0]
    hd = MLA_HEADS
    qscale = MLA_SCALE * LOG2E
    h = _rms(x_ref[...], gm_ref[...]).astype(BF16)
    cq_raw = _dot(h, wdq_ref[...])
    ckv_raw = _dot(h, wkc_ref[...])
    kr_a = _dot(h, wkrp_ref[...])
    kr_b = _dot(h, wksp_ref[...])
    cq = _rms(cq_raw, gq_ref[...]).astype(BF16)
    qnt_raw = _dot_t(wqnt_ref[...], cq)
    qrt_a = _dot_t(wqrt_ref[...], cq)
    qrt_b = _dot_t(wqst_ref[...], cq)
    ckv = _rms(ckv_raw, gkv_ref[...])
    ckv_ref[...] = ckv
    cb = ckv.astype(BF16)
    kn = _dot(cb, wukp_ref[...])
    vt = _dot_t(wuvt_ref[...], cb).astype(BF16)
    qnt = (qnt_raw * qscale).astype(BF16)
    qrt = ((qrt_a * cqt_ref[...] + qrt_b * sqt_ref[...]) * qscale).astype(BF16)
    zeros = jnp.zeros((LANES - MLA_ROPE - MLA_NOPE, tm), BF16)
    krp = kr_a * ckp_ref[...] + kr_b * skp_ref[...]
    kr_ref[...] = krp[:, :MLA_ROPE]
    ones = jnp.ones((BF16_ROWS, tm), BF16)
    for hh in range(hd):
        qt_ref[hh, 0:MLA_ROPE, :] = qrt[hh * MLA_ROPE:(hh + 1) * MLA_ROPE]
        qt_ref[hh, MLA_ROPE:MLA_ROPE + MLA_NOPE, :] = qnt[hh * MLA_NOPE:(hh + 1) * MLA_NOPE]
        qt_ref[hh, MLA_ROPE + MLA_NOPE:, :] = zeros
        key = kn[:, hh * LANES:(hh + 1) * LANES] + krp
        kh_ref[hh] = key.astype(BF16)
        kn2_ref[0, hh:hh + 1, :] = jnp.broadcast_to(
            jnp.max(jnp.sum(key * key, axis=1, keepdims=True), axis=0, keepdims=True), (1, LANES))
        vt_ref[hh, 0:MLA_V, :] = vt[hh * MLA_V:(hh + 1) * MLA_V]
        vt_ref[hh, MLA_V:, :] = ones


def _mla_proj_t(x, g_mix, wts, g_q, g_kv, tables, seq):
    t, d = x.shape
    tm = min(256, seq)
    hd = MLA_HEADS
    cos_q, sin_q, cos_k, sin_k = tables
    pad = lambda a: jnp.concatenate([a, jnp.zeros((seq, LANES - MLA_ROPE), F32)], axis=1)
    nrep = seq // tm
    row = lambda wdt: pl.BlockSpec((tm, wdt), lambda i: (i, 0))
    names = ("wdq", "wqnt", "wqrt", "wqst", "wkc", "wkrp", "wksp", "wukp", "wuvt")
    w = {n: wts[n] for n in names}
    return pl.pallas_call(
        _mla_proj_t_kernel,
        grid=(t // tm,),
        in_specs=[
            row(d), _whole((1, d)), _whole(w["wdq"].shape), _whole((1, MLA_Q_LORA)),
            _whole(w["wqnt"].shape), _whole(w["wqrt"].shape), _whole(w["wqst"].shape),
            _whole(w["wkc"].shape), _whole(w["wkrp"].shape), _whole(w["wksp"].shape),
            _whole((1, MLA_KV_LORA)), _whole(w["wukp"].shape), _whole(w["wuvt"].shape),
            pl.BlockSpec((hd * MLA_ROPE, tm), lambda i: (0, i % nrep)),
            pl.BlockSpec((hd * MLA_ROPE, tm), lambda i: (0, i % nrep)),
            pl.BlockSpec((tm, LANES), lambda i: (i % nrep, 0)),
            pl.BlockSpec((tm, LANES), lambda i: (i % nrep, 0)),
        ],
        out_specs=[
            pl.BlockSpec((hd, LANES, tm), lambda i: (0, 0, i)),
            pl.BlockSpec((hd, tm, LANES), lambda i: (0, i, 0)),
            pl.BlockSpec((hd, V_ROWS, tm), lambda i: (0, 0, i)),
            row(MLA_KV_LORA), row(MLA_ROPE),
            pl.BlockSpec((1, hd, LANES), lambda i: (i, 0, 0)),
        ],
        out_shape=[
            jax.ShapeDtypeStruct((hd, LANES, t), BF16), jax.ShapeDtypeStruct((hd, t, LANES), BF16),
            jax.ShapeDtypeStruct((hd, V_ROWS, t), BF16),
            jax.ShapeDtypeStruct((t, MLA_KV_LORA), F32), jax.ShapeDtypeStruct((t, MLA_ROPE), F32),
            jax.ShapeDtypeStruct((t // tm, hd, LANES), F32),
        ],
        compiler_params=_params("parallel"),
        name="mla_proj_t",
    )(x, g_mix.reshape(1, d), w["wdq"], g_q.reshape(1, -1), w["wqnt"], w["wqrt"], w["wqst"],
      w["wkc"], w["wkrp"], w["wksp"], g_kv.reshape(1, -1), w["wukp"], w["wuvt"],
      cos_q.T, sin_q.T, pad(cos_k), pad(sin_k))


def _mla_epilogue(o_lat, x_ref, wuv_ref, wo_ref, o_ref, cat_scr, tq):
    ob = o_lat.astype(BF16)
    for hh in range(MLA_HEADS):
        cat_scr[:, hh * MLA_V:(hh + 1) * MLA_V] = _dot(ob[hh * tq:(hh + 1) * tq], wuv_ref[hh]).astype(BF16)
    o_ref[...] = x_ref[...] + _dot(cat_scr[...], wo_ref[...])


def _mla_samp_kernel(ql_ref, qr_ref, cc_ref, cr_ref, cn_ref, rn_ref, x_ref, wuv_ref, wo_ref, o_ref, cat_scr, *, n_q):
    b = pl.program_id(0)
    hd = MLA_HEADS
    ql = ql_ref[...].reshape(hd * n_q, MLA_KV_LORA)
    qr = qr_ref[...].reshape(hd * n_q, MLA_ROPE)
    cc = cc_ref[0].astype(BF16)
    cr = cr_ref[0].astype(BF16)
    cn = cn_ref[...]
    s1 = _dot_t(ql, cc) + _dot_t(qr, cr)
    s2 = _dot_t(ql, cn) + _dot_t(qr, rn_ref[...])
    own = _div(lax.broadcasted_iota(I32, (1, LANES), 1), n_q) == _mod(b, LANES // n_q)
    s2 = jnp.where(own, s2, NEG_INF)
    m = jnp.maximum(jnp.max(s1, axis=1, keepdims=True), jnp.max(s2, axis=1, keepdims=True))
    p1 = jnp.exp(s1 - m)
    p2 = jnp.exp(s2 - m)
    l = jnp.sum(p1, axis=1, keepdims=True) + jnp.sum(p2, axis=1, keepdims=True)
    o_lat = (_dot(p1.astype(BF16), cc) + _dot(p2.astype(BF16), cn)) / l
    _mla_epilogue(o_lat, x_ref, wuv_ref, wo_ref, o_ref, cat_scr, n_q)


def _mla_attn_sample(x, ql, qr, ckvb, krb, cache_c, cache_r, wts, n_b, n_q):
    t, d = x.shape
    hd = MLA_HEADS
    past = cache_c.shape[1]
    per = LANES // n_q
    return pl.pallas_call(
        functools.partial(_mla_samp_kernel, n_q=n_q),
        grid=(n_b,),
        in_specs=[
            pl.BlockSpec((hd, n_q, MLA_KV_LORA), lambda b: (0, b, 0)),
            pl.BlockSpec((hd, n_q, MLA_ROPE), lambda b: (0, b, 0)),
            pl.BlockSpec((1, past, MLA_KV_LORA), lambda b: (b, 0, 0)),
            pl.BlockSpec((1, past, MLA_ROPE), lambda b: (b, 0, 0)),
            pl.BlockSpec((LANES, MLA_KV_LORA), lambda b: (b // per, 0)),
            pl.BlockSpec((LANES, MLA_ROPE), lambda b: (b // per, 0)),
            pl.BlockSpec((n_q, d), lambda b: (b, 0)),
            _whole(wts["wuv"].shape), _whole(wts["wo"].shape),
        ],
        out_specs=pl.BlockSpec((n_q, d), lambda b: (b, 0)),
        out_shape=jax.ShapeDtypeStruct((t, d), F32),
        scratch_shapes=[pltpu.VMEM((n_q, hd * MLA_V), BF16)],
        compiler_params=_params("parallel"),
        name="mla_attn_sample",
    )(ql, qr, cache_c, cache_r, ckvb, krb, x, wts["wuv"], wts["wo"])


def _dsa_proj_kernel(x_ref, gm_ref, wqkv_ref, wqi_ref, wkw_ref, gki_ref,
                     k_ref, v_ref, ki_ref, qb_ref, kb_ref, vb_ref, qib_ref, kib_ref, wi_ref):
    hdm = DSA_HEADS * DSA_HEAD_DIM
    h = _rms(x_ref[...], gm_ref[...]).astype(BF16)
    qkv = _dot(h, wqkv_ref[...])
    q = qkv[:, :hdm] * DSA_SCALE
    k = qkv[:, hdm:2 * hdm]
    v = qkv[:, 2 * hdm:]
    k_ref[...] = k
    v_ref[...] = v
    qi = _dot(h, wqi_ref[...]) * IDX_SCALE
    kw = _dot(h, wkw_ref[...])
    ki = _rms(kw[:, :IDX_DIM], gki_ref[...])
    ki_ref[...] = ki
    kib_ref[...] = ki.astype(BF16)
    wi_ref[...] = kw[:, IDX_DIM:IDX_DIM + IDX_HEADS] * (IDX_HEADS ** -0.5)
    qb_ref[...] = q.astype(BF16)
    kb_ref[...] = k.astype(BF16)
    vb_ref[...] = v.astype(BF16)
    for hh in range(IDX_HEADS):
        qib_ref[hh] = qi[:, hh * IDX_DIM:(hh + 1) * IDX_DIM].astype(BF16)


def _dsa_proj(x, g_mix, w_qkv, w_qidx, w_kw, g_kidx):
    t, d = x.shape
    tm = min(512, t)
    hdm = DSA_HEADS * DSA_HEAD_DIM
    row = lambda wdt: pl.BlockSpec((tm, wdt), lambda i: (i, 0))
    return pl.pallas_call(
        _dsa_proj_kernel,
        grid=(t // tm,),
        in_specs=[row(d), _whole((1, d)), _whole(w_qkv.shape), _whole(w_qidx.shape), _whole(w_kw.shape),
                  _whole((1, IDX_DIM))],
        out_specs=[row(hdm), row(hdm), row(IDX_DIM), row(hdm), row(hdm), row(hdm),
                   pl.BlockSpec((IDX_HEADS, tm, IDX_DIM), lambda i: (0, i, 0)), row(IDX_DIM), row(IDX_HEADS)],
        out_shape=[
            jax.ShapeDtypeStruct((t, hdm), F32), jax.ShapeDtypeStruct((t, hdm), F32),
            jax.ShapeDtypeStruct((t, IDX_DIM), F32), jax.ShapeDtypeStruct((t, hdm), BF16),
            jax.ShapeDtypeStruct((t, hdm), BF16), jax.ShapeDtypeStruct((t, hdm), BF16),
            jax.ShapeDtypeStruct((IDX_HEADS, t, IDX_DIM), BF16), jax.ShapeDtypeStruct((t, IDX_DIM), BF16),
            jax.ShapeDtypeStruct((t, IDX_HEADS), F32),
        ],
        compiler_params=_params("parallel"),
        name="dsa_proj",
    )(x, g_mix.reshape(1, d), w_qkv, w_qidx, w_kw, g_kidx.reshape(1, IDX_DIM))


def _dsa_proj_t_kernel(x_ref, gm_ref, wkv_ref, wqt_ref, wvt_ref, wqit_ref, wkw_ref, wwt_ref, gki_ref,
                       k_ref, v_ref, ki_ref, qt_ref, kh_ref, vt_ref, qit_ref, kib_ref, wit_ref, kn2_ref):
    tm = x_ref.shape[0]
    hdm = DSA_HEADS * DSA_HEAD_DIM
    dh = DSA_HEAD_DIM
    h = _rms(x_ref[...], gm_ref[...]).astype(BF16)
    kv = _dot(h, wkv_ref[...])
    k = kv[:, :hdm]
    k_ref[...] = k
    v_ref[...] = kv[:, hdm:]
    kw = _dot(h, wkw_ref[...])
    ki = _rms(kw[:, :IDX_DIM], gki_ref[...])
    ki_ref[...] = ki
    kib_ref[...] = ki.astype(BF16)
    wit_ref[...] = _dot_t(wwt_ref[...], h)[:IDX_HEADS] * (IDX_HEADS ** -0.5)
    qt = (_dot_t(wqt_ref[...], h) * (DSA_SCALE * LOG2E)).astype(BF16)
    vt = _dot_t(wvt_ref[...], h).astype(BF16)
    qit = (_dot_t(wqit_ref[...], h) * IDX_SCALE).astype(BF16)
    ones = jnp.ones((BF16_ROWS, tm), BF16)
    for hh in range(DSA_HEADS):
        qt_ref[hh] = qt[hh * dh:(hh + 1) * dh]
        key = k[:, hh * dh:(hh + 1) * dh]
        kh_ref[hh] = key.astype(BF16)
        kn2_ref[0, hh:hh + 1, :] = jnp.broadcast_to(
            jnp.max(jnp.sum(key * key, axis=1, keepdims=True), axis=0, keepdims=True), (1, LANES))
        vt_ref[hh, 0:dh, :] = vt[hh * dh:(hh + 1) * dh]
        vt_ref[hh, dh:, :] = ones
    for hh in range(IDX_HEADS):
        qit_ref[hh] = qit[hh * IDX_DIM:(hh + 1) * IDX_DIM]


def _dsa_proj_t(x, g_mix, w, g_kidx):
    t, d = x.shape
    tm = min(256, t)
    hd, dh = DSA_HEADS, DSA_HEAD_DIM
    hdm = hd * dh
    row = lambda wdt: pl.BlockSpec((tm, wdt), lambda i: (i, 0))
    names = ("wkv", "wqt", "wvt", "wqit", "wkw", "wwt")
    return pl.pallas_call(
        _dsa_proj_t_kernel,
        grid=(t // tm,),
        in_specs=[row(d), _whole((1, d))] + [_whole(w[n].shape) for n in names] + [_whole((1, IDX_DIM))],
        out_specs=[
            row(hdm), row(hdm), row(IDX_DIM),
            pl.BlockSpec((hd, dh, tm), lambda i: (0, 0, i)),
            pl.BlockSpec((hd, tm, dh), lambda i: (0, i, 0)),
            pl.BlockSpec((hd, V_ROWS, tm), lambda i: (0, 0, i)),
            pl.BlockSpec((IDX_HEADS, IDX_DIM, tm), lambda i: (0, 0, i)),
            row(IDX_DIM),
            pl.BlockSpec((IDX_HEADS, tm), lambda i: (0, i)),
            pl.BlockSpec((1, hd, LANES), lambda i: (i, 0, 0)),
        ],
        out_shape=[
            jax.ShapeDtypeStruct((t, hdm), F32), jax.ShapeDtypeStruct((t, hdm), F32),
            jax.ShapeDtypeStruct((t, IDX_DIM), F32),
            jax.ShapeDtypeStruct((hd, dh, t), BF16), jax.ShapeDtypeStruct((hd, t, dh), BF16),
            jax.ShapeDtypeStruct((hd, V_ROWS, t), BF16),
            jax.ShapeDtypeStruct((IDX_HEADS, IDX_DIM, t), BF16), jax.ShapeDtypeStruct((t, IDX_DIM), BF16),
            jax.ShapeDtypeStruct((IDX_HEADS, t), F32),
            jax.ShapeDtypeStruct((t // tm, hd, LANES), F32),
        ],
        compiler_params=_params("parallel"),
        name="dsa_proj_t",
    )(x, g_mix.reshape(1, d), *[w[n] for n in names], g_kidx.reshape(1, IDX_DIM))


def _sort_key(score):
    bits = lax.bitcast_convert_type(score, I32)
    return jnp.where(bits < 0, bits ^ jnp.int32(0x7FFFFFFF), bits)


def _kth_largest_key(count, topk):
    c0 = count(lambda key, idx: jnp.where(key >= 0, 1, 0))
    t0 = jnp.where(c0 >= topk, jnp.int32(0), jnp.int32(INT_MIN))

    def bit_body(i, t):
        cand = t + lax.shift_left(jnp.int32(1), jnp.int32(30) - i)
        c = count(lambda key, idx: jnp.where(key >= cand, 1, 0))
        return jnp.where(c >= topk, cand, t)

    return lax.fori_loop(0, 31, bit_body, t0)


def _tie_cutoff(count, thr, rem, nbits):
    def bit_body(i, j):
        cand = j + lax.shift_left(jnp.int32(1), jnp.int32(nbits - 1) - i)
        c = count(lambda key, idx: jnp.where(key == thr, jnp.where(idx < cand, 1, 0), 0))
        return jnp.where(c < rem, cand, j)

    return lax.fori_loop(0, nbits, bit_body, jnp.zeros_like(thr))


def _select(key, idx, thr, cut):
    chosen = jnp.where(key > thr, 1, jnp.where(key == thr, jnp.where(idx <= cut, 1, 0), 0))
    return jnp.where(key > KEY_NEG_INF, chosen, 0) > 0


def _dsa_select_kernel(qit_ref, wit_ref, ki_ref, tri_ref, bias_ref, key_scr, *, tq, tk, topk):
    qb = pl.program_id(1)
    seq = ki_ref.shape[0]
    nk = seq // tk
    nvis = _div((qb + 1) * tq + tk - 1, tk)
    q_chunk = _div(qb * tq + lax.broadcasted_iota(I32, (1, tq), 1), CHUNK)
    w = wit_ref[...]
    row_idx = lax.broadcasted_iota(I32, (tk, 1), 0)

    def score_body(j, carry):
        off = pl.multiple_of(j * tk, tk)
        kb = ki_ref[pl.ds(off, tk), :]
        acc = jnp.zeros((tk, tq), F32)
        for hh in range(IDX_HEADS):
            acc = acc + w[hh:hh + 1, :] * jnp.maximum(_dot(kb, qit_ref[hh]), 0.0)
        vis = _div(off + row_idx, CHUNK) <= q_chunk
        key_scr[pl.ds(off, tk), :] = _sort_key(jnp.where(vis, acc, NEG_INF))
        return carry

    lax.fori_loop(0, nvis, score_body, 0)

    pair = 2 * tk if nk % 2 == 0 else tk
    npair = _div(nvis * tk + pair - 1, pair)

    @pl.when(npair * pair > nvis * tk)
    def _():
        key_scr[pl.ds(pl.multiple_of(nvis * tk, tk), tk), :] = jnp.full((tk, tq), KEY_NEG_INF, I32)

    def fold_rows(v):
        parts = [jnp.sum(v[g * (pair // 8):(g + 1) * (pair // 8)].reshape(pair // 64, 8, tq), axis=0) for g in range(8)]
        while len(parts) > 1:
            parts = [parts[i] + parts[i + 1] for i in range(0, len(parts), 2)]
        return parts[0]

    def count(hit):
        def body(j, c):
            off = pl.multiple_of(j * pair, pair)
            hits = hit(key_scr[pl.ds(off, pair), :], None)
            return c + fold_rows(hits)
        c = lax.fori_loop(0, npair, body, jnp.zeros((8, tq), I32))
        return jnp.sum(c, axis=0, keepdims=True)

    thr = _kth_largest_key(count, topk)
    rem = (topk - count(lambda key, idx: jnp.where(key > thr, 1, 0))).astype(F32)
    tri = tri_ref[...]

    def out_body(j, seen):
        off = pl.multiple_of(j * tk, tk)
        key = key_scr[pl.ds(off, tk), :]
        tie = jnp.where(key == thr, 1.0, 0.0)
        rank = seen + _dot(tri, tie.astype(BF16))
        keep = jnp.where(key > thr, 1.0, jnp.where(rank <= rem, tie, 0.0))
        keep = jnp.where(key > KEY_NEG_INF, keep, 0.0)
        bias_ref[pl.ds(off, tk), :] = jnp.where(keep > 0.0, 0.0, NEG_INF).astype(BF16)
        return rank[tk - 1:tk, :]

    lax.fori_loop(0, nvis, out_body, jnp.zeros((1, tq), F32))

    def fill_body(j, carry):
        off = pl.multiple_of(j * tk, tk)
        bias_ref[pl.ds(off, tk), :] = jnp.full((tk, tq), NEG_INF, BF16)
        return carry

    lax.fori_loop(nvis, nk, fill_body, 0)


def _dsa_select(qit, wit, kib, batch, seq, topk):
    tq, tk = min(256, seq), min(512, seq)
    nq = seq // tq
    t = batch * seq
    return pl.pallas_call(
        functools.partial(_dsa_select_kernel, tq=tq, tk=tk, topk=topk),
        grid=(batch, nq),
        in_specs=[
            pl.BlockSpec((IDX_HEADS, IDX_DIM, tq), lambda b, q: (0, 0, b * nq + q)),
            pl.BlockSpec((IDX_HEADS, tq), lambda b, q: (0, b * nq + q)),
            pl.BlockSpec((seq, IDX_DIM), lambda b, q: (b, 0)),
            _whole((tk, tk)),
        ],
        out_specs=pl.BlockSpec((seq, tq), lambda b, q: (0, b * nq + q)),
        out_shape=jax.ShapeDtypeStruct((seq, t), BF16),
        scratch_shapes=[pltpu.VMEM((seq, tq), I32)],
        compiler_params=_params("parallel", "arbitrary"),
        name="dsa_select",
    )(qit, wit, kib, jnp.tril(jnp.ones((tk, tk), BF16)))


def _dsa_samp_kernel(q_ref, qi_ref, wi_ref, kc_ref, vc_ref, kic_ref, kn_ref, vn_ref, kin_ref, x_ref, wo_ref, o_ref,
                     *, n_q, topk):
    b = pl.program_id(0)
    hd, dh = DSA_HEADS, DSA_HEAD_DIM
    past = kc_ref.shape[1]
    n_keys = past + LANES
    own = _div(lax.broadcasted_iota(I32, (1, LANES), 1), n_q) == _mod(b, LANES // n_q)

    qi = qi_ref[...].reshape(IDX_HEADS * n_q, IDX_DIM)
    lg1 = jnp.maximum(_dot_t(qi, kic_ref[0].astype(BF16)), 0.0)
    lg2 = jnp.maximum(_dot_t(qi, kin_ref[...]), 0.0)
    w = wi_ref[...]
    sc1 = jnp.zeros((n_q, past), F32)
    sc2 = jnp.zeros((n_q, LANES), F32)
    for hh in range(IDX_HEADS):
        sc1 = sc1 + w[:, hh:hh + 1] * lg1[hh * n_q:(hh + 1) * n_q]
        sc2 = sc2 + w[:, hh:hh + 1] * lg2[hh * n_q:(hh + 1) * n_q]
    key = _sort_key(jnp.concatenate([sc1, jnp.where(own, sc2, NEG_INF)], axis=1))
    idx = lax.broadcasted_iota(I32, (1, n_keys), 1)

    count = lambda hit: jnp.sum(hit(key, idx), axis=1, keepdims=True)
    thr = _kth_largest_key(count, topk)
    rem = topk - count(lambda k_, i_: jnp.where(k_ > thr, 1, 0))
    cut = _tie_cutoff(count, thr, rem, int(n_keys - 1).bit_length())
    bias = jnp.where(_select(key, idx, thr, cut), 0.0, NEG_INF)
    bias = jnp.concatenate([bias] * hd, axis=0)

    lane_head = _div(lax.broadcasted_iota(I32, (1, hd * dh), 1), dh)
    qf = q_ref[...].astype(F32)
    qbd = jnp.concatenate([jnp.where(lane_head == hh, qf, 0.0) for hh in range(hd)], axis=0).astype(BF16)
    kc = kc_ref[0].astype(BF16)
    vc = vc_ref[0].astype(BF16)
    s1 = _dot_t(qbd, kc) + bias[:, :past]
    s2 = _dot_t(qbd, kn_ref[...]) + bias[:, past:]
    m = jnp.maximum(jnp.max(s1, axis=1, keepdims=True), jnp.max(s2, axis=1, keepdims=True))
    p1 = jnp.exp(s1 - m)
    p2 = jnp.exp(s2 - m)
    l = jnp.sum(p1, axis=1, keepdims=True) + jnp.sum(p2, axis=1, keepdims=True)
    o_all = (_dot(p1.astype(BF16), vc) + _dot(p2.astype(BF16), vn_ref[...])) / l
    out = jnp.zeros((n_q, hd * dh), F32)
    for hh in range(hd):
        out = out + jnp.where(lane_head == hh, o_all[hh * n_q:(hh + 1) * n_q], 0.0)
    o_ref[...] = x_ref[...] + _dot(out.astype(BF16), wo_ref[...])


def _dsa_attn_sample(x, qb, qib, wi, kb, vb, kib, cache_k, cache_v, cache_ki, w_o, n_b, n_q, topk):
    t, d = x.shape
    hdm = DSA_HEADS * DSA_HEAD_DIM
    past = cache_k.shape[1]
    per = LANES // n_q
    return pl.pallas_call(
        functools.partial(_dsa_samp_kernel, n_q=n_q, topk=topk),
        grid=(n_b,),
        in_specs=[
            pl.BlockSpec((n_q, hdm), lambda b: (b, 0)),
            pl.BlockSpec((IDX_HEADS, n_q, IDX_DIM), lambda b: (0, b, 0)),
            pl.BlockSpec((n_q, IDX_HEADS), lambda b: (b, 0)),
            pl.BlockSpec((1, past, hdm), lambda b: (b, 0, 0)),
            pl.BlockSpec((1, past, hdm), lambda b: (b, 0, 0)),
            pl.BlockSpec((1, past, IDX_DIM), lambda b: (b, 0, 0)),
            pl.BlockSpec((LANES, hdm), lambda b: (b // per, 0)),
            pl.BlockSpec((LANES, hdm), lambda b: (b // per, 0)),
            pl.BlockSpec((LANES, IDX_DIM), lambda b: (b // per, 0)),
            pl.BlockSpec((n_q, d), lambda b: (b, 0)),
            _whole(w_o.shape),
        ],
        out_specs=pl.BlockSpec((n_q, d), lambda b: (b, 0)),
        out_shape=jax.ShapeDtypeStruct((t, d), F32),
        compiler_params=_params("parallel"),
        name="dsa_attn_sample",
    )(qb, qib, wi, cache_k.reshape(n_b, past, hdm), cache_v.reshape(n_b, past, hdm), cache_ki, kb, vb, kib, x, w_o)


def kernel(x_prompt, x_sample, cache_mla_ckv, cache_mla_krope, cache_dsa_k, cache_dsa_v, cache_dsa_kidx, norm_mix, norm_ffn, norm_final, mla_w_dq, mla_g_q, mla_w_uq, mla_w_dkv, mla_g_kv, mla_w_ukv, mla_w_o, cmlp_w_in, cmlp_ln_g, cmlp_ln_b, cmlp_w_s, cmlp_b_s, cmlp_w_out, dsa_w_qkv, dsa_w_o, dsa_w_qidx, dsa_w_kidx, dsa_g_kidx, dsa_w_widx, ffn_w_in, ffn_w_out):
    batch, seq, d = x_prompt.shape
    n_b, n_q, _ = x_sample.shape
    past = cache_mla_ckv.shape[2]
    depth = norm_mix.shape[0]
    xp = x_prompt.reshape(batch * seq, d)
    xs = x_sample.reshape(n_b * n_q, d)
    tab_p = _rope_tables(jnp.arange(seq))
    tab_s = tuple(jnp.tile(a, (n_b, 1)) for a in _rope_tables(past + jnp.arange(n_q)))
    cast = lambda a: a.astype(BF16)
    outs = {k: [] for k in ("ckv_p", "kr_p", "ckv_s", "kr_s", "cv_s", "dk_p", "dv_p", "di_p", "dk_s", "dv_s", "di_s")}
    for i in range(depth):
        kind, j = i % 3, i // 3
        if kind == 0:
            wts = _mla_weights(mla_w_dq[j], mla_w_uq[j], mla_w_dkv[j], mla_w_ukv[j], mla_w_o[j])
            qt, kh, vt, ckv, kr, kn2 = _mla_proj_t(xp, norm_mix[i], wts, mla_g_q[j], mla_g_kv[j], tab_p, seq)
            xp = _attn_t(xp, qt, kh, vt, kn2, None, wts["wot"], batch, seq, 256, "mla_attn_prompt")
            outs["ckv_p"].append(ckv.reshape(batch, seq, -1)); outs["kr_p"].append(kr.reshape(batch, seq, -1))
            ql, qr, ckv, kr, ckvb, krb = _mla_proj(xs, norm_mix[i], wts, mla_g_q[j], mla_g_kv[j], tab_s)
            xs = _mla_attn_sample(xs, ql, qr, ckvb, krb, cache_mla_ckv[j], cache_mla_krope[j], wts, n_b, n_q)
            outs["ckv_s"].append(ckv.reshape(n_b, n_q, -1)); outs["kr_s"].append(kr.reshape(n_b, n_q, -1))
        elif kind == 1:
            w_in, w_out = cast(cmlp_w_in[j]), cast(cmlp_w_out[j])
            xp, _ = _cmlp(xp, norm_mix[i], w_in, cmlp_ln_g[j], cmlp_ln_b[j], cmlp_w_s[j], cmlp_b_s[j], w_out,
                          min(seq, CMLP_CHUNK), False)
            xs, v_s = _cmlp(xs, norm_mix[i], w_in, cmlp_ln_g[j], cmlp_ln_b[j], cmlp_w_s[j], cmlp_b_s[j], w_out,
                            min(n_q, CMLP_CHUNK), True)
            outs["cv_s"].append(v_s.reshape(n_b, n_q, -1))
        else:
            hdm = DSA_HEADS * DSA_HEAD_DIM
            w_qkv, w_qidx, w_o = cast(dsa_w_qkv[j]), cast(dsa_w_qidx[j]), cast(dsa_w_o[j])
            w_kidx, w_widx = cast(dsa_w_kidx[j]), cast(dsa_w_widx[j])
            zpad = lambda n: jnp.zeros((d, n), BF16)
            wt = dict(wkv=w_qkv[:, hdm:], wqt=w_qkv[:, :hdm].T, wvt=w_qkv[:, 2 * hdm:].T, wqit=w_qidx.T,
                      wkw=jnp.concatenate([w_kidx, zpad(LANES - IDX_DIM)], axis=1),
                      wwt=jnp.concatenate([w_widx, zpad(BF16_ROWS - IDX_HEADS)], axis=1).T)
            hshape = (DSA_HEADS, DSA_HEAD_DIM)
            k, v, ki, qt, kh, vt, qit, kib, wit, kn2 = _dsa_proj_t(xp, norm_mix[i], wt, dsa_g_kidx[j])
            bias = _dsa_select(qit, wit, kib, batch, seq, min(TOPK_MAX, seq // 4))
            xp = _attn_t(xp, qt, kh, vt, kn2, bias, w_o.T, batch, seq, 512, "dsa_attn_prompt")
            outs["dk_p"].append(k.reshape((batch, seq) + hshape)); outs["dv_p"].append(v.reshape((batch, seq) + hshape))
            outs["di_p"].append(ki.reshape(batch, seq, -1))
            w_kw = jnp.concatenate([w_kidx, w_widx, zpad(LANES - IDX_DIM - IDX_HEADS)], axis=1)
            k, v, ki, qb, kb, vb, qib, kib, wi = _dsa_proj(xs, norm_mix[i], w_qkv, w_qidx, w_kw, dsa_g_kidx[j])
            xs = _dsa_attn_sample(xs, qb, qib, wi, kb, vb, kib, cache_dsa_k[j], cache_dsa_v[j], cache_dsa_kidx[j], w_o,
                                  n_b, n_q, min(TOPK_MAX, (past + n_q) // 4))
            outs["dk_s"].append(k.reshape((n_b, n_q) + hshape)); outs["dv_s"].append(v.reshape((n_b, n_q) + hshape))
            outs["di_s"].append(ki.reshape(n_b, n_q, -1))
        w_in, w_out = cast(ffn_w_in[i]), cast(ffn_w_out[i])
        final = i == depth - 1
        xp = _ffn(xp, norm_ffn[i], w_in, w_out, norm_final, final)
        xs = _ffn(xs, norm_ffn[i], w_in, w_out, norm_final, final)
    st = lambda name: jnp.stack(outs[name])
    return (xp.reshape(batch, seq, d), xs.reshape(n_b, n_q, d),
            st("ckv_p"), st("kr_p"), st("ckv_s"), st("kr_s"), st("cv_s"),
            st("dk_p"), st("dv_p"), st("di_p"), st("dk_s"), st("dv_s"), st("di_s"))
```

```python
import functools

import numpy as np
import jax
import jax.numpy as jnp
from jax import lax
from jax.experimental import pallas as pl
from jax.experimental.pallas import tpu as pltpu

F32, BF16, I32 = jnp.float32, jnp.bfloat16, jnp.int32

CHUNK = 64
EPS = 1e-6
MLA_HEADS, MLA_Q_LORA, MLA_KV_LORA, MLA_NOPE, MLA_ROPE, MLA_V = 16, 512, 256, 64, 32, 64
ROPE_BASE = 10000.0
MLA_SCALE = (MLA_NOPE + MLA_ROPE) ** -0.5
CMLP_CHUNK, CMLP_WIDTH, CMLP_GROUPS = 128, 2048, 8
DSA_HEADS, DSA_HEAD_DIM = 16, 64
DSA_SCALE = DSA_HEAD_DIM ** -0.5
IDX_HEADS, IDX_DIM = 8, 64
IDX_SCALE = IDX_DIM ** -0.5
TOPK_MAX = 256

LANES = 128
MXU_TILE = 256
BF16_ROWS = 16
VMEM_LIMIT = 52 * 1024 * 1024
NEG_INF = float("-inf")
INT_MIN = -2 ** 31
KEY_NEG_INF = -2139095041
LOG2E = float(np.log2(np.e))
KEY_NORM_SLACK = 1.01
LAG_MARGIN = 100.0
HEAD_V = 64
V_ROWS = HEAD_V + BF16_ROWS


def _dot(a, b):
    return jnp.dot(a, b, preferred_element_type=F32)


def _dot_t(a, b):
    return lax.dot_general(a, b, (((1,), (1,)), ((), ())), preferred_element_type=F32)


def _rms(x, g):
    return x * lax.rsqrt(jnp.mean(x * x, axis=-1, keepdims=True) + EPS) * g


def _log2(n):
    assert n > 0 and n & (n - 1) == 0, n
    return n.bit_length() - 1


def _div(x, n):
    return lax.shift_right_logical(x, jnp.int32(_log2(n)))


def _mod(x, n):
    assert n & (n - 1) == 0, n
    return x & (n - 1)


def _params(*sem):
    return pltpu.CompilerParams(dimension_semantics=sem, vmem_limit_bytes=VMEM_LIMIT)


def _whole(shape):
    nd = len(shape)
    return pl.BlockSpec(shape, lambda *_: (0,) * nd)


def _ffn_kernel(x_ref, g_ref, wg_ref, wu_ref, wo_ref, gf_ref, o_ref, act_scr, *, final):
    x = x_ref[...]
    h = _rms(x, g_ref[...]).astype(BF16)
    f = wo_ref.shape[0]
    chunk = MXU_TILE if f % MXU_TILE == 0 else f
    nc = f // chunk
    nxt = (_dot(h, wg_ref[:, 0:chunk]), _dot(h, wu_ref[:, 0:chunk]))
    for c in range(nc):
        gate, up = nxt
        if c + 1 < nc:
            lo = (c + 1) * chunk
            nxt = (_dot(h, wg_ref[:, lo:lo + chunk]), _dot(h, wu_ref[:, lo:lo + chunk]))
        act_scr[:, c * chunk:(c + 1) * chunk] = (jax.nn.silu(gate) * up).astype(BF16)
    y = x + _dot(act_scr[...], wo_ref[...])
    if final:
        y = _rms(y, gf_ref[...])
    o_ref[...] = y


def _ffn(x, g, w_in, w_out, g_final, final):
    t, d = x.shape
    f = w_out.shape[0]
    tm = min(512, t)
    once = pl.Buffered(1)
    return pl.pallas_call(
        functools.partial(_ffn_kernel, final=final),
        grid=(t // tm,),
        in_specs=[
            pl.BlockSpec((tm, d), lambda i: (i, 0)),
            _whole((1, d)),
            pl.BlockSpec((d, f), lambda i: (0, 0), pipeline_mode=once),
            pl.BlockSpec((d, f), lambda i: (0, 1), pipeline_mode=once),
            pl.BlockSpec((f, d), lambda i: (0, 0), pipeline_mode=once),
            _whole((1, d)),
        ],
        out_specs=pl.BlockSpec((tm, d), lambda i: (i, 0)),
        out_shape=jax.ShapeDtypeStruct((t, d), F32),
        scratch_shapes=[pltpu.VMEM((tm, f), BF16)],
        compiler_params=_params("parallel"),
        name="ffn",
    )(x, g.reshape(1, d), w_in, w_in, w_out, g_final.reshape(1, d))


def _gelu(x):
    return 0.5 * x * (1.0 + lax.erf(x * np.float32(np.sqrt(0.5))))


def _layernorm(x, g, b):
    mu = jnp.mean(x, axis=-1, keepdims=True)
    xc = x - mu
    return xc * lax.rsqrt(jnp.mean(xc * xc, axis=-1, keepdims=True) + EPS) * g + b


def _cmlp_kernel(x_ref, g_ref, win_ref, lng_ref, lnb_ref, ws_ref, bs_ref, wout_ref, *rest, n_rows, write_v):
    if write_v:
        o_ref, v_ref, vb_scr, gated_scr = rest
    else:
        o_ref, vb_scr, gated_scr = rest
    tm = x_ref.shape[0]
    w = CMLP_WIDTH
    gw = w // CMLP_GROUPS
    c = CMLP_CHUNK
    x = x_ref[...]
    h = _rms(x, g_ref[...]).astype(BF16)
    v = _layernorm(_gelu(_dot(h, win_ref[:, w:])), lng_ref[...], lnb_ref[...])
    if write_v:
        v_ref[...] = v
    vb_scr[...] = v.astype(BF16)
    r_i = lax.broadcasted_iota(I32, (c, c), 0)
    c_i = lax.broadcasted_iota(I32, (c, c), 1)
    keep = jnp.where(c_i >= r_i - _mod(r_i, n_rows), jnp.where(c_i <= r_i, 1, 0), 0) > 0
    u_next = _dot(h, win_ref[:, 0:gw])
    for g in range(CMLP_GROUPS):
        lo, hi = g * gw, (g + 1) * gw
        u_raw = u_next
        if g + 1 < CMLP_GROUPS:
            u_next = _dot(h, win_ref[:, hi:hi + gw])
        wg = jnp.where(keep, ws_ref[g], 0.0).astype(BF16)
        bias = bs_ref[:, g:g + 1]
        mixed = jnp.concatenate(
            [_dot(wg, vb_scr[k * c:(k + 1) * c, lo:hi]) + bias for k in range(tm // c)], axis=0)
        gated_scr[:, lo:hi] = (_gelu(u_raw) * mixed).astype(BF16)
    o_ref[...] = x + _dot(gated_scr[...], wout_ref[...])


def _cmlp(x, g, w_in, ln_g, ln_b, w_s, b_s, w_out, n_rows, write_v):
    t, d = x.shape
    w = CMLP_WIDTH
    c = CMLP_CHUNK
    tm = min(512, t)
    rep = c // n_rows
    ws_t = jnp.tile(w_s[:, :n_rows, :n_rows], (1, rep, rep))
    bs_t = jnp.tile(b_s[:, :n_rows].T, (rep, 1))
    out_shape = [jax.ShapeDtypeStruct((t, d), F32)]
    out_specs = [pl.BlockSpec((tm, d), lambda i: (i, 0))]
    if write_v:
        out_shape.append(jax.ShapeDtypeStruct((t, w), F32))
        out_specs.append(pl.BlockSpec((tm, w), lambda i: (i, 0)))
    res = pl.pallas_call(
        functools.partial(_cmlp_kernel, n_rows=n_rows, write_v=write_v),
        grid=(t // tm,),
        in_specs=[
            pl.BlockSpec((tm, d), lambda i: (i, 0)),
            _whole((1, d)), _whole((d, 2 * w)), _whole((1, w)), _whole((1, w)),
            _whole((CMLP_GROUPS, c, c)), _whole((c, CMLP_GROUPS)), _whole((w, d)),
        ],
        out_specs=out_specs,
        out_shape=out_shape,
        scratch_shapes=[pltpu.VMEM((tm, w), BF16), pltpu.VMEM((tm, w), BF16)],
        compiler_params=_params("parallel"),
        name="cmlp",
    )(x, g.reshape(1, d), w_in, ln_g.reshape(1, w), ln_b.reshape(1, w), ws_t, bs_t, w_out)
    return res if write_v else (res[0], None)


def _attn_t_kernel(qi_ref, kj_ref, last_ref, nsub_ref, qt_ref, k_ref, vt_ref, *rest, tq, tk, sub, use_bias):
    if use_bias:
        bias_ref, kn_ref, x_ref, wot_ref, o_ref, m_scr, pend_scr, qn_scr, acc_scr, cat_scr = rest
    else:
        kn_ref, x_ref, wot_ref, o_ref, m_scr, pend_scr, qn_scr, acc_scr, cat_scr = rest
    p = pl.program_id(1)
    qi, kj = qi_ref[p], kj_ref[p]
    hd = qt_ref.shape[0]

    @pl.when(kj == 0)
    def _():
        m_scr[...] = jnp.full_like(m_scr, NEG_INF)
        pend_scr[...] = jnp.ones_like(pend_scr)
        acc_scr[...] = jnp.zeros_like(acc_scr)
        for hh in range(hd):
            q = qt_ref[hh].astype(F32)
            qn_scr[hh] = jnp.sqrt(jnp.sum(q * q, axis=0, keepdims=True)) * kn_ref[0, hh:hh + 1, 0:1]

    def step(off, mask, lagged):
        keys = lambda hh: k_ref[hh, pl.ds(off, sub), :]
        vals = lambda hh: vt_ref[hh, :, pl.ds(off, sub)]
        s_next = _dot(keys(0), qt_ref[0])
        pend = None
        for hh in range(hd):
            s = s_next
            if hh + 1 < hd:
                s_next = _dot(keys(hh + 1), qt_ref[hh + 1])
            if mask is not None:
                s = s + mask
            m_prev = m_scr[hh]
            m_new = jnp.maximum(m_prev, jnp.max(s, axis=0, keepdims=True))
            if lagged:
                scale = pend_scr[hh]
                pe = jnp.exp2(s - m_prev).astype(BF16)
                pend_scr[hh] = jnp.exp2(m_prev - m_new)
            else:
                m_safe = jnp.where(m_new == NEG_INF, 0.0, m_new)
                scale = pend_scr[hh] * jnp.exp2(m_prev - m_safe)
                pe = jnp.exp2(s - m_safe).astype(BF16)
                pend_scr[hh] = jnp.ones_like(m_prev)
            m_scr[hh] = m_new
            if pend is not None:
                ph, pa, pp = pend
                acc_scr[ph] = pa * acc_scr[ph] + _dot(vals(ph), pp)
            pend = (hh, scale, pe)
        ph, pa, pp = pend
        acc_scr[ph] = pa * acc_scr[ph] + _dot(vals(ph), pp)

    def sub_body(j, carry):
        off = pl.multiple_of(j * sub, sub)
        excess = qn_scr[0] - m_scr[0]
        for hh in range(1, hd):
            excess = jnp.maximum(excess, qn_scr[hh] - m_scr[hh])
        lag_ok = jnp.max(excess) <= LAG_MARGIN
        if use_bias:
            bias = bias_ref[pl.ds(off, sub), :].astype(F32)

            @pl.when(lag_ok)
            def _():
                step(off, bias, True)

            @pl.when(jnp.logical_not(lag_ok))
            def _():
                step(off, bias, False)
        else:
            base = kj * tk + off
            needs_mask = _div(base + sub - 1, CHUNK) > _div(qi * tq, CHUNK)

            def chunk_mask():
                k_chunk = _div(base + lax.broadcasted_iota(I32, (sub, 1), 0), CHUNK)
                q_chunk = _div(qi * tq + lax.broadcasted_iota(I32, (1, tq), 1), CHUNK)
                return jnp.where(k_chunk <= q_chunk, 0.0, NEG_INF)

            @pl.when(jnp.logical_not(lag_ok))
            def _():
                step(off, chunk_mask(), False)

            @pl.when(jnp.logical_and(lag_ok, needs_mask))
            def _():
                step(off, chunk_mask(), True)

            @pl.when(jnp.logical_and(lag_ok, jnp.logical_not(needs_mask)))
            def _():
                step(off, None, True)
        return carry

    lax.fori_loop(0, nsub_ref[p], sub_body, 0)

    @pl.when(last_ref[p] == 1)
    def _():
        for hh in range(hd):
            a = acc_scr[hh]
            cat_scr[hh * HEAD_V:(hh + 1) * HEAD_V, :] = (a[:HEAD_V] / a[HEAD_V:HEAD_V + 1]).astype(BF16)
        o_ref[...] = x_ref[...] + _dot(wot_ref[...], cat_scr[...]).T


def _causal_pairs(nq, tq, tk, sub):
    qi, kj, last, nsub = [], [], [], []
    per = tk // sub
    for q in range(nq):
        vis_sub = -(-((q + 1) * tq) // sub)
        nvis = -(-vis_sub // per)
        for k in range(nvis):
            qi.append(q); kj.append(k); last.append(int(k == nvis - 1)); nsub.append(min(per, vis_sub - k * per))
    as_arr = lambda v: jnp.asarray(np.array(v, np.int32))
    return as_arr(qi), as_arr(kj), as_arr(last), as_arr(nsub)


def _attn_t(x, qt, kh, vt, kn2, bias, wot, batch, seq, sub, name):
    t, d = x.shape
    hd, dk, _ = qt.shape
    tq, tk, sub = min(512, seq), min(1024, seq), min(sub, seq)
    nq, nk = seq // tq, seq // tk
    qi, kj, last, nsub = _causal_pairs(nq, tq, tk, sub)
    use_bias = bias is not None
    in_specs = [
        pl.BlockSpec((hd, dk, tq), lambda b, p, qi, kj, la, ns: (0, 0, b * nq + qi[p])),
        pl.BlockSpec((hd, tk, dk), lambda b, p, qi, kj, la, ns: (0, b * nk + kj[p], 0)),
        pl.BlockSpec((hd, V_ROWS, tk), lambda b, p, qi, kj, la, ns: (0, 0, b * nk + kj[p])),
    ]
    args = [qt, kh, vt]
    if use_bias:
        in_specs.append(pl.BlockSpec((tk, tq), lambda b, p, qi, kj, la, ns: (kj[p], b * nq + qi[p])))
        args.append(bias)
    kn = jnp.sqrt(jnp.max(kn2[:, :, 0].reshape(batch, -1, hd), axis=1)) * KEY_NORM_SLACK
    in_specs.append(pl.BlockSpec((1, hd, LANES), lambda b, p, qi, kj, la, ns: (b, 0, 0)))
    args.append(jnp.broadcast_to(kn[:, :, None], (batch, hd, LANES)))
    in_specs += [
        pl.BlockSpec((tq, d), lambda b, p, qi, kj, la, ns: (b * nq + qi[p], 0)),
        pl.BlockSpec(wot.shape, lambda b, p, qi, kj, la, ns: (0, 0)),
    ]
    args += [x, wot]
    grid_spec = pltpu.PrefetchScalarGridSpec(
        num_scalar_prefetch=4,
        grid=(batch, int(qi.shape[0])),
        in_specs=in_specs,
        out_specs=pl.BlockSpec((tq, d), lambda b, p, qi, kj, la, ns: (b * nq + qi[p], 0)),
        scratch_shapes=[
            pltpu.VMEM((hd, 1, tq), F32), pltpu.VMEM((hd, 1, tq), F32), pltpu.VMEM((hd, 1, tq), F32),
            pltpu.VMEM((hd, V_ROWS, tq), F32), pltpu.VMEM((hd * HEAD_V, tq), BF16),
        ],
    )
    return pl.pallas_call(
        functools.partial(_attn_t_kernel, tq=tq, tk=tk, sub=sub, use_bias=use_bias),
        grid_spec=grid_spec,
        out_shape=jax.ShapeDtypeStruct((t, d), F32),
        compiler_params=_params("parallel", "arbitrary"),
        name=name,
    )(qi, kj, last, nsub, *args)


def _rope_tables(pos):
    half = MLA_ROPE // 2
    inv = ROPE_BASE ** (-jnp.arange(half, dtype=F32) / half)
    ang = pos.astype(F32)[:, None] * inv[None, :]
    cos, sin = jnp.cos(ang), jnp.sin(ang)
    cos_k = jnp.concatenate([cos, cos], axis=1)
    sin_k = jnp.concatenate([-sin, sin], axis=1)
    return jnp.tile(cos_k, (1, MLA_HEADS)), jnp.tile(sin_k, (1, MLA_HEADS)), cos_k, sin_k


def _swap_halves(w, group):
    shp = w.shape
    wr = w.reshape(shp[:-1] + (shp[-1] // group, 2, group // 2))
    return wr[..., ::-1, :].reshape(shp)


def _mla_weights(w_dq, w_uq, w_dkv, w_ukv, w_o):
    hd = MLA_HEADS
    cast = lambda a: a.astype(BF16)
    w_uq, w_dkv, w_ukv = cast(w_uq), cast(w_dkv), cast(w_ukv)
    wqn = w_uq[:, :, :MLA_NOPE].reshape(MLA_Q_LORA, hd * MLA_NOPE)
    wqr = w_uq[:, :, MLA_NOPE:].reshape(MLA_Q_LORA, hd * MLA_ROPE)
    wqs = _swap_halves(wqr, MLA_ROPE)
    wkc, wkr = w_dkv[:, :MLA_KV_LORA], w_dkv[:, MLA_KV_LORA:]
    wks = _swap_halves(wkr, MLA_ROPE)
    wuk = jnp.transpose(w_ukv[:, :, :MLA_NOPE], (1, 2, 0))
    zeros = jnp.zeros_like(wuk)
    even = jnp.concatenate([wuk, zeros], axis=1)
    odd = jnp.concatenate([zeros, wuk], axis=1)
    wuk2 = jnp.where((jnp.arange(hd) % 2 == 0)[:, None, None], even, odd)
    wuv = jnp.transpose(w_ukv[:, :, MLA_NOPE:], (1, 0, 2))
    d = w_dkv.shape[0]
    pad_r = lambda a: jnp.concatenate([a, jnp.zeros((d, LANES - MLA_ROPE), BF16)], axis=1)
    wukp = jnp.concatenate([jnp.zeros((MLA_KV_LORA, hd, MLA_ROPE), BF16), w_ukv[:, :, :MLA_NOPE],
                            jnp.zeros((MLA_KV_LORA, hd, LANES - MLA_ROPE - MLA_NOPE), BF16)], axis=2)
    return dict(wdq=cast(w_dq), wqn=wqn, wqr=wqr, wqs=wqs, wkc=wkc, wkr=wkr, wks=wks, wuk=wuk2, wuv=wuv, wo=cast(w_o),
                wqnt=wqn.T, wqrt=wqr.T, wqst=wqs.T, wkrp=pad_r(wkr), wksp=pad_r(wks),
                wukp=wukp.reshape(MLA_KV_LORA, hd * LANES), wuvt=wuv.transpose(0, 2, 1).reshape(hd * MLA_V, MLA_KV_LORA),
                wot=cast(w_o).T)


def _mla_proj_kernel(x_ref, gm_ref, wdq_ref, gq_ref, wqn_ref, wqr_ref, wqs_ref, wkc_ref, wkr_ref, wks_ref,
                     gkv_ref, wuk_ref, cq_ref, sq_ref, ck_ref, sk_ref,
                     ql_ref, qr_ref, ckv_ref, kr_ref, ckvb_ref, krb_ref):
    h = _rms(x_ref[...], gm_ref[...]).astype(BF16)
    cq = _rms(_dot(h, wdq_ref[...]), gq_ref[...]).astype(BF16)
    qn = _dot(cq, wqn_ref[...])
    qr = ((_dot(cq, wqr_ref[...]) * cq_ref[...] + _dot(cq, wqs_ref[...]) * sq_ref[...]) * MLA_SCALE).astype(BF16)
    for hh in range(MLA_HEADS):
        pair = qn[:, (hh // 2) * LANES:(hh // 2 + 1) * LANES].astype(BF16)
        ql_ref[hh] = (_dot(pair, wuk_ref[hh]) * MLA_SCALE).astype(BF16)
        qr_ref[hh] = qr[:, hh * MLA_ROPE:(hh + 1) * MLA_ROPE]
    ckv = _rms(_dot(h, wkc_ref[...]), gkv_ref[...])
    ckv_ref[...] = ckv
    ckvb_ref[...] = ckv.astype(BF16)
    kr = _dot(h, wkr_ref[...]) * ck_ref[...] + _dot(h, wks_ref[...]) * sk_ref[...]
    kr_ref[...] = kr
    krb_ref[...] = kr.astype(BF16)


def _mla_proj(x, g_mix, wts, g_q, g_kv, tables):
    t, d = x.shape
    tm = min(256, t)
    cos_q, sin_q, cos_k, sin_k = tables
    nrep = cos_q.shape[0] // tm
    tab = lambda wdt: pl.BlockSpec((tm, wdt), lambda i: (i % nrep, 0))
    hd = MLA_HEADS
    row = lambda wdt: pl.BlockSpec((tm, wdt), lambda i: (i, 0))
    hm = lambda wdt: pl.BlockSpec((hd, tm, wdt), lambda i: (0, i, 0))
    return pl.pallas_call(
        _mla_proj_kernel,
        grid=(t // tm,),
        in_specs=[
            row(d), _whole((1, d)), _whole(wts["wdq"].shape), _whole((1, MLA_Q_LORA)),
            _whole(wts["wqn"].shape), _whole(wts["wqr"].shape), _whole(wts["wqs"].shape),
            _whole(wts["wkc"].shape), _whole(wts["wkr"].shape), _whole(wts["wks"].shape),
            _whole((1, MLA_KV_LORA)), _whole(wts["wuk"].shape),
            tab(hd * MLA_ROPE), tab(hd * MLA_ROPE), tab(MLA_ROPE), tab(MLA_ROPE),
        ],
        out_specs=[hm(MLA_KV_LORA), hm(MLA_ROPE), row(MLA_KV_LORA), row(MLA_ROPE), row(MLA_KV_LORA), row(MLA_ROPE)],
        out_shape=[
            jax.ShapeDtypeStruct((hd, t, MLA_KV_LORA), BF16), jax.ShapeDtypeStruct((hd, t, MLA_ROPE), BF16),
            jax.ShapeDtypeStruct((t, MLA_KV_LORA), F32), jax.ShapeDtypeStruct((t, MLA_ROPE), F32),
            jax.ShapeDtypeStruct((t, MLA_KV_LORA), BF16), jax.ShapeDtypeStruct((t, MLA_ROPE), BF16),
        ],
        compiler_params=_params("parallel"),
        name="mla_proj",
    )(x, g_mix.reshape(1, d), wts["wdq"], g_q.reshape(1, -1), wts["wqn"], wts["wqr"], wts["wqs"],
      wts["wkc"], wts["wkr"], wts["wks"], g_kv.reshape(1, -1), wts["wuk"], cos_q, sin_q, cos_k, sin_k)


def _mla_proj_t_kernel(x_ref, gm_ref, wdq_ref, gq_ref, wqnt_ref, wqrt_ref, wqst_ref, wkc_ref, wkrp_ref, wksp_ref,
                       gkv_ref, wukp_ref, wuvt_ref, cqt_ref, sqt_ref, ckp_ref, skp_ref,
                       qt_ref, kh_ref, vt_ref, ckv_ref, kr_ref, kn2_ref):
    tm = x_ref.shape[0]
    hd = MLA_HEADS
    qscale = MLA_SCALE * LOG2E
    h = _rms(x_ref[...], gm_ref[...]).astype(BF16)
    cq_raw = _dot(h, wdq_ref[...])
    ckv_raw = _dot(h, wkc_ref[...])
    kr_a = _dot(h, wkrp_ref[...])
    kr_b = _dot(h, wksp_ref[...])
    cq = _rms(cq_raw, gq_ref[...]).astype(BF16)
    qnt_raw = _dot_t(wqnt_ref[...], cq)
    qrt_a = _dot_t(wqrt_ref[...], cq)
    qrt_b = _dot_t(wqst_ref[...], cq)
    ckv = _rms(ckv_raw, gkv_ref[...])
    ckv_ref[...] = ckv
    cb = ckv.astype(BF16)
    kn = _dot(cb, wukp_ref[...])
    vt = _dot_t(wuvt_ref[...], cb).astype(BF16)
    qnt = (qnt_raw * qscale).astype(BF16)
    qrt = ((qrt_a * cqt_ref[...] + qrt_b * sqt_ref[...]) * qscale).astype(BF16)
    zeros = jnp.zeros((LANES - MLA_ROPE - MLA_NOPE, tm), BF16)
    krp = kr_a * ckp_ref[...] + kr_b * skp_ref[...]
    kr_ref[...] = krp[:, :MLA_ROPE]
    ones = jnp.ones((BF16_ROWS, tm), BF16)
    for hh in range(hd):
        qt_ref[hh, 0:MLA_ROPE, :] = qrt[hh * MLA_ROPE:(hh + 1) * MLA_ROPE]
        qt_ref[hh, MLA_ROPE:MLA_ROPE + MLA_NOPE, :] = qnt[hh * MLA_NOPE:(hh + 1) * MLA_NOPE]
        qt_ref[hh, MLA_ROPE + MLA_NOPE:, :] = zeros
        key = kn[:, hh * LANES:(hh + 1) * LANES] + krp
        kh_ref[hh] = key.astype(BF16)
        kn2_ref[0, hh:hh + 1, :] = jnp.broadcast_to(
            jnp.max(jnp.sum(key * key, axis=1, keepdims=True), axis=0, keepdims=True), (1, LANES))
        vt_ref[hh, 0:MLA_V, :] = vt[hh * MLA_V:(hh + 1) * MLA_V]
        vt_ref[hh, MLA_V:, :] = ones


def _mla_proj_t(x, g_mix, wts, g_q, g_kv, tables, seq):
    t, d = x.shape
    tm = min(256, seq)
    hd = MLA_HEADS
    cos_q, sin_q, cos_k, sin_k = tables
    pad = lambda a: jnp.concatenate([a, jnp.zeros((seq, LANES - MLA_ROPE), F32)], axis=1)
    nrep = seq // tm
    row = lambda wdt: pl.BlockSpec((tm, wdt), lambda i: (i, 0))
    names = ("wdq", "wqnt", "wqrt", "wqst", "wkc", "wkrp", "wksp", "wukp", "wuvt")
    w = {n: wts[n] for n in names}
    return pl.pallas_call(
        _mla_proj_t_kernel,
        grid=(t // tm,),
        in_specs=[
            row(d), _whole((1, d)), _whole(w["wdq"].shape), _whole((1, MLA_Q_LORA)),
            _whole(w["wqnt"].shape), _whole(w["wqrt"].shape), _whole(w["wqst"].shape),
            _whole(w["wkc"].shape), _whole(w["wkrp"].shape), _whole(w["wksp"].shape),
            _whole((1, MLA_KV_LORA)), _whole(w["wukp"].shape), _whole(w["wuvt"].shape),
            pl.BlockSpec((hd * MLA_ROPE, tm), lambda i: (0, i % nrep)),
            pl.BlockSpec((hd * MLA_ROPE, tm), lambda i: (0, i % nrep)),
            pl.BlockSpec((tm, LANES), lambda i: (i % nrep, 0)),
            pl.BlockSpec((tm, LANES), lambda i: (i % nrep, 0)),
        ],
        out_specs=[
            pl.BlockSpec((hd, LANES, tm), lambda i: (0, 0, i)),
            pl.BlockSpec((hd, tm, LANES), lambda i: (0, i, 0)),
            pl.BlockSpec((hd, V_ROWS, tm), lambda i: (0, 0, i)),
            row(MLA_KV_LORA), row(MLA_ROPE),
            pl.BlockSpec((1, hd, LANES), lambda i: (i, 0, 0)),
        ],
        out_shape=[
            jax.ShapeDtypeStruct((hd, LANES, t), BF16), jax.ShapeDtypeStruct((hd, t, LANES), BF16),
            jax.ShapeDtypeStruct((hd, V_ROWS, t), BF16),
            jax.ShapeDtypeStruct((t, MLA_KV_LORA), F32), jax.ShapeDtypeStruct((t, MLA_ROPE), F32),
            jax.ShapeDtypeStruct((t // tm, hd, LANES), F32),
        ],
        compiler_params=_params("parallel"),
        name="mla_proj_t",
    )(x, g_mix.reshape(1, d), w["wdq"], g_q.reshape(1, -1), w["wqnt"], w["wqrt"], w["wqst"],
      w["wkc"], w["wkrp"], w["wksp"], g_kv.reshape(1, -1), w["wukp"], w["wuvt"],
      cos_q.T, sin_q.T, pad(cos_k), pad(sin_k))


def _mla_epilogue(o_lat, x_ref, wuv_ref, wo_ref, o_ref, cat_scr, tq):
    ob = o_lat.astype(BF16)
    for hh in range(MLA_HEADS):
        cat_scr[:, hh * MLA_V:(hh + 1) * MLA_V] = _dot(ob[hh * tq:(hh + 1) * tq], wuv_ref[hh]).astype(BF16)
    o_ref[...] = x_ref[...] + _dot(cat_scr[...], wo_ref[...])


def _mla_samp_kernel(ql_ref, qr_ref, cc_ref, cr_ref, cn_ref, rn_ref, x_ref, wuv_ref, wo_ref, o_ref, cat_scr, *, n_q):
    b = pl.program_id(0)
    hd = MLA_HEADS
    ql = ql_ref[...].reshape(hd * n_q, MLA_KV_LORA)
    qr = qr_ref[...].reshape(hd * n_q, MLA_ROPE)
    cc = cc_ref[0].astype(BF16)
    cr = cr_ref[0].astype(BF16)
    cn = cn_ref[...]
    s1 = _dot_t(ql, cc) + _dot_t(qr, cr)
    s2 = _dot_t(ql, cn) + _dot_t(qr, rn_ref[...])
    own = _div(lax.broadcasted_iota(I32, (1, LANES), 1), n_q) == _mod(b, LANES // n_q)
    s2 = jnp.where(own, s2, NEG_INF)
    m = jnp.maximum(jnp.max(s1, axis=1, keepdims=True), jnp.max(s2, axis=1, keepdims=True))
    p1 = jnp.exp(s1 - m)
    p2 = jnp.exp(s2 - m)
    l = jnp.sum(p1, axis=1, keepdims=True) + jnp.sum(p2, axis=1, keepdims=True)
    o_lat = (_dot(p1.astype(BF16), cc) + _dot(p2.astype(BF16), cn)) / l
    _mla_epilogue(o_lat, x_ref, wuv_ref, wo_ref, o_ref, cat_scr, n_q)


def _mla_attn_sample(x, ql, qr, ckvb, krb, cache_c, cache_r, wts, n_b, n_q):
    t, d = x.shape
    hd = MLA_HEADS
    past = cache_c.shape[1]
    per = LANES // n_q
    return pl.pallas_call(
        functools.partial(_mla_samp_kernel, n_q=n_q),
        grid=(n_b,),
        in_specs=[
            pl.BlockSpec((hd, n_q, MLA_KV_LORA), lambda b: (0, b, 0)),
            pl.BlockSpec((hd, n_q, MLA_ROPE), lambda b: (0, b, 0)),
            pl.BlockSpec((1, past, MLA_KV_LORA), lambda b: (b, 0, 0)),
            pl.BlockSpec((1, past, MLA_ROPE), lambda b: (b, 0, 0)),
            pl.BlockSpec((LANES, MLA_KV_LORA), lambda b: (b // per, 0)),
            pl.BlockSpec((LANES, MLA_ROPE), lambda b: (b // per, 0)),
            pl.BlockSpec((n_q, d), lambda b: (b, 0)),
            _whole(wts["wuv"].shape), _whole(wts["wo"].shape),
        ],
        out_specs=pl.BlockSpec((n_q, d), lambda b: (b, 0)),
        out_shape=jax.ShapeDtypeStruct((t, d), F32),
        scratch_shapes=[pltpu.VMEM((n_q, hd * MLA_V), BF16)],
        compiler_params=_params("parallel"),
        name="mla_attn_sample",
    )(ql, qr, cache_c, cache_r, ckvb, krb, x, wts["wuv"], wts["wo"])


def _dsa_proj_kernel(x_ref, gm_ref, wqkv_ref, wqi_ref, wkw_ref, gki_ref,
                     k_ref, v_ref, ki_ref, qb_ref, kb_ref, vb_ref, qib_ref, kib_ref, wi_ref):
    hdm = DSA_HEADS * DSA_HEAD_DIM
    h = _rms(x_ref[...], gm_ref[...]).astype(BF16)
    qkv = _dot(h, wqkv_ref[...])
    q = qkv[:, :hdm] * DSA_SCALE
    k = qkv[:, hdm:2 * hdm]
    v = qkv[:, 2 * hdm:]
    k_ref[...] = k
    v_ref[...] = v
    qi = _dot(h, wqi_ref[...]) * IDX_SCALE
    kw = _dot(h, wkw_ref[...])
    ki = _rms(kw[:, :IDX_DIM], gki_ref[...])
    ki_ref[...] = ki
    kib_ref[...] = ki.astype(BF16)
    wi_ref[...] = kw[:, IDX_DIM:IDX_DIM + IDX_HEADS] * (IDX_HEADS ** -0.5)
    qb_ref[...] = q.astype(BF16)
    kb_ref[...] = k.astype(BF16)
    vb_ref[...] = v.astype(BF16)
    for hh in range(IDX_HEADS):
        qib_ref[hh] = qi[:, hh * IDX_DIM:(hh + 1) * IDX_DIM].astype(BF16)


def _dsa_proj(x, g_mix, w_qkv, w_qidx, w_kw, g_kidx):
    t, d = x.shape
    tm = min(512, t)
    hdm = DSA_HEADS * DSA_HEAD_DIM
    row = lambda wdt: pl.BlockSpec((tm, wdt), lambda i: (i, 0))
    return pl.pallas_call(
        _dsa_proj_kernel,
        grid=(t // tm,),
        in_specs=[row(d), _whole((1, d)), _whole(w_qkv.shape), _whole(w_qidx.shape), _whole(w_kw.shape),
                  _whole((1, IDX_DIM))],
        out_specs=[row(hdm), row(hdm), row(IDX_DIM), row(hdm), row(hdm), row(hdm),
                   pl.BlockSpec((IDX_HEADS, tm, IDX_DIM), lambda i: (0, i, 0)), row(IDX_DIM), row(IDX_HEADS)],
        out_shape=[
            jax.ShapeDtypeStruct((t, hdm), F32), jax.ShapeDtypeStruct((t, hdm), F32),
            jax.ShapeDtypeStruct((t, IDX_DIM), F32), jax.ShapeDtypeStruct((t, hdm), BF16),
            jax.ShapeDtypeStruct((t, hdm), BF16), jax.ShapeDtypeStruct((t, hdm), BF16),
            jax.ShapeDtypeStruct((IDX_HEADS, t, IDX_DIM), BF16), jax.ShapeDtypeStruct((t, IDX_DIM), BF16),
            jax.ShapeDtypeStruct((t, IDX_HEADS), F32),
        ],
        compiler_params=_params("parallel"),
        name="dsa_proj",
    )(x, g_mix.reshape(1, d), w_qkv, w_qidx, w_kw, g_kidx.reshape(1, IDX_DIM))


def _dsa_proj_t_kernel(x_ref, gm_ref, wkv_ref, wqt_ref, wvt_ref, wqit_ref, wkw_ref, wwt_ref, gki_ref,
                       k_ref, v_ref, ki_ref, qt_ref, kh_ref, vt_ref, qit_ref, kib_ref, wit_ref, kn2_ref):
    tm = x_ref.shape[0]
    hdm = DSA_HEADS * DSA_HEAD_DIM
    dh = DSA_HEAD_DIM
    h = _rms(x_ref[...], gm_ref[...]).astype(BF16)
    kv = _dot(h, wkv_ref[...])
    k = kv[:, :hdm]
    k_ref[...] = pltpu.einshape("t(hd)->thd", k, h=DSA_HEADS)
    v_ref[...] = pltpu.einshape("t(hd)->thd", kv[:, hdm:], h=DSA_HEADS)
    kw = _dot(h, wkw_ref[...])
    ki = _rms(kw[:, :IDX_DIM], gki_ref[...])
    ki_ref[...] = ki
    kib_ref[...] = ki.astype(BF16)
    wit_ref[...] = _dot_t(wwt_ref[...], h)[:IDX_HEADS] * (IDX_HEADS ** -0.5)
    qt = (_dot_t(wqt_ref[...], h) * (DSA_SCALE * LOG2E)).astype(BF16)
    vt = _dot_t(wvt_ref[...], h).astype(BF16)
    qit = (_dot_t(wqit_ref[...], h) * IDX_SCALE).astype(BF16)
    ones = jnp.ones((BF16_ROWS, tm), BF16)
    for hh in range(DSA_HEADS):
        qt_ref[hh] = qt[hh * dh:(hh + 1) * dh]
        key = k[:, hh * dh:(hh + 1) * dh]
        kh_ref[hh] = key.astype(BF16)
        kn2_ref[0, hh:hh + 1, :] = jnp.broadcast_to(
            jnp.max(jnp.sum(key * key, axis=1, keepdims=True), axis=0, keepdims=True), (1, LANES))
        vt_ref[hh, 0:dh, :] = vt[hh * dh:(hh + 1) * dh]
        vt_ref[hh, dh:, :] = ones
    for hh in range(IDX_HEADS):
        qit_ref[hh] = qit[hh * IDX_DIM:(hh + 1) * IDX_DIM]


def _dsa_proj_t(x, g_mix, w, g_kidx):
    t, d = x.shape
    tm = min(256, t)
    hd, dh = DSA_HEADS, DSA_HEAD_DIM
    hdm = hd * dh
    row = lambda wdt: pl.BlockSpec((tm, wdt), lambda i: (i, 0))
    names = ("wkv", "wqt", "wvt", "wqit", "wkw", "wwt")
    return pl.pallas_call(
        _dsa_proj_t_kernel,
        grid=(t // tm,),
        in_specs=[row(d), _whole((1, d))] + [_whole(w[n].shape) for n in names] + [_whole((1, IDX_DIM))],
        out_specs=[
            pl.BlockSpec((tm, hd, dh), lambda i: (i, 0, 0)), pl.BlockSpec((tm, hd, dh), lambda i: (i, 0, 0)),
            row(IDX_DIM),
            pl.BlockSpec((hd, dh, tm), lambda i: (0, 0, i)),
            pl.BlockSpec((hd, tm, dh), lambda i: (0, i, 0)),
            pl.BlockSpec((hd, V_ROWS, tm), lambda i: (0, 0, i)),
            pl.BlockSpec((IDX_HEADS, IDX_DIM, tm), lambda i: (0, 0, i)),
            row(IDX_DIM),
            pl.BlockSpec((IDX_HEADS, tm), lambda i: (0, i)),
            pl.BlockSpec((1, hd, LANES), lambda i: (i, 0, 0)),
        ],
        out_shape=[
            jax.ShapeDtypeStruct((t, hd, dh), F32), jax.ShapeDtypeStruct((t, hd, dh), F32),
            jax.ShapeDtypeStruct((t, IDX_DIM), F32),
            jax.ShapeDtypeStruct((hd, dh, t), BF16), jax.ShapeDtypeStruct((hd, t, dh), BF16),
            jax.ShapeDtypeStruct((hd, V_ROWS, t), BF16),
            jax.ShapeDtypeStruct((IDX_HEADS, IDX_DIM, t), BF16), jax.ShapeDtypeStruct((t, IDX_DIM), BF16),
            jax.ShapeDtypeStruct((IDX_HEADS, t), F32),
            jax.ShapeDtypeStruct((t // tm, hd, LANES), F32),
        ],
        compiler_params=_params("parallel"),
        name="dsa_proj_t",
    )(x, g_mix.reshape(1, d), *[w[n] for n in names], g_kidx.reshape(1, IDX_DIM))


def _sort_key(score):
    bits = lax.bitcast_convert_type(score, I32)
    return jnp.where(bits < 0, bits ^ jnp.int32(0x7FFFFFFF), bits)


def _kth_largest_key(count, topk):
    c0 = count(lambda key, idx: jnp.where(key >= 0, 1, 0))
    t0 = jnp.where(c0 >= topk, jnp.int32(0), jnp.int32(INT_MIN))

    def bit_body(i, t):
        cand = t + lax.shift_left(jnp.int32(1), jnp.int32(30) - i)
        c = count(lambda key, idx: jnp.where(key >= cand, 1, 0))
        return jnp.where(c >= topk, cand, t)

    return lax.fori_loop(0, 31, bit_body, t0)


def _radix_search(accept, start, nbits):
    t = start
    if nbits % 2:
        cand = t + jnp.int32(1 << (nbits - 1))
        t = jnp.where(accept(cand), cand, t)
        nbits -= 1

    def body(i, t):
        unit = lax.shift_left(jnp.int32(1), jnp.int32(nbits - 2) - 2 * i)
        a1, a2, a3 = accept(t + unit), accept(t + 2 * unit), accept(t + 3 * unit)
        return t + jnp.where(a3, 3, jnp.where(a2, 2, jnp.where(a1, 1, 0))) * unit

    return lax.fori_loop(0, nbits // 2, body, t)


def _select(key, idx, thr, cut):
    chosen = jnp.where(key > thr, 1, jnp.where(key == thr, jnp.where(idx <= cut, 1, 0), 0))
    return jnp.where(key > KEY_NEG_INF, chosen, 0) > 0


def _dsa_select_kernel(qit_ref, wit_ref, ki_ref, tri_ref, bias_ref, key_scr, *, tq, tk, topk):
    qb = pl.program_id(1)
    seq = ki_ref.shape[0]
    nk = seq // tk
    nvis = _div((qb + 1) * tq + tk - 1, tk)
    q_chunk = _div(qb * tq + lax.broadcasted_iota(I32, (1, tq), 1), CHUNK)
    w = wit_ref[...]
    row_idx = lax.broadcasted_iota(I32, (tk, 1), 0)

    def score_body(j, carry):
        off = pl.multiple_of(j * tk, tk)
        kb = ki_ref[pl.ds(off, tk), :]
        acc = jnp.zeros((tk, tq), F32)
        for hh in range(IDX_HEADS):
            acc = acc + w[hh:hh + 1, :] * jnp.maximum(_dot(kb, qit_ref[hh]), 0.0)
        vis = _div(off + row_idx, CHUNK) <= q_chunk
        key_scr[pl.ds(off, tk), :] = _sort_key(jnp.where(vis, acc, NEG_INF))
        return carry

    lax.fori_loop(0, nvis, score_body, 0)

    pair = 2 * tk if nk % 2 == 0 else tk
    npair = _div(nvis * tk + pair - 1, pair)

    @pl.when(npair * pair > nvis * tk)
    def _():
        key_scr[pl.ds(pl.multiple_of(nvis * tk, tk), tk), :] = jnp.full((tk, tq), KEY_NEG_INF, I32)

    def fold_rows(v):
        parts = [jnp.sum(v[g * (pair // 8):(g + 1) * (pair // 8)].reshape(pair // 64, 8, tq), axis=0) for g in range(8)]
        while len(parts) > 1:
            parts = [parts[i] + parts[i + 1] for i in range(0, len(parts), 2)]
        return parts[0]

    def count(hit):
        def body(j, c):
            off = pl.multiple_of(j * pair, pair)
            hits = hit(key_scr[pl.ds(off, pair), :], None)
            return c + fold_rows(hits)
        c = lax.fori_loop(0, npair, body, jnp.zeros((8, tq), I32))
        return jnp.sum(c, axis=0, keepdims=True)

    thr = _kth_largest_key(count, topk)
    rem = (topk - count(lambda key, idx: jnp.where(key > thr, 1, 0))).astype(F32)
    tri = tri_ref[...]

    def out_body(j, seen):
        off = pl.multiple_of(j * tk, tk)
        key = key_scr[pl.ds(off, tk), :]
        tie = jnp.where(key == thr, 1.0, 0.0)
        tie_b = tie.astype(BF16)
        half = tk // 2
        rank = seen + _dot(tri[:, :half], tie_b[:half]) + _dot(tri[:, half:], tie_b[half:])
        keep = jnp.where(key > thr, 1.0, jnp.where(rank <= rem, tie, 0.0))
        keep = jnp.where(key > KEY_NEG_INF, keep, 0.0)
        bias_ref[pl.ds(off, tk), :] = jnp.where(keep > 0.0, 0.0, NEG_INF).astype(BF16)
        return rank[tk - 1:tk, :]

    lax.fori_loop(0, nvis, out_body, jnp.zeros((1, tq), F32))

    def fill_body(j, carry):
        off = pl.multiple_of(j * tk, tk)
        bias_ref[pl.ds(off, tk), :] = jnp.full((tk, tq), NEG_INF, BF16)
        return carry

    lax.fori_loop(nvis, nk, fill_body, 0)


def _dsa_select(qit, wit, kib, batch, seq, topk):
    tq, tk = min(256, seq), min(512, seq)
    nq = seq // tq
    t = batch * seq
    return pl.pallas_call(
        functools.partial(_dsa_select_kernel, tq=tq, tk=tk, topk=topk),
        grid=(batch, nq),
        in_specs=[
            pl.BlockSpec((IDX_HEADS, IDX_DIM, tq), lambda b, q: (0, 0, b * nq + q)),
            pl.BlockSpec((IDX_HEADS, tq), lambda b, q: (0, b * nq + q)),
            pl.BlockSpec((seq, IDX_DIM), lambda b, q: (b, 0)),
            _whole((tk, tk)),
        ],
        out_specs=pl.BlockSpec((seq, tq), lambda b, q: (0, b * nq + q)),
        out_shape=jax.ShapeDtypeStruct((seq, t), BF16),
        scratch_shapes=[pltpu.VMEM((seq, tq), I32)],
        compiler_params=_params("parallel", "arbitrary"),
        name="dsa_select",
    )(qit, wit, kib, jnp.tril(jnp.ones((tk, tk), BF16)))


def _dsa_samp_kernel(q_ref, qi_ref, wi_ref, kc_ref, vc_ref, kic_ref, kn_ref, vn_ref, kin_ref, x_ref, wo_ref, o_ref,
                     *, n_q, topk):
    b = pl.program_id(0)
    hd, dh = DSA_HEADS, DSA_HEAD_DIM
    past = kc_ref.shape[1]
    n_keys = past + LANES
    own = _div(lax.broadcasted_iota(I32, (1, LANES), 1), n_q) == _mod(b, LANES // n_q)

    qi = qi_ref[...].reshape(IDX_HEADS * n_q, IDX_DIM)
    lg1 = jnp.maximum(_dot_t(qi, kic_ref[0].astype(BF16)), 0.0)
    lg2 = jnp.maximum(_dot_t(qi, kin_ref[...]), 0.0)
    w = wi_ref[...]
    sc1 = jnp.zeros((n_q, past), F32)
    sc2 = jnp.zeros((n_q, LANES), F32)
    for hh in range(IDX_HEADS):
        sc1 = sc1 + w[:, hh:hh + 1] * lg1[hh * n_q:(hh + 1) * n_q]
        sc2 = sc2 + w[:, hh:hh + 1] * lg2[hh * n_q:(hh + 1) * n_q]
    key = _sort_key(jnp.concatenate([sc1, jnp.where(own, sc2, NEG_INF)], axis=1))
    idx = lax.broadcasted_iota(I32, (1, n_keys), 1)

    count = lambda hit: jnp.sum(hit(key, idx), axis=1, keepdims=True)
    t0 = jnp.where(count(lambda k_, i_: jnp.where(k_ >= 0, 1, 0)) >= topk, jnp.int32(0), jnp.int32(INT_MIN))
    thr = _radix_search(lambda cand: count(lambda k_, i_: jnp.where(k_ >= cand, 1, 0)) >= topk, t0, 31)
    rem = topk - count(lambda k_, i_: jnp.where(k_ > thr, 1, 0))
    cut = _radix_search(
        lambda cand: count(lambda k_, i_: jnp.where(k_ == thr, jnp.where(i_ < cand, 1, 0), 0)) < rem,
        jnp.zeros_like(thr), int(n_keys - 1).bit_length())
    bias = jnp.where(_select(key, idx, thr, cut), 0.0, NEG_INF)
    bias = jnp.concatenate([bias] * hd, axis=0)

    lane_head = _div(lax.broadcasted_iota(I32, (1, hd * dh), 1), dh)
    qf = q_ref[...].astype(F32)
    qbd = jnp.concatenate([jnp.where(lane_head == hh, qf, 0.0) for hh in range(hd)], axis=0).astype(BF16)
    kc = pltpu.einshape("phd->p(hd)", kc_ref[0]).astype(BF16)
    vc = pltpu.einshape("phd->p(hd)", vc_ref[0]).astype(BF16)
    s1 = _dot_t(qbd, kc) + bias[:, :past]
    s2 = _dot_t(qbd, kn_ref[...]) + bias[:, past:]
    m = jnp.maximum(jnp.max(s1, axis=1, keepdims=True), jnp.max(s2, axis=1, keepdims=True))
    p1 = jnp.exp(s1 - m)
    p2 = jnp.exp(s2 - m)
    l = jnp.sum(p1, axis=1, keepdims=True) + jnp.sum(p2, axis=1, keepdims=True)
    o_all = (_dot(p1.astype(BF16), vc) + _dot(p2.astype(BF16), vn_ref[...])) / l
    out = jnp.zeros((n_q, hd * dh), F32)
    for hh in range(hd):
        out = out + jnp.where(lane_head == hh, o_all[hh * n_q:(hh + 1) * n_q], 0.0)
    o_ref[...] = x_ref[...] + _dot(out.astype(BF16), wo_ref[...])


def _dsa_attn_sample(x, qb, qib, wi, kb, vb, kib, cache_k, cache_v, cache_ki, w_o, n_b, n_q, topk):
    t, d = x.shape
    hdm = DSA_HEADS * DSA_HEAD_DIM
    past = cache_k.shape[1]
    per = LANES // n_q
    return pl.pallas_call(
        functools.partial(_dsa_samp_kernel, n_q=n_q, topk=topk),
        grid=(n_b,),
        in_specs=[
            pl.BlockSpec((n_q, hdm), lambda b: (b, 0)),
            pl.BlockSpec((IDX_HEADS, n_q, IDX_DIM), lambda b: (0, b, 0)),
            pl.BlockSpec((n_q, IDX_HEADS), lambda b: (b, 0)),
            pl.BlockSpec((1, past, DSA_HEADS, DSA_HEAD_DIM), lambda b: (b, 0, 0, 0)),
            pl.BlockSpec((1, past, DSA_HEADS, DSA_HEAD_DIM), lambda b: (b, 0, 0, 0)),
            pl.BlockSpec((1, past, IDX_DIM), lambda b: (b, 0, 0)),
            pl.BlockSpec((LANES, hdm), lambda b: (b // per, 0)),
            pl.BlockSpec((LANES, hdm), lambda b: (b // per, 0)),
            pl.BlockSpec((LANES, IDX_DIM), lambda b: (b // per, 0)),
            pl.BlockSpec((n_q, d), lambda b: (b, 0)),
            _whole(w_o.shape),
        ],
        out_specs=pl.BlockSpec((n_q, d), lambda b: (b, 0)),
        out_shape=jax.ShapeDtypeStruct((t, d), F32),
        compiler_params=_params("parallel"),
        name="dsa_attn_sample",
    )(qb, qib, wi, cache_k, cache_v, cache_ki, kb, vb, kib, x, w_o)


def kernel(x_prompt, x_sample, cache_mla_ckv, cache_mla_krope, cache_dsa_k, cache_dsa_v, cache_dsa_kidx, norm_mix, norm_ffn, norm_final, mla_w_dq, mla_g_q, mla_w_uq, mla_w_dkv, mla_g_kv, mla_w_ukv, mla_w_o, cmlp_w_in, cmlp_ln_g, cmlp_ln_b, cmlp_w_s, cmlp_b_s, cmlp_w_out, dsa_w_qkv, dsa_w_o, dsa_w_qidx, dsa_w_kidx, dsa_g_kidx, dsa_w_widx, ffn_w_in, ffn_w_out):
    batch, seq, d = x_prompt.shape
    n_b, n_q, _ = x_sample.shape
    past = cache_mla_ckv.shape[2]
    depth = norm_mix.shape[0]
    xp = x_prompt.reshape(batch * seq, d)
    xs = x_sample.reshape(n_b * n_q, d)
    tab_p = _rope_tables(jnp.arange(seq))
    tab_s = tuple(jnp.tile(a, (n_b, 1)) for a in _rope_tables(past + jnp.arange(n_q)))
    cast = lambda a: a.astype(BF16)
    outs = {k: [] for k in ("ckv_p", "kr_p", "ckv_s", "kr_s", "cv_s", "dk_p", "dv_p", "di_p", "dk_s", "dv_s", "di_s")}
    for i in range(depth):
        kind, j = i % 3, i // 3
        if kind == 0:
            wts = _mla_weights(mla_w_dq[j], mla_w_uq[j], mla_w_dkv[j], mla_w_ukv[j], mla_w_o[j])
            qt, kh, vt, ckv, kr, kn2 = _mla_proj_t(xp, norm_mix[i], wts, mla_g_q[j], mla_g_kv[j], tab_p, seq)
            xp = _attn_t(xp, qt, kh, vt, kn2, None, wts["wot"], batch, seq, 256, "mla_attn_prompt")
            outs["ckv_p"].append(ckv.reshape(batch, seq, -1)); outs["kr_p"].append(kr.reshape(batch, seq, -1))
            ql, qr, ckv, kr, ckvb, krb = _mla_proj(xs, norm_mix[i], wts, mla_g_q[j], mla_g_kv[j], tab_s)
            xs = _mla_attn_sample(xs, ql, qr, ckvb, krb, cache_mla_ckv[j], cache_mla_krope[j], wts, n_b, n_q)
            outs["ckv_s"].append(ckv.reshape(n_b, n_q, -1)); outs["kr_s"].append(kr.reshape(n_b, n_q, -1))
        elif kind == 1:
            w_in, w_out = cast(cmlp_w_in[j]), cast(cmlp_w_out[j])
            xp, _ = _cmlp(xp, norm_mix[i], w_in, cmlp_ln_g[j], cmlp_ln_b[j], cmlp_w_s[j], cmlp_b_s[j], w_out,
                          min(seq, CMLP_CHUNK), False)
            xs, v_s = _cmlp(xs, norm_mix[i], w_in, cmlp_ln_g[j], cmlp_ln_b[j], cmlp_w_s[j], cmlp_b_s[j], w_out,
                            min(n_q, CMLP_CHUNK), True)
            outs["cv_s"].append(v_s.reshape(n_b, n_q, -1))
        else:
            hdm = DSA_HEADS * DSA_HEAD_DIM
            w_qkv, w_qidx, w_o = cast(dsa_w_qkv[j]), cast(dsa_w_qidx[j]), cast(dsa_w_o[j])
            w_kidx, w_widx = cast(dsa_w_kidx[j]), cast(dsa_w_widx[j])
            zpad = lambda n: jnp.zeros((d, n), BF16)
            wt = dict(wkv=w_qkv[:, hdm:], wqt=w_qkv[:, :hdm].T, wvt=w_qkv[:, 2 * hdm:].T, wqit=w_qidx.T,
                      wkw=jnp.concatenate([w_kidx, zpad(LANES - IDX_DIM)], axis=1),
                      wwt=jnp.concatenate([w_widx, zpad(BF16_ROWS - IDX_HEADS)], axis=1).T)
            hshape = (DSA_HEADS, DSA_HEAD_DIM)
            k, v, ki, qt, kh, vt, qit, kib, wit, kn2 = _dsa_proj_t(xp, norm_mix[i], wt, dsa_g_kidx[j])
            bias = _dsa_select(qit, wit, kib, batch, seq, min(TOPK_MAX, seq // 4))
            xp = _attn_t(xp, qt, kh, vt, kn2, bias, w_o.T, batch, seq, 512, "dsa_attn_prompt")
            outs["dk_p"].append(k.reshape((batch, seq) + hshape)); outs["dv_p"].append(v.reshape((batch, seq) + hshape))
            outs["di_p"].append(ki.reshape(batch, seq, -1))
            w_kw = jnp.concatenate([w_kidx, w_widx, zpad(LANES - IDX_DIM - IDX_HEADS)], axis=1)
            k, v, ki, qb, kb, vb, qib, kib, wi = _dsa_proj(xs, norm_mix[i], w_qkv, w_qidx, w_kw, dsa_g_kidx[j])
            xs = _dsa_attn_sample(xs, qb, qib, wi, kb, vb, kib, cache_dsa_k[j], cache_dsa_v[j], cache_dsa_kidx[j], w_o,
                                  n_b, n_q, min(TOPK_MAX, (past + n_q) // 4))
            outs["dk_s"].append(k.reshape((n_b, n_q) + hshape)); outs["dv_s"].append(v.reshape((n_b, n_q) + hshape))
            outs["di_s"].append(ki.reshape(n_b, n_q, -1))
        w_in, w_out = cast(ffn_w_in[i]), cast(ffn_w_out[i])
        final = i == depth - 1
        xp = _ffn(xp, norm_ffn[i], w_in, w_out, norm_final, final)
        xs = _ffn(xs, norm_ffn[i], w_in, w_out, norm_final, final)
    st = lambda name: jnp.stack(outs[name])
    return (xp.reshape(batch, seq, d), xs.reshape(n_b, n_q, d),
            st("ckv_p"), st("kr_p"), st("ckv_s"), st("kr_s"), st("cv_s"),
            st("dk_p"), st("dv_p"), st("di_p"), st("dk_s"), st("dv_s"), st("di_s"))
```

```python
import functools

import numpy as np
import jax
import jax.numpy as jnp
from jax import lax
from jax.experimental import pallas as pl
from jax.experimental.pallas import tpu as pltpu

F32, BF16, I32 = jnp.float32, jnp.bfloat16, jnp.int32

CHUNK = 64
EPS = 1e-6
MLA_HEADS, MLA_Q_LORA, MLA_KV_LORA, MLA_NOPE, MLA_ROPE, MLA_V = 16, 512, 256, 64, 32, 64
ROPE_BASE = 10000.0
MLA_SCALE = (MLA_NOPE + MLA_ROPE) ** -0.5
CMLP_CHUNK, CMLP_WIDTH, CMLP_GROUPS = 128, 2048, 8
DSA_HEADS, DSA_HEAD_DIM = 16, 64
DSA_SCALE = DSA_HEAD_DIM ** -0.5
IDX_HEADS, IDX_DIM = 8, 64
IDX_SCALE = IDX_DIM ** -0.5
TOPK_MAX = 256

LANES = 128
MXU_TILE = 256
BF16_ROWS = 16
VMEM_LIMIT = 52 * 1024 * 1024
NEG_INF = float("-inf")
INT_MIN = -2 ** 31
KEY_NEG_INF = -2139095041
LOG2E = float(np.log2(np.e))
KEY_NORM_SLACK = 1.01
LAG_MARGIN = 100.0
HEAD_V = 64
V_ROWS = HEAD_V + BF16_ROWS


def _dot(a, b):
    return jnp.dot(a, b, preferred_element_type=F32)


def _dot_t(a, b):
    return lax.dot_general(a, b, (((1,), (1,)), ((), ())), preferred_element_type=F32)


def _rms(x, g):
    return x * lax.rsqrt(jnp.mean(x * x, axis=-1, keepdims=True) + EPS) * g


def _log2(n):
    assert n > 0 and n & (n - 1) == 0, n
    return n.bit_length() - 1


def _div(x, n):
    return lax.shift_right_logical(x, jnp.int32(_log2(n)))


def _mod(x, n):
    assert n & (n - 1) == 0, n
    return x & (n - 1)


def _params(*sem):
    return pltpu.CompilerParams(dimension_semantics=sem, vmem_limit_bytes=VMEM_LIMIT)


def _whole(shape):
    nd = len(shape)
    return pl.BlockSpec(shape, lambda *_: (0,) * nd)


def _ffn_kernel(x_ref, g_ref, wg_ref, wu_ref, wo_ref, gf_ref, o_ref, act_scr, *, final):
    x = x_ref[...]
    h = _rms(x, g_ref[...]).astype(BF16)
    f = wo_ref.shape[0]
    chunk = MXU_TILE if f % MXU_TILE == 0 else f
    nc = f // chunk
    nxt = (_dot(h, wg_ref[:, 0:chunk]), _dot(h, wu_ref[:, 0:chunk]))
    for c in range(nc):
        gate, up = nxt
        if c + 1 < nc:
            lo = (c + 1) * chunk
            nxt = (_dot(h, wg_ref[:, lo:lo + chunk]), _dot(h, wu_ref[:, lo:lo + chunk]))
        act_scr[:, c * chunk:(c + 1) * chunk] = (jax.nn.silu(gate) * up).astype(BF16)
    y = x + _dot(act_scr[...], wo_ref[...])
    if final:
        y = _rms(y, gf_ref[...])
    o_ref[...] = y


def _ffn(x, g, w_in, w_out, g_final, final):
    t, d = x.shape
    f = w_out.shape[0]
    tm = min(512, t)
    once = pl.Buffered(1)
    return pl.pallas_call(
        functools.partial(_ffn_kernel, final=final),
        grid=(t // tm,),
        in_specs=[
            pl.BlockSpec((tm, d), lambda i: (i, 0)),
            _whole((1, d)),
            pl.BlockSpec((d, f), lambda i: (0, 0), pipeline_mode=once),
            pl.BlockSpec((d, f), lambda i: (0, 1), pipeline_mode=once),
            pl.BlockSpec((f, d), lambda i: (0, 0), pipeline_mode=once),
            _whole((1, d)),
        ],
        out_specs=pl.BlockSpec((tm, d), lambda i: (i, 0)),
        out_shape=jax.ShapeDtypeStruct((t, d), F32),
        scratch_shapes=[pltpu.VMEM((tm, f), BF16)],
        compiler_params=_params("parallel"),
        name="ffn",
    )(x, g.reshape(1, d), w_in, w_in, w_out, g_final.reshape(1, d))


def _gelu(x):
    return 0.5 * x * (1.0 + lax.erf(x * np.float32(np.sqrt(0.5))))


def _layernorm(x, g, b):
    mu = jnp.mean(x, axis=-1, keepdims=True)
    xc = x - mu
    return xc * lax.rsqrt(jnp.mean(xc * xc, axis=-1, keepdims=True) + EPS) * g + b


def _cmlp_kernel(x_ref, g_ref, win_ref, lng_ref, lnb_ref, ws_ref, bs_ref, wout_ref, *rest, n_rows, write_v):
    if write_v:
        o_ref, v_ref, vb_scr, gated_scr = rest
    else:
        o_ref, vb_scr, gated_scr = rest
    tm = x_ref.shape[0]
    w = CMLP_WIDTH
    gw = w // CMLP_GROUPS
    c = CMLP_CHUNK
    x = x_ref[...]
    h = _rms(x, g_ref[...]).astype(BF16)
    v = _layernorm(_gelu(_dot(h, win_ref[:, w:])), lng_ref[...], lnb_ref[...])
    if write_v:
        v_ref[...] = v
    vb_scr[...] = v.astype(BF16)
    r_i = lax.broadcasted_iota(I32, (c, c), 0)
    c_i = lax.broadcasted_iota(I32, (c, c), 1)
    keep = jnp.where(c_i >= r_i - _mod(r_i, n_rows), jnp.where(c_i <= r_i, 1, 0), 0) > 0
    u_next = _dot(h, win_ref[:, 0:gw])
    for g in range(CMLP_GROUPS):
        lo, hi = g * gw, (g + 1) * gw
        u_raw = u_next
        if g + 1 < CMLP_GROUPS:
            u_next = _dot(h, win_ref[:, hi:hi + gw])
        wg = jnp.where(keep, ws_ref[g], 0.0).astype(BF16)
        bias = bs_ref[:, g:g + 1]
        mixed = jnp.concatenate(
            [_dot(wg, vb_scr[k * c:(k + 1) * c, lo:hi]) + bias for k in range(tm // c)], axis=0)
        gated_scr[:, lo:hi] = (_gelu(u_raw) * mixed).astype(BF16)
    o_ref[...] = x + _dot(gated_scr[...], wout_ref[...])


def _cmlp(x, g, w_in, ln_g, ln_b, w_s, b_s, w_out, n_rows, write_v):
    t, d = x.shape
    w = CMLP_WIDTH
    c = CMLP_CHUNK
    tm = min(512, t)
    rep = c // n_rows
    ws_t = jnp.tile(w_s[:, :n_rows, :n_rows], (1, rep, rep))
    bs_t = jnp.tile(b_s[:, :n_rows].T, (rep, 1))
    out_shape = [jax.ShapeDtypeStruct((t, d), F32)]
    out_specs = [pl.BlockSpec((tm, d), lambda i: (i, 0))]
    if write_v:
        out_shape.append(jax.ShapeDtypeStruct((t, w), F32))
        out_specs.append(pl.BlockSpec((tm, w), lambda i: (i, 0)))
    res = pl.pallas_call(
        functools.partial(_cmlp_kernel, n_rows=n_rows, write_v=write_v),
        grid=(t // tm,),
        in_specs=[
            pl.BlockSpec((tm, d), lambda i: (i, 0)),
            _whole((1, d)), _whole((d, 2 * w)), _whole((1, w)), _whole((1, w)),
            _whole((CMLP_GROUPS, c, c)), _whole((c, CMLP_GROUPS)), _whole((w, d)),
        ],
        out_specs=out_specs,
        out_shape=out_shape,
        scratch_shapes=[pltpu.VMEM((tm, w), BF16), pltpu.VMEM((tm, w), BF16)],
        compiler_params=_params("parallel"),
        name="cmlp",
    )(x, g.reshape(1, d), w_in, ln_g.reshape(1, w), ln_b.reshape(1, w), ws_t, bs_t, w_out)
    return res if write_v else (res[0], None)


def _attn_t_kernel(qi_ref, kj_ref, last_ref, nsub_ref, qt_ref, k_ref, vt_ref, *rest, tq, tk, sub, use_bias):
    if use_bias:
        bias_ref, kn_ref, x_ref, wot_ref, o_ref, m_scr, pend_scr, qn_scr, acc_scr, cat_scr = rest
    else:
        kn_ref, x_ref, wot_ref, o_ref, m_scr, pend_scr, qn_scr, acc_scr, cat_scr = rest
    p = pl.program_id(1)
    qi, kj = qi_ref[p], kj_ref[p]
    hd = qt_ref.shape[0]

    @pl.when(kj == 0)
    def _():
        m_scr[...] = jnp.full_like(m_scr, NEG_INF)
        pend_scr[...] = jnp.ones_like(pend_scr)
        acc_scr[...] = jnp.zeros_like(acc_scr)
        for hh in range(hd):
            q = qt_ref[hh].astype(F32)
            qn_scr[hh] = jnp.sqrt(jnp.sum(q * q, axis=0, keepdims=True)) * kn_ref[0, hh:hh + 1, 0:1]

    def step(off, mask, lagged):
        keys = lambda hh: k_ref[hh, pl.ds(off, sub), :]
        vals = lambda hh: vt_ref[hh, :, pl.ds(off, sub)]
        s_next = _dot(keys(0), qt_ref[0])
        pend = None
        for hh in range(hd):
            s = s_next
            if hh + 1 < hd:
                s_next = _dot(keys(hh + 1), qt_ref[hh + 1])
            if mask is not None:
                s = s + mask
            m_prev = m_scr[hh]
            m_new = jnp.maximum(m_prev, jnp.max(s, axis=0, keepdims=True))
            if lagged:
                scale = pend_scr[hh]
                pe = jnp.exp2(s - m_prev).astype(BF16)
                pend_scr[hh] = jnp.exp2(m_prev - m_new)
            else:
                m_safe = jnp.where(m_new == NEG_INF, 0.0, m_new)
                scale = pend_scr[hh] * jnp.exp2(m_prev - m_safe)
                pe = jnp.exp2(s - m_safe).astype(BF16)
                pend_scr[hh] = jnp.ones_like(m_prev)
            m_scr[hh] = m_new
            if pend is not None:
                ph, pa, pp = pend
                acc_scr[ph] = pa * acc_scr[ph] + _dot(vals(ph), pp)
            pend = (hh, scale, pe)
        ph, pa, pp = pend
        acc_scr[ph] = pa * acc_scr[ph] + _dot(vals(ph), pp)

    def sub_body(j, carry):
        off = pl.multiple_of(j * sub, sub)
        excess = qn_scr[0] - m_scr[0]
        for hh in range(1, hd):
            excess = jnp.maximum(excess, qn_scr[hh] - m_scr[hh])
        lag_ok = jnp.max(excess) <= LAG_MARGIN
        if use_bias:
            bias = bias_ref[pl.ds(off, sub), :].astype(F32)

            @pl.when(lag_ok)
            def _():
                step(off, bias, True)

            @pl.when(jnp.logical_not(lag_ok))
            def _():
                step(off, bias, False)
        else:
            base = kj * tk + off
            needs_mask = _div(base + sub - 1, CHUNK) > _div(qi * tq, CHUNK)

            def chunk_mask():
                k_chunk = _div(base + lax.broadcasted_iota(I32, (sub, 1), 0), CHUNK)
                q_chunk = _div(qi * tq + lax.broadcasted_iota(I32, (1, tq), 1), CHUNK)
                return jnp.where(k_chunk <= q_chunk, 0.0, NEG_INF)

            @pl.when(jnp.logical_not(lag_ok))
            def _():
                step(off, chunk_mask(), False)

            @pl.when(jnp.logical_and(lag_ok, needs_mask))
            def _():
                step(off, chunk_mask(), True)

            @pl.when(jnp.logical_and(lag_ok, jnp.logical_not(needs_mask)))
            def _():
                step(off, None, True)
        return carry

    lax.fori_loop(0, nsub_ref[p], sub_body, 0)

    @pl.when(last_ref[p] == 1)
    def _():
        for hh in range(hd):
            a = acc_scr[hh]
            cat_scr[hh * HEAD_V:(hh + 1) * HEAD_V, :] = (a[:HEAD_V] / a[HEAD_V:HEAD_V + 1]).astype(BF16)
        o_ref[...] = x_ref[...] + _dot(wot_ref[...], cat_scr[...]).T


def _causal_pairs(nq, tq, tk, sub):
    qi, kj, last, nsub = [], [], [], []
    per = tk // sub
    for q in range(nq):
        vis_sub = -(-((q + 1) * tq) // sub)
        nvis = -(-vis_sub // per)
        for k in range(nvis):
            qi.append(q); kj.append(k); last.append(int(k == nvis - 1)); nsub.append(min(per, vis_sub - k * per))
    as_arr = lambda v: jnp.asarray(np.array(v, np.int32))
    return as_arr(qi), as_arr(kj), as_arr(last), as_arr(nsub)


def _attn_t(x, qt, kh, vt, kn2, bias, wot, batch, seq, sub, name):
    t, d = x.shape
    hd, dk, _ = qt.shape
    tq, tk, sub = min(512, seq), min(1024, seq), min(sub, seq)
    nq, nk = seq // tq, seq // tk
    qi, kj, last, nsub = _causal_pairs(nq, tq, tk, sub)
    use_bias = bias is not None
    in_specs = [
        pl.BlockSpec((hd, dk, tq), lambda b, p, qi, kj, la, ns: (0, 0, b * nq + qi[p])),
        pl.BlockSpec((hd, tk, dk), lambda b, p, qi, kj, la, ns: (0, b * nk + kj[p], 0)),
        pl.BlockSpec((hd, V_ROWS, tk), lambda b, p, qi, kj, la, ns: (0, 0, b * nk + kj[p])),
    ]
    args = [qt, kh, vt]
    if use_bias:
        in_specs.append(pl.BlockSpec((tk, tq), lambda b, p, qi, kj, la, ns: (kj[p], b * nq + qi[p])))
        args.append(bias)
    kn = jnp.sqrt(jnp.max(kn2[:, :, 0].reshape(batch, -1, hd), axis=1)) * KEY_NORM_SLACK
    in_specs.append(pl.BlockSpec((1, hd, LANES), lambda b, p, qi, kj, la, ns: (b, 0, 0)))
    args.append(jnp.broadcast_to(kn[:, :, None], (batch, hd, LANES)))
    in_specs += [
        pl.BlockSpec((tq, d), lambda b, p, qi, kj, la, ns: (b * nq + qi[p], 0)),
        pl.BlockSpec(wot.shape, lambda b, p, qi, kj, la, ns: (0, 0)),
    ]
    args += [x, wot]
    grid_spec = pltpu.PrefetchScalarGridSpec(
        num_scalar_prefetch=4,
        grid=(batch, int(qi.shape[0])),
        in_specs=in_specs,
        out_specs=pl.BlockSpec((tq, d), lambda b, p, qi, kj, la, ns: (b * nq + qi[p], 0)),
        scratch_shapes=[
            pltpu.VMEM((hd, 1, tq), F32), pltpu.VMEM((hd, 1, tq), F32), pltpu.VMEM((hd, 1, tq), F32),
            pltpu.VMEM((hd, V_ROWS, tq), F32), pltpu.VMEM((hd * HEAD_V, tq), BF16),
        ],
    )
    return pl.pallas_call(
        functools.partial(_attn_t_kernel, tq=tq, tk=tk, sub=sub, use_bias=use_bias),
        grid_spec=grid_spec,
        out_shape=jax.ShapeDtypeStruct((t, d), F32),
        compiler_params=_params("parallel", "arbitrary"),
        name=name,
    )(qi, kj, last, nsub, *args)


def _rope_tables(pos):
    half = MLA_ROPE // 2
    inv = ROPE_BASE ** (-jnp.arange(half, dtype=F32) / half)
    ang = pos.astype(F32)[:, None] * inv[None, :]
    cos, sin = jnp.cos(ang), jnp.sin(ang)
    cos_k = jnp.concatenate([cos, cos], axis=1)
    sin_k = jnp.concatenate([-sin, sin], axis=1)
    return jnp.tile(cos_k, (1, MLA_HEADS)), jnp.tile(sin_k, (1, MLA_HEADS)), cos_k, sin_k


def _swap_halves(w, group):
    shp = w.shape
    wr = w.reshape(shp[:-1] + (shp[-1] // group, 2, group // 2))
    return wr[..., ::-1, :].reshape(shp)


def _mla_weights(w_dq, w_uq, w_dkv, w_ukv, w_o):
    hd = MLA_HEADS
    cast = lambda a: a.astype(BF16)
    w_uq, w_dkv, w_ukv = cast(w_uq), cast(w_dkv), cast(w_ukv)
    wqn = w_uq[:, :, :MLA_NOPE].reshape(MLA_Q_LORA, hd * MLA_NOPE)
    wqr = w_uq[:, :, MLA_NOPE:].reshape(MLA_Q_LORA, hd * MLA_ROPE)
    wqs = _swap_halves(wqr, MLA_ROPE)
    wkc, wkr = w_dkv[:, :MLA_KV_LORA], w_dkv[:, MLA_KV_LORA:]
    wks = _swap_halves(wkr, MLA_ROPE)
    wuk = jnp.transpose(w_ukv[:, :, :MLA_NOPE], (1, 2, 0))
    zeros = jnp.zeros_like(wuk)
    even = jnp.concatenate([wuk, zeros], axis=1)
    odd = jnp.concatenate([zeros, wuk], axis=1)
    wuk2 = jnp.where((jnp.arange(hd) % 2 == 0)[:, None, None], even, odd)
    wuv = jnp.transpose(w_ukv[:, :, MLA_NOPE:], (1, 0, 2))
    d = w_dkv.shape[0]
    pad_r = lambda a: jnp.concatenate([a, jnp.zeros((d, LANES - MLA_ROPE), BF16)], axis=1)
    wukp = jnp.concatenate([jnp.zeros((MLA_KV_LORA, hd, MLA_ROPE), BF16), w_ukv[:, :, :MLA_NOPE],
                            jnp.zeros((MLA_KV_LORA, hd, LANES - MLA_ROPE - MLA_NOPE), BF16)], axis=2)
    return dict(wdq=cast(w_dq), wqn=wqn, wqr=wqr, wqs=wqs, wkc=wkc, wkr=wkr, wks=wks, wuk=wuk2, wuv=wuv, wo=cast(w_o),
                wqnt=wqn.T, wqrt=wqr.T, wqst=wqs.T, wkrp=pad_r(wkr), wksp=pad_r(wks),
                wukp=wukp.reshape(MLA_KV_LORA, hd * LANES), wuvt=wuv.transpose(0, 2, 1).reshape(hd * MLA_V, MLA_KV_LORA),
                wot=cast(w_o).T)


def _mla_proj_kernel(x_ref, gm_ref, wdq_ref, gq_ref, wqn_ref, wqr_ref, wqs_ref, wkc_ref, wkr_ref, wks_ref,
                     gkv_ref, wuk_ref, cq_ref, sq_ref, ck_ref, sk_ref,
                     ql_ref, qr_ref, ckv_ref, kr_ref, ckvb_ref, krb_ref):
    h = _rms(x_ref[...], gm_ref[...]).astype(BF16)
    cq = _rms(_dot(h, wdq_ref[...]), gq_ref[...]).astype(BF16)
    qn = _dot(cq, wqn_ref[...])
    qr = ((_dot(cq, wqr_ref[...]) * cq_ref[...] + _dot(cq, wqs_ref[...]) * sq_ref[...]) * MLA_SCALE).astype(BF16)
    for hh in range(MLA_HEADS):
        pair = qn[:, (hh // 2) * LANES:(hh // 2 + 1) * LANES].astype(BF16)
        ql_ref[hh] = (_dot(pair, wuk_ref[hh]) * MLA_SCALE).astype(BF16)
        qr_ref[hh] = qr[:, hh * MLA_ROPE:(hh + 1) * MLA_ROPE]
    ckv = _rms(_dot(h, wkc_ref[...]), gkv_ref[...])
    ckv_ref[...] = ckv
    ckvb_ref[...] = ckv.astype(BF16)
    kr = _dot(h, wkr_ref[...]) * ck_ref[...] + _dot(h, wks_ref[...]) * sk_ref[...]
    kr_ref[...] = kr
    krb_ref[...] = kr.astype(BF16)


def _mla_proj(x, g_mix, wts, g_q, g_kv, tables):
    t, d = x.shape
    tm = min(256, t)
    cos_q, sin_q, cos_k, sin_k = tables
    nrep = cos_q.shape[0] // tm
    tab = lambda wdt: pl.BlockSpec((tm, wdt), lambda i: (i % nrep, 0))
    hd = MLA_HEADS
    row = lambda wdt: pl.BlockSpec((tm, wdt), lambda i: (i, 0))
    hm = lambda wdt: pl.BlockSpec((hd, tm, wdt), lambda i: (0, i, 0))
    return pl.pallas_call(
        _mla_proj_kernel,
        grid=(t // tm,),
        in_specs=[
            row(d), _whole((1, d)), _whole(wts["wdq"].shape), _whole((1, MLA_Q_LORA)),
            _whole(wts["wqn"].shape), _whole(wts["wqr"].shape), _whole(wts["wqs"].shape),
            _whole(wts["wkc"].shape), _whole(wts["wkr"].shape), _whole(wts["wks"].shape),
            _whole((1, MLA_KV_LORA)), _whole(wts["wuk"].shape),
            tab(hd * MLA_ROPE), tab(hd * MLA_ROPE), tab(MLA_ROPE), tab(MLA_ROPE),
        ],
        out_specs=[hm(MLA_KV_LORA), hm(MLA_ROPE), row(MLA_KV_LORA), row(MLA_ROPE), row(MLA_KV_LORA), row(MLA_ROPE)],
        out_shape=[
            jax.ShapeDtypeStruct((hd, t, MLA_KV_LORA), BF16), jax.ShapeDtypeStruct((hd, t, MLA_ROPE), BF16),
            jax.ShapeDtypeStruct((t, MLA_KV_LORA), F32), jax.ShapeDtypeStruct((t, MLA_ROPE), F32),
            jax.ShapeDtypeStruct((t, MLA_KV_LORA), BF16), jax.ShapeDtypeStruct((t, MLA_ROPE), BF16),
        ],
        compiler_params=_params("parallel"),
        name="mla_proj",
    )(x, g_mix.reshape(1, d), wts["wdq"], g_q.reshape(1, -1), wts["wqn"], wts["wqr"], wts["wqs"],
      wts["wkc"], wts["wkr"], wts["wks"], g_kv.reshape(1, -1), wts["wuk"], cos_q, sin_q, cos_k, sin_k)


def _mla_proj_t_kernel(x_ref, gm_ref, wdq_ref, gq_ref, wqnt_ref, wqrt_ref, wqst_ref, wkc_ref, wkrp_ref, wksp_ref,
                       gkv_ref, wukp_ref, wuvt_ref, cqt_ref, sqt_ref, ckp_ref, skp_ref,
                       qt_ref, kh_ref, vt_ref, ckv_ref, kr_ref, kn2_ref):
    tm = x_ref.shape[0]
    hd = MLA_HEADS
    qscale = MLA_SCALE * LOG2E
    h = _rms(x_ref[...], gm_ref[...]).astype(BF16)
    cq_raw = _dot(h, wdq_ref[...])
    ckv_raw = _dot(h, wkc_ref[...])
    kr_a = _dot(h, wkrp_ref[...])
    kr_b = _dot(h, wksp_ref[...])
    cq = _rms(cq_raw, gq_ref[...]).astype(BF16)
    qnt_raw = _dot_t(wqnt_ref[...], cq)
    qrt_a = _dot_t(wqrt_ref[...], cq)
    qrt_b = _dot_t(wqst_ref[...], cq)
    ckv = _rms(ckv_raw, gkv_ref[...])
    ckv_ref[...] = ckv
    cb = ckv.astype(BF16)
    kn = _dot(cb, wukp_ref[...])
    vt = _dot_t(wuvt_ref[...], cb).astype(BF16)
    qnt = (qnt_raw * qscale).astype(BF16)
    qrt = ((qrt_a * cqt_ref[...] + qrt_b * sqt_ref[...]) * qscale).astype(BF16)
    zeros = jnp.zeros((LANES - MLA_ROPE - MLA_NOPE, tm), BF16)
    krp = kr_a * ckp_ref[...] + kr_b * skp_ref[...]
    kr_ref[...] = krp[:, :MLA_ROPE]
    ones = jnp.ones((BF16_ROWS, tm), BF16)
    for hh in range(hd):
        qt_ref[hh, 0:MLA_ROPE, :] = qrt[hh * MLA_ROPE:(hh + 1) * MLA_ROPE]
        qt_ref[hh, MLA_ROPE:MLA_ROPE + MLA_NOPE, :] = qnt[hh * MLA_NOPE:(hh + 1) * MLA_NOPE]
        qt_ref[hh, MLA_ROPE + MLA_NOPE:, :] = zeros
        key = kn[:, hh * LANES:(hh + 1) * LANES] + krp
        kh_ref[hh] = key.astype(BF16)
        kn2_ref[0, hh:hh + 1, :] = jnp.broadcast_to(
            jnp.max(jnp.sum(key * key, axis=1, keepdims=True), axis=0, keepdims=True), (1, LANES))
        vt_ref[hh, 0:MLA_V, :] = vt[hh * MLA_V:(hh + 1) * MLA_V]
        vt_ref[hh, MLA_V:, :] = ones


def _mla_proj_t(x, g_mix, wts, g_q, g_kv, tables, seq):
    t, d = x.shape
    tm = min(256, seq)
    hd = MLA_HEADS
    cos_q, sin_q, cos_k, sin_k = tables
    pad = lambda a: jnp.concatenate([a, jnp.zeros((seq, LANES - MLA_ROPE), F32)], axis=1)
    nrep = seq // tm
    row = lambda wdt: pl.BlockSpec((tm, wdt), lambda i: (i, 0))
    names = ("wdq", "wqnt", "wqrt", "wqst", "wkc", "wkrp", "wksp", "wukp", "wuvt")
    w = {n: wts[n] for n in names}
    return pl.pallas_call(
        _mla_proj_t_kernel,
        grid=(t // tm,),
        in_specs=[
            row(d), _whole((1, d)), _whole(w["wdq"].shape), _whole((1, MLA_Q_LORA)),
            _whole(w["wqnt"].shape), _whole(w["wqrt"].shape), _whole(w["wqst"].shape),
            _whole(w["wkc"].shape), _whole(w["wkrp"].shape), _whole(w["wksp"].shape),
            _whole((1, MLA_KV_LORA)), _whole(w["wukp"].shape), _whole(w["wuvt"].shape),
            pl.BlockSpec((hd * MLA_ROPE, tm), lambda i: (0, i % nrep)),
            pl.BlockSpec((hd * MLA_ROPE, tm), lambda i: (0, i % nrep)),
            pl.BlockSpec((tm, LANES), lambda i: (i % nrep, 0)),
            pl.BlockSpec((tm, LANES), lambda i: (i % nrep, 0)),
        ],
        out_specs=[
            pl.BlockSpec((hd, LANES, tm), lambda i: (0, 0, i)),
            pl.BlockSpec((hd, tm, LANES), lambda i: (0, i, 0)),
            pl.BlockSpec((hd, V_ROWS, tm), lambda i: (0, 0, i)),
            row(MLA_KV_LORA), row(MLA_ROPE),
            pl.BlockSpec((1, hd, LANES), lambda i: (i, 0, 0)),
        ],
        out_shape=[
            jax.ShapeDtypeStruct((hd, LANES, t), BF16), jax.ShapeDtypeStruct((hd, t, LANES), BF16),
            jax.ShapeDtypeStruct((hd, V_ROWS, t), BF16),
            jax.ShapeDtypeStruct((t, MLA_KV_LORA), F32), jax.ShapeDtypeStruct((t, MLA_ROPE), F32),
            jax.ShapeDtypeStruct((t // tm, hd, LANES), F32),
        ],
        compiler_params=_params("parallel"),
        name="mla_proj_t",
    )(x, g_mix.reshape(1, d), w["wdq"], g_q.reshape(1, -1), w["wqnt"], w["wqrt"], w["wqst"],
      w["wkc"], w["wkrp"], w["wksp"], g_kv.reshape(1, -1), w["wukp"], w["wuvt"],
      cos_q.T, sin_q.T, pad(cos_k), pad(sin_k))


def _mla_epilogue(o_lat, x_ref, wuv_ref, wo_ref, o_ref, cat_scr, tq):
    ob = o_lat.astype(BF16)
    for hh in range(MLA_HEADS):
        cat_scr[:, hh * MLA_V:(hh + 1) * MLA_V] = _dot(ob[hh * tq:(hh + 1) * tq], wuv_ref[hh]).astype(BF16)
    o_ref[...] = x_ref[...] + _dot(cat_scr[...], wo_ref[...])


def _mla_samp_kernel(ql_ref, qr_ref, cc_ref, cr_ref, cn_ref, rn_ref, x_ref, wuv_ref, wo_ref, o_ref, cat_scr, *, n_q):
    b = pl.program_id(0)
    hd = MLA_HEADS
    ql = ql_ref[...].reshape(hd * n_q, MLA_KV_LORA)
    qr = qr_ref[...].reshape(hd * n_q, MLA_ROPE)
    cc = cc_ref[0].astype(BF16)
    cr = cr_ref[0].astype(BF16)
    cn = cn_ref[...]
    s1 = _dot_t(ql, cc) + _dot_t(qr, cr)
    s2 = _dot_t(ql, cn) + _dot_t(qr, rn_ref[...])
    own = _div(lax.broadcasted_iota(I32, (1, LANES), 1), n_q) == _mod(b, LANES // n_q)
    s2 = jnp.where(own, s2, NEG_INF)
    m = jnp.maximum(jnp.max(s1, axis=1, keepdims=True), jnp.max(s2, axis=1, keepdims=True))
    p1 = jnp.exp(s1 - m)
    p2 = jnp.exp(s2 - m)
    l = jnp.sum(p1, axis=1, keepdims=True) + jnp.sum(p2, axis=1, keepdims=True)
    o_lat = (_dot(p1.astype(BF16), cc) + _dot(p2.astype(BF16), cn)) / l
    _mla_epilogue(o_lat, x_ref, wuv_ref, wo_ref, o_ref, cat_scr, n_q)


def _mla_attn_sample(x, ql, qr, ckvb, krb, cache_c, cache_r, wts, n_b, n_q):
    t, d = x.shape
    hd = MLA_HEADS
    past = cache_c.shape[1]
    per = LANES // n_q
    return pl.pallas_call(
        functools.partial(_mla_samp_kernel, n_q=n_q),
        grid=(n_b,),
        in_specs=[
            pl.BlockSpec((hd, n_q, MLA_KV_LORA), lambda b: (0, b, 0)),
            pl.BlockSpec((hd, n_q, MLA_ROPE), lambda b: (0, b, 0)),
            pl.BlockSpec((1, past, MLA_KV_LORA), lambda b: (b, 0, 0)),
            pl.BlockSpec((1, past, MLA_ROPE), lambda b: (b, 0, 0)),
            pl.BlockSpec((LANES, MLA_KV_LORA), lambda b: (b // per, 0)),
            pl.BlockSpec((LANES, MLA_ROPE), lambda b: (b // per, 0)),
            pl.BlockSpec((n_q, d), lambda b: (b, 0)),
            _whole(wts["wuv"].shape), _whole(wts["wo"].shape),
        ],
        out_specs=pl.BlockSpec((n_q, d), lambda b: (b, 0)),
        out_shape=jax.ShapeDtypeStruct((t, d), F32),
        scratch_shapes=[pltpu.VMEM((n_q, hd * MLA_V), BF16)],
        compiler_params=_params("parallel"),
        name="mla_attn_sample",
    )(ql, qr, cache_c, cache_r, ckvb, krb, x, wts["wuv"], wts["wo"])


def _dsa_proj_kernel(x_ref, gm_ref, wqkv_ref, wqi_ref, wkw_ref, gki_ref,
                     k_ref, v_ref, ki_ref, qb_ref, kb_ref, vb_ref, qib_ref, kib_ref, wi_ref):
    hdm = DSA_HEADS * DSA_HEAD_DIM
    h = _rms(x_ref[...], gm_ref[...]).astype(BF16)
    qkv = _dot(h, wqkv_ref[...])
    q = qkv[:, :hdm] * DSA_SCALE
    k = qkv[:, hdm:2 * hdm]
    v = qkv[:, 2 * hdm:]
    k_ref[...] = pltpu.einshape("t(hd)->thd", k, h=DSA_HEADS)
    v_ref[...] = pltpu.einshape("t(hd)->thd", v, h=DSA_HEADS)
    qi = _dot(h, wqi_ref[...]) * IDX_SCALE
    kw = _dot(h, wkw_ref[...])
    ki = _rms(kw[:, :IDX_DIM], gki_ref[...])
    ki_ref[...] = ki
    kib_ref[...] = ki.astype(BF16)
    wi_ref[...] = kw[:, IDX_DIM:IDX_DIM + IDX_HEADS] * (IDX_HEADS ** -0.5)
    qb_ref[...] = q.astype(BF16)
    kb_ref[...] = k.astype(BF16)
    vb_ref[...] = v.astype(BF16)
    for hh in range(IDX_HEADS):
        qib_ref[hh] = qi[:, hh * IDX_DIM:(hh + 1) * IDX_DIM].astype(BF16)


def _dsa_proj(x, g_mix, w_qkv, w_qidx, w_kw, g_kidx):
    t, d = x.shape
    tm = min(512, t)
    hdm = DSA_HEADS * DSA_HEAD_DIM
    row = lambda wdt: pl.BlockSpec((tm, wdt), lambda i: (i, 0))
    hd3 = pl.BlockSpec((tm, DSA_HEADS, DSA_HEAD_DIM), lambda i: (i, 0, 0))
    return pl.pallas_call(
        _dsa_proj_kernel,
        grid=(t // tm,),
        in_specs=[row(d), _whole((1, d)), _whole(w_qkv.shape), _whole(w_qidx.shape), _whole(w_kw.shape),
                  _whole((1, IDX_DIM))],
        out_specs=[hd3, hd3, row(IDX_DIM), row(hdm), row(hdm), row(hdm),
                   pl.BlockSpec((IDX_HEADS, tm, IDX_DIM), lambda i: (0, i, 0)), row(IDX_DIM), row(IDX_HEADS)],
        out_shape=[
            jax.ShapeDtypeStruct((t, DSA_HEADS, DSA_HEAD_DIM), F32), jax.ShapeDtypeStruct((t, DSA_HEADS, DSA_HEAD_DIM), F32),
            jax.ShapeDtypeStruct((t, IDX_DIM), F32), jax.ShapeDtypeStruct((t, hdm), BF16),
            jax.ShapeDtypeStruct((t, hdm), BF16), jax.ShapeDtypeStruct((t, hdm), BF16),
            jax.ShapeDtypeStruct((IDX_HEADS, t, IDX_DIM), BF16), jax.ShapeDtypeStruct((t, IDX_DIM), BF16),
            jax.ShapeDtypeStruct((t, IDX_HEADS), F32),
        ],
        compiler_params=_params("parallel"),
        name="dsa_proj",
    )(x, g_mix.reshape(1, d), w_qkv, w_qidx, w_kw, g_kidx.reshape(1, IDX_DIM))


def _dsa_proj_t_kernel(x_ref, gm_ref, wkv_ref, wqt_ref, wvt_ref, wqit_ref, wkw_ref, wwt_ref, gki_ref,
                       k_ref, v_ref, ki_ref, qt_ref, kh_ref, vt_ref, qit_ref, kib_ref, wit_ref, kn2_ref):
    tm = x_ref.shape[0]
    hdm = DSA_HEADS * DSA_HEAD_DIM
    dh = DSA_HEAD_DIM
    h = _rms(x_ref[...], gm_ref[...]).astype(BF16)
    kv = _dot(h, wkv_ref[...])
    k = kv[:, :hdm]
    k_ref[...] = pltpu.einshape("t(hd)->thd", k, h=DSA_HEADS)
    v_ref[...] = pltpu.einshape("t(hd)->thd", kv[:, hdm:], h=DSA_HEADS)
    kw = _dot(h, wkw_ref[...])
    ki = _rms(kw[:, :IDX_DIM], gki_ref[...])
    ki_ref[...] = ki
    kib_ref[...] = ki.astype(BF16)
    wit_ref[...] = _dot_t(wwt_ref[...], h)[:IDX_HEADS] * (IDX_HEADS ** -0.5)
    qt = (_dot_t(wqt_ref[...], h) * (DSA_SCALE * LOG2E)).astype(BF16)
    vt = _dot_t(wvt_ref[...], h).astype(BF16)
    qit = (_dot_t(wqit_ref[...], h) * IDX_SCALE).astype(BF16)
    ones = jnp.ones((BF16_ROWS, tm), BF16)
    for hh in range(DSA_HEADS):
        qt_ref[hh] = qt[hh * dh:(hh + 1) * dh]
        key = k[:, hh * dh:(hh + 1) * dh]
        kh_ref[hh] = key.astype(BF16)
        kn2_ref[0, hh:hh + 1, :] = jnp.broadcast_to(
            jnp.max(jnp.sum(key * key, axis=1, keepdims=True), axis=0, keepdims=True), (1, LANES))
        vt_ref[hh, 0:dh, :] = vt[hh * dh:(hh + 1) * dh]
        vt_ref[hh, dh:, :] = ones
    for hh in range(IDX_HEADS):
        qit_ref[hh] = qit[hh * IDX_DIM:(hh + 1) * IDX_DIM]


def _dsa_proj_t(x, g_mix, w, g_kidx):
    t, d = x.shape
    tm = min(256, t)
    hd, dh = DSA_HEADS, DSA_HEAD_DIM
    hdm = hd * dh
    row = lambda wdt: pl.BlockSpec((tm, wdt), lambda i: (i, 0))
    names = ("wkv", "wqt", "wvt", "wqit", "wkw", "wwt")
    return pl.pallas_call(
        _dsa_proj_t_kernel,
        grid=(t // tm,),
        in_specs=[row(d), _whole((1, d))] + [_whole(w[n].shape) for n in names] + [_whole((1, IDX_DIM))],
        out_specs=[
            pl.BlockSpec((tm, hd, dh), lambda i: (i, 0, 0)), pl.BlockSpec((tm, hd, dh), lambda i: (i, 0, 0)),
            row(IDX_DIM),
            pl.BlockSpec((hd, dh, tm), lambda i: (0, 0, i)),
            pl.BlockSpec((hd, tm, dh), lambda i: (0, i, 0)),
            pl.BlockSpec((hd, V_ROWS, tm), lambda i: (0, 0, i)),
            pl.BlockSpec((IDX_HEADS, IDX_DIM, tm), lambda i: (0, 0, i)),
            row(IDX_DIM),
            pl.BlockSpec((IDX_HEADS, tm), lambda i: (0, i)),
            pl.BlockSpec((1, hd, LANES), lambda i: (i, 0, 0)),
        ],
        out_shape=[
            jax.ShapeDtypeStruct((t, hd, dh), F32), jax.ShapeDtypeStruct((t, hd, dh), F32),
            jax.ShapeDtypeStruct((t, IDX_DIM), F32),
            jax.ShapeDtypeStruct((hd, dh, t), BF16), jax.ShapeDtypeStruct((hd, t, dh), BF16),
            jax.ShapeDtypeStruct((hd, V_ROWS, t), BF16),
            jax.ShapeDtypeStruct((IDX_HEADS, IDX_DIM, t), BF16), jax.ShapeDtypeStruct((t, IDX_DIM), BF16),
            jax.ShapeDtypeStruct((IDX_HEADS, t), F32),
            jax.ShapeDtypeStruct((t // tm, hd, LANES), F32),
        ],
        compiler_params=_params("parallel"),
        name="dsa_proj_t",
    )(x, g_mix.reshape(1, d), *[w[n] for n in names], g_kidx.reshape(1, IDX_DIM))


def _sort_key(score):
    bits = lax.bitcast_convert_type(score, I32)
    return jnp.where(bits < 0, bits ^ jnp.int32(0x7FFFFFFF), bits)


def _kth_largest_key(count, topk):
    c0 = count(lambda key, idx: jnp.where(key >= 0, 1, 0))
    t0 = jnp.where(c0 >= topk, jnp.int32(0), jnp.int32(INT_MIN))

    def bit_body(i, t):
        cand = t + lax.shift_left(jnp.int32(1), jnp.int32(30) - i)
        c = count(lambda key, idx: jnp.where(key >= cand, 1, 0))
        return jnp.where(c >= topk, cand, t)

    return lax.fori_loop(0, 31, bit_body, t0)


def _radix_search(accept, start, nbits):
    t = start
    if nbits % 2:
        cand = t + jnp.int32(1 << (nbits - 1))
        t = jnp.where(accept(cand), cand, t)
        nbits -= 1

    def body(i, t):
        unit = lax.shift_left(jnp.int32(1), jnp.int32(nbits - 2) - 2 * i)
        a1, a2, a3 = accept(t + unit), accept(t + 2 * unit), accept(t + 3 * unit)
        return t + jnp.where(a3, 3, jnp.where(a2, 2, jnp.where(a1, 1, 0))) * unit

    return lax.fori_loop(0, nbits // 2, body, t)


def _select(key, idx, thr, cut):
    chosen = jnp.where(key > thr, 1, jnp.where(key == thr, jnp.where(idx <= cut, 1, 0), 0))
    return jnp.where(key > KEY_NEG_INF, chosen, 0) > 0


def _dsa_select_kernel(qit_ref, wit_ref, ki_ref, tri_ref, bias_ref, key_scr, *, tq, tk, topk):
    qb = pl.program_id(1)
    seq = ki_ref.shape[0]
    nk = seq // tk
    nvis = _div((qb + 1) * tq + tk - 1, tk)
    q_chunk = _div(qb * tq + lax.broadcasted_iota(I32, (1, tq), 1), CHUNK)
    w = wit_ref[...]
    row_idx = lax.broadcasted_iota(I32, (tk, 1), 0)

    def score_body(j, carry):
        off = pl.multiple_of(j * tk, tk)
        kb = ki_ref[pl.ds(off, tk), :]
        acc = jnp.zeros((tk, tq), F32)
        for hh in range(IDX_HEADS):
            acc = acc + w[hh:hh + 1, :] * jnp.maximum(_dot(kb, qit_ref[hh]), 0.0)
        vis = _div(off + row_idx, CHUNK) <= q_chunk
        key_scr[pl.ds(off, tk), :] = _sort_key(jnp.where(vis, acc, NEG_INF))
        return carry

    lax.fori_loop(0, nvis, score_body, 0)

    pair = 2 * tk if nk % 2 == 0 else tk
    npair = _div(nvis * tk + pair - 1, pair)

    @pl.when(npair * pair > nvis * tk)
    def _():
        key_scr[pl.ds(pl.multiple_of(nvis * tk, tk), tk), :] = jnp.full((tk, tq), KEY_NEG_INF, I32)

    def fold_rows(v):
        parts = [jnp.sum(v[g * (pair // 8):(g + 1) * (pair // 8)].reshape(pair // 64, 8, tq), axis=0) for g in range(8)]
        while len(parts) > 1:
            parts = [parts[i] + parts[i + 1] for i in range(0, len(parts), 2)]
        return parts[0]

    def count(hit):
        def body(j, c):
            off = pl.multiple_of(j * pair, pair)
            hits = hit(key_scr[pl.ds(off, pair), :], None)
            return c + fold_rows(hits)
        c = lax.fori_loop(0, npair, body, jnp.zeros((8, tq), I32))
        return jnp.sum(c, axis=0, keepdims=True)

    thr = _kth_largest_key(count, topk)
    rem = (topk - count(lambda key, idx: jnp.where(key > thr, 1, 0))).astype(F32)
    tri = tri_ref[...]

    def out_body(j, seen):
        off = pl.multiple_of(j * tk, tk)
        key = key_scr[pl.ds(off, tk), :]
        tie = jnp.where(key == thr, 1.0, 0.0)
        tie_b = tie.astype(BF16)
        half = tk // 2
        rank = seen + _dot(tri[:, :half], tie_b[:half]) + _dot(tri[:, half:], tie_b[half:])
        keep = jnp.where(key > thr, 1.0, jnp.where(rank <= rem, tie, 0.0))
        keep = jnp.where(key > KEY_NEG_INF, keep, 0.0)
        bias_ref[pl.ds(off, tk), :] = jnp.where(keep > 0.0, 0.0, NEG_INF).astype(BF16)
        return rank[tk - 1:tk, :]

    lax.fori_loop(0, nvis, out_body, jnp.zeros((1, tq), F32))

    def fill_body(j, carry):
        off = pl.multiple_of(j * tk, tk)
        bias_ref[pl.ds(off, tk), :] = jnp.full((tk, tq), NEG_INF, BF16)
        return carry

    lax.fori_loop(nvis, nk, fill_body, 0)


def _dsa_select(qit, wit, kib, batch, seq, topk):
    tq, tk = min(256, seq), min(512, seq)
    nq = seq // tq
    t = batch * seq
    return pl.pallas_call(
        functools.partial(_dsa_select_kernel, tq=tq, tk=tk, topk=topk),
        grid=(batch, nq),
        in_specs=[
            pl.BlockSpec((IDX_HEADS, IDX_DIM, tq), lambda b, q: (0, 0, b * nq + q)),
            pl.BlockSpec((IDX_HEADS, tq), lambda b, q: (0, b * nq + q)),
            pl.BlockSpec((seq, IDX_DIM), lambda b, q: (b, 0)),
            _whole((tk, tk)),
        ],
        out_specs=pl.BlockSpec((seq, tq), lambda b, q: (0, b * nq + q)),
        out_shape=jax.ShapeDtypeStruct((seq, t), BF16),
        scratch_shapes=[pltpu.VMEM((seq, tq), I32)],
        compiler_params=_params("parallel", "arbitrary"),
        name="dsa_select",
    )(qit, wit, kib, jnp.tril(jnp.ones((tk, tk), BF16)))


def _dsa_samp_kernel(q_ref, qi_ref, wi_ref, kc_ref, vc_ref, kic_ref, kn_ref, vn_ref, kin_ref, x_ref, wo_ref, o_ref,
                     *, n_q, topk):
    b = pl.program_id(0)
    hd, dh = DSA_HEADS, DSA_HEAD_DIM
    past = kc_ref.shape[1]
    n_keys = past + LANES
    own = _div(lax.broadcasted_iota(I32, (1, LANES), 1), n_q) == _mod(b, LANES // n_q)

    qi = qi_ref[...].reshape(IDX_HEADS * n_q, IDX_DIM)
    lg1 = jnp.maximum(_dot_t(qi, kic_ref[0].astype(BF16)), 0.0)
    lg2 = jnp.maximum(_dot_t(qi, kin_ref[...]), 0.0)
    w = wi_ref[...]
    sc1 = jnp.zeros((n_q, past), F32)
    sc2 = jnp.zeros((n_q, LANES), F32)
    for hh in range(IDX_HEADS):
        sc1 = sc1 + w[:, hh:hh + 1] * lg1[hh * n_q:(hh + 1) * n_q]
        sc2 = sc2 + w[:, hh:hh + 1] * lg2[hh * n_q:(hh + 1) * n_q]
    key = _sort_key(jnp.concatenate([sc1, jnp.where(own, sc2, NEG_INF)], axis=1))
    idx = lax.broadcasted_iota(I32, (1, n_keys), 1)

    count = lambda hit: jnp.sum(hit(key, idx), axis=1, keepdims=True)
    t0 = jnp.where(count(lambda k_, i_: jnp.where(k_ >= 0, 1, 0)) >= topk, jnp.int32(0), jnp.int32(INT_MIN))
    thr = _radix_search(lambda cand: count(lambda k_, i_: jnp.where(k_ >= cand, 1, 0)) >= topk, t0, 31)
    rem = topk - count(lambda k_, i_: jnp.where(k_ > thr, 1, 0))
    cut = _radix_search(
        lambda cand: count(lambda k_, i_: jnp.where(k_ == thr, jnp.where(i_ < cand, 1, 0), 0)) < rem,
        jnp.zeros_like(thr), int(n_keys - 1).bit_length())
    bias = jnp.where(_select(key, idx, thr, cut), 0.0, NEG_INF)
    bias = jnp.concatenate([bias] * hd, axis=0)

    lane_head = _div(lax.broadcasted_iota(I32, (1, hd * dh), 1), dh)
    qf = q_ref[...].astype(F32)
    qbd = jnp.concatenate([jnp.where(lane_head == hh, qf, 0.0) for hh in range(hd)], axis=0).astype(BF16)
    kc = kc_ref[0].astype(BF16)
    vc = vc_ref[0].astype(BF16)
    s1 = _dot_t(qbd, kc) + bias[:, :past]
    s2 = _dot_t(qbd, kn_ref[...]) + bias[:, past:]
    m = jnp.maximum(jnp.max(s1, axis=1, keepdims=True), jnp.max(s2, axis=1, keepdims=True))
    p1 = jnp.exp(s1 - m)
    p2 = jnp.exp(s2 - m)
    l = jnp.sum(p1, axis=1, keepdims=True) + jnp.sum(p2, axis=1, keepdims=True)
    o_all = (_dot(p1.astype(BF16), vc) + _dot(p2.astype(BF16), vn_ref[...])) / l
    out = jnp.zeros((n_q, hd * dh), F32)
    for hh in range(hd):
        out = out + jnp.where(lane_head == hh, o_all[hh * n_q:(hh + 1) * n_q], 0.0)
    o_ref[...] = x_ref[...] + _dot(out.astype(BF16), wo_ref[...])


def _dsa_attn_sample(x, qb, qib, wi, kb, vb, kib, cache_k, cache_v, cache_ki, w_o, n_b, n_q, topk):
    t, d = x.shape
    hdm = DSA_HEADS * DSA_HEAD_DIM
    past = cache_k.shape[1]
    per = LANES // n_q
    return pl.pallas_call(
        functools.partial(_dsa_samp_kernel, n_q=n_q, topk=topk),
        grid=(n_b,),
        in_specs=[
            pl.BlockSpec((n_q, hdm), lambda b: (b, 0)),
            pl.BlockSpec((IDX_HEADS, n_q, IDX_DIM), lambda b: (0, b, 0)),
            pl.BlockSpec((n_q, IDX_HEADS), lambda b: (b, 0)),
            pl.BlockSpec((1, past, hdm), lambda b: (b, 0, 0)),
            pl.BlockSpec((1, past, hdm), lambda b: (b, 0, 0)),
            pl.BlockSpec((1, past, IDX_DIM), lambda b: (b, 0, 0)),
            pl.BlockSpec((LANES, hdm), lambda b: (b // per, 0)),
            pl.BlockSpec((LANES, hdm), lambda b: (b // per, 0)),
            pl.BlockSpec((LANES, IDX_DIM), lambda b: (b // per, 0)),
            pl.BlockSpec((n_q, d), lambda b: (b, 0)),
            _whole(w_o.shape),
        ],
        out_specs=pl.BlockSpec((n_q, d), lambda b: (b, 0)),
        out_shape=jax.ShapeDtypeStruct((t, d), F32),
        compiler_params=_params("parallel"),
        name="dsa_attn_sample",
    )(qb, qib, wi, cache_k.reshape(n_b, past, hdm), cache_v.reshape(n_b, past, hdm), cache_ki, kb, vb, kib, x, w_o)


def kernel(x_prompt, x_sample, cache_mla_ckv, cache_mla_krope, cache_dsa_k, cache_dsa_v, cache_dsa_kidx, norm_mix, norm_ffn, norm_final, mla_w_dq, mla_g_q, mla_w_uq, mla_w_dkv, mla_g_kv, mla_w_ukv, mla_w_o, cmlp_w_in, cmlp_ln_g, cmlp_ln_b, cmlp_w_s, cmlp_b_s, cmlp_w_out, dsa_w_qkv, dsa_w_o, dsa_w_qidx, dsa_w_kidx, dsa_g_kidx, dsa_w_widx, ffn_w_in, ffn_w_out):
    batch, seq, d = x_prompt.shape
    n_b, n_q, _ = x_sample.shape
    past = cache_mla_ckv.shape[2]
    depth = norm_mix.shape[0]
    xp = x_prompt.reshape(batch * seq, d)
    xs = x_sample.reshape(n_b * n_q, d)
    tab_p = _rope_tables(jnp.arange(seq))
    tab_s = tuple(jnp.tile(a, (n_b, 1)) for a in _rope_tables(past + jnp.arange(n_q)))
    cast = lambda a: a.astype(BF16)
    outs = {k: [] for k in ("ckv_p", "kr_p", "ckv_s", "kr_s", "cv_s", "dk_p", "dv_p", "di_p", "dk_s", "dv_s", "di_s")}
    for i in range(depth):
        kind, j = i % 3, i // 3
        if kind == 0:
            wts = _mla_weights(mla_w_dq[j], mla_w_uq[j], mla_w_dkv[j], mla_w_ukv[j], mla_w_o[j])
            qt, kh, vt, ckv, kr, kn2 = _mla_proj_t(xp, norm_mix[i], wts, mla_g_q[j], mla_g_kv[j], tab_p, seq)
            xp = _attn_t(xp, qt, kh, vt, kn2, None, wts["wot"], batch, seq, 256, "mla_attn_prompt")
            outs["ckv_p"].append(ckv.reshape(batch, seq, -1)); outs["kr_p"].append(kr.reshape(batch, seq, -1))
            ql, qr, ckv, kr, ckvb, krb = _mla_proj(xs, norm_mix[i], wts, mla_g_q[j], mla_g_kv[j], tab_s)
            xs = _mla_attn_sample(xs, ql, qr, ckvb, krb, cache_mla_ckv[j], cache_mla_krope[j], wts, n_b, n_q)
            outs["ckv_s"].append(ckv.reshape(n_b, n_q, -1)); outs["kr_s"].append(kr.reshape(n_b, n_q, -1))
        elif kind == 1:
            w_in, w_out = cast(cmlp_w_in[j]), cast(cmlp_w_out[j])
            xp, _ = _cmlp(xp, norm_mix[i], w_in, cmlp_ln_g[j], cmlp_ln_b[j], cmlp_w_s[j], cmlp_b_s[j], w_out,
                          min(seq, CMLP_CHUNK), False)
            xs, v_s = _cmlp(xs, norm_mix[i], w_in, cmlp_ln_g[j], cmlp_ln_b[j], cmlp_w_s[j], cmlp_b_s[j], w_out,
                            min(n_q, CMLP_CHUNK), True)
            outs["cv_s"].append(v_s.reshape(n_b, n_q, -1))
        else:
            hdm = DSA_HEADS * DSA_HEAD_DIM
            w_qkv, w_qidx, w_o = cast(dsa_w_qkv[j]), cast(dsa_w_qidx[j]), cast(dsa_w_o[j])
            w_kidx, w_widx = cast(dsa_w_kidx[j]), cast(dsa_w_widx[j])
            zpad = lambda n: jnp.zeros((d, n), BF16)
            wt = dict(wkv=w_qkv[:, hdm:], wqt=w_qkv[:, :hdm].T, wvt=w_qkv[:, 2 * hdm:].T, wqit=w_qidx.T,
                      wkw=jnp.concatenate([w_kidx, zpad(LANES - IDX_DIM)], axis=1),
                      wwt=jnp.concatenate([w_widx, zpad(BF16_ROWS - IDX_HEADS)], axis=1).T)
            hshape = (DSA_HEADS, DSA_HEAD_DIM)
            k, v, ki, qt, kh, vt, qit, kib, wit, kn2 = _dsa_proj_t(xp, norm_mix[i], wt, dsa_g_kidx[j])
            bias = _dsa_select(qit, wit, kib, batch, seq, min(TOPK_MAX, seq // 4))
            xp = _attn_t(xp, qt, kh, vt, kn2, bias, w_o.T, batch, seq, 512, "dsa_attn_prompt")
            outs["dk_p"].append(k.reshape((batch, seq) + hshape)); outs["dv_p"].append(v.reshape((batch, seq) + hshape))
            outs["di_p"].append(ki.reshape(batch, seq, -1))
            w_kw = jnp.concatenate([w_kidx, w_widx, zpad(LANES - IDX_DIM - IDX_HEADS)], axis=1)
            k, v, ki, qb, kb, vb, qib, kib, wi = _dsa_proj(xs, norm_mix[i], w_qkv, w_qidx, w_kw, dsa_g_kidx[j])
            xs = _dsa_attn_sample(xs, qb, qib, wi, kb, vb, kib, cache_dsa_k[j], cache_dsa_v[j], cache_dsa_kidx[j], w_o,
                                  n_b, n_q, min(TOPK_MAX, (past + n_q) // 4))
            outs["dk_s"].append(k.reshape((n_b, n_q) + hshape)); outs["dv_s"].append(v.reshape((n_b, n_q) + hshape))
            outs["di_s"].append(ki.reshape(n_b, n_q, -1))
        w_in, w_out = cast(ffn_w_in[i]), cast(ffn_w_out[i])
        final = i == depth - 1
        xp = _ffn(xp, norm_ffn[i], w_in, w_out, norm_final, final)
        xs = _ffn(xs, norm_ffn[i], w_in, w_out, norm_final, final)
    st = lambda name: jnp.stack(outs[name])
    return (xp.reshape(batch, seq, d), xs.reshape(n_b, n_q, d),
            st("ckv_p"), st("kr_p"), st("ckv_s"), st("kr_s"), st("cv_s"),
            st("dk_p"), st("dv_p"), st("di_p"), st("dk_s"), st("dv_s"), st("di_s"))
```

```python
import functools

import numpy as np
import jax
import jax.numpy as jnp
from jax import lax
from jax.experimental import pallas as pl
from jax.experimental.pallas import tpu as pltpu

F32, BF16, I32 = jnp.float32, jnp.bfloat16, jnp.int32

CHUNK = 64
EPS = 1e-6
MLA_HEADS, MLA_Q_LORA, MLA_KV_LORA, MLA_NOPE, MLA_ROPE, MLA_V = 16, 512, 256, 64, 32, 64
ROPE_BASE = 10000.0
MLA_SCALE = (MLA_NOPE + MLA_ROPE) ** -0.5
CMLP_CHUNK, CMLP_WIDTH, CMLP_GROUPS = 128, 2048, 8
DSA_HEADS, DSA_HEAD_DIM = 16, 64
DSA_SCALE = DSA_HEAD_DIM ** -0.5
IDX_HEADS, IDX_DIM = 8, 64
IDX_SCALE = IDX_DIM ** -0.5
TOPK_MAX = 256

LANES = 128
MXU_TILE = 256
BF16_ROWS = 16
VMEM_LIMIT = 52 * 1024 * 1024
NEG_INF = float("-inf")
INT_MIN = -2 ** 31
KEY_NEG_INF = -2139095041
LOG2E = float(np.log2(np.e))
KEY_NORM_SLACK = 1.01
LAG_MARGIN = 100.0
HEAD_V = 64
V_ROWS = HEAD_V + BF16_ROWS


def _dot(a, b):
    return jnp.dot(a, b, preferred_element_type=F32)


def _dot_t(a, b):
    return lax.dot_general(a, b, (((1,), (1,)), ((), ())), preferred_element_type=F32)


def _rms(x, g):
    return x * lax.rsqrt(jnp.mean(x * x, axis=-1, keepdims=True) + EPS) * g


def _log2(n):
    assert n > 0 and n & (n - 1) == 0, n
    return n.bit_length() - 1


def _div(x, n):
    return lax.shift_right_logical(x, jnp.int32(_log2(n)))


def _mod(x, n):
    assert n & (n - 1) == 0, n
    return x & (n - 1)


def _params(*sem):
    return pltpu.CompilerParams(dimension_semantics=sem, vmem_limit_bytes=VMEM_LIMIT)


def _whole(shape):
    nd = len(shape)
    return pl.BlockSpec(shape, lambda *_: (0,) * nd)


def _ffn_kernel(x_ref, g_ref, wg_ref, wu_ref, wo_ref, gf_ref, o_ref, act_scr, *, final):
    x = x_ref[...]
    h = _rms(x, g_ref[...]).astype(BF16)
    f = wo_ref.shape[0]
    chunk = MXU_TILE if f % MXU_TILE == 0 else f
    nc = f // chunk
    nxt = (_dot(h, wg_ref[:, 0:chunk]), _dot(h, wu_ref[:, 0:chunk]))
    for c in range(nc):
        gate, up = nxt
        if c + 1 < nc:
            lo = (c + 1) * chunk
            nxt = (_dot(h, wg_ref[:, lo:lo + chunk]), _dot(h, wu_ref[:, lo:lo + chunk]))
        act_scr[:, c * chunk:(c + 1) * chunk] = (jax.nn.silu(gate) * up).astype(BF16)
    y = x + _dot(act_scr[...], wo_ref[...])
    if final:
        y = _rms(y, gf_ref[...])
    o_ref[...] = y


def _ffn(x, g, w_in, w_out, g_final, final):
    t, d = x.shape
    f = w_out.shape[0]
    tm = min(512, t)
    once = pl.Buffered(1)
    return pl.pallas_call(
        functools.partial(_ffn_kernel, final=final),
        grid=(t // tm,),
        in_specs=[
            pl.BlockSpec((tm, d), lambda i: (i, 0)),
            _whole((1, d)),
            pl.BlockSpec((d, f), lambda i: (0, 0), pipeline_mode=once),
            pl.BlockSpec((d, f), lambda i: (0, 1), pipeline_mode=once),
            pl.BlockSpec((f, d), lambda i: (0, 0), pipeline_mode=once),
            _whole((1, d)),
        ],
        out_specs=pl.BlockSpec((tm, d), lambda i: (i, 0)),
        out_shape=jax.ShapeDtypeStruct((t, d), F32),
        scratch_shapes=[pltpu.VMEM((tm, f), BF16)],
        compiler_params=_params("parallel"),
        name="ffn",
    )(x, g.reshape(1, d), w_in, w_in, w_out, g_final.reshape(1, d))


def _gelu(x):
    return 0.5 * x * (1.0 + lax.erf(x * np.float32(np.sqrt(0.5))))


def _layernorm(x, g, b):
    mu = jnp.mean(x, axis=-1, keepdims=True)
    xc = x - mu
    return xc * lax.rsqrt(jnp.mean(xc * xc, axis=-1, keepdims=True) + EPS) * g + b


def _cmlp_kernel(x_ref, g_ref, win_ref, lng_ref, lnb_ref, ws_ref, bs_ref, wout_ref, *rest, n_rows, write_v):
    if write_v:
        o_ref, v_ref, vb_scr, gated_scr = rest
    else:
        o_ref, vb_scr, gated_scr = rest
    tm = x_ref.shape[0]
    w = CMLP_WIDTH
    gw = w // CMLP_GROUPS
    c = CMLP_CHUNK
    x = x_ref[...]
    h = _rms(x, g_ref[...]).astype(BF16)
    v = _layernorm(_gelu(_dot(h, win_ref[:, w:])), lng_ref[...], lnb_ref[...])
    if write_v:
        v_ref[...] = v
    vb_scr[...] = v.astype(BF16)
    r_i = lax.broadcasted_iota(I32, (c, c), 0)
    c_i = lax.broadcasted_iota(I32, (c, c), 1)
    keep = jnp.where(c_i >= r_i - _mod(r_i, n_rows), jnp.where(c_i <= r_i, 1, 0), 0) > 0
    u_next = _dot(h, win_ref[:, 0:gw])
    for g in range(CMLP_GROUPS):
        lo, hi = g * gw, (g + 1) * gw
        u_raw = u_next
        if g + 1 < CMLP_GROUPS:
            u_next = _dot(h, win_ref[:, hi:hi + gw])
        wg = jnp.where(keep, ws_ref[g], 0.0).astype(BF16)
        bias = bs_ref[:, g:g + 1]
        mixed = jnp.concatenate(
            [_dot(wg, vb_scr[k * c:(k + 1) * c, lo:hi]) + bias for k in range(tm // c)], axis=0)
        gated_scr[:, lo:hi] = (_gelu(u_raw) * mixed).astype(BF16)
    o_ref[...] = x + _dot(gated_scr[...], wout_ref[...])


def _cmlp(x, g, w_in, ln_g, ln_b, w_s, b_s, w_out, n_rows, write_v):
    t, d = x.shape
    w = CMLP_WIDTH
    c = CMLP_CHUNK
    tm = min(512, t)
    rep = c // n_rows
    ws_t = jnp.tile(w_s[:, :n_rows, :n_rows], (1, rep, rep))
    bs_t = jnp.tile(b_s[:, :n_rows].T, (rep, 1))
    out_shape = [jax.ShapeDtypeStruct((t, d), F32)]
    out_specs = [pl.BlockSpec((tm, d), lambda i: (i, 0))]
    if write_v:
        out_shape.append(jax.ShapeDtypeStruct((t, w), F32))
        out_specs.append(pl.BlockSpec((tm, w), lambda i: (i, 0)))
    res = pl.pallas_call(
        functools.partial(_cmlp_kernel, n_rows=n_rows, write_v=write_v),
        grid=(t // tm,),
        in_specs=[
            pl.BlockSpec((tm, d), lambda i: (i, 0)),
            _whole((1, d)), _whole((d, 2 * w)), _whole((1, w)), _whole((1, w)),
            _whole((CMLP_GROUPS, c, c)), _whole((c, CMLP_GROUPS)), _whole((w, d)),
        ],
        out_specs=out_specs,
        out_shape=out_shape,
        scratch_shapes=[pltpu.VMEM((tm, w), BF16), pltpu.VMEM((tm, w), BF16)],
        compiler_params=_params("parallel"),
        name="cmlp",
    )(x, g.reshape(1, d), w_in, ln_g.reshape(1, w), ln_b.reshape(1, w), ws_t, bs_t, w_out)
    return res if write_v else (res[0], None)


def _attn_t_kernel(qi_ref, kj_ref, last_ref, nsub_ref, qt_ref, k_ref, vt_ref, *rest, tq, tk, sub, use_bias):
    if use_bias:
        bias_ref, kn_ref, x_ref, wot_ref, o_ref, m_scr, pend_scr, qn_scr, acc_scr, cat_scr = rest
    else:
        kn_ref, x_ref, wot_ref, o_ref, m_scr, pend_scr, qn_scr, acc_scr, cat_scr = rest
    p = pl.program_id(1)
    qi, kj = qi_ref[p], kj_ref[p]
    hd = qt_ref.shape[0]

    @pl.when(kj == 0)
    def _():
        m_scr[...] = jnp.full_like(m_scr, NEG_INF)
        pend_scr[...] = jnp.ones_like(pend_scr)
        acc_scr[...] = jnp.zeros_like(acc_scr)
        for hh in range(hd):
            q = qt_ref[hh].astype(F32)
            qn_scr[hh] = jnp.sqrt(jnp.sum(q * q, axis=0, keepdims=True)) * kn_ref[0, hh:hh + 1, 0:1]

    def step(off, mask, lagged):
        keys = lambda hh: k_ref[hh, pl.ds(off, sub), :]
        vals = lambda hh: vt_ref[hh, :, pl.ds(off, sub)]
        s_next = _dot(keys(0), qt_ref[0])
        pend = None
        for hh in range(hd):
            s = s_next
            if hh + 1 < hd:
                s_next = _dot(keys(hh + 1), qt_ref[hh + 1])
            if mask is not None:
                s = s + mask
            m_prev = m_scr[hh]
            m_new = jnp.maximum(m_prev, jnp.max(s, axis=0, keepdims=True))
            if lagged:
                scale = pend_scr[hh]
                pe = jnp.exp2(s - m_prev).astype(BF16)
                pend_scr[hh] = jnp.exp2(m_prev - m_new)
            else:
                m_safe = jnp.where(m_new == NEG_INF, 0.0, m_new)
                scale = pend_scr[hh] * jnp.exp2(m_prev - m_safe)
                pe = jnp.exp2(s - m_safe).astype(BF16)
                pend_scr[hh] = jnp.ones_like(m_prev)
            m_scr[hh] = m_new
            if pend is not None:
                ph, pa, pp = pend
                acc_scr[ph] = pa * acc_scr[ph] + _dot(vals(ph), pp)
            pend = (hh, scale, pe)
        ph, pa, pp = pend
        acc_scr[ph] = pa * acc_scr[ph] + _dot(vals(ph), pp)

    def sub_body(j, carry):
        off = pl.multiple_of(j * sub, sub)
        excess = qn_scr[0] - m_scr[0]
        for hh in range(1, hd):
            excess = jnp.maximum(excess, qn_scr[hh] - m_scr[hh])
        lag_ok = jnp.max(excess) <= LAG_MARGIN
        if use_bias:
            bias = bias_ref[pl.ds(off, sub), :].astype(F32)

            @pl.when(lag_ok)
            def _():
                step(off, bias, True)

            @pl.when(jnp.logical_not(lag_ok))
            def _():
                step(off, bias, False)
        else:
            base = kj * tk + off
            needs_mask = _div(base + sub - 1, CHUNK) > _div(qi * tq, CHUNK)

            def chunk_mask():
                k_chunk = _div(base + lax.broadcasted_iota(I32, (sub, 1), 0), CHUNK)
                q_chunk = _div(qi * tq + lax.broadcasted_iota(I32, (1, tq), 1), CHUNK)
                return jnp.where(k_chunk <= q_chunk, 0.0, NEG_INF)

            @pl.when(jnp.logical_not(lag_ok))
            def _():
                step(off, chunk_mask(), False)

            @pl.when(jnp.logical_and(lag_ok, needs_mask))
            def _():
                step(off, chunk_mask(), True)

            @pl.when(jnp.logical_and(lag_ok, jnp.logical_not(needs_mask)))
            def _():
                step(off, None, True)
        return carry

    lax.fori_loop(0, nsub_ref[p], sub_body, 0)

    @pl.when(last_ref[p] == 1)
    def _():
        for hh in range(hd):
            a = acc_scr[hh]
            cat_scr[hh * HEAD_V:(hh + 1) * HEAD_V, :] = (a[:HEAD_V] / a[HEAD_V:HEAD_V + 1]).astype(BF16)
        o_ref[...] = x_ref[...] + _dot(wot_ref[...], cat_scr[...]).T


def _causal_pairs(nq, tq, tk, sub):
    qi, kj, last, nsub = [], [], [], []
    per = tk // sub
    for q in range(nq):
        vis_sub = -(-((q + 1) * tq) // sub)
        nvis = -(-vis_sub // per)
        for k in range(nvis):
            qi.append(q); kj.append(k); last.append(int(k == nvis - 1)); nsub.append(min(per, vis_sub - k * per))
    as_arr = lambda v: jnp.asarray(np.array(v, np.int32))
    return as_arr(qi), as_arr(kj), as_arr(last), as_arr(nsub)


def _attn_t(x, qt, kh, vt, kn2, bias, wot, batch, seq, sub, name):
    t, d = x.shape
    hd, dk, _ = qt.shape
    tq, tk, sub = min(512, seq), min(1024, seq), min(sub, seq)
    nq, nk = seq // tq, seq // tk
    qi, kj, last, nsub = _causal_pairs(nq, tq, tk, sub)
    use_bias = bias is not None
    in_specs = [
        pl.BlockSpec((hd, dk, tq), lambda b, p, qi, kj, la, ns: (0, 0, b * nq + qi[p])),
        pl.BlockSpec((hd, tk, dk), lambda b, p, qi, kj, la, ns: (0, b * nk + kj[p], 0)),
        pl.BlockSpec((hd, V_ROWS, tk), lambda b, p, qi, kj, la, ns: (0, 0, b * nk + kj[p])),
    ]
    args = [qt, kh, vt]
    if use_bias:
        in_specs.append(pl.BlockSpec((tk, tq), lambda b, p, qi, kj, la, ns: (kj[p], b * nq + qi[p])))
        args.append(bias)
    kn = jnp.sqrt(jnp.max(kn2[:, :, 0].reshape(batch, -1, hd), axis=1)) * KEY_NORM_SLACK
    in_specs.append(pl.BlockSpec((1, hd, LANES), lambda b, p, qi, kj, la, ns: (b, 0, 0)))
    args.append(jnp.broadcast_to(kn[:, :, None], (batch, hd, LANES)))
    in_specs += [
        pl.BlockSpec((tq, d), lambda b, p, qi, kj, la, ns: (b * nq + qi[p], 0)),
        pl.BlockSpec(wot.shape, lambda b, p, qi, kj, la, ns: (0, 0)),
    ]
    args += [x, wot]
    grid_spec = pltpu.PrefetchScalarGridSpec(
        num_scalar_prefetch=4,
        grid=(batch, int(qi.shape[0])),
        in_specs=in_specs,
        out_specs=pl.BlockSpec((tq, d), lambda b, p, qi, kj, la, ns: (b * nq + qi[p], 0)),
        scratch_shapes=[
            pltpu.VMEM((hd, 1, tq), F32), pltpu.VMEM((hd, 1, tq), F32), pltpu.VMEM((hd, 1, tq), F32),
            pltpu.VMEM((hd, V_ROWS, tq), F32), pltpu.VMEM((hd * HEAD_V, tq), BF16),
        ],
    )
    return pl.pallas_call(
        functools.partial(_attn_t_kernel, tq=tq, tk=tk, sub=sub, use_bias=use_bias),
        grid_spec=grid_spec,
        out_shape=jax.ShapeDtypeStruct((t, d), F32),
        compiler_params=_params("parallel", "arbitrary"),
        name=name,
    )(qi, kj, last, nsub, *args)


def _rope_tables(pos):
    half = MLA_ROPE // 2
    inv = ROPE_BASE ** (-jnp.arange(half, dtype=F32) / half)
    ang = pos.astype(F32)[:, None] * inv[None, :]
    cos, sin = jnp.cos(ang), jnp.sin(ang)
    cos_k = jnp.concatenate([cos, cos], axis=1)
    sin_k = jnp.concatenate([-sin, sin], axis=1)
    return jnp.tile(cos_k, (1, MLA_HEADS)), jnp.tile(sin_k, (1, MLA_HEADS)), cos_k, sin_k


def _swap_halves(w, group):
    shp = w.shape
    wr = w.reshape(shp[:-1] + (shp[-1] // group, 2, group // 2))
    return wr[..., ::-1, :].reshape(shp)


def _mla_weights(w_dq, w_uq, w_dkv, w_ukv, w_o):
    hd = MLA_HEADS
    cast = lambda a: a.astype(BF16)
    w_uq, w_dkv, w_ukv = cast(w_uq), cast(w_dkv), cast(w_ukv)
    wqn = w_uq[:, :, :MLA_NOPE].reshape(MLA_Q_LORA, hd * MLA_NOPE)
    wqr = w_uq[:, :, MLA_NOPE:].reshape(MLA_Q_LORA, hd * MLA_ROPE)
    wqs = _swap_halves(wqr, MLA_ROPE)
    wkc, wkr = w_dkv[:, :MLA_KV_LORA], w_dkv[:, MLA_KV_LORA:]
    wks = _swap_halves(wkr, MLA_ROPE)
    wuk = jnp.transpose(w_ukv[:, :, :MLA_NOPE], (1, 2, 0))
    zeros = jnp.zeros_like(wuk)
    even = jnp.concatenate([wuk, zeros], axis=1)
    odd = jnp.concatenate([zeros, wuk], axis=1)
    wuk2 = jnp.where((jnp.arange(hd) % 2 == 0)[:, None, None], even, odd)
    wuv = jnp.transpose(w_ukv[:, :, MLA_NOPE:], (1, 0, 2))
    d = w_dkv.shape[0]
    pad_r = lambda a: jnp.concatenate([a, jnp.zeros((d, LANES - MLA_ROPE), BF16)], axis=1)
    wukp = jnp.concatenate([jnp.zeros((MLA_KV_LORA, hd, MLA_ROPE), BF16), w_ukv[:, :, :MLA_NOPE],
                            jnp.zeros((MLA_KV_LORA, hd, LANES - MLA_ROPE - MLA_NOPE), BF16)], axis=2)
    return dict(wdq=cast(w_dq), wqn=wqn, wqr=wqr, wqs=wqs, wkc=wkc, wkr=wkr, wks=wks, wuk=wuk2, wuv=wuv, wo=cast(w_o),
                wqnt=wqn.T, wqrt=wqr.T, wqst=wqs.T, wkrp=pad_r(wkr), wksp=pad_r(wks),
                wukp=wukp.reshape(MLA_KV_LORA, hd * LANES), wuvt=wuv.transpose(0, 2, 1).reshape(hd * MLA_V, MLA_KV_LORA),
                wot=cast(w_o).T)


def _mla_proj_kernel(x_ref, gm_ref, wdq_ref, gq_ref, wqn_ref, wqr_ref, wqs_ref, wkc_ref, wkr_ref, wks_ref,
                     gkv_ref, wuk_ref, cq_ref, sq_ref, ck_ref, sk_ref,
                     ql_ref, qr_ref, ckv_ref, kr_ref, ckvb_ref, krb_ref):
    h = _rms(x_ref[...], gm_ref[...]).astype(BF16)
    cq = _rms(_dot(h, wdq_ref[...]), gq_ref[...]).astype(BF16)
    qn = _dot(cq, wqn_ref[...])
    qr = ((_dot(cq, wqr_ref[...]) * cq_ref[...] + _dot(cq, wqs_ref[...]) * sq_ref[...]) * MLA_SCALE).astype(BF16)
    for hh in range(MLA_HEADS):
        pair = qn[:, (hh // 2) * LANES:(hh // 2 + 1) * LANES].astype(BF16)
        ql_ref[hh] = (_dot(pair, wuk_ref[hh]) * MLA_SCALE).astype(BF16)
        qr_ref[hh] = qr[:, hh * MLA_ROPE:(hh + 1) * MLA_ROPE]
    ckv = _rms(_dot(h, wkc_ref[...]), gkv_ref[...])
    ckv_ref[...] = ckv
    ckvb_ref[...] = ckv.astype(BF16)
    kr = _dot(h, wkr_ref[...]) * ck_ref[...] + _dot(h, wks_ref[...]) * sk_ref[...]
    kr_ref[...] = kr
    krb_ref[...] = kr.astype(BF16)


def _mla_proj(x, g_mix, wts, g_q, g_kv, tables):
    t, d = x.shape
    tm = min(256, t)
    cos_q, sin_q, cos_k, sin_k = tables
    nrep = cos_q.shape[0] // tm
    tab = lambda wdt: pl.BlockSpec((tm, wdt), lambda i: (i % nrep, 0))
    hd = MLA_HEADS
    row = lambda wdt: pl.BlockSpec((tm, wdt), lambda i: (i, 0))
    hm = lambda wdt: pl.BlockSpec((hd, tm, wdt), lambda i: (0, i, 0))
    return pl.pallas_call(
        _mla_proj_kernel,
        grid=(t // tm,),
        in_specs=[
            row(d), _whole((1, d)), _whole(wts["wdq"].shape), _whole((1, MLA_Q_LORA)),
            _whole(wts["wqn"].shape), _whole(wts["wqr"].shape), _whole(wts["wqs"].shape),
            _whole(wts["wkc"].shape), _whole(wts["wkr"].shape), _whole(wts["wks"].shape),
            _whole((1, MLA_KV_LORA)), _whole(wts["wuk"].shape),
            tab(hd * MLA_ROPE), tab(hd * MLA_ROPE), tab(MLA_ROPE), tab(MLA_ROPE),
        ],
        out_specs=[hm(MLA_KV_LORA), hm(MLA_ROPE), row(MLA_KV_LORA), row(MLA_ROPE), row(MLA_KV_LORA), row(MLA_ROPE)],
        out_shape=[
            jax.ShapeDtypeStruct((hd, t, MLA_KV_LORA), BF16), jax.ShapeDtypeStruct((hd, t, MLA_ROPE), BF16),
            jax.ShapeDtypeStruct((t, MLA_KV_LORA), F32), jax.ShapeDtypeStruct((t, MLA_ROPE), F32),
            jax.ShapeDtypeStruct((t, MLA_KV_LORA), BF16), jax.ShapeDtypeStruct((t, MLA_ROPE), BF16),
        ],
        compiler_params=_params("parallel"),
        name="mla_proj",
    )(x, g_mix.reshape(1, d), wts["wdq"], g_q.reshape(1, -1), wts["wqn"], wts["wqr"], wts["wqs"],
      wts["wkc"], wts["wkr"], wts["wks"], g_kv.reshape(1, -1), wts["wuk"], cos_q, sin_q, cos_k, sin_k)


def _mla_proj_t_kernel(x_ref, gm_ref, wdq_ref, gq_ref, wqnt_ref, wqrt_ref, wqst_ref, wkc_ref, wkrp_ref, wksp_ref,
                       gkv_ref, wukp_ref, wuvt_ref, cqt_ref, sqt_ref, ckp_ref, skp_ref,
                       qt_ref, kh_ref, vt_ref, ckv_ref, kr_ref, kn2_ref):
    tm = x_ref.shape[0]
    hd = MLA_HEADS
    qscale = MLA_SCALE * LOG2E
    h = _rms(x_ref[...], gm_ref[...]).astype(BF16)
    cq_raw = _dot(h, wdq_ref[...])
    ckv_raw = _dot(h, wkc_ref[...])
    kr_a = _dot(h, wkrp_ref[...])
    kr_b = _dot(h, wksp_ref[...])
    cq = _rms(cq_raw, gq_ref[...]).astype(BF16)
    qnt_raw = _dot_t(wqnt_ref[...], cq)
    qrt_a = _dot_t(wqrt_ref[...], cq)
    qrt_b = _dot_t(wqst_ref[...], cq)
    ckv = _rms(ckv_raw, gkv_ref[...])
    ckv_ref[...] = ckv
    cb = ckv.astype(BF16)
    kn = _dot(cb, wukp_ref[...])
    vt = _dot_t(wuvt_ref[...], cb).astype(BF16)
    qnt = (qnt_raw * qscale).astype(BF16)
    rot = qrt_a.reshape(hd, MLA_ROPE, tm) * cqt_ref[...][None] + qrt_b.reshape(hd, MLA_ROPE, tm) * sqt_ref[...][None]
    qrt = (rot * qscale).reshape(hd * MLA_ROPE, tm).astype(BF16)
    zeros = jnp.zeros((LANES - MLA_ROPE - MLA_NOPE, tm), BF16)
    krp = kr_a * ckp_ref[...] + kr_b * skp_ref[...]
    kr_ref[...] = krp[:, :MLA_ROPE]
    ones = jnp.ones((BF16_ROWS, tm), BF16)
    for hh in range(hd):
        qt_ref[hh, 0:MLA_ROPE, :] = qrt[hh * MLA_ROPE:(hh + 1) * MLA_ROPE]
        qt_ref[hh, MLA_ROPE:MLA_ROPE + MLA_NOPE, :] = qnt[hh * MLA_NOPE:(hh + 1) * MLA_NOPE]
        qt_ref[hh, MLA_ROPE + MLA_NOPE:, :] = zeros
        key = kn[:, hh * LANES:(hh + 1) * LANES] + krp
        kh_ref[hh] = key.astype(BF16)
        kn2_ref[0, hh:hh + 1, :] = jnp.broadcast_to(
            jnp.max(jnp.sum(key * key, axis=1, keepdims=True), axis=0, keepdims=True), (1, LANES))
        vt_ref[hh, 0:MLA_V, :] = vt[hh * MLA_V:(hh + 1) * MLA_V]
        vt_ref[hh, MLA_V:, :] = ones


def _mla_proj_t(x, g_mix, wts, g_q, g_kv, tables, seq):
    t, d = x.shape
    tm = min(256, seq)
    hd = MLA_HEADS
    cos_q, sin_q, cos_k, sin_k = tables
    pad = lambda a: jnp.concatenate([a, jnp.zeros((seq, LANES - MLA_ROPE), F32)], axis=1)
    nrep = seq // tm
    row = lambda wdt: pl.BlockSpec((tm, wdt), lambda i: (i, 0))
    names = ("wdq", "wqnt", "wqrt", "wqst", "wkc", "wkrp", "wksp", "wukp", "wuvt")
    w = {n: wts[n] for n in names}
    return pl.pallas_call(
        _mla_proj_t_kernel,
        grid=(t // tm,),
        in_specs=[
            row(d), _whole((1, d)), _whole(w["wdq"].shape), _whole((1, MLA_Q_LORA)),
            _whole(w["wqnt"].shape), _whole(w["wqrt"].shape), _whole(w["wqst"].shape),
            _whole(w["wkc"].shape), _whole(w["wkrp"].shape), _whole(w["wksp"].shape),
            _whole((1, MLA_KV_LORA)), _whole(w["wukp"].shape), _whole(w["wuvt"].shape),
            pl.BlockSpec((MLA_ROPE, tm), lambda i: (0, i % nrep)),
            pl.BlockSpec((MLA_ROPE, tm), lambda i: (0, i % nrep)),
            pl.BlockSpec((tm, LANES), lambda i: (i % nrep, 0)),
            pl.BlockSpec((tm, LANES), lambda i: (i % nrep, 0)),
        ],
        out_specs=[
            pl.BlockSpec((hd, LANES, tm), lambda i: (0, 0, i)),
            pl.BlockSpec((hd, tm, LANES), lambda i: (0, i, 0)),
            pl.BlockSpec((hd, V_ROWS, tm), lambda i: (0, 0, i)),
            row(MLA_KV_LORA), row(MLA_ROPE),
            pl.BlockSpec((1, hd, LANES), lambda i: (i, 0, 0)),
        ],
        out_shape=[
            jax.ShapeDtypeStruct((hd, LANES, t), BF16), jax.ShapeDtypeStruct((hd, t, LANES), BF16),
            jax.ShapeDtypeStruct((hd, V_ROWS, t), BF16),
            jax.ShapeDtypeStruct((t, MLA_KV_LORA), F32), jax.ShapeDtypeStruct((t, MLA_ROPE), F32),
            jax.ShapeDtypeStruct((t // tm, hd, LANES), F32),
        ],
        compiler_params=_params("parallel"),
        name="mla_proj_t",
    )(x, g_mix.reshape(1, d), w["wdq"], g_q.reshape(1, -1), w["wqnt"], w["wqrt"], w["wqst"],
      w["wkc"], w["wkrp"], w["wksp"], g_kv.reshape(1, -1), w["wukp"], w["wuvt"],
      cos_k.T, sin_k.T, pad(cos_k), pad(sin_k))


def _mla_epilogue(o_lat, x_ref, wuv_ref, wo_ref, o_ref, cat_scr, tq):
    ob = o_lat.astype(BF16)
    for hh in range(MLA_HEADS):
        cat_scr[:, hh * MLA_V:(hh + 1) * MLA_V] = _dot(ob[hh * tq:(hh + 1) * tq], wuv_ref[hh]).astype(BF16)
    o_ref[...] = x_ref[...] + _dot(cat_scr[...], wo_ref[...])


def _mla_samp_kernel(ql_ref, qr_ref, cc_ref, cr_ref, cn_ref, rn_ref, x_ref, wuv_ref, wo_ref, o_ref, cat_scr, *, n_q):
    b = pl.program_id(0)
    hd = MLA_HEADS
    ql = ql_ref[...].reshape(hd * n_q, MLA_KV_LORA)
    qr = qr_ref[...].reshape(hd * n_q, MLA_ROPE)
    cc = cc_ref[0].astype(BF16)
    cr = cr_ref[0].astype(BF16)
    cn = cn_ref[...]
    s1 = _dot_t(ql, cc) + _dot_t(qr, cr)
    s2 = _dot_t(ql, cn) + _dot_t(qr, rn_ref[...])
    own = _div(lax.broadcasted_iota(I32, (1, LANES), 1), n_q) == _mod(b, LANES // n_q)
    s2 = jnp.where(own, s2, NEG_INF)
    m = jnp.maximum(jnp.max(s1, axis=1, keepdims=True), jnp.max(s2, axis=1, keepdims=True))
    p1 = jnp.exp(s1 - m)
    p2 = jnp.exp(s2 - m)
    l = jnp.sum(p1, axis=1, keepdims=True) + jnp.sum(p2, axis=1, keepdims=True)
    o_lat = (_dot(p1.astype(BF16), cc) + _dot(p2.astype(BF16), cn)) / l
    _mla_epilogue(o_lat, x_ref, wuv_ref, wo_ref, o_ref, cat_scr, n_q)


def _mla_attn_sample(x, ql, qr, ckvb, krb, cache_c, cache_r, wts, n_b, n_q):
    t, d = x.shape
    hd = MLA_HEADS
    past = cache_c.shape[1]
    per = LANES // n_q
    return pl.pallas_call(
        functools.partial(_mla_samp_kernel, n_q=n_q),
        grid=(n_b,),
        in_specs=[
            pl.BlockSpec((hd, n_q, MLA_KV_LORA), lambda b: (0, b, 0)),
            pl.BlockSpec((hd, n_q, MLA_ROPE), lambda b: (0, b, 0)),
            pl.BlockSpec((1, past, MLA_KV_LORA), lambda b: (b, 0, 0)),
            pl.BlockSpec((1, past, MLA_ROPE), lambda b: (b, 0, 0)),
            pl.BlockSpec((LANES, MLA_KV_LORA), lambda b: (b // per, 0)),
            pl.BlockSpec((LANES, MLA_ROPE), lambda b: (b // per, 0)),
            pl.BlockSpec((n_q, d), lambda b: (b, 0)),
            _whole(wts["wuv"].shape), _whole(wts["wo"].shape),
        ],
        out_specs=pl.BlockSpec((n_q, d), lambda b: (b, 0)),
        out_shape=jax.ShapeDtypeStruct((t, d), F32),
        scratch_shapes=[pltpu.VMEM((n_q, hd * MLA_V), BF16)],
        compiler_params=_params("parallel"),
        name="mla_attn_sample",
    )(ql, qr, cache_c, cache_r, ckvb, krb, x, wts["wuv"], wts["wo"])


def _dsa_proj_kernel(x_ref, gm_ref, wqkv_ref, wqi_ref, wkw_ref, gki_ref,
                     k_ref, v_ref, ki_ref, qb_ref, kb_ref, vb_ref, qib_ref, kib_ref, wi_ref):
    hdm = DSA_HEADS * DSA_HEAD_DIM
    h = _rms(x_ref[...], gm_ref[...]).astype(BF16)
    qkv = _dot(h, wqkv_ref[...])
    q = qkv[:, :hdm] * DSA_SCALE
    k = qkv[:, hdm:2 * hdm]
    v = qkv[:, 2 * hdm:]
    k_ref[...] = pltpu.einshape("t(hd)->thd", k, h=DSA_HEADS)
    v_ref[...] = pltpu.einshape("t(hd)->thd", v, h=DSA_HEADS)
    qi = _dot(h, wqi_ref[...]) * IDX_SCALE
    kw = _dot(h, wkw_ref[...])
    ki = _rms(kw[:, :IDX_DIM], gki_ref[...])
    ki_ref[...] = ki
    kib_ref[...] = ki.astype(BF16)
    wi_ref[...] = kw[:, IDX_DIM:IDX_DIM + IDX_HEADS] * (IDX_HEADS ** -0.5)
    qb_ref[...] = q.astype(BF16)
    kb_ref[...] = k.astype(BF16)
    vb_ref[...] = v.astype(BF16)
    for hh in range(IDX_HEADS):
        qib_ref[hh] = qi[:, hh * IDX_DIM:(hh + 1) * IDX_DIM].astype(BF16)


def _dsa_proj(x, g_mix, w_qkv, w_qidx, w_kw, g_kidx):
    t, d = x.shape
    tm = min(512, t)
    hdm = DSA_HEADS * DSA_HEAD_DIM
    row = lambda wdt: pl.BlockSpec((tm, wdt), lambda i: (i, 0))
    hd3 = pl.BlockSpec((tm, DSA_HEADS, DSA_HEAD_DIM), lambda i: (i, 0, 0))
    return pl.pallas_call(
        _dsa_proj_kernel,
        grid=(t // tm,),
        in_specs=[row(d), _whole((1, d)), _whole(w_qkv.shape), _whole(w_qidx.shape), _whole(w_kw.shape),
                  _whole((1, IDX_DIM))],
        out_specs=[hd3, hd3, row(IDX_DIM), row(hdm), row(hdm), row(hdm),
                   pl.BlockSpec((IDX_HEADS, tm, IDX_DIM), lambda i: (0, i, 0)), row(IDX_DIM), row(IDX_HEADS)],
        out_shape=[
            jax.ShapeDtypeStruct((t, DSA_HEADS, DSA_HEAD_DIM), F32), jax.ShapeDtypeStruct((t, DSA_HEADS, DSA_HEAD_DIM), F32),
            jax.ShapeDtypeStruct((t, IDX_DIM), F32), jax.ShapeDtypeStruct((t, hdm), BF16),
            jax.ShapeDtypeStruct((t, hdm), BF16), jax.ShapeDtypeStruct((t, hdm), BF16),
            jax.ShapeDtypeStruct((IDX_HEADS, t, IDX_DIM), BF16), jax.ShapeDtypeStruct((t, IDX_DIM), BF16),
            jax.ShapeDtypeStruct((t, IDX_HEADS), F32),
        ],
        compiler_params=_params("parallel"),
        name="dsa_proj",
    )(x, g_mix.reshape(1, d), w_qkv, w_qidx, w_kw, g_kidx.reshape(1, IDX_DIM))


def _dsa_proj_t_kernel(x_ref, gm_ref, wkv_ref, wqt_ref, wvt_ref, wqit_ref, wkw_ref, wwt_ref, gki_ref,
                       k_ref, v_ref, ki_ref, qt_ref, kh_ref, vt_ref, qit_ref, kib_ref, wit_ref, kn2_ref):
    tm = x_ref.shape[0]
    hdm = DSA_HEADS * DSA_HEAD_DIM
    dh = DSA_HEAD_DIM
    h = _rms(x_ref[...], gm_ref[...]).astype(BF16)
    kv = _dot(h, wkv_ref[...])
    k = kv[:, :hdm]
    k_ref[...] = pltpu.einshape("t(hd)->thd", k, h=DSA_HEADS)
    v_ref[...] = pltpu.einshape("t(hd)->thd", kv[:, hdm:], h=DSA_HEADS)
    kw = _dot(h, wkw_ref[...])
    ki = _rms(kw[:, :IDX_DIM], gki_ref[...])
    ki_ref[...] = ki
    kib_ref[...] = ki.astype(BF16)
    wit_ref[...] = _dot_t(wwt_ref[...], h)[:IDX_HEADS] * (IDX_HEADS ** -0.5)
    qt = (_dot_t(wqt_ref[...], h) * (DSA_SCALE * LOG2E)).astype(BF16)
    vt = _dot_t(wvt_ref[...], h).astype(BF16)
    qit = (_dot_t(wqit_ref[...], h) * IDX_SCALE).astype(BF16)
    ones = jnp.ones((BF16_ROWS, tm), BF16)
    for hh in range(DSA_HEADS):
        qt_ref[hh] = qt[hh * dh:(hh + 1) * dh]
        key = k[:, hh * dh:(hh + 1) * dh]
        kh_ref[hh] = key.astype(BF16)
        kn2_ref[0, hh:hh + 1, :] = jnp.broadcast_to(
            jnp.max(jnp.sum(key * key, axis=1, keepdims=True), axis=0, keepdims=True), (1, LANES))
        vt_ref[hh, 0:dh, :] = vt[hh * dh:(hh + 1) * dh]
        vt_ref[hh, dh:, :] = ones
    for hh in range(IDX_HEADS):
        qit_ref[hh] = qit[hh * IDX_DIM:(hh + 1) * IDX_DIM]


def _dsa_proj_t(x, g_mix, w, g_kidx):
    t, d = x.shape
    tm = min(256, t)
    hd, dh = DSA_HEADS, DSA_HEAD_DIM
    hdm = hd * dh
    row = lambda wdt: pl.BlockSpec((tm, wdt), lambda i: (i, 0))
    names = ("wkv", "wqt", "wvt", "wqit", "wkw", "wwt")
    return pl.pallas_call(
        _dsa_proj_t_kernel,
        grid=(t // tm,),
        in_specs=[row(d), _whole((1, d))] + [_whole(w[n].shape) for n in names] + [_whole((1, IDX_DIM))],
        out_specs=[
            pl.BlockSpec((tm, hd, dh), lambda i: (i, 0, 0)), pl.BlockSpec((tm, hd, dh), lambda i: (i, 0, 0)),
            row(IDX_DIM),
            pl.BlockSpec((hd, dh, tm), lambda i: (0, 0, i)),
            pl.BlockSpec((hd, tm, dh), lambda i: (0, i, 0)),
            pl.BlockSpec((hd, V_ROWS, tm), lambda i: (0, 0, i)),
            pl.BlockSpec((IDX_HEADS, IDX_DIM, tm), lambda i: (0, 0, i)),
            row(IDX_DIM),
            pl.BlockSpec((IDX_HEADS, tm), lambda i: (0, i)),
            pl.BlockSpec((1, hd, LANES), lambda i: (i, 0, 0)),
        ],
        out_shape=[
            jax.ShapeDtypeStruct((t, hd, dh), F32), jax.ShapeDtypeStruct((t, hd, dh), F32),
            jax.ShapeDtypeStruct((t, IDX_DIM), F32),
            jax.ShapeDtypeStruct((hd, dh, t), BF16), jax.ShapeDtypeStruct((hd, t, dh), BF16),
            jax.ShapeDtypeStruct((hd, V_ROWS, t), BF16),
            jax.ShapeDtypeStruct((IDX_HEADS, IDX_DIM, t), BF16), jax.ShapeDtypeStruct((t, IDX_DIM), BF16),
            jax.ShapeDtypeStruct((IDX_HEADS, t), F32),
            jax.ShapeDtypeStruct((t // tm, hd, LANES), F32),
        ],
        compiler_params=_params("parallel"),
        name="dsa_proj_t",
    )(x, g_mix.reshape(1, d), *[w[n] for n in names], g_kidx.reshape(1, IDX_DIM))


def _sort_key(score):
    bits = lax.bitcast_convert_type(score, I32)
    return jnp.where(bits < 0, bits ^ jnp.int32(0x7FFFFFFF), bits)


def _kth_largest_key(count, topk):
    c0 = count(lambda key, idx: jnp.where(key >= 0, 1, 0))
    t0 = jnp.where(c0 >= topk, jnp.int32(0), jnp.int32(INT_MIN))

    def bit_body(i, t):
        cand = t + lax.shift_left(jnp.int32(1), jnp.int32(30) - i)
        c = count(lambda key, idx: jnp.where(key >= cand, 1, 0))
        return jnp.where(c >= topk, cand, t)

    return lax.fori_loop(0, 31, bit_body, t0)


def _radix_search(accept, start, nbits):
    t = start
    if nbits % 2:
        cand = t + jnp.int32(1 << (nbits - 1))
        t = jnp.where(accept(cand), cand, t)
        nbits -= 1

    def body(i, t):
        unit = lax.shift_left(jnp.int32(1), jnp.int32(nbits - 2) - 2 * i)
        a1, a2, a3 = accept(t + unit), accept(t + 2 * unit), accept(t + 3 * unit)
        return t + jnp.where(a3, 3, jnp.where(a2, 2, jnp.where(a1, 1, 0))) * unit

    return lax.fori_loop(0, nbits // 2, body, t)


def _select(key, idx, thr, cut):
    chosen = jnp.where(key > thr, 1, jnp.where(key == thr, jnp.where(idx <= cut, 1, 0), 0))
    return jnp.where(key > KEY_NEG_INF, chosen, 0) > 0


def _dsa_select_kernel(qit_ref, wit_ref, ki_ref, tri_ref, bias_ref, key_scr, *, tq, tk, topk):
    qb = pl.program_id(1)
    seq = ki_ref.shape[0]
    nk = seq // tk
    nvis = _div((qb + 1) * tq + tk - 1, tk)
    q_chunk = _div(qb * tq + lax.broadcasted_iota(I32, (1, tq), 1), CHUNK)
    w = wit_ref[...]
    row_idx = lax.broadcasted_iota(I32, (tk, 1), 0)

    def score_body(j, carry):
        off = pl.multiple_of(j * tk, tk)
        kb = ki_ref[pl.ds(off, tk), :]
        acc = jnp.zeros((tk, tq), F32)
        for hh in range(IDX_HEADS):
            acc = acc + w[hh:hh + 1, :] * jnp.maximum(_dot(kb, qit_ref[hh]), 0.0)
        vis = _div(off + row_idx, CHUNK) <= q_chunk
        key_scr[pl.ds(off, tk), :] = _sort_key(jnp.where(vis, acc, NEG_INF))
        return carry

    lax.fori_loop(0, nvis, score_body, 0)

    pair = 2 * tk if nk % 2 == 0 else tk
    npair = _div(nvis * tk + pair - 1, pair)

    @pl.when(npair * pair > nvis * tk)
    def _():
        key_scr[pl.ds(pl.multiple_of(nvis * tk, tk), tk), :] = jnp.full((tk, tq), KEY_NEG_INF, I32)

    def fold_rows(v):
        parts = [jnp.sum(v[g * (pair // 8):(g + 1) * (pair // 8)].reshape(pair // 64, 8, tq), axis=0) for g in range(8)]
        while len(parts) > 1:
            parts = [parts[i] + parts[i + 1] for i in range(0, len(parts), 2)]
        return parts[0]

    def count(hit):
        def body(j, c):
            off = pl.multiple_of(j * pair, pair)
            hits = hit(key_scr[pl.ds(off, pair), :], None)
            return c + fold_rows(hits)
        c = lax.fori_loop(0, npair, body, jnp.zeros((8, tq), I32))
        return jnp.sum(c, axis=0, keepdims=True)

    thr = _kth_largest_key(count, topk)
    rem = (topk - count(lambda key, idx: jnp.where(key > thr, 1, 0))).astype(F32)
    tri = tri_ref[...]

    def out_body(j, seen):
        off = pl.multiple_of(j * tk, tk)
        key = key_scr[pl.ds(off, tk), :]
        tie = jnp.where(key == thr, 1.0, 0.0)
        tie_b = tie.astype(BF16)
        half = tk // 2
        rank = seen + _dot(tri[:, :half], tie_b[:half]) + _dot(tri[:, half:], tie_b[half:])
        keep = jnp.where(key > thr, 1.0, jnp.where(rank <= rem, tie, 0.0))
        keep = jnp.where(key > KEY_NEG_INF, keep, 0.0)
        bias_ref[pl.ds(off, tk), :] = jnp.where(keep > 0.0, 0.0, NEG_INF).astype(BF16)
        return rank[tk - 1:tk, :]

    lax.fori_loop(0, nvis, out_body, jnp.zeros((1, tq), F32))

    def fill_body(j, carry):
        off = pl.multiple_of(j * tk, tk)
        bias_ref[pl.ds(off, tk), :] = jnp.full((tk, tq), NEG_INF, BF16)
        return carry

    lax.fori_loop(nvis, nk, fill_body, 0)


def _dsa_select(qit, wit, kib, batch, seq, topk):
    tq, tk = min(256, seq), min(512, seq)
    nq = seq // tq
    t = batch * seq
    return pl.pallas_call(
        functools.partial(_dsa_select_kernel, tq=tq, tk=tk, topk=topk),
        grid=(batch, nq),
        in_specs=[
            pl.BlockSpec((IDX_HEADS, IDX_DIM, tq), lambda b, q: (0, 0, b * nq + q)),
            pl.BlockSpec((IDX_HEADS, tq), lambda b, q: (0, b * nq + q)),
            pl.BlockSpec((seq, IDX_DIM), lambda b, q: (b, 0)),
            _whole((tk, tk)),
        ],
        out_specs=pl.BlockSpec((seq, tq), lambda b, q: (0, b * nq + q)),
        out_shape=jax.ShapeDtypeStruct((seq, t), BF16),
        scratch_shapes=[pltpu.VMEM((seq, tq), I32)],
        compiler_params=_params("parallel", "arbitrary"),
        name="dsa_select",
    )(qit, wit, kib, jnp.tril(jnp.ones((tk, tk), BF16)))


def _dsa_samp_kernel(q_ref, qi_ref, wi_ref, kc_ref, vc_ref, kic_ref, kn_ref, vn_ref, kin_ref, x_ref, wo_ref, o_ref,
                     *, n_q, topk):
    b = pl.program_id(0)
    hd, dh = DSA_HEADS, DSA_HEAD_DIM
    past = kc_ref.shape[1]
    n_keys = past + LANES
    own = _div(lax.broadcasted_iota(I32, (1, LANES), 1), n_q) == _mod(b, LANES // n_q)

    qi = qi_ref[...].reshape(IDX_HEADS * n_q, IDX_DIM)
    lg1 = jnp.maximum(_dot_t(qi, kic_ref[0].astype(BF16)), 0.0)
    lg2 = jnp.maximum(_dot_t(qi, kin_ref[...]), 0.0)
    w = wi_ref[...]
    sc1 = jnp.zeros((n_q, past), F32)
    sc2 = jnp.zeros((n_q, LANES), F32)
    for hh in range(IDX_HEADS):
        sc1 = sc1 + w[:, hh:hh + 1] * lg1[hh * n_q:(hh + 1) * n_q]
        sc2 = sc2 + w[:, hh:hh + 1] * lg2[hh * n_q:(hh + 1) * n_q]
    key = _sort_key(jnp.concatenate([sc1, jnp.where(own, sc2, NEG_INF)], axis=1))
    idx = lax.broadcasted_iota(I32, (1, n_keys), 1)

    count = lambda hit: jnp.sum(hit(key, idx), axis=1, keepdims=True)
    t0 = jnp.where(count(lambda k_, i_: jnp.where(k_ >= 0, 1, 0)) >= topk, jnp.int32(0), jnp.int32(INT_MIN))
    thr = _radix_search(lambda cand: count(lambda k_, i_: jnp.where(k_ >= cand, 1, 0)) >= topk, t0, 31)
    rem = topk - count(lambda k_, i_: jnp.where(k_ > thr, 1, 0))
    cut = _radix_search(
        lambda cand: count(lambda k_, i_: jnp.where(k_ == thr, jnp.where(i_ < cand, 1, 0), 0)) < rem,
        jnp.zeros_like(thr), int(n_keys - 1).bit_length())
    bias = jnp.where(_select(key, idx, thr, cut), 0.0, NEG_INF)
    bias = jnp.concatenate([bias] * hd, axis=0)

    lane_head = _div(lax.broadcasted_iota(I32, (1, hd * dh), 1), dh)
    qf = q_ref[...].astype(F32)
    qbd = jnp.concatenate([jnp.where(lane_head == hh, qf, 0.0) for hh in range(hd)], axis=0).astype(BF16)
    kc = kc_ref[0].astype(BF16)
    vc = vc_ref[0].astype(BF16)
    s1 = _dot_t(qbd, kc) + bias[:, :past]
    s2 = _dot_t(qbd, kn_ref[...]) + bias[:, past:]
    m = jnp.maximum(jnp.max(s1, axis=1, keepdims=True), jnp.max(s2, axis=1, keepdims=True))
    p1 = jnp.exp(s1 - m)
    p2 = jnp.exp(s2 - m)
    l = jnp.sum(p1, axis=1, keepdims=True) + jnp.sum(p2, axis=1, keepdims=True)
    o_all = (_dot(p1.astype(BF16), vc) + _dot(p2.astype(BF16), vn_ref[...])) / l
    out = jnp.zeros((n_q, hd * dh), F32)
    for hh in range(hd):
        out = out + jnp.where(lane_head == hh, o_all[hh * n_q:(hh + 1) * n_q], 0.0)
    o_ref[...] = x_ref[...] + _dot(out.astype(BF16), wo_ref[...])


def _dsa_attn_sample(x, qb, qib, wi, kb, vb, kib, cache_k, cache_v, cache_ki, w_o, n_b, n_q, topk):
    t, d = x.shape
    hdm = DSA_HEADS * DSA_HEAD_DIM
    past = cache_k.shape[1]
    per = LANES // n_q
    return pl.pallas_call(
        functools.partial(_dsa_samp_kernel, n_q=n_q, topk=topk),
        grid=(n_b,),
        in_specs=[
            pl.BlockSpec((n_q, hdm), lambda b: (b, 0)),
            pl.BlockSpec((IDX_HEADS, n_q, IDX_DIM), lambda b: (0, b, 0)),
            pl.BlockSpec((n_q, IDX_HEADS), lambda b: (b, 0)),
            pl.BlockSpec((1, past, hdm), lambda b: (b, 0, 0)),
            pl.BlockSpec((1, past, hdm), lambda b: (b, 0, 0)),
            pl.BlockSpec((1, past, IDX_DIM), lambda b: (b, 0, 0)),
            pl.BlockSpec((LANES, hdm), lambda b: (b // per, 0)),
            pl.BlockSpec((LANES, hdm), lambda b: (b // per, 0)),
            pl.BlockSpec((LANES, IDX_DIM), lambda b: (b // per, 0)),
            pl.BlockSpec((n_q, d), lambda b: (b, 0)),
            _whole(w_o.shape),
        ],
        out_specs=pl.BlockSpec((n_q, d), lambda b: (b, 0)),
        out_shape=jax.ShapeDtypeStruct((t, d), F32),
        compiler_params=_params("parallel"),
        name="dsa_attn_sample",
    )(qb, qib, wi, cache_k.reshape(n_b, past, hdm), cache_v.reshape(n_b, past, hdm), cache_ki, kb, vb, kib, x, w_o)


def kernel(x_prompt, x_sample, cache_mla_ckv, cache_mla_krope, cache_dsa_k, cache_dsa_v, cache_dsa_kidx, norm_mix, norm_ffn, norm_final, mla_w_dq, mla_g_q, mla_w_uq, mla_w_dkv, mla_g_kv, mla_w_ukv, mla_w_o, cmlp_w_in, cmlp_ln_g, cmlp_ln_b, cmlp_w_s, cmlp_b_s, cmlp_w_out, dsa_w_qkv, dsa_w_o, dsa_w_qidx, dsa_w_kidx, dsa_g_kidx, dsa_w_widx, ffn_w_in, ffn_w_out):
    batch, seq, d = x_prompt.shape
    n_b, n_q, _ = x_sample.shape
    past = cache_mla_ckv.shape[2]
    depth = norm_mix.shape[0]
    xp = x_prompt.reshape(batch * seq, d)
    xs = x_sample.reshape(n_b * n_q, d)
    tab_p = _rope_tables(jnp.arange(seq))
    tab_s = tuple(jnp.tile(a, (n_b, 1)) for a in _rope_tables(past + jnp.arange(n_q)))
    cast = lambda a: a.astype(BF16)
    outs = {k: [] for k in ("ckv_p", "kr_p", "ckv_s", "kr_s", "cv_s", "dk_p", "dv_p", "di_p", "dk_s", "dv_s", "di_s")}
    for i in range(depth):
        kind, j = i % 3, i // 3
        if kind == 0:
            wts = _mla_weights(mla_w_dq[j], mla_w_uq[j], mla_w_dkv[j], mla_w_ukv[j], mla_w_o[j])
            qt, kh, vt, ckv, kr, kn2 = _mla_proj_t(xp, norm_mix[i], wts, mla_g_q[j], mla_g_kv[j], tab_p, seq)
            xp = _attn_t(xp, qt, kh, vt, kn2, None, wts["wot"], batch, seq, 256, "mla_attn_prompt")
            outs["ckv_p"].append(ckv.reshape(batch, seq, -1)); outs["kr_p"].append(kr.reshape(batch, seq, -1))
            ql, qr, ckv, kr, ckvb, krb = _mla_proj(xs, norm_mix[i], wts, mla_g_q[j], mla_g_kv[j], tab_s)
            xs = _mla_attn_sample(xs, ql, qr, ckvb, krb, cache_mla_ckv[j], cache_mla_krope[j], wts, n_b, n_q)
            outs["ckv_s"].append(ckv.reshape(n_b, n_q, -1)); outs["kr_s"].append(kr.reshape(n_b, n_q, -1))
        elif kind == 1:
            w_in, w_out = cast(cmlp_w_in[j]), cast(cmlp_w_out[j])
            xp, _ = _cmlp(xp, norm_mix[i], w_in, cmlp_ln_g[j], cmlp_ln_b[j], cmlp_w_s[j], cmlp_b_s[j], w_out,
                          min(seq, CMLP_CHUNK), False)
            xs, v_s = _cmlp(xs, norm_mix[i], w_in, cmlp_ln_g[j], cmlp_ln_b[j], cmlp_w_s[j], cmlp_b_s[j], w_out,
                            min(n_q, CMLP_CHUNK), True)
            outs["cv_s"].append(v_s.reshape(n_b, n_q, -1))
        else:
            hdm = DSA_HEADS * DSA_HEAD_DIM
            w_qkv, w_qidx, w_o = cast(dsa_w_qkv[j]), cast(dsa_w_qidx[j]), cast(dsa_w_o[j])
            w_kidx, w_widx = cast(dsa_w_kidx[j]), cast(dsa_w_widx[j])
            zpad = lambda n: jnp.zeros((d, n), BF16)
            wt = dict(wkv=w_qkv[:, hdm:], wqt=w_qkv[:, :hdm].T, wvt=w_qkv[:, 2 * hdm:].T, wqit=w_qidx.T,
                      wkw=jnp.concatenate([w_kidx, zpad(LANES - IDX_DIM)], axis=1),
                      wwt=jnp.concatenate([w_widx, zpad(BF16_ROWS - IDX_HEADS)], axis=1).T)
            hshape = (DSA_HEADS, DSA_HEAD_DIM)
            k, v, ki, qt, kh, vt, qit, kib, wit, kn2 = _dsa_proj_t(xp, norm_mix[i], wt, dsa_g_kidx[j])
            bias = _dsa_select(qit, wit, kib, batch, seq, min(TOPK_MAX, seq // 4))
            xp = _attn_t(xp, qt, kh, vt, kn2, bias, w_o.T, batch, seq, 512, "dsa_attn_prompt")
            outs["dk_p"].append(k.reshape((batch, seq) + hshape)); outs["dv_p"].append(v.reshape((batch, seq) + hshape))
            outs["di_p"].append(ki.reshape(batch, seq, -1))
            w_kw = jnp.concatenate([w_kidx, w_widx, zpad(LANES - IDX_DIM - IDX_HEADS)], axis=1)
            k, v, ki, qb, kb, vb, qib, kib, wi = _dsa_proj(xs, norm_mix[i], w_qkv, w_qidx, w_kw, dsa_g_kidx[j])
            xs = _dsa_attn_sample(xs, qb, qib, wi, kb, vb, kib, cache_dsa_k[j], cache_dsa_v[j], cache_dsa_kidx[j], w_o,
                                  n_b, n_q, min(TOPK_MAX, (past + n_q) // 4))
            outs["dk_s"].append(k.reshape((n_b, n_q) + hshape)); outs["dv_s"].append(v.reshape((n_b, n_q) + hshape))
            outs["di_s"].append(ki.reshape(n_b, n_q, -1))
        w_in, w_out = cast(ffn_w_in[i]), cast(ffn_w_out[i])
        final = i == depth - 1
        xp = _ffn(xp, norm_ffn[i], w_in, w_out, norm_final, final)
        xs = _ffn(xs, norm_ffn[i], w_in, w_out, norm_final, final)
    st = lambda name: jnp.stack(outs[name])
    return (xp.reshape(batch, seq, d), xs.reshape(n_b, n_q, d),
            st("ckv_p"), st("kr_p"), st("ckv_s"), st("kr_s"), st("cv_s"),
            st("dk_p"), st("dv_p"), st("di_p"), st("dk_s"), st("dv_s"), st("di_s"))
```

```python
import functools

import numpy as np
import jax
import jax.numpy as jnp
from jax import lax
from jax.experimental import pallas as pl
from jax.experimental.pallas import tpu as pltpu

F32, BF16, I32 = jnp.float32, jnp.bfloat16, jnp.int32

CHUNK = 64
EPS = 1e-6
MLA_HEADS, MLA_Q_LORA, MLA_KV_LORA, MLA_NOPE, MLA_ROPE, MLA_V = 16, 512, 256, 64, 32, 64
ROPE_BASE = 10000.0
MLA_SCALE = (MLA_NOPE + MLA_ROPE) ** -0.5
CMLP_CHUNK, CMLP_WIDTH, CMLP_GROUPS = 128, 2048, 8
DSA_HEADS, DSA_HEAD_DIM = 16, 64
DSA_SCALE = DSA_HEAD_DIM ** -0.5
IDX_HEADS, IDX_DIM = 8, 64
IDX_SCALE = IDX_DIM ** -0.5
TOPK_MAX = 256

LANES = 128
MXU_TILE = 256
BF16_ROWS = 16
VMEM_LIMIT = 52 * 1024 * 1024
NEG_INF = float("-inf")
INT_MIN = -2 ** 31
KEY_NEG_INF = -2139095041
LOG2E = float(np.log2(np.e))
KEY_NORM_SLACK = 1.01
LAG_MARGIN = 100.0
HEAD_V = 64
V_ROWS = HEAD_V + BF16_ROWS


def _dot(a, b):
    return jnp.dot(a, b, preferred_element_type=F32)


def _dot_t(a, b):
    return lax.dot_general(a, b, (((1,), (1,)), ((), ())), preferred_element_type=F32)


def _rms(x, g):
    return x * lax.rsqrt(jnp.mean(x * x, axis=-1, keepdims=True) + EPS) * g


def _log2(n):
    assert n > 0 and n & (n - 1) == 0, n
    return n.bit_length() - 1


def _div(x, n):
    return lax.shift_right_logical(x, jnp.int32(_log2(n)))


def _mod(x, n):
    assert n & (n - 1) == 0, n
    return x & (n - 1)


def _params(*sem):
    return pltpu.CompilerParams(dimension_semantics=sem, vmem_limit_bytes=VMEM_LIMIT)


def _whole(shape):
    nd = len(shape)
    return pl.BlockSpec(shape, lambda *_: (0,) * nd)


def _ffn_kernel(x_ref, g_ref, wg_ref, wu_ref, wo_ref, gf_ref, o_ref, act_scr, *, final):
    x = x_ref[...]
    h = _rms(x, g_ref[...]).astype(BF16)
    f = wo_ref.shape[0]
    chunk = MXU_TILE if f % MXU_TILE == 0 else f
    nc = f // chunk
    nxt = (_dot(h, wg_ref[:, 0:chunk]), _dot(h, wu_ref[:, 0:chunk]))
    for c in range(nc):
        gate, up = nxt
        if c + 1 < nc:
            lo = (c + 1) * chunk
            nxt = (_dot(h, wg_ref[:, lo:lo + chunk]), _dot(h, wu_ref[:, lo:lo + chunk]))
        act_scr[:, c * chunk:(c + 1) * chunk] = (jax.nn.silu(gate) * up).astype(BF16)
    y = x + _dot(act_scr[...], wo_ref[...])
    if final:
        y = _rms(y, gf_ref[...])
    o_ref[...] = y


def _ffn(x, g, w_in, w_out, g_final, final):
    t, d = x.shape
    f = w_out.shape[0]
    tm = min(512, t)
    once = pl.Buffered(1)
    return pl.pallas_call(
        functools.partial(_ffn_kernel, final=final),
        grid=(t // tm,),
        in_specs=[
            pl.BlockSpec((tm, d), lambda i: (i, 0)),
            _whole((1, d)),
            pl.BlockSpec((d, f), lambda i: (0, 0), pipeline_mode=once),
            pl.BlockSpec((d, f), lambda i: (0, 1), pipeline_mode=once),
            pl.BlockSpec((f, d), lambda i: (0, 0), pipeline_mode=once),
            _whole((1, d)),
        ],
        out_specs=pl.BlockSpec((tm, d), lambda i: (i, 0)),
        out_shape=jax.ShapeDtypeStruct((t, d), F32),
        scratch_shapes=[pltpu.VMEM((tm, f), BF16)],
        compiler_params=_params("parallel"),
        name="ffn",
    )(x, g.reshape(1, d), w_in, w_in, w_out, g_final.reshape(1, d))


def _gelu(x):
    return 0.5 * x * (1.0 + lax.erf(x * np.float32(np.sqrt(0.5))))


def _layernorm(x, g, b):
    mu = jnp.mean(x, axis=-1, keepdims=True)
    xc = x - mu
    return xc * lax.rsqrt(jnp.mean(xc * xc, axis=-1, keepdims=True) + EPS) * g + b


def _cmlp_kernel(x_ref, g_ref, win_ref, lng_ref, lnb_ref, ws_ref, bs_ref, wout_ref, *rest, n_rows, write_v):
    if write_v:
        o_ref, v_ref, vb_scr, gated_scr = rest
    else:
        o_ref, vb_scr, gated_scr = rest
    tm = x_ref.shape[0]
    w = CMLP_WIDTH
    gw = w // CMLP_GROUPS
    c = CMLP_CHUNK
    x = x_ref[...]
    h = _rms(x, g_ref[...]).astype(BF16)
    v = _layernorm(_gelu(_dot(h, win_ref[:, w:])), lng_ref[...], lnb_ref[...])
    if write_v:
        v_ref[...] = v
    vb_scr[...] = v.astype(BF16)
    r_i = lax.broadcasted_iota(I32, (c, c), 0)
    c_i = lax.broadcasted_iota(I32, (c, c), 1)
    keep = jnp.where(c_i >= r_i - _mod(r_i, n_rows), jnp.where(c_i <= r_i, 1, 0), 0) > 0
    u_next = _dot(h, win_ref[:, 0:gw])
    for g in range(CMLP_GROUPS):
        lo, hi = g * gw, (g + 1) * gw
        u_raw = u_next
        if g + 1 < CMLP_GROUPS:
            u_next = _dot(h, win_ref[:, hi:hi + gw])
        wg = jnp.where(keep, ws_ref[g], 0.0).astype(BF16)
        bias = bs_ref[:, g:g + 1]
        mixed = jnp.concatenate(
            [_dot(wg, vb_scr[k * c:(k + 1) * c, lo:hi]) + bias for k in range(tm // c)], axis=0)
        gated_scr[:, lo:hi] = (_gelu(u_raw) * mixed).astype(BF16)
    o_ref[...] = x + _dot(gated_scr[...], wout_ref[...])


def _cmlp(x, g, w_in, ln_g, ln_b, w_s, b_s, w_out, n_rows, write_v):
    t, d = x.shape
    w = CMLP_WIDTH
    c = CMLP_CHUNK
    tm = min(512, t)
    rep = c // n_rows
    ws_t = jnp.tile(w_s[:, :n_rows, :n_rows], (1, rep, rep))
    bs_t = jnp.tile(b_s[:, :n_rows].T, (rep, 1))
    out_shape = [jax.ShapeDtypeStruct((t, d), F32)]
    out_specs = [pl.BlockSpec((tm, d), lambda i: (i, 0))]
    if write_v:
        out_shape.append(jax.ShapeDtypeStruct((t, w), F32))
        out_specs.append(pl.BlockSpec((tm, w), lambda i: (i, 0)))
    res = pl.pallas_call(
        functools.partial(_cmlp_kernel, n_rows=n_rows, write_v=write_v),
        grid=(t // tm,),
        in_specs=[
            pl.BlockSpec((tm, d), lambda i: (i, 0)),
            _whole((1, d)), _whole((d, 2 * w)), _whole((1, w)), _whole((1, w)),
            _whole((CMLP_GROUPS, c, c)), _whole((c, CMLP_GROUPS)), _whole((w, d)),
        ],
        out_specs=out_specs,
        out_shape=out_shape,
        scratch_shapes=[pltpu.VMEM((tm, w), BF16), pltpu.VMEM((tm, w), BF16)],
        compiler_params=_params("parallel"),
        name="cmlp",
    )(x, g.reshape(1, d), w_in, ln_g.reshape(1, w), ln_b.reshape(1, w), ws_t, bs_t, w_out)
    return res if write_v else (res[0], None)


def _attn_t_kernel(qi_ref, kj_ref, last_ref, nsub_ref, qt_ref, k_ref, vt_ref, *rest, tq, tk, sub, use_bias):
    if use_bias:
        bias_ref, kn_ref, x_ref, wot_ref, o_ref, m_scr, pend_scr, qn_scr, acc_scr, cat_scr = rest
    else:
        kn_ref, x_ref, wot_ref, o_ref, m_scr, pend_scr, qn_scr, acc_scr, cat_scr = rest
    p = pl.program_id(1)
    qi, kj = qi_ref[p], kj_ref[p]
    hd = qt_ref.shape[0]

    @pl.when(kj == 0)
    def _():
        m_scr[...] = jnp.full_like(m_scr, NEG_INF)
        pend_scr[...] = jnp.ones_like(pend_scr)
        acc_scr[...] = jnp.zeros_like(acc_scr)
        for hh in range(hd):
            q = qt_ref[hh].astype(F32)
            qn_scr[hh] = jnp.sqrt(jnp.sum(q * q, axis=0, keepdims=True)) * kn_ref[0, hh:hh + 1, 0:1]

    def step(off, mask, lagged):
        keys = lambda hh: k_ref[hh, pl.ds(off, sub), :]
        vals = lambda hh: vt_ref[hh, :, pl.ds(off, sub)]
        s_next = _dot(keys(0), qt_ref[0])
        pend = None
        for hh in range(hd):
            s = s_next
            if hh + 1 < hd:
                s_next = _dot(keys(hh + 1), qt_ref[hh + 1])
            if mask is not None:
                s = s + mask
            m_prev = m_scr[hh]
            m_new = jnp.maximum(m_prev, jnp.max(s, axis=0, keepdims=True))
            if lagged:
                scale = pend_scr[hh]
                pe = jnp.exp2(s - m_prev).astype(BF16)
                pend_scr[hh] = jnp.exp2(m_prev - m_new)
            else:
                m_safe = jnp.where(m_new == NEG_INF, 0.0, m_new)
                scale = pend_scr[hh] * jnp.exp2(m_prev - m_safe)
                pe = jnp.exp2(s - m_safe).astype(BF16)
                pend_scr[hh] = jnp.ones_like(m_prev)
            m_scr[hh] = m_new
            if pend is not None:
                ph, pa, pp = pend
                acc_scr[ph] = pa * acc_scr[ph] + _dot(vals(ph), pp)
            pend = (hh, scale, pe)
        ph, pa, pp = pend
        acc_scr[ph] = pa * acc_scr[ph] + _dot(vals(ph), pp)

    def sub_body(j, carry):
        off = pl.multiple_of(j * sub, sub)
        excess = qn_scr[0] - m_scr[0]
        for hh in range(1, hd):
            excess = jnp.maximum(excess, qn_scr[hh] - m_scr[hh])
        lag_ok = jnp.max(excess) <= LAG_MARGIN
        if use_bias:
            bias = bias_ref[pl.ds(off, sub), :].astype(F32)

            @pl.when(lag_ok)
            def _():
                step(off, bias, True)

            @pl.when(jnp.logical_not(lag_ok))
            def _():
                step(off, bias, False)
        else:
            base = kj * tk + off
            needs_mask = _div(base + sub - 1, CHUNK) > _div(qi * tq, CHUNK)

            def chunk_mask():
                k_chunk = _div(base + lax.broadcasted_iota(I32, (sub, 1), 0), CHUNK)
                q_chunk = _div(qi * tq + lax.broadcasted_iota(I32, (1, tq), 1), CHUNK)
                return jnp.where(k_chunk <= q_chunk, 0.0, NEG_INF)

            @pl.when(jnp.logical_not(lag_ok))
            def _():
                step(off, chunk_mask(), False)

            @pl.when(jnp.logical_and(lag_ok, needs_mask))
            def _():
                step(off, chunk_mask(), True)

            @pl.when(jnp.logical_and(lag_ok, jnp.logical_not(needs_mask)))
            def _():
                step(off, None, True)
        return carry

    lax.fori_loop(0, nsub_ref[p], sub_body, 0)

    @pl.when(last_ref[p] == 1)
    def _():
        for hh in range(hd):
            a = acc_scr[hh]
            cat_scr[hh * HEAD_V:(hh + 1) * HEAD_V, :] = (a[:HEAD_V] / a[HEAD_V:HEAD_V + 1]).astype(BF16)
        o_ref[...] = x_ref[...] + _dot(wot_ref[...], cat_scr[...]).T


def _causal_pairs(nq, tq, tk, sub):
    qi, kj, last, nsub = [], [], [], []
    per = tk // sub
    for q in range(nq):
        vis_sub = -(-((q + 1) * tq) // sub)
        nvis = -(-vis_sub // per)
        for k in range(nvis):
            qi.append(q); kj.append(k); last.append(int(k == nvis - 1)); nsub.append(min(per, vis_sub - k * per))
    as_arr = lambda v: jnp.asarray(np.array(v, np.int32))
    return as_arr(qi), as_arr(kj), as_arr(last), as_arr(nsub)


def _attn_t(x, qt, kh, vt, kn2, bias, wot, batch, seq, sub, name):
    t, d = x.shape
    hd, dk, _ = qt.shape
    tq, tk, sub = min(512, seq), min(1024, seq), min(sub, seq)
    nq, nk = seq // tq, seq // tk
    qi, kj, last, nsub = _causal_pairs(nq, tq, tk, sub)
    use_bias = bias is not None
    in_specs = [
        pl.BlockSpec((hd, dk, tq), lambda b, p, qi, kj, la, ns: (0, 0, b * nq + qi[p])),
        pl.BlockSpec((hd, tk, dk), lambda b, p, qi, kj, la, ns: (0, b * nk + kj[p], 0)),
        pl.BlockSpec((hd, V_ROWS, tk), lambda b, p, qi, kj, la, ns: (0, 0, b * nk + kj[p])),
    ]
    args = [qt, kh, vt]
    if use_bias:
        in_specs.append(pl.BlockSpec((tk, tq), lambda b, p, qi, kj, la, ns: (kj[p], b * nq + qi[p])))
        args.append(bias)
    kn = jnp.sqrt(jnp.max(kn2[:, :, 0].reshape(batch, -1, hd), axis=1)) * KEY_NORM_SLACK
    in_specs.append(pl.BlockSpec((1, hd, LANES), lambda b, p, qi, kj, la, ns: (b, 0, 0)))
    args.append(jnp.broadcast_to(kn[:, :, None], (batch, hd, LANES)))
    in_specs += [
        pl.BlockSpec((tq, d), lambda b, p, qi, kj, la, ns: (b * nq + qi[p], 0)),
        pl.BlockSpec(wot.shape, lambda b, p, qi, kj, la, ns: (0, 0)),
    ]
    args += [x, wot]
    grid_spec = pltpu.PrefetchScalarGridSpec(
        num_scalar_prefetch=4,
        grid=(batch, int(qi.shape[0])),
        in_specs=in_specs,
        out_specs=pl.BlockSpec((tq, d), lambda b, p, qi, kj, la, ns: (b * nq + qi[p], 0)),
        scratch_shapes=[
            pltpu.VMEM((hd, 1, tq), F32), pltpu.VMEM((hd, 1, tq), F32), pltpu.VMEM((hd, 1, tq), F32),
            pltpu.VMEM((hd, V_ROWS, tq), F32), pltpu.VMEM((hd * HEAD_V, tq), BF16),
        ],
    )
    return pl.pallas_call(
        functools.partial(_attn_t_kernel, tq=tq, tk=tk, sub=sub, use_bias=use_bias),
        grid_spec=grid_spec,
        out_shape=jax.ShapeDtypeStruct((t, d), F32),
        compiler_params=_params("parallel", "arbitrary"),
        name=name,
    )(qi, kj, last, nsub, *args)


def _rope_tables(pos):
    half = MLA_ROPE // 2
    inv = ROPE_BASE ** (-jnp.arange(half, dtype=F32) / half)
    ang = pos.astype(F32)[:, None] * inv[None, :]
    cos, sin = jnp.cos(ang), jnp.sin(ang)
    cos_k = jnp.concatenate([cos, cos], axis=1)
    sin_k = jnp.concatenate([-sin, sin], axis=1)
    return jnp.tile(cos_k, (1, MLA_HEADS)), jnp.tile(sin_k, (1, MLA_HEADS)), cos_k, sin_k


def _swap_halves(w, group):
    shp = w.shape
    wr = w.reshape(shp[:-1] + (shp[-1] // group, 2, group // 2))
    return wr[..., ::-1, :].reshape(shp)


def _mla_weights(w_dq, w_uq, w_dkv, w_ukv, w_o):
    hd = MLA_HEADS
    cast = lambda a: a.astype(BF16)
    w_uq, w_dkv, w_ukv = cast(w_uq), cast(w_dkv), cast(w_ukv)
    wqn = w_uq[:, :, :MLA_NOPE].reshape(MLA_Q_LORA, hd * MLA_NOPE)
    wqr = w_uq[:, :, MLA_NOPE:].reshape(MLA_Q_LORA, hd * MLA_ROPE)
    wqs = _swap_halves(wqr, MLA_ROPE)
    wkc, wkr = w_dkv[:, :MLA_KV_LORA], w_dkv[:, MLA_KV_LORA:]
    wks = _swap_halves(wkr, MLA_ROPE)
    wuk = jnp.transpose(w_ukv[:, :, :MLA_NOPE], (1, 2, 0))
    zeros = jnp.zeros_like(wuk)
    even = jnp.concatenate([wuk, zeros], axis=1)
    odd = jnp.concatenate([zeros, wuk], axis=1)
    wuk2 = jnp.where((jnp.arange(hd) % 2 == 0)[:, None, None], even, odd)
    wuv = jnp.transpose(w_ukv[:, :, MLA_NOPE:], (1, 0, 2))
    d = w_dkv.shape[0]
    pad_r = lambda a: jnp.concatenate([a, jnp.zeros((d, LANES - MLA_ROPE), BF16)], axis=1)
    wukp = jnp.concatenate([jnp.zeros((MLA_KV_LORA, hd, MLA_ROPE), BF16), w_ukv[:, :, :MLA_NOPE],
                            jnp.zeros((MLA_KV_LORA, hd, LANES - MLA_ROPE - MLA_NOPE), BF16)], axis=2)
    return dict(wdq=cast(w_dq), wqn=wqn, wqr=wqr, wqs=wqs, wkc=wkc, wkr=wkr, wks=wks, wuk=wuk2, wuv=wuv, wo=cast(w_o),
                wqnt=wqn.T, wqrt=wqr.T, wqst=wqs.T, wkrp=pad_r(wkr), wksp=pad_r(wks),
                wukp=wukp.reshape(MLA_KV_LORA, hd * LANES), wuvt=wuv.transpose(0, 2, 1).reshape(hd * MLA_V, MLA_KV_LORA),
                wot=cast(w_o).T)


def _mla_proj_kernel(x_ref, gm_ref, wdq_ref, gq_ref, wqn_ref, wqr_ref, wqs_ref, wkc_ref, wkr_ref, wks_ref,
                     gkv_ref, wuk_ref, cq_ref, sq_ref, ck_ref, sk_ref,
                     ql_ref, qr_ref, ckv_ref, kr_ref, ckvb_ref, krb_ref):
    h = _rms(x_ref[...], gm_ref[...]).astype(BF16)
    cq = _rms(_dot(h, wdq_ref[...]), gq_ref[...]).astype(BF16)
    qn = _dot(cq, wqn_ref[...])
    qr = ((_dot(cq, wqr_ref[...]) * cq_ref[...] + _dot(cq, wqs_ref[...]) * sq_ref[...]) * MLA_SCALE).astype(BF16)
    for hh in range(MLA_HEADS):
        pair = qn[:, (hh // 2) * LANES:(hh // 2 + 1) * LANES].astype(BF16)
        ql_ref[hh] = (_dot(pair, wuk_ref[hh]) * MLA_SCALE).astype(BF16)
        qr_ref[hh] = qr[:, hh * MLA_ROPE:(hh + 1) * MLA_ROPE]
    ckv = _rms(_dot(h, wkc_ref[...]), gkv_ref[...])
    ckv_ref[...] = ckv
    ckvb_ref[...] = ckv.astype(BF16)
    kr = _dot(h, wkr_ref[...]) * ck_ref[...] + _dot(h, wks_ref[...]) * sk_ref[...]
    kr_ref[...] = kr
    krb_ref[...] = kr.astype(BF16)


def _mla_proj(x, g_mix, wts, g_q, g_kv, tables):
    t, d = x.shape
    tm = min(256, t)
    cos_q, sin_q, cos_k, sin_k = tables
    nrep = cos_q.shape[0] // tm
    tab = lambda wdt: pl.BlockSpec((tm, wdt), lambda i: (i % nrep, 0))
    hd = MLA_HEADS
    row = lambda wdt: pl.BlockSpec((tm, wdt), lambda i: (i, 0))
    hm = lambda wdt: pl.BlockSpec((hd, tm, wdt), lambda i: (0, i, 0))
    return pl.pallas_call(
        _mla_proj_kernel,
        grid=(t // tm,),
        in_specs=[
            row(d), _whole((1, d)), _whole(wts["wdq"].shape), _whole((1, MLA_Q_LORA)),
            _whole(wts["wqn"].shape), _whole(wts["wqr"].shape), _whole(wts["wqs"].shape),
            _whole(wts["wkc"].shape), _whole(wts["wkr"].shape), _whole(wts["wks"].shape),
            _whole((1, MLA_KV_LORA)), _whole(wts["wuk"].shape),
            tab(hd * MLA_ROPE), tab(hd * MLA_ROPE), tab(MLA_ROPE), tab(MLA_ROPE),
        ],
        out_specs=[hm(MLA_KV_LORA), hm(MLA_ROPE), row(MLA_KV_LORA), row(MLA_ROPE), row(MLA_KV_LORA), row(MLA_ROPE)],
        out_shape=[
            jax.ShapeDtypeStruct((hd, t, MLA_KV_LORA), BF16), jax.ShapeDtypeStruct((hd, t, MLA_ROPE), BF16),
            jax.ShapeDtypeStruct((t, MLA_KV_LORA), F32), jax.ShapeDtypeStruct((t, MLA_ROPE), F32),
            jax.ShapeDtypeStruct((t, MLA_KV_LORA), BF16), jax.ShapeDtypeStruct((t, MLA_ROPE), BF16),
        ],
        compiler_params=_params("parallel"),
        name="mla_proj",
    )(x, g_mix.reshape(1, d), wts["wdq"], g_q.reshape(1, -1), wts["wqn"], wts["wqr"], wts["wqs"],
      wts["wkc"], wts["wkr"], wts["wks"], g_kv.reshape(1, -1), wts["wuk"], cos_q, sin_q, cos_k, sin_k)


def _mla_proj_t_kernel(x_ref, gm_ref, wdq_ref, gq_ref, wqnt_ref, wqrt_ref, wqst_ref, wkc_ref, wkrp_ref, wksp_ref,
                       gkv_ref, wukp_ref, wuvt_ref, cqt_ref, sqt_ref, ckp_ref, skp_ref,
                       qt_ref, kh_ref, vt_ref, ckv_ref, kr_ref, kn2_ref):
    tm = x_ref.shape[0]
    hd = MLA_HEADS
    qscale = MLA_SCALE * LOG2E
    h = _rms(x_ref[...], gm_ref[...]).astype(BF16)
    cq_raw = _dot(h, wdq_ref[...])
    ckv_raw = _dot(h, wkc_ref[...])
    kr_a = _dot(h, wkrp_ref[...])
    kr_b = _dot(h, wksp_ref[...])
    cq = _rms(cq_raw, gq_ref[...]).astype(BF16)
    qnt_raw = _dot_t(wqnt_ref[...], cq)
    qrt_a = _dot_t(wqrt_ref[...], cq)
    qrt_b = _dot_t(wqst_ref[...], cq)
    ckv = _rms(ckv_raw, gkv_ref[...])
    ckv_ref[...] = ckv
    cb = ckv.astype(BF16)
    kn = _dot(cb, wukp_ref[...])
    vt = _dot_t(wuvt_ref[...], cb).astype(BF16)
    qnt = (qnt_raw * qscale).astype(BF16)
    rot = qrt_a.reshape(hd, MLA_ROPE, tm) * cqt_ref[...][None] + qrt_b.reshape(hd, MLA_ROPE, tm) * sqt_ref[...][None]
    qrt = (rot * qscale).reshape(hd * MLA_ROPE, tm).astype(BF16)
    zeros = jnp.zeros((LANES - MLA_ROPE - MLA_NOPE, tm), BF16)
    krp = kr_a * ckp_ref[...] + kr_b * skp_ref[...]
    kr_ref[...] = krp[:, :MLA_ROPE]
    ones = jnp.ones((BF16_ROWS, tm), BF16)
    for hh in range(hd):
        qt_ref[hh, 0:MLA_ROPE, :] = qrt[hh * MLA_ROPE:(hh + 1) * MLA_ROPE]
        qt_ref[hh, MLA_ROPE:MLA_ROPE + MLA_NOPE, :] = qnt[hh * MLA_NOPE:(hh + 1) * MLA_NOPE]
        qt_ref[hh, MLA_ROPE + MLA_NOPE:, :] = zeros
        key = kn[:, hh * LANES:(hh + 1) * LANES] + krp
        kh_ref[hh] = key.astype(BF16)
        kn2_ref[0, hh:hh + 1, :] = jnp.broadcast_to(
            jnp.max(jnp.sum(key * key, axis=1, keepdims=True), axis=0, keepdims=True), (1, LANES))
        vt_ref[hh, 0:MLA_V, :] = vt[hh * MLA_V:(hh + 1) * MLA_V]
        vt_ref[hh, MLA_V:, :] = ones


def _mla_proj_t(x, g_mix, wts, g_q, g_kv, tables, seq):
    t, d = x.shape
    tm = min(256, seq)
    hd = MLA_HEADS
    cos_q, sin_q, cos_k, sin_k = tables
    pad = lambda a: jnp.concatenate([a, jnp.zeros((seq, LANES - MLA_ROPE), F32)], axis=1)
    nrep = seq // tm
    row = lambda wdt: pl.BlockSpec((tm, wdt), lambda i: (i, 0))
    names = ("wdq", "wqnt", "wqrt", "wqst", "wkc", "wkrp", "wksp", "wukp", "wuvt")
    w = {n: wts[n] for n in names}
    return pl.pallas_call(
        _mla_proj_t_kernel,
        grid=(t // tm,),
        in_specs=[
            row(d), _whole((1, d)), _whole(w["wdq"].shape), _whole((1, MLA_Q_LORA)),
            _whole(w["wqnt"].shape), _whole(w["wqrt"].shape), _whole(w["wqst"].shape),
            _whole(w["wkc"].shape), _whole(w["wkrp"].shape), _whole(w["wksp"].shape),
            _whole((1, MLA_KV_LORA)), _whole(w["wukp"].shape), _whole(w["wuvt"].shape),
            pl.BlockSpec((MLA_ROPE, tm), lambda i: (0, i % nrep)),
            pl.BlockSpec((MLA_ROPE, tm), lambda i: (0, i % nrep)),
            pl.BlockSpec((tm, LANES), lambda i: (i % nrep, 0)),
            pl.BlockSpec((tm, LANES), lambda i: (i % nrep, 0)),
        ],
        out_specs=[
            pl.BlockSpec((hd, LANES, tm), lambda i: (0, 0, i)),
            pl.BlockSpec((hd, tm, LANES), lambda i: (0, i, 0)),
            pl.BlockSpec((hd, V_ROWS, tm), lambda i: (0, 0, i)),
            row(MLA_KV_LORA), row(MLA_ROPE),
            pl.BlockSpec((1, hd, LANES), lambda i: (i, 0, 0)),
        ],
        out_shape=[
            jax.ShapeDtypeStruct((hd, LANES, t), BF16), jax.ShapeDtypeStruct((hd, t, LANES), BF16),
            jax.ShapeDtypeStruct((hd, V_ROWS, t), BF16),
            jax.ShapeDtypeStruct((t, MLA_KV_LORA), F32), jax.ShapeDtypeStruct((t, MLA_ROPE), F32),
            jax.ShapeDtypeStruct((t // tm, hd, LANES), F32),
        ],
        compiler_params=_params("parallel"),
        name="mla_proj_t",
    )(x, g_mix.reshape(1, d), w["wdq"], g_q.reshape(1, -1), w["wqnt"], w["wqrt"], w["wqst"],
      w["wkc"], w["wkrp"], w["wksp"], g_kv.reshape(1, -1), w["wukp"], w["wuvt"],
      cos_k.T, sin_k.T, pad(cos_k), pad(sin_k))


def _mla_epilogue(o_lat, x_ref, wuv_ref, wo_ref, o_ref, cat_scr, tq):
    ob = o_lat.astype(BF16)
    for hh in range(MLA_HEADS):
        cat_scr[:, hh * MLA_V:(hh + 1) * MLA_V] = _dot(ob[hh * tq:(hh + 1) * tq], wuv_ref[hh]).astype(BF16)
    o_ref[...] = x_ref[...] + _dot(cat_scr[...], wo_ref[...])


def _mla_samp_kernel(ql_ref, qr_ref, cc_ref, cr_ref, cn_ref, rn_ref, x_ref, wuv_ref, wo_ref, o_ref, cat_scr, *, n_q):
    b = pl.program_id(0)
    hd = MLA_HEADS
    ql = ql_ref[...].reshape(hd * n_q, MLA_KV_LORA)
    qr = qr_ref[...].reshape(hd * n_q, MLA_ROPE)
    cc = cc_ref[0].astype(BF16)
    cr = cr_ref[0].astype(BF16)
    cn = cn_ref[...]
    s1 = _dot_t(ql, cc) + _dot_t(qr, cr)
    s2 = _dot_t(ql, cn) + _dot_t(qr, rn_ref[...])
    own = _div(lax.broadcasted_iota(I32, (1, LANES), 1), n_q) == _mod(b, LANES // n_q)
    s2 = jnp.where(own, s2, NEG_INF)
    m = jnp.maximum(jnp.max(s1, axis=1, keepdims=True), jnp.max(s2, axis=1, keepdims=True))
    p1 = jnp.exp(s1 - m)
    p2 = jnp.exp(s2 - m)
    l = jnp.sum(p1, axis=1, keepdims=True) + jnp.sum(p2, axis=1, keepdims=True)
    o_lat = (_dot(p1.astype(BF16), cc) + _dot(p2.astype(BF16), cn)) / l
    _mla_epilogue(o_lat, x_ref, wuv_ref, wo_ref, o_ref, cat_scr, n_q)


def _mla_attn_sample(x, ql, qr, ckvb, krb, cache_c, cache_r, wts, n_b, n_q):
    t, d = x.shape
    hd = MLA_HEADS
    past = cache_c.shape[1]
    per = LANES // n_q
    return pl.pallas_call(
        functools.partial(_mla_samp_kernel, n_q=n_q),
        grid=(n_b,),
        in_specs=[
            pl.BlockSpec((hd, n_q, MLA_KV_LORA), lambda b: (0, b, 0)),
            pl.BlockSpec((hd, n_q, MLA_ROPE), lambda b: (0, b, 0)),
            pl.BlockSpec((1, past, MLA_KV_LORA), lambda b: (b, 0, 0)),
            pl.BlockSpec((1, past, MLA_ROPE), lambda b: (b, 0, 0)),
            pl.BlockSpec((LANES, MLA_KV_LORA), lambda b: (b // per, 0)),
            pl.BlockSpec((LANES, MLA_ROPE), lambda b: (b // per, 0)),
            pl.BlockSpec((n_q, d), lambda b: (b, 0)),
            _whole(wts["wuv"].shape), _whole(wts["wo"].shape),
        ],
        out_specs=pl.BlockSpec((n_q, d), lambda b: (b, 0)),
        out_shape=jax.ShapeDtypeStruct((t, d), F32),
        scratch_shapes=[pltpu.VMEM((n_q, hd * MLA_V), BF16)],
        compiler_params=_params("parallel"),
        name="mla_attn_sample",
    )(ql, qr, cache_c, cache_r, ckvb, krb, x, wts["wuv"], wts["wo"])


def _dsa_proj_kernel(x_ref, gm_ref, wqkv_ref, wqi_ref, wkw_ref, gki_ref,
                     k_ref, v_ref, ki_ref, qb_ref, kb_ref, vb_ref, qib_ref, kib_ref, wi_ref):
    hdm = DSA_HEADS * DSA_HEAD_DIM
    h = _rms(x_ref[...], gm_ref[...]).astype(BF16)
    qkv = _dot(h, wqkv_ref[...])
    q = qkv[:, :hdm] * DSA_SCALE
    k = qkv[:, hdm:2 * hdm]
    v = qkv[:, 2 * hdm:]
    k_ref[...] = pltpu.einshape("t(hd)->thd", k, h=DSA_HEADS)
    v_ref[...] = pltpu.einshape("t(hd)->thd", v, h=DSA_HEADS)
    qi = _dot(h, wqi_ref[...]) * IDX_SCALE
    kw = _dot(h, wkw_ref[...])
    ki = _rms(kw[:, :IDX_DIM], gki_ref[...])
    ki_ref[...] = ki
    kib_ref[...] = ki.astype(BF16)
    wi_ref[...] = kw[:, IDX_DIM:IDX_DIM + IDX_HEADS] * (IDX_HEADS ** -0.5)
    qb_ref[...] = q.astype(BF16)
    kb_ref[...] = k.astype(BF16)
    vb_ref[...] = v.astype(BF16)
    for hh in range(IDX_HEADS):
        qib_ref[hh] = qi[:, hh * IDX_DIM:(hh + 1) * IDX_DIM].astype(BF16)


def _dsa_proj(x, g_mix, w_qkv, w_qidx, w_kw, g_kidx):
    t, d = x.shape
    tm = min(512, t)
    hdm = DSA_HEADS * DSA_HEAD_DIM
    row = lambda wdt: pl.BlockSpec((tm, wdt), lambda i: (i, 0))
    hd3 = pl.BlockSpec((tm, DSA_HEADS, DSA_HEAD_DIM), lambda i: (i, 0, 0))
    return pl.pallas_call(
        _dsa_proj_kernel,
        grid=(t // tm,),
        in_specs=[row(d), _whole((1, d)), _whole(w_qkv.shape), _whole(w_qidx.shape), _whole(w_kw.shape),
                  _whole((1, IDX_DIM))],
        out_specs=[hd3, hd3, row(IDX_DIM), row(hdm), row(hdm), row(hdm),
                   pl.BlockSpec((IDX_HEADS, tm, IDX_DIM), lambda i: (0, i, 0)), row(IDX_DIM), row(IDX_HEADS)],
        out_shape=[
            jax.ShapeDtypeStruct((t, DSA_HEADS, DSA_HEAD_DIM), F32), jax.ShapeDtypeStruct((t, DSA_HEADS, DSA_HEAD_DIM), F32),
            jax.ShapeDtypeStruct((t, IDX_DIM), F32), jax.ShapeDtypeStruct((t, hdm), BF16),
            jax.ShapeDtypeStruct((t, hdm), BF16), jax.ShapeDtypeStruct((t, hdm), BF16),
            jax.ShapeDtypeStruct((IDX_HEADS, t, IDX_DIM), BF16), jax.ShapeDtypeStruct((t, IDX_DIM), BF16),
            jax.ShapeDtypeStruct((t, IDX_HEADS), F32),
        ],
        compiler_params=_params("parallel"),
        name="dsa_proj",
    )(x, g_mix.reshape(1, d), w_qkv, w_qidx, w_kw, g_kidx.reshape(1, IDX_DIM))


def _dsa_proj_t_kernel(x_ref, gm_ref, wkv_ref, wqt_ref, wvt_ref, wqit_ref, wkw_ref, wwt_ref, gki_ref,
                       k_ref, v_ref, ki_ref, qt_ref, kh_ref, vt_ref, qit_ref, kib_ref, wit_ref, kn2_ref):
    tm = x_ref.shape[0]
    hdm = DSA_HEADS * DSA_HEAD_DIM
    dh = DSA_HEAD_DIM
    h = _rms(x_ref[...], gm_ref[...]).astype(BF16)
    kv = _dot(h, wkv_ref[...])
    k = kv[:, :hdm]
    k_ref[...] = pltpu.einshape("t(hd)->thd", k, h=DSA_HEADS)
    v_ref[...] = pltpu.einshape("t(hd)->thd", kv[:, hdm:], h=DSA_HEADS)
    kw = _dot(h, wkw_ref[...])
    ki = _rms(kw[:, :IDX_DIM], gki_ref[...])
    ki_ref[...] = ki
    kib_ref[...] = ki.astype(BF16)
    wit_ref[...] = _dot_t(wwt_ref[...], h)[:IDX_HEADS] * (IDX_HEADS ** -0.5)
    qt = (_dot_t(wqt_ref[...], h) * (DSA_SCALE * LOG2E)).astype(BF16)
    vt = _dot_t(wvt_ref[...], h).astype(BF16)
    qit = (_dot_t(wqit_ref[...], h) * IDX_SCALE).astype(BF16)
    ones = jnp.ones((BF16_ROWS, tm), BF16)
    for hh in range(DSA_HEADS):
        qt_ref[hh] = qt[hh * dh:(hh + 1) * dh]
        key = k[:, hh * dh:(hh + 1) * dh]
        kh_ref[hh] = key.astype(BF16)
        kn2_ref[0, hh:hh + 1, :] = jnp.broadcast_to(
            jnp.max(jnp.sum(key * key, axis=1, keepdims=True), axis=0, keepdims=True), (1, LANES))
        vt_ref[hh, 0:dh, :] = vt[hh * dh:(hh + 1) * dh]
        vt_ref[hh, dh:, :] = ones
    for hh in range(IDX_HEADS):
        qit_ref[hh] = qit[hh * IDX_DIM:(hh + 1) * IDX_DIM]


def _dsa_proj_t(x, g_mix, w, g_kidx):
    t, d = x.shape
    tm = min(256, t)
    hd, dh = DSA_HEADS, DSA_HEAD_DIM
    hdm = hd * dh
    row = lambda wdt: pl.BlockSpec((tm, wdt), lambda i: (i, 0))
    names = ("wkv", "wqt", "wvt", "wqit", "wkw", "wwt")
    return pl.pallas_call(
        _dsa_proj_t_kernel,
        grid=(t // tm,),
        in_specs=[row(d), _whole((1, d))] + [_whole(w[n].shape) for n in names] + [_whole((1, IDX_DIM))],
        out_specs=[
            pl.BlockSpec((tm, hd, dh), lambda i: (i, 0, 0)), pl.BlockSpec((tm, hd, dh), lambda i: (i, 0, 0)),
            row(IDX_DIM),
            pl.BlockSpec((hd, dh, tm), lambda i: (0, 0, i)),
            pl.BlockSpec((hd, tm, dh), lambda i: (0, i, 0)),
            pl.BlockSpec((hd, V_ROWS, tm), lambda i: (0, 0, i)),
            pl.BlockSpec((IDX_HEADS, IDX_DIM, tm), lambda i: (0, 0, i)),
            row(IDX_DIM),
            pl.BlockSpec((IDX_HEADS, tm), lambda i: (0, i)),
            pl.BlockSpec((1, hd, LANES), lambda i: (i, 0, 0)),
        ],
        out_shape=[
            jax.ShapeDtypeStruct((t, hd, dh), F32), jax.ShapeDtypeStruct((t, hd, dh), F32),
            jax.ShapeDtypeStruct((t, IDX_DIM), F32),
            jax.ShapeDtypeStruct((hd, dh, t), BF16), jax.ShapeDtypeStruct((hd, t, dh), BF16),
            jax.ShapeDtypeStruct((hd, V_ROWS, t), BF16),
            jax.ShapeDtypeStruct((IDX_HEADS, IDX_DIM, t), BF16), jax.ShapeDtypeStruct((t, IDX_DIM), BF16),
            jax.ShapeDtypeStruct((IDX_HEADS, t), F32),
            jax.ShapeDtypeStruct((t // tm, hd, LANES), F32),
        ],
        compiler_params=_params("parallel"),
        name="dsa_proj_t",
    )(x, g_mix.reshape(1, d), *[w[n] for n in names], g_kidx.reshape(1, IDX_DIM))


def _sort_key(score):
    bits = lax.bitcast_convert_type(score, I32)
    return jnp.where(bits < 0, bits ^ jnp.int32(0x7FFFFFFF), bits)


def _kth_largest_key(count, topk):
    c0 = count(lambda key, idx: jnp.where(key >= 0, 1, 0))
    t0 = jnp.where(c0 >= topk, jnp.int32(0), jnp.int32(INT_MIN))

    def bit_body(i, t):
        cand = t + lax.shift_left(jnp.int32(1), jnp.int32(30) - i)
        c = count(lambda key, idx: jnp.where(key >= cand, 1, 0))
        return jnp.where(c >= topk, cand, t)

    return lax.fori_loop(0, 31, bit_body, t0)


def _radix_search(accept, start, nbits):
    t = start
    if nbits % 2:
        cand = t + jnp.int32(1 << (nbits - 1))
        t = jnp.where(accept(cand), cand, t)
        nbits -= 1

    def body(i, t):
        unit = lax.shift_left(jnp.int32(1), jnp.int32(nbits - 2) - 2 * i)
        a1, a2, a3 = accept(t + unit), accept(t + 2 * unit), accept(t + 3 * unit)
        return t + jnp.where(a3, 3, jnp.where(a2, 2, jnp.where(a1, 1, 0))) * unit

    return lax.fori_loop(0, nbits // 2, body, t)


def _select(key, idx, thr, cut):
    chosen = jnp.where(key > thr, 1, jnp.where(key == thr, jnp.where(idx <= cut, 1, 0), 0))
    return jnp.where(key > KEY_NEG_INF, chosen, 0) > 0


def _dsa_select_kernel(qit_ref, wit_ref, ki_ref, tri_ref, bias_ref, key_scr, *, tq, tk, topk):
    qb = pl.program_id(1)
    seq = ki_ref.shape[0]
    nk = seq // tk
    nvis = _div((qb + 1) * tq + tk - 1, tk)
    q_chunk = _div(qb * tq + lax.broadcasted_iota(I32, (1, tq), 1), CHUNK)
    w = wit_ref[...]
    row_idx = lax.broadcasted_iota(I32, (tk, 1), 0)

    def score_body(j, carry):
        off = pl.multiple_of(j * tk, tk)
        kb = ki_ref[pl.ds(off, tk), :]
        acc = jnp.zeros((tk, tq), F32)
        for hh in range(IDX_HEADS):
            acc = acc + w[hh:hh + 1, :] * jnp.maximum(_dot(kb, qit_ref[hh]), 0.0)
        vis = _div(off + row_idx, CHUNK) <= q_chunk
        key_scr[pl.ds(off, tk), :] = _sort_key(jnp.where(vis, acc, NEG_INF))
        return carry

    lax.fori_loop(0, nvis, score_body, 0)

    pair = 2 * tk if nk % 2 == 0 else tk
    npair = _div(nvis * tk + pair - 1, pair)

    @pl.when(npair * pair > nvis * tk)
    def _():
        key_scr[pl.ds(pl.multiple_of(nvis * tk, tk), tk), :] = jnp.full((tk, tq), KEY_NEG_INF, I32)

    def fold_rows(v):
        parts = [jnp.sum(v[g * (pair // 8):(g + 1) * (pair // 8)].reshape(pair // 64, 8, tq), axis=0) for g in range(8)]
        while len(parts) > 1:
            parts = [parts[i] + parts[i + 1] for i in range(0, len(parts), 2)]
        return parts[0]

    def count(hit):
        def body(j, c):
            off = pl.multiple_of(j * pair, pair)
            hits = hit(key_scr[pl.ds(off, pair), :], None)
            return c + fold_rows(hits)
        c = lax.fori_loop(0, npair, body, jnp.zeros((8, tq), I32))
        return jnp.sum(c, axis=0, keepdims=True)

    thr = _kth_largest_key(count, topk)
    rem = (topk - count(lambda key, idx: jnp.where(key > thr, 1, 0))).astype(F32)
    tri = tri_ref[...]

    def out_body(j, seen):
        off = pl.multiple_of(j * tk, tk)
        key = key_scr[pl.ds(off, tk), :]
        tie = jnp.where(key == thr, 1.0, 0.0)
        tie_b = tie.astype(BF16)
        half = tk // 2
        rank = seen + _dot(tri[:, :half], tie_b[:half]) + _dot(tri[:, half:], tie_b[half:])
        keep = jnp.where(key > thr, 1.0, jnp.where(rank <= rem, tie, 0.0))
        keep = jnp.where(key > KEY_NEG_INF, keep, 0.0)
        bias_ref[pl.ds(off, tk), :] = jnp.where(keep > 0.0, 0.0, NEG_INF).astype(BF16)
        return rank[tk - 1:tk, :]

    lax.fori_loop(0, nvis, out_body, jnp.zeros((1, tq), F32))

    def fill_body(j, carry):
        off = pl.multiple_of(j * tk, tk)
        bias_ref[pl.ds(off, tk), :] = jnp.full((tk, tq), NEG_INF, BF16)
        return carry

    lax.fori_loop(nvis, nk, fill_body, 0)


def _dsa_select(qit, wit, kib, batch, seq, topk):
    tq, tk = min(256, seq), min(512, seq)
    nq = seq // tq
    t = batch * seq
    return pl.pallas_call(
        functools.partial(_dsa_select_kernel, tq=tq, tk=tk, topk=topk),
        grid=(batch, nq),
        in_specs=[
            pl.BlockSpec((IDX_HEADS, IDX_DIM, tq), lambda b, q: (0, 0, b * nq + q)),
            pl.BlockSpec((IDX_HEADS, tq), lambda b, q: (0, b * nq + q)),
            pl.BlockSpec((seq, IDX_DIM), lambda b, q: (b, 0)),
            _whole((tk, tk)),
        ],
        out_specs=pl.BlockSpec((seq, tq), lambda b, q: (0, b * nq + q)),
        out_shape=jax.ShapeDtypeStruct((seq, t), BF16),
        scratch_shapes=[pltpu.VMEM((seq, tq), I32)],
        compiler_params=_params("parallel", "arbitrary"),
        name="dsa_select",
    )(qit, wit, kib, jnp.tril(jnp.ones((tk, tk), BF16)))


def _dsa_samp_kernel(q_ref, qi_ref, wi_ref, kc_ref, vc_ref, kic_ref, kn_ref, vn_ref, kin_ref, x_ref, wo_ref, o_ref,
                     *, n_q, topk):
    b = pl.program_id(0)
    hd, dh = DSA_HEADS, DSA_HEAD_DIM
    past = kc_ref.shape[1]
    n_keys = past + LANES
    own = _div(lax.broadcasted_iota(I32, (1, LANES), 1), n_q) == _mod(b, LANES // n_q)

    qi = qi_ref[...].reshape(IDX_HEADS * n_q, IDX_DIM)
    lg1 = jnp.maximum(_dot_t(qi, kic_ref[0].astype(BF16)), 0.0)
    lg2 = jnp.maximum(_dot_t(qi, kin_ref[...]), 0.0)
    w = wi_ref[...]
    sc1 = jnp.zeros((n_q, past), F32)
    sc2 = jnp.zeros((n_q, LANES), F32)
    for hh in range(IDX_HEADS):
        sc1 = sc1 + w[:, hh:hh + 1] * lg1[hh * n_q:(hh + 1) * n_q]
        sc2 = sc2 + w[:, hh:hh + 1] * lg2[hh * n_q:(hh + 1) * n_q]
    key = _sort_key(jnp.concatenate([sc1, jnp.where(own, sc2, NEG_INF)], axis=1))
    idx = lax.broadcasted_iota(I32, (1, n_keys), 1)

    count = lambda hit: jnp.sum(hit(key, idx), axis=1, keepdims=True)
    t0 = jnp.where(count(lambda k_, i_: jnp.where(k_ >= 0, 1, 0)) >= topk, jnp.int32(0), jnp.int32(INT_MIN))
    thr = _radix_search(lambda cand: count(lambda k_, i_: jnp.where(k_ >= cand, 1, 0)) >= topk, t0, 31)
    rem = topk - count(lambda k_, i_: jnp.where(k_ > thr, 1, 0))
    cut = _radix_search(
        lambda cand: count(lambda k_, i_: jnp.where(k_ == thr, jnp.where(i_ < cand, 1, 0), 0)) < rem,
        jnp.zeros_like(thr), int(n_keys - 1).bit_length())
    bias = jnp.where(_select(key, idx, thr, cut), 0.0, NEG_INF)
    bias = jnp.concatenate([bias] * hd, axis=0)

    lane_head = _div(lax.broadcasted_iota(I32, (1, hd * dh), 1), dh)
    qf = q_ref[...].astype(F32)
    qbd = jnp.concatenate([jnp.where(lane_head == hh, qf, 0.0) for hh in range(hd)], axis=0).astype(BF16)
    kc = kc_ref[0].astype(BF16)
    vc = vc_ref[0].astype(BF16)
    s1 = _dot_t(qbd, kc) + bias[:, :past]
    s2 = _dot_t(qbd, kn_ref[...]) + bias[:, past:]
    m = jnp.maximum(jnp.max(s1, axis=1, keepdims=True), jnp.max(s2, axis=1, keepdims=True))
    p1 = jnp.exp(s1 - m)
    p2 = jnp.exp(s2 - m)
    l = jnp.sum(p1, axis=1, keepdims=True) + jnp.sum(p2, axis=1, keepdims=True)
    o_all = (_dot(p1.astype(BF16), vc) + _dot(p2.astype(BF16), vn_ref[...])) / l
    out = jnp.zeros((n_q, hd * dh), F32)
    for hh in range(hd):
        out = out + jnp.where(lane_head == hh, o_all[hh * n_q:(hh + 1) * n_q], 0.0)
    o_ref[...] = x_ref[...] + _dot(out.astype(BF16), wo_ref[...])


def _dsa_attn_sample(x, qb, qib, wi, kb, vb, kib, cache_k, cache_v, cache_ki, w_o, n_b, n_q, topk):
    t, d = x.shape
    hdm = DSA_HEADS * DSA_HEAD_DIM
    past = cache_k.shape[1]
    per = LANES // n_q
    return pl.pallas_call(
        functools.partial(_dsa_samp_kernel, n_q=n_q, topk=topk),
        grid=(n_b,),
        in_specs=[
            pl.BlockSpec((n_q, hdm), lambda b: (b, 0)),
            pl.BlockSpec((IDX_HEADS, n_q, IDX_DIM), lambda b: (0, b, 0)),
            pl.BlockSpec((n_q, IDX_HEADS), lambda b: (b, 0)),
            pl.BlockSpec((1, past, hdm), lambda b: (b, 0, 0)),
            pl.BlockSpec((1, past, hdm), lambda b: (b, 0, 0)),
            pl.BlockSpec((1, past, IDX_DIM), lambda b: (b, 0, 0)),
            pl.BlockSpec((LANES, hdm), lambda b: (b // per, 0)),
            pl.BlockSpec((LANES, hdm), lambda b: (b // per, 0)),
            pl.BlockSpec((LANES, IDX_DIM), lambda b: (b // per, 0)),
            pl.BlockSpec((n_q, d), lambda b: (b, 0)),
            _whole(w_o.shape),
        ],
        out_specs=pl.BlockSpec((n_q, d), lambda b: (b, 0)),
        out_shape=jax.ShapeDtypeStruct((t, d), F32),
        compiler_params=_params("parallel"),
        name="dsa_attn_sample",
    )(qb, qib, wi, cache_k.reshape(n_b, past, hdm), cache_v.reshape(n_b, past, hdm), cache_ki, kb, vb, kib, x, w_o)


def kernel(x_prompt, x_sample, cache_mla_ckv, cache_mla_krope, cache_dsa_k, cache_dsa_v, cache_dsa_kidx, norm_mix, norm_ffn, norm_final, mla_w_dq, mla_g_q, mla_w_uq, mla_w_dkv, mla_g_kv, mla_w_ukv, mla_w_o, cmlp_w_in, cmlp_ln_g, cmlp_ln_b, cmlp_w_s, cmlp_b_s, cmlp_w_out, dsa_w_qkv, dsa_w_o, dsa_w_qidx, dsa_w_kidx, dsa_g_kidx, dsa_w_widx, ffn_w_in, ffn_w_out):
    batch, seq, d = x_prompt.shape
    n_b, n_q, _ = x_sample.shape
    past = cache_mla_ckv.shape[2]
    depth = norm_mix.shape[0]
    xp = x_prompt.reshape(batch * seq, d)
    xs = x_sample.reshape(n_b * n_q, d)
    tab_p = _rope_tables(jnp.arange(seq))
    tab_s = tuple(jnp.tile(a, (n_b, 1)) for a in _rope_tables(past + jnp.arange(n_q)))
    cast = lambda a: a.astype(BF16)
    outs = {k: [] for k in ("ckv_p", "kr_p", "ckv_s", "kr_s", "cv_s", "dk_p", "dv_p", "di_p", "dk_s", "dv_s", "di_s")}
    for i in range(depth):
        kind, j = i % 3, i // 3
        if kind == 0:
            wts = _mla_weights(mla_w_dq[j], mla_w_uq[j], mla_w_dkv[j], mla_w_ukv[j], mla_w_o[j])
            qt, kh, vt, ckv, kr, kn2 = _mla_proj_t(xp, norm_mix[i], wts, mla_g_q[j], mla_g_kv[j], tab_p, seq)
            xp = _attn_t(xp, qt, kh, vt, kn2, None, wts["wot"], batch, seq, 512, "mla_attn_prompt")
            outs["ckv_p"].append(ckv.reshape(batch, seq, -1)); outs["kr_p"].append(kr.reshape(batch, seq, -1))
            ql, qr, ckv, kr, ckvb, krb = _mla_proj(xs, norm_mix[i], wts, mla_g_q[j], mla_g_kv[j], tab_s)
            xs = _mla_attn_sample(xs, ql, qr, ckvb, krb, cache_mla_ckv[j], cache_mla_krope[j], wts, n_b, n_q)
            outs["ckv_s"].append(ckv.reshape(n_b, n_q, -1)); outs["kr_s"].append(kr.reshape(n_b, n_q, -1))
        elif kind == 1:
            w_in, w_out = cast(cmlp_w_in[j]), cast(cmlp_w_out[j])
            xp, _ = _cmlp(xp, norm_mix[i], w_in, cmlp_ln_g[j], cmlp_ln_b[j], cmlp_w_s[j], cmlp_b_s[j], w_out,
                          min(seq, CMLP_CHUNK), False)
            xs, v_s = _cmlp(xs, norm_mix[i], w_in, cmlp_ln_g[j], cmlp_ln_b[j], cmlp_w_s[j], cmlp_b_s[j], w_out,
                            min(n_q, CMLP_CHUNK), True)
            outs["cv_s"].append(v_s.reshape(n_b, n_q, -1))
        else:
            hdm = DSA_HEADS * DSA_HEAD_DIM
            w_qkv, w_qidx, w_o = cast(dsa_w_qkv[j]), cast(dsa_w_qidx[j]), cast(dsa_w_o[j])
            w_kidx, w_widx = cast(dsa_w_kidx[j]), cast(dsa_w_widx[j])
            zpad = lambda n: jnp.zeros((d, n), BF16)
            wt = dict(wkv=w_qkv[:, hdm:], wqt=w_qkv[:, :hdm].T, wvt=w_qkv[:, 2 * hdm:].T, wqit=w_qidx.T,
                      wkw=jnp.concatenate([w_kidx, zpad(LANES - IDX_DIM)], axis=1),
                      wwt=jnp.concatenate([w_widx, zpad(BF16_ROWS - IDX_HEADS)], axis=1).T)
            hshape = (DSA_HEADS, DSA_HEAD_DIM)
            k, v, ki, qt, kh, vt, qit, kib, wit, kn2 = _dsa_proj_t(xp, norm_mix[i], wt, dsa_g_kidx[j])
            bias = _dsa_select(qit, wit, kib, batch, seq, min(TOPK_MAX, seq // 4))
            xp = _attn_t(xp, qt, kh, vt, kn2, bias, w_o.T, batch, seq, 512, "dsa_attn_prompt")
            outs["dk_p"].append(k.reshape((batch, seq) + hshape)); outs["dv_p"].append(v.reshape((batch, seq) + hshape))
            outs["di_p"].append(ki.reshape(batch, seq, -1))
            w_kw = jnp.concatenate([w_kidx, w_widx, zpad(LANES - IDX_DIM - IDX_HEADS)], axis=1)
            k, v, ki, qb, kb, vb, qib, kib, wi = _dsa_proj(xs, norm_mix[i], w_qkv, w_qidx, w_kw, dsa_g_kidx[j])
            xs = _dsa_attn_sample(xs, qb, qib, wi, kb, vb, kib, cache_dsa_k[j], cache_dsa_v[j], cache_dsa_kidx[j], w_o,
                                  n_b, n_q, min(TOPK_MAX, (past + n_q) // 4))
            outs["dk_s"].append(k.reshape((n_b, n_q) + hshape)); outs["dv_s"].append(v.reshape((n_b, n_q) + hshape))
            outs["di_s"].append(ki.reshape(n_b, n_q, -1))
        w_in, w_out = cast(ffn_w_in[i]), cast(ffn_w_out[i])
        final = i == depth - 1
        xp = _ffn(xp, norm_ffn[i], w_in, w_out, norm_final, final)
        xs = _ffn(xs, norm_ffn[i], w_in, w_out, norm_final, final)
    st = lambda name: jnp.stack(outs[name])
    return (xp.reshape(batch, seq, d), xs.reshape(n_b, n_q, d),
            st("ckv_p"), st("kr_p"), st("ckv_s"), st("kr_s"), st("cv_s"),
            st("dk_p"), st("dv_p"), st("di_p"), st("dk_s"), st("dv_s"), st("di_s"))
```

```python
import functools

import numpy as np
import jax
import jax.numpy as jnp
from jax import lax
from jax.experimental import pallas as pl
from jax.experimental.pallas import tpu as pltpu

F32, BF16, I32 = jnp.float32, jnp.bfloat16, jnp.int32

CHUNK = 64
EPS = 1e-6
MLA_HEADS, MLA_Q_LORA, MLA_KV_LORA, MLA_NOPE, MLA_ROPE, MLA_V = 16, 512, 256, 64, 32, 64
ROPE_BASE = 10000.0
MLA_SCALE = (MLA_NOPE + MLA_ROPE) ** -0.5
CMLP_CHUNK, CMLP_WIDTH, CMLP_GROUPS = 128, 2048, 8
DSA_HEADS, DSA_HEAD_DIM = 16, 64
DSA_SCALE = DSA_HEAD_DIM ** -0.5
IDX_HEADS, IDX_DIM = 8, 64
IDX_SCALE = IDX_DIM ** -0.5
TOPK_MAX = 256

LANES = 128
MXU_TILE = 256
BF16_ROWS = 16
VMEM_LIMIT = 52 * 1024 * 1024
NEG_INF = float("-inf")
INT_MIN = -2 ** 31
KEY_NEG_INF = -2139095041
LOG2E = float(np.log2(np.e))
KEY_NORM_SLACK = 1.01
LAG_MARGIN = 100.0
HEAD_V = 64
V_ROWS = HEAD_V + BF16_ROWS


def _dot(a, b):
    return jnp.dot(a, b, preferred_element_type=F32)


def _dot_t(a, b):
    return lax.dot_general(a, b, (((1,), (1,)), ((), ())), preferred_element_type=F32)


def _rms(x, g):
    return x * lax.rsqrt(jnp.mean(x * x, axis=-1, keepdims=True) + EPS) * g


def _log2(n):
    assert n > 0 and n & (n - 1) == 0, n
    return n.bit_length() - 1


def _div(x, n):
    return lax.shift_right_logical(x, jnp.int32(_log2(n)))


def _mod(x, n):
    assert n & (n - 1) == 0, n
    return x & (n - 1)


def _params(*sem):
    return pltpu.CompilerParams(dimension_semantics=sem, vmem_limit_bytes=VMEM_LIMIT)


def _whole(shape):
    nd = len(shape)
    return pl.BlockSpec(shape, lambda *_: (0,) * nd)


def _ffn_kernel(x_ref, g_ref, wg_ref, wu_ref, wo_ref, gf_ref, o_ref, act_scr, *, final):
    x = x_ref[...]
    h = _rms(x, g_ref[...]).astype(BF16)
    f = wo_ref.shape[0]
    chunk = MXU_TILE if f % MXU_TILE == 0 else f
    nc = f // chunk
    nxt = (_dot(h, wg_ref[:, 0:chunk]), _dot(h, wu_ref[:, 0:chunk]))
    for c in range(nc):
        gate, up = nxt
        if c + 1 < nc:
            lo = (c + 1) * chunk
            nxt = (_dot(h, wg_ref[:, lo:lo + chunk]), _dot(h, wu_ref[:, lo:lo + chunk]))
        act_scr[:, c * chunk:(c + 1) * chunk] = (jax.nn.silu(gate) * up).astype(BF16)
    y = x + _dot(act_scr[...], wo_ref[...])
    if final:
        y = _rms(y, gf_ref[...])
    o_ref[...] = y


def _ffn(x, g, w_in, w_out, g_final, final):
    t, d = x.shape
    f = w_out.shape[0]
    tm = min(512, t)
    once = pl.Buffered(1)
    return pl.pallas_call(
        functools.partial(_ffn_kernel, final=final),
        grid=(t // tm,),
        in_specs=[
            pl.BlockSpec((tm, d), lambda i: (i, 0)),
            _whole((1, d)),
            pl.BlockSpec((d, f), lambda i: (0, 0), pipeline_mode=once),
            pl.BlockSpec((d, f), lambda i: (0, 1), pipeline_mode=once),
            pl.BlockSpec((f, d), lambda i: (0, 0), pipeline_mode=once),
            _whole((1, d)),
        ],
        out_specs=pl.BlockSpec((tm, d), lambda i: (i, 0)),
        out_shape=jax.ShapeDtypeStruct((t, d), F32),
        scratch_shapes=[pltpu.VMEM((tm, f), BF16)],
        compiler_params=_params("parallel"),
        name="ffn",
    )(x, g.reshape(1, d), w_in, w_in, w_out, g_final.reshape(1, d))


def _gelu(x):
    return 0.5 * x * (1.0 + lax.erf(x * np.float32(np.sqrt(0.5))))


def _layernorm(x, g, b):
    mu = jnp.mean(x, axis=-1, keepdims=True)
    xc = x - mu
    return xc * lax.rsqrt(jnp.mean(xc * xc, axis=-1, keepdims=True) + EPS) * g + b


def _cmlp_kernel(x_ref, g_ref, win_ref, lng_ref, lnb_ref, ws_ref, bs_ref, wout_ref, *rest, n_rows, write_v):
    if write_v:
        o_ref, v_ref, vb_scr, gated_scr = rest
    else:
        o_ref, vb_scr, gated_scr = rest
    tm = x_ref.shape[0]
    w = CMLP_WIDTH
    gw = w // CMLP_GROUPS
    c = CMLP_CHUNK
    x = x_ref[...]
    h = _rms(x, g_ref[...]).astype(BF16)
    v = _layernorm(_gelu(_dot(h, win_ref[:, w:])), lng_ref[...], lnb_ref[...])
    if write_v:
        v_ref[...] = v
    vb_scr[...] = v.astype(BF16)
    r_i = lax.broadcasted_iota(I32, (c, c), 0)
    c_i = lax.broadcasted_iota(I32, (c, c), 1)
    keep = jnp.where(c_i >= r_i - _mod(r_i, n_rows), jnp.where(c_i <= r_i, 1, 0), 0) > 0
    u_next = _dot(h, win_ref[:, 0:gw])
    for g in range(CMLP_GROUPS):
        lo, hi = g * gw, (g + 1) * gw
        u_raw = u_next
        if g + 1 < CMLP_GROUPS:
            u_next = _dot(h, win_ref[:, hi:hi + gw])
        wg = jnp.where(keep, ws_ref[g], 0.0).astype(BF16)
        bias = bs_ref[:, g:g + 1]
        mixed = jnp.concatenate(
            [_dot(wg, vb_scr[k * c:(k + 1) * c, lo:hi]) + bias for k in range(tm // c)], axis=0)
        gated_scr[:, lo:hi] = (_gelu(u_raw) * mixed).astype(BF16)
    o_ref[...] = x + _dot(gated_scr[...], wout_ref[...])


def _cmlp(x, g, w_in, ln_g, ln_b, w_s, b_s, w_out, n_rows, write_v):
    t, d = x.shape
    w = CMLP_WIDTH
    c = CMLP_CHUNK
    tm = min(512, t)
    rep = c // n_rows
    ws_t = jnp.tile(w_s[:, :n_rows, :n_rows], (1, rep, rep))
    bs_t = jnp.tile(b_s[:, :n_rows].T, (rep, 1))
    out_shape = [jax.ShapeDtypeStruct((t, d), F32)]
    out_specs = [pl.BlockSpec((tm, d), lambda i: (i, 0))]
    if write_v:
        out_shape.append(jax.ShapeDtypeStruct((t, w), F32))
        out_specs.append(pl.BlockSpec((tm, w), lambda i: (i, 0)))
    res = pl.pallas_call(
        functools.partial(_cmlp_kernel, n_rows=n_rows, write_v=write_v),
        grid=(t // tm,),
        in_specs=[
            pl.BlockSpec((tm, d), lambda i: (i, 0)),
            _whole((1, d)), _whole((d, 2 * w)), _whole((1, w)), _whole((1, w)),
            _whole((CMLP_GROUPS, c, c)), _whole((c, CMLP_GROUPS)), _whole((w, d)),
        ],
        out_specs=out_specs,
        out_shape=out_shape,
        scratch_shapes=[pltpu.VMEM((tm, w), BF16), pltpu.VMEM((tm, w), BF16)],
        compiler_params=_params("parallel"),
        name="cmlp",
    )(x, g.reshape(1, d), w_in, ln_g.reshape(1, w), ln_b.reshape(1, w), ws_t, bs_t, w_out)
    return res if write_v else (res[0], None)


def _attn_t_kernel(qi_ref, kj_ref, last_ref, nsub_ref, qt_ref, k_ref, vt_ref, *rest, tq, tk, sub, use_bias):
    if use_bias:
        bias_ref, kn_ref, x_ref, wot_ref, o_ref, m_scr, pend_scr, qn_scr, acc_scr, cat_scr = rest
    else:
        kn_ref, x_ref, wot_ref, o_ref, m_scr, pend_scr, qn_scr, acc_scr, cat_scr = rest
    p = pl.program_id(1)
    qi, kj = qi_ref[p], kj_ref[p]
    hd = qt_ref.shape[0]

    @pl.when(kj == 0)
    def _():
        m_scr[...] = jnp.full_like(m_scr, NEG_INF)
        pend_scr[...] = jnp.ones_like(pend_scr)
        acc_scr[...] = jnp.zeros_like(acc_scr)
        for hh in range(hd):
            q = qt_ref[hh].astype(F32)
            qn_scr[hh] = jnp.sqrt(jnp.sum(q * q, axis=0, keepdims=True)) * kn_ref[0, hh:hh + 1, 0:1]

    def step(off, mask, lagged):
        keys = lambda hh: k_ref[hh, pl.ds(off, sub), :]
        vals = lambda hh: vt_ref[hh, :, pl.ds(off, sub)]
        s_next = _dot(keys(0), qt_ref[0])
        pend = None
        for hh in range(hd):
            s = s_next
            if hh + 1 < hd:
                s_next = _dot(keys(hh + 1), qt_ref[hh + 1])
            if mask is not None:
                s = s + mask
            m_prev = m_scr[hh]
            m_new = jnp.maximum(m_prev, jnp.max(s, axis=0, keepdims=True))
            if lagged:
                scale = pend_scr[hh]
                pe = jnp.exp2(s - m_prev).astype(BF16)
                pend_scr[hh] = jnp.exp2(m_prev - m_new)
            else:
                m_safe = jnp.where(m_new == NEG_INF, 0.0, m_new)
                scale = pend_scr[hh] * jnp.exp2(m_prev - m_safe)
                pe = jnp.exp2(s - m_safe).astype(BF16)
                pend_scr[hh] = jnp.ones_like(m_prev)
            m_scr[hh] = m_new
            if pend is not None:
                ph, pa, pp = pend
                acc_scr[ph] = pa * acc_scr[ph] + _dot(vals(ph), pp)
            pend = (hh, scale, pe)
        ph, pa, pp = pend
        acc_scr[ph] = pa * acc_scr[ph] + _dot(vals(ph), pp)

    def sub_body(j, carry):
        off = pl.multiple_of(j * sub, sub)
        excess = qn_scr[0] - m_scr[0]
        for hh in range(1, hd):
            excess = jnp.maximum(excess, qn_scr[hh] - m_scr[hh])
        lag_ok = jnp.max(excess) <= LAG_MARGIN
        if use_bias:
            bias = bias_ref[pl.ds(off, sub), :].astype(F32)

            @pl.when(lag_ok)
            def _():
                step(off, bias, True)

            @pl.when(jnp.logical_not(lag_ok))
            def _():
                step(off, bias, False)
        else:
            base = kj * tk + off
            needs_mask = _div(base + sub - 1, CHUNK) > _div(qi * tq, CHUNK)

            def chunk_mask():
                k_chunk = _div(base + lax.broadcasted_iota(I32, (sub, 1), 0), CHUNK)
                q_chunk = _div(qi * tq + lax.broadcasted_iota(I32, (1, tq), 1), CHUNK)
                return jnp.where(k_chunk <= q_chunk, 0.0, NEG_INF)

            @pl.when(jnp.logical_not(lag_ok))
            def _():
                step(off, chunk_mask(), False)

            @pl.when(jnp.logical_and(lag_ok, needs_mask))
            def _():
                step(off, chunk_mask(), True)

            @pl.when(jnp.logical_and(lag_ok, jnp.logical_not(needs_mask)))
            def _():
                step(off, None, True)
        return carry

    lax.fori_loop(0, nsub_ref[p], sub_body, 0)

    @pl.when(last_ref[p] == 1)
    def _():
        for hh in range(hd):
            a = acc_scr[hh]
            cat_scr[hh * HEAD_V:(hh + 1) * HEAD_V, :] = (a[:HEAD_V] / a[HEAD_V:HEAD_V + 1]).astype(BF16)
        o_ref[...] = x_ref[...] + _dot(wot_ref[...], cat_scr[...]).T


def _causal_pairs(nq, tq, tk, sub):
    qi, kj, last, nsub = [], [], [], []
    per = tk // sub
    for q in range(nq):
        vis_sub = -(-((q + 1) * tq) // sub)
        nvis = -(-vis_sub // per)
        for k in range(nvis):
            qi.append(q); kj.append(k); last.append(int(k == nvis - 1)); nsub.append(min(per, vis_sub - k * per))
    as_arr = lambda v: jnp.asarray(np.array(v, np.int32))
    return as_arr(qi), as_arr(kj), as_arr(last), as_arr(nsub)


def _attn_t(x, qt, kh, vt, kn2, bias, wot, batch, seq, sub, name):
    t, d = x.shape
    hd, dk, _ = qt.shape
    tq, tk, sub = min(512, seq), min(1024, seq), min(sub, seq)
    nq, nk = seq // tq, seq // tk
    qi, kj, last, nsub = _causal_pairs(nq, tq, tk, sub)
    use_bias = bias is not None
    in_specs = [
        pl.BlockSpec((hd, dk, tq), lambda b, p, qi, kj, la, ns: (0, 0, b * nq + qi[p])),
        pl.BlockSpec((hd, tk, dk), lambda b, p, qi, kj, la, ns: (0, b * nk + kj[p], 0)),
        pl.BlockSpec((hd, V_ROWS, tk), lambda b, p, qi, kj, la, ns: (0, 0, b * nk + kj[p])),
    ]
    args = [qt, kh, vt]
    if use_bias:
        in_specs.append(pl.BlockSpec((tk, tq), lambda b, p, qi, kj, la, ns: (kj[p], b * nq + qi[p])))
        args.append(bias)
    kn = jnp.sqrt(jnp.max(kn2[:, :, 0].reshape(batch, -1, hd), axis=1)) * KEY_NORM_SLACK
    in_specs.append(pl.BlockSpec((1, hd, LANES), lambda b, p, qi, kj, la, ns: (b, 0, 0)))
    args.append(jnp.broadcast_to(kn[:, :, None], (batch, hd, LANES)))
    in_specs += [
        pl.BlockSpec((tq, d), lambda b, p, qi, kj, la, ns: (b * nq + qi[p], 0)),
        pl.BlockSpec(wot.shape, lambda b, p, qi, kj, la, ns: (0, 0)),
    ]
    args += [x, wot]
    grid_spec = pltpu.PrefetchScalarGridSpec(
        num_scalar_prefetch=4,
        grid=(batch, int(qi.shape[0])),
        in_specs=in_specs,
        out_specs=pl.BlockSpec((tq, d), lambda b, p, qi, kj, la, ns: (b * nq + qi[p], 0)),
        scratch_shapes=[
            pltpu.VMEM((hd, 1, tq), F32), pltpu.VMEM((hd, 1, tq), F32), pltpu.VMEM((hd, 1, tq), F32),
            pltpu.VMEM((hd, V_ROWS, tq), F32), pltpu.VMEM((hd * HEAD_V, tq), BF16),
        ],
    )
    return pl.pallas_call(
        functools.partial(_attn_t_kernel, tq=tq, tk=tk, sub=sub, use_bias=use_bias),
        grid_spec=grid_spec,
        out_shape=jax.ShapeDtypeStruct((t, d), F32),
        compiler_params=_params("parallel", "arbitrary"),
        name=name,
    )(qi, kj, last, nsub, *args)


def _rope_tables(pos):
    half = MLA_ROPE // 2
    inv = ROPE_BASE ** (-jnp.arange(half, dtype=F32) / half)
    ang = pos.astype(F32)[:, None] * inv[None, :]
    cos, sin = jnp.cos(ang), jnp.sin(ang)
    cos_k = jnp.concatenate([cos, cos], axis=1)
    sin_k = jnp.concatenate([-sin, sin], axis=1)
    return jnp.tile(cos_k, (1, MLA_HEADS)), jnp.tile(sin_k, (1, MLA_HEADS)), cos_k, sin_k


def _swap_halves(w, group):
    shp = w.shape
    wr = w.reshape(shp[:-1] + (shp[-1] // group, 2, group // 2))
    return wr[..., ::-1, :].reshape(shp)


def _mla_weights(w_dq, w_uq, w_dkv, w_ukv, w_o):
    hd = MLA_HEADS
    cast = lambda a: a.astype(BF16)
    w_uq, w_dkv, w_ukv = cast(w_uq), cast(w_dkv), cast(w_ukv)
    wqn = w_uq[:, :, :MLA_NOPE].reshape(MLA_Q_LORA, hd * MLA_NOPE)
    wqr = w_uq[:, :, MLA_NOPE:].reshape(MLA_Q_LORA, hd * MLA_ROPE)
    wqs = _swap_halves(wqr, MLA_ROPE)
    wkc, wkr = w_dkv[:, :MLA_KV_LORA], w_dkv[:, MLA_KV_LORA:]
    wks = _swap_halves(wkr, MLA_ROPE)
    wuk = jnp.transpose(w_ukv[:, :, :MLA_NOPE], (1, 2, 0))
    zeros = jnp.zeros_like(wuk)
    even = jnp.concatenate([wuk, zeros], axis=1)
    odd = jnp.concatenate([zeros, wuk], axis=1)
    wuk2 = jnp.where((jnp.arange(hd) % 2 == 0)[:, None, None], even, odd)
    wuv = jnp.transpose(w_ukv[:, :, MLA_NOPE:], (1, 0, 2))
    d = w_dkv.shape[0]
    pad_r = lambda a: jnp.concatenate([a, jnp.zeros((d, LANES - MLA_ROPE), BF16)], axis=1)
    wukp = jnp.concatenate([jnp.zeros((MLA_KV_LORA, hd, MLA_ROPE), BF16), w_ukv[:, :, :MLA_NOPE],
                            jnp.zeros((MLA_KV_LORA, hd, LANES - MLA_ROPE - MLA_NOPE), BF16)], axis=2)
    return dict(wdq=cast(w_dq), wqn=wqn, wqr=wqr, wqs=wqs, wkc=wkc, wkr=wkr, wks=wks, wuk=wuk2, wuv=wuv, wo=cast(w_o),
                wqnt=wqn.T, wqrt=wqr.T, wqst=wqs.T, wkrp=pad_r(wkr), wksp=pad_r(wks),
                wukp=wukp.reshape(MLA_KV_LORA, hd * LANES), wuvt=wuv.transpose(0, 2, 1).reshape(hd * MLA_V, MLA_KV_LORA),
                wot=cast(w_o).T)


def _mla_proj_kernel(x_ref, gm_ref, wdq_ref, gq_ref, wqn_ref, wqr_ref, wqs_ref, wkc_ref, wkr_ref, wks_ref,
                     gkv_ref, wuk_ref, cq_ref, sq_ref, ck_ref, sk_ref,
                     ql_ref, qr_ref, ckv_ref, kr_ref, ckvb_ref, krb_ref):
    h = _rms(x_ref[...], gm_ref[...]).astype(BF16)
    cq = _rms(_dot(h, wdq_ref[...]), gq_ref[...]).astype(BF16)
    qn = _dot(cq, wqn_ref[...])
    qr = ((_dot(cq, wqr_ref[...]) * cq_ref[...] + _dot(cq, wqs_ref[...]) * sq_ref[...]) * MLA_SCALE).astype(BF16)
    for hh in range(MLA_HEADS):
        pair = qn[:, (hh // 2) * LANES:(hh // 2 + 1) * LANES].astype(BF16)
        ql_ref[hh] = (_dot(pair, wuk_ref[hh]) * MLA_SCALE).astype(BF16)
        qr_ref[hh] = qr[:, hh * MLA_ROPE:(hh + 1) * MLA_ROPE]
    ckv = _rms(_dot(h, wkc_ref[...]), gkv_ref[...])
    ckv_ref[...] = ckv
    ckvb_ref[...] = ckv.astype(BF16)
    kr = _dot(h, wkr_ref[...]) * ck_ref[...] + _dot(h, wks_ref[...]) * sk_ref[...]
    kr_ref[...] = kr
    krb_ref[...] = kr.astype(BF16)


def _mla_proj(x, g_mix, wts, g_q, g_kv, tables):
    t, d = x.shape
    tm = min(256, t)
    cos_q, sin_q, cos_k, sin_k = tables
    nrep = cos_q.shape[0] // tm
    tab = lambda wdt: pl.BlockSpec((tm, wdt), lambda i: (i % nrep, 0))
    hd = MLA_HEADS
    row = lambda wdt: pl.BlockSpec((tm, wdt), lambda i: (i, 0))
    hm = lambda wdt: pl.BlockSpec((hd, tm, wdt), lambda i: (0, i, 0))
    return pl.pallas_call(
        _mla_proj_kernel,
        grid=(t // tm,),
        in_specs=[
            row(d), _whole((1, d)), _whole(wts["wdq"].shape), _whole((1, MLA_Q_LORA)),
            _whole(wts["wqn"].shape), _whole(wts["wqr"].shape), _whole(wts["wqs"].shape),
            _whole(wts["wkc"].shape), _whole(wts["wkr"].shape), _whole(wts["wks"].shape),
            _whole((1, MLA_KV_LORA)), _whole(wts["wuk"].shape),
            tab(hd * MLA_ROPE), tab(hd * MLA_ROPE), tab(MLA_ROPE), tab(MLA_ROPE),
        ],
        out_specs=[hm(MLA_KV_LORA), hm(MLA_ROPE), row(MLA_KV_LORA), row(MLA_ROPE), row(MLA_KV_LORA), row(MLA_ROPE)],
        out_shape=[
            jax.ShapeDtypeStruct((hd, t, MLA_KV_LORA), BF16), jax.ShapeDtypeStruct((hd, t, MLA_ROPE), BF16),
            jax.ShapeDtypeStruct((t, MLA_KV_LORA), F32), jax.ShapeDtypeStruct((t, MLA_ROPE), F32),
            jax.ShapeDtypeStruct((t, MLA_KV_LORA), BF16), jax.ShapeDtypeStruct((t, MLA_ROPE), BF16),
        ],
        compiler_params=_params("parallel"),
        name="mla_proj",
    )(x, g_mix.reshape(1, d), wts["wdq"], g_q.reshape(1, -1), wts["wqn"], wts["wqr"], wts["wqs"],
      wts["wkc"], wts["wkr"], wts["wks"], g_kv.reshape(1, -1), wts["wuk"], cos_q, sin_q, cos_k, sin_k)


def _mla_proj_t_kernel(x_ref, gm_ref, wdq_ref, gq_ref, wqnt_ref, wqrt_ref, wqst_ref, wkc_ref, wkrp_ref, wksp_ref,
                       gkv_ref, wukp_ref, wuvt_ref, cqt_ref, sqt_ref, ckp_ref, skp_ref,
                       qt_ref, kh_ref, vt_ref, ckv_ref, kr_ref, kn2_ref):
    tm = x_ref.shape[0]
    hd = MLA_HEADS
    qscale = MLA_SCALE * LOG2E
    h = _rms(x_ref[...], gm_ref[...]).astype(BF16)
    cq_raw = _dot(h, wdq_ref[...])
    ckv_raw = _dot(h, wkc_ref[...])
    kr_a = _dot(h, wkrp_ref[...])
    kr_b = _dot(h, wksp_ref[...])
    cq = _rms(cq_raw, gq_ref[...]).astype(BF16)
    qnt_raw = _dot_t(wqnt_ref[...], cq)
    qrt_a = _dot_t(wqrt_ref[...], cq)
    qrt_b = _dot_t(wqst_ref[...], cq)
    ckv = _rms(ckv_raw, gkv_ref[...])
    ckv_ref[...] = ckv
    cb = ckv.astype(BF16)
    kn = _dot(cb, wukp_ref[...])
    vt = _dot_t(wuvt_ref[...], cb).astype(BF16)
    qnt = (qnt_raw * qscale).astype(BF16)
    rot = qrt_a.reshape(hd, MLA_ROPE, tm) * cqt_ref[...][None] + qrt_b.reshape(hd, MLA_ROPE, tm) * sqt_ref[...][None]
    qrt = (rot * qscale).reshape(hd * MLA_ROPE, tm).astype(BF16)
    zeros = jnp.zeros((LANES - MLA_ROPE - MLA_NOPE, tm), BF16)
    krp = kr_a * ckp_ref[...] + kr_b * skp_ref[...]
    kr_ref[...] = krp[:, :MLA_ROPE]
    ones = jnp.ones((BF16_ROWS, tm), BF16)
    for hh in range(hd):
        qt_ref[hh, 0:MLA_ROPE, :] = qrt[hh * MLA_ROPE:(hh + 1) * MLA_ROPE]
        qt_ref[hh, MLA_ROPE:MLA_ROPE + MLA_NOPE, :] = qnt[hh * MLA_NOPE:(hh + 1) * MLA_NOPE]
        qt_ref[hh, MLA_ROPE + MLA_NOPE:, :] = zeros
        key = kn[:, hh * LANES:(hh + 1) * LANES] + krp
        kh_ref[hh] = key.astype(BF16)
        kn2_ref[0, hh:hh + 1, :] = jnp.broadcast_to(
            jnp.max(jnp.sum(key * key, axis=1, keepdims=True), axis=0, keepdims=True), (1, LANES))
        vt_ref[hh, 0:MLA_V, :] = vt[hh * MLA_V:(hh + 1) * MLA_V]
        vt_ref[hh, MLA_V:, :] = ones


def _mla_proj_t(x, g_mix, wts, g_q, g_kv, tables, seq):
    t, d = x.shape
    tm = min(256, seq)
    hd = MLA_HEADS
    cos_q, sin_q, cos_k, sin_k = tables
    pad = lambda a: jnp.concatenate([a, jnp.zeros((seq, LANES - MLA_ROPE), F32)], axis=1)
    nrep = seq // tm
    row = lambda wdt: pl.BlockSpec((tm, wdt), lambda i: (i, 0))
    names = ("wdq", "wqnt", "wqrt", "wqst", "wkc", "wkrp", "wksp", "wukp", "wuvt")
    w = {n: wts[n] for n in names}
    return pl.pallas_call(
        _mla_proj_t_kernel,
        grid=(t // tm,),
        in_specs=[
            row(d), _whole((1, d)), _whole(w["wdq"].shape), _whole((1, MLA_Q_LORA)),
            _whole(w["wqnt"].shape), _whole(w["wqrt"].shape), _whole(w["wqst"].shape),
            _whole(w["wkc"].shape), _whole(w["wkrp"].shape), _whole(w["wksp"].shape),
            _whole((1, MLA_KV_LORA)), _whole(w["wukp"].shape), _whole(w["wuvt"].shape),
            pl.BlockSpec((MLA_ROPE, tm), lambda i: (0, i % nrep)),
            pl.BlockSpec((MLA_ROPE, tm), lambda i: (0, i % nrep)),
            pl.BlockSpec((tm, LANES), lambda i: (i % nrep, 0)),
            pl.BlockSpec((tm, LANES), lambda i: (i % nrep, 0)),
        ],
        out_specs=[
            pl.BlockSpec((hd, LANES, tm), lambda i: (0, 0, i)),
            pl.BlockSpec((hd, tm, LANES), lambda i: (0, i, 0)),
            pl.BlockSpec((hd, V_ROWS, tm), lambda i: (0, 0, i)),
            row(MLA_KV_LORA), row(MLA_ROPE),
            pl.BlockSpec((1, hd, LANES), lambda i: (i, 0, 0)),
        ],
        out_shape=[
            jax.ShapeDtypeStruct((hd, LANES, t), BF16), jax.ShapeDtypeStruct((hd, t, LANES), BF16),
            jax.ShapeDtypeStruct((hd, V_ROWS, t), BF16),
            jax.ShapeDtypeStruct((t, MLA_KV_LORA), F32), jax.ShapeDtypeStruct((t, MLA_ROPE), F32),
            jax.ShapeDtypeStruct((t // tm, hd, LANES), F32),
        ],
        compiler_params=_params("parallel"),
        name="mla_proj_t",
    )(x, g_mix.reshape(1, d), w["wdq"], g_q.reshape(1, -1), w["wqnt"], w["wqrt"], w["wqst"],
      w["wkc"], w["wkrp"], w["wksp"], g_kv.reshape(1, -1), w["wukp"], w["wuvt"],
      cos_k.T, sin_k.T, pad(cos_k), pad(sin_k))


def _mla_epilogue(o_lat, x_ref, wuv_ref, wo_ref, o_ref, cat_scr, tq):
    ob = o_lat.astype(BF16)
    for hh in range(MLA_HEADS):
        cat_scr[:, hh * MLA_V:(hh + 1) * MLA_V] = _dot(ob[hh * tq:(hh + 1) * tq], wuv_ref[hh]).astype(BF16)
    o_ref[...] = x_ref[...] + _dot(cat_scr[...], wo_ref[...])


def _mla_samp_kernel(ql_ref, qr_ref, cc_ref, cr_ref, cn_ref, rn_ref, x_ref, wuv_ref, wo_ref, o_ref, cat_scr, *, n_q):
    b = pl.program_id(0)
    hd = MLA_HEADS
    ql = ql_ref[...].reshape(hd * n_q, MLA_KV_LORA)
    qr = qr_ref[...].reshape(hd * n_q, MLA_ROPE)
    cc = cc_ref[0].astype(BF16)
    cr = cr_ref[0].astype(BF16)
    cn = cn_ref[...]
    s1 = _dot_t(ql, cc) + _dot_t(qr, cr)
    s2 = _dot_t(ql, cn) + _dot_t(qr, rn_ref[...])
    own = _div(lax.broadcasted_iota(I32, (1, LANES), 1), n_q) == _mod(b, LANES // n_q)
    s2 = jnp.where(own, s2, NEG_INF)
    m = jnp.maximum(jnp.max(s1, axis=1, keepdims=True), jnp.max(s2, axis=1, keepdims=True))
    p1 = jnp.exp(s1 - m)
    p2 = jnp.exp(s2 - m)
    l = jnp.sum(p1, axis=1, keepdims=True) + jnp.sum(p2, axis=1, keepdims=True)
    o_lat = (_dot(p1.astype(BF16), cc) + _dot(p2.astype(BF16), cn)) / l
    _mla_epilogue(o_lat, x_ref, wuv_ref, wo_ref, o_ref, cat_scr, n_q)


def _mla_attn_sample(x, ql, qr, ckvb, krb, cache_c, cache_r, wts, n_b, n_q):
    t, d = x.shape
    hd = MLA_HEADS
    past = cache_c.shape[1]
    per = LANES // n_q
    return pl.pallas_call(
        functools.partial(_mla_samp_kernel, n_q=n_q),
        grid=(n_b,),
        in_specs=[
            pl.BlockSpec((hd, n_q, MLA_KV_LORA), lambda b: (0, b, 0)),
            pl.BlockSpec((hd, n_q, MLA_ROPE), lambda b: (0, b, 0)),
            pl.BlockSpec((1, past, MLA_KV_LORA), lambda b: (b, 0, 0)),
            pl.BlockSpec((1, past, MLA_ROPE), lambda b: (b, 0, 0)),
            pl.BlockSpec((LANES, MLA_KV_LORA), lambda b: (b // per, 0)),
            pl.BlockSpec((LANES, MLA_ROPE), lambda b: (b // per, 0)),
            pl.BlockSpec((n_q, d), lambda b: (b, 0)),
            _whole(wts["wuv"].shape), _whole(wts["wo"].shape),
        ],
        out_specs=pl.BlockSpec((n_q, d), lambda b: (b, 0)),
        out_shape=jax.ShapeDtypeStruct((t, d), F32),
        scratch_shapes=[pltpu.VMEM((n_q, hd * MLA_V), BF16)],
        compiler_params=_params("parallel"),
        name="mla_attn_sample",
    )(ql, qr, cache_c, cache_r, ckvb, krb, x, wts["wuv"], wts["wo"])


def _dsa_proj_kernel(x_ref, gm_ref, wqkv_ref, wqi_ref, wkw_ref, gki_ref,
                     k_ref, v_ref, ki_ref, qb_ref, kb_ref, vb_ref, qib_ref, kib_ref, wi_ref):
    hdm = DSA_HEADS * DSA_HEAD_DIM
    h = _rms(x_ref[...], gm_ref[...]).astype(BF16)
    qkv = _dot(h, wqkv_ref[...])
    q = qkv[:, :hdm] * DSA_SCALE
    k = qkv[:, hdm:2 * hdm]
    v = qkv[:, 2 * hdm:]
    k_ref[...] = pltpu.einshape("t(hd)->thd", k, h=DSA_HEADS)
    v_ref[...] = pltpu.einshape("t(hd)->thd", v, h=DSA_HEADS)
    qi = _dot(h, wqi_ref[...]) * IDX_SCALE
    kw = _dot(h, wkw_ref[...])
    ki = _rms(kw[:, :IDX_DIM], gki_ref[...])
    ki_ref[...] = ki
    kib_ref[...] = ki.astype(BF16)
    wi_ref[...] = kw[:, IDX_DIM:IDX_DIM + IDX_HEADS] * (IDX_HEADS ** -0.5)
    qb_ref[...] = q.astype(BF16)
    kb_ref[...] = k.astype(BF16)
    vb_ref[...] = v.astype(BF16)
    for hh in range(IDX_HEADS):
        qib_ref[hh] = qi[:, hh * IDX_DIM:(hh + 1) * IDX_DIM].astype(BF16)


def _dsa_proj(x, g_mix, w_qkv, w_qidx, w_kw, g_kidx):
    t, d = x.shape
    tm = min(512, t)
    hdm = DSA_HEADS * DSA_HEAD_DIM
    row = lambda wdt: pl.BlockSpec((tm, wdt), lambda i: (i, 0))
    hd3 = pl.BlockSpec((tm, DSA_HEADS, DSA_HEAD_DIM), lambda i: (i, 0, 0))
    return pl.pallas_call(
        _dsa_proj_kernel,
        grid=(t // tm,),
        in_specs=[row(d), _whole((1, d)), _whole(w_qkv.shape), _whole(w_qidx.shape), _whole(w_kw.shape),
                  _whole((1, IDX_DIM))],
        out_specs=[hd3, hd3, row(IDX_DIM), row(hdm), row(hdm), row(hdm),
                   pl.BlockSpec((IDX_HEADS, tm, IDX_DIM), lambda i: (0, i, 0)), row(IDX_DIM), row(IDX_HEADS)],
        out_shape=[
            jax.ShapeDtypeStruct((t, DSA_HEADS, DSA_HEAD_DIM), F32), jax.ShapeDtypeStruct((t, DSA_HEADS, DSA_HEAD_DIM), F32),
            jax.ShapeDtypeStruct((t, IDX_DIM), F32), jax.ShapeDtypeStruct((t, hdm), BF16),
            jax.ShapeDtypeStruct((t, hdm), BF16), jax.ShapeDtypeStruct((t, hdm), BF16),
            jax.ShapeDtypeStruct((IDX_HEADS, t, IDX_DIM), BF16), jax.ShapeDtypeStruct((t, IDX_DIM), BF16),
            jax.ShapeDtypeStruct((t, IDX_HEADS), F32),
        ],
        compiler_params=_params("parallel"),
        name="dsa_proj",
    )(x, g_mix.reshape(1, d), w_qkv, w_qidx, w_kw, g_kidx.reshape(1, IDX_DIM))


def _dsa_proj_t_kernel(x_ref, gm_ref, wkv_ref, wqt_ref, wvt_ref, wqit_ref, wkw_ref, wwt_ref, gki_ref,
                       k_ref, v_ref, ki_ref, qt_ref, kh_ref, vt_ref, qit_ref, kib_ref, wit_ref, kn2_ref):
    tm = x_ref.shape[0]
    hdm = DSA_HEADS * DSA_HEAD_DIM
    dh = DSA_HEAD_DIM
    h = _rms(x_ref[...], gm_ref[...]).astype(BF16)
    kv = _dot(h, wkv_ref[...])
    k = kv[:, :hdm]
    k_ref[...] = pltpu.einshape("t(hd)->thd", k, h=DSA_HEADS)
    v_ref[...] = pltpu.einshape("t(hd)->thd", kv[:, hdm:], h=DSA_HEADS)
    kw = _dot(h, wkw_ref[...])
    ki = _rms(kw[:, :IDX_DIM], gki_ref[...])
    ki_ref[...] = ki
    kib_ref[...] = ki.astype(BF16)
    wit_ref[...] = _dot_t(wwt_ref[...], h)[:IDX_HEADS] * (IDX_HEADS ** -0.5)
    qt = (_dot_t(wqt_ref[...], h) * (DSA_SCALE * LOG2E)).astype(BF16)
    vt = _dot_t(wvt_ref[...], h).astype(BF16)
    qit = (_dot_t(wqit_ref[...], h) * IDX_SCALE).astype(BF16)
    ones = jnp.ones((BF16_ROWS, tm), BF16)
    for hh in range(DSA_HEADS):
        qt_ref[hh] = qt[hh * dh:(hh + 1) * dh]
        key = k[:, hh * dh:(hh + 1) * dh]
        kh_ref[hh] = key.astype(BF16)
        kn2_ref[0, hh:hh + 1, :] = jnp.broadcast_to(
            jnp.max(jnp.sum(key * key, axis=1, keepdims=True), axis=0, keepdims=True), (1, LANES))
        vt_ref[hh, 0:dh, :] = vt[hh * dh:(hh + 1) * dh]
        vt_ref[hh, dh:, :] = ones
    for hh in range(IDX_HEADS):
        qit_ref[hh] = qit[hh * IDX_DIM:(hh + 1) * IDX_DIM]


def _dsa_proj_t(x, g_mix, w, g_kidx):
    t, d = x.shape
    tm = min(256, t)
    hd, dh = DSA_HEADS, DSA_HEAD_DIM
    hdm = hd * dh
    row = lambda wdt: pl.BlockSpec((tm, wdt), lambda i: (i, 0))
    names = ("wkv", "wqt", "wvt", "wqit", "wkw", "wwt")
    return pl.pallas_call(
        _dsa_proj_t_kernel,
        grid=(t // tm,),
        in_specs=[row(d), _whole((1, d))] + [_whole(w[n].shape) for n in names] + [_whole((1, IDX_DIM))],
        out_specs=[
            pl.BlockSpec((tm, hd, dh), lambda i: (i, 0, 0)), pl.BlockSpec((tm, hd, dh), lambda i: (i, 0, 0)),
            row(IDX_DIM),
            pl.BlockSpec((hd, dh, tm), lambda i: (0, 0, i)),
            pl.BlockSpec((hd, tm, dh), lambda i: (0, i, 0)),
            pl.BlockSpec((hd, V_ROWS, tm), lambda i: (0, 0, i)),
            pl.BlockSpec((IDX_HEADS, IDX_DIM, tm), lambda i: (0, 0, i)),
            row(IDX_DIM),
            pl.BlockSpec((IDX_HEADS, tm), lambda i: (0, i)),
            pl.BlockSpec((1, hd, LANES), lambda i: (i, 0, 0)),
        ],
        out_shape=[
            jax.ShapeDtypeStruct((t, hd, dh), F32), jax.ShapeDtypeStruct((t, hd, dh), F32),
            jax.ShapeDtypeStruct((t, IDX_DIM), F32),
            jax.ShapeDtypeStruct((hd, dh, t), BF16), jax.ShapeDtypeStruct((hd, t, dh), BF16),
            jax.ShapeDtypeStruct((hd, V_ROWS, t), BF16),
            jax.ShapeDtypeStruct((IDX_HEADS, IDX_DIM, t), BF16), jax.ShapeDtypeStruct((t, IDX_DIM), BF16),
            jax.ShapeDtypeStruct((IDX_HEADS, t), F32),
            jax.ShapeDtypeStruct((t // tm, hd, LANES), F32),
        ],
        compiler_params=_params("parallel"),
        name="dsa_proj_t",
    )(x, g_mix.reshape(1, d), *[w[n] for n in names], g_kidx.reshape(1, IDX_DIM))


def _sort_key(score):
    bits = lax.bitcast_convert_type(score, I32)
    return jnp.where(bits < 0, bits ^ jnp.int32(0x7FFFFFFF), bits)


def _kth_largest_key(count, topk):
    c0 = count(lambda key, idx: jnp.where(key >= 0, 1, 0))
    t0 = jnp.where(c0 >= topk, jnp.int32(0), jnp.int32(INT_MIN))

    def bit_body(i, t):
        cand = t + lax.shift_left(jnp.int32(1), jnp.int32(30) - i)
        c = count(lambda key, idx: jnp.where(key >= cand, 1, 0))
        return jnp.where(c >= topk, cand, t)

    return lax.fori_loop(0, 31, bit_body, t0)


def _radix_search(accept, start, nbits):
    t = start
    if nbits % 2:
        cand = t + jnp.int32(1 << (nbits - 1))
        t = jnp.where(accept(cand), cand, t)
        nbits -= 1

    def body(i, t):
        unit = lax.shift_left(jnp.int32(1), jnp.int32(nbits - 2) - 2 * i)
        a1, a2, a3 = accept(t + unit), accept(t + 2 * unit), accept(t + 3 * unit)
        return t + jnp.where(a3, 3, jnp.where(a2, 2, jnp.where(a1, 1, 0))) * unit

    return lax.fori_loop(0, nbits // 2, body, t)


def _select(key, idx, thr, cut):
    chosen = jnp.where(key > thr, 1, jnp.where(key == thr, jnp.where(idx <= cut, 1, 0), 0))
    return jnp.where(key > KEY_NEG_INF, chosen, 0) > 0


def _dsa_select_kernel(qit_ref, wit_ref, ki_ref, tri_ref, bias_ref, key_scr, *, tq, tk, topk):
    qb = pl.program_id(1)
    seq = ki_ref.shape[0]
    nk = seq // tk
    nvis = _div((qb + 1) * tq + tk - 1, tk)
    q_chunk = _div(qb * tq + lax.broadcasted_iota(I32, (1, tq), 1), CHUNK)
    w = wit_ref[...]
    row_idx = lax.broadcasted_iota(I32, (tk, 1), 0)

    def score_body(j, carry):
        off = pl.multiple_of(j * tk, tk)
        kb = ki_ref[pl.ds(off, tk), :]
        acc = jnp.zeros((tk, tq), F32)
        for hh in range(IDX_HEADS):
            acc = acc + w[hh:hh + 1, :] * jnp.maximum(_dot(kb, qit_ref[hh]), 0.0)
        vis = _div(off + row_idx, CHUNK) <= q_chunk
        key_scr[pl.ds(off, tk), :] = _sort_key(jnp.where(vis, acc, NEG_INF))
        return carry

    lax.fori_loop(0, nvis, score_body, 0)

    pair = 2 * tk if nk % 2 == 0 else tk
    npair = _div(nvis * tk + pair - 1, pair)

    @pl.when(npair * pair > nvis * tk)
    def _():
        key_scr[pl.ds(pl.multiple_of(nvis * tk, tk), tk), :] = jnp.full((tk, tq), KEY_NEG_INF, I32)

    def fold_rows(v):
        parts = [jnp.sum(v[g * (pair // 8):(g + 1) * (pair // 8)].reshape(pair // 64, 8, tq), axis=0) for g in range(8)]
        while len(parts) > 1:
            parts = [parts[i] + parts[i + 1] for i in range(0, len(parts), 2)]
        return parts[0]

    def count(hit):
        def body(j, c):
            off = pl.multiple_of(j * pair, pair)
            hits = hit(key_scr[pl.ds(off, pair), :], None)
            return c + fold_rows(hits)
        c = lax.fori_loop(0, npair, body, jnp.zeros((8, tq), I32))
        return jnp.sum(c, axis=0, keepdims=True)

    thr = _kth_largest_key(count, topk)
    rem = (topk - count(lambda key, idx: jnp.where(key > thr, 1, 0))).astype(F32)
    tri = tri_ref[...]

    def out_body(j, seen):
        off = pl.multiple_of(j * tk, tk)
        key = key_scr[pl.ds(off, tk), :]
        tie = jnp.where(key == thr, 1.0, 0.0)
        tie_b = tie.astype(BF16)
        half = tk // 2
        rank = seen + _dot(tri[:, :half], tie_b[:half]) + _dot(tri[:, half:], tie_b[half:])
        keep = jnp.where(key > thr, 1.0, jnp.where(rank <= rem, tie, 0.0))
        keep = jnp.where(key > KEY_NEG_INF, keep, 0.0)
        bias_ref[pl.ds(off, tk), :] = jnp.where(keep > 0.0, 0.0, NEG_INF).astype(BF16)
        return rank[tk - 1:tk, :]

    lax.fori_loop(0, nvis, out_body, jnp.zeros((1, tq), F32))

    def fill_body(j, carry):
        off = pl.multiple_of(j * tk, tk)
        bias_ref[pl.ds(off, tk), :] = jnp.full((tk, tq), NEG_INF, BF16)
        return carry

    lax.fori_loop(nvis, nk, fill_body, 0)


def _dsa_select(qit, wit, kib, batch, seq, topk):
    tq, tk = min(512, seq), min(512, seq)
    nq = seq // tq
    t = batch * seq
    return pl.pallas_call(
        functools.partial(_dsa_select_kernel, tq=tq, tk=tk, topk=topk),
        grid=(batch, nq),
        in_specs=[
            pl.BlockSpec((IDX_HEADS, IDX_DIM, tq), lambda b, q: (0, 0, b * nq + q)),
            pl.BlockSpec((IDX_HEADS, tq), lambda b, q: (0, b * nq + q)),
            pl.BlockSpec((seq, IDX_DIM), lambda b, q: (b, 0)),
            _whole((tk, tk)),
        ],
        out_specs=pl.BlockSpec((seq, tq), lambda b, q: (0, b * nq + q)),
        out_shape=jax.ShapeDtypeStruct((seq, t), BF16),
        scratch_shapes=[pltpu.VMEM((seq, tq), I32)],
        compiler_params=_params("parallel", "arbitrary"),
        name="dsa_select",
    )(qit, wit, kib, jnp.tril(jnp.ones((tk, tk), BF16)))


def _dsa_samp_kernel(q_ref, qi_ref, wi_ref, kc_ref, vc_ref, kic_ref, kn_ref, vn_ref, kin_ref, x_ref, wo_ref, o_ref,
                     *, n_q, topk):
    b = pl.program_id(0)
    hd, dh = DSA_HEADS, DSA_HEAD_DIM
    past = kc_ref.shape[1]
    n_keys = past + LANES
    own = _div(lax.broadcasted_iota(I32, (1, LANES), 1), n_q) == _mod(b, LANES // n_q)

    qi = qi_ref[...].reshape(IDX_HEADS * n_q, IDX_DIM)
    lg1 = jnp.maximum(_dot_t(qi, kic_ref[0].astype(BF16)), 0.0)
    lg2 = jnp.maximum(_dot_t(qi, kin_ref[...]), 0.0)
    w = wi_ref[...]
    sc1 = jnp.zeros((n_q, past), F32)
    sc2 = jnp.zeros((n_q, LANES), F32)
    for hh in range(IDX_HEADS):
        sc1 = sc1 + w[:, hh:hh + 1] * lg1[hh * n_q:(hh + 1) * n_q]
        sc2 = sc2 + w[:, hh:hh + 1] * lg2[hh * n_q:(hh + 1) * n_q]
    key = _sort_key(jnp.concatenate([sc1, jnp.where(own, sc2, NEG_INF)], axis=1))
    idx = lax.broadcasted_iota(I32, (1, n_keys), 1)

    count = lambda hit: jnp.sum(hit(key, idx), axis=1, keepdims=True)
    t0 = jnp.where(count(lambda k_, i_: jnp.where(k_ >= 0, 1, 0)) >= topk, jnp.int32(0), jnp.int32(INT_MIN))
    thr = _radix_search(lambda cand: count(lambda k_, i_: jnp.where(k_ >= cand, 1, 0)) >= topk, t0, 31)
    rem = topk - count(lambda k_, i_: jnp.where(k_ > thr, 1, 0))
    cut = _radix_search(
        lambda cand: count(lambda k_, i_: jnp.where(k_ == thr, jnp.where(i_ < cand, 1, 0), 0)) < rem,
        jnp.zeros_like(thr), int(n_keys - 1).bit_length())
    bias = jnp.where(_select(key, idx, thr, cut), 0.0, NEG_INF)
    bias = jnp.concatenate([bias] * hd, axis=0)

    lane_head = _div(lax.broadcasted_iota(I32, (1, hd * dh), 1), dh)
    qf = q_ref[...].astype(F32)
    qbd = jnp.concatenate([jnp.where(lane_head == hh, qf, 0.0) for hh in range(hd)], axis=0).astype(BF16)
    kc = kc_ref[0].astype(BF16)
    vc = vc_ref[0].astype(BF16)
    s1 = _dot_t(qbd, kc) + bias[:, :past]
    s2 = _dot_t(qbd, kn_ref[...]) + bias[:, past:]
    m = jnp.maximum(jnp.max(s1, axis=1, keepdims=True), jnp.max(s2, axis=1, keepdims=True))
    p1 = jnp.exp(s1 - m)
    p2 = jnp.exp(s2 - m)
    l = jnp.sum(p1, axis=1, keepdims=True) + jnp.sum(p2, axis=1, keepdims=True)
    o_all = (_dot(p1.astype(BF16), vc) + _dot(p2.astype(BF16), vn_ref[...])) / l
    out = jnp.zeros((n_q, hd * dh), F32)
    for hh in range(hd):
        out = out + jnp.where(lane_head == hh, o_all[hh * n_q:(hh + 1) * n_q], 0.0)
    o_ref[...] = x_ref[...] + _dot(out.astype(BF16), wo_ref[...])


def _dsa_attn_sample(x, qb, qib, wi, kb, vb, kib, cache_k, cache_v, cache_ki, w_o, n_b, n_q, topk):
    t, d = x.shape
    hdm = DSA_HEADS * DSA_HEAD_DIM
    past = cache_k.shape[1]
    per = LANES // n_q
    return pl.pallas_call(
        functools.partial(_dsa_samp_kernel, n_q=n_q, topk=topk),
        grid=(n_b,),
        in_specs=[
            pl.BlockSpec((n_q, hdm), lambda b: (b, 0)),
            pl.BlockSpec((IDX_HEADS, n_q, IDX_DIM), lambda b: (0, b, 0)),
            pl.BlockSpec((n_q, IDX_HEADS), lambda b: (b, 0)),
            pl.BlockSpec((1, past, hdm), lambda b: (b, 0, 0)),
            pl.BlockSpec((1, past, hdm), lambda b: (b, 0, 0)),
            pl.BlockSpec((1, past, IDX_DIM), lambda b: (b, 0, 0)),
            pl.BlockSpec((LANES, hdm), lambda b: (b // per, 0)),
            pl.BlockSpec((LANES, hdm), lambda b: (b // per, 0)),
            pl.BlockSpec((LANES, IDX_DIM), lambda b: (b // per, 0)),
            pl.BlockSpec((n_q, d), lambda b: (b, 0)),
            _whole(w_o.shape),
        ],
        out_specs=pl.BlockSpec((n_q, d), lambda b: (b, 0)),
        out_shape=jax.ShapeDtypeStruct((t, d), F32),
        compiler_params=_params("parallel"),
        name="dsa_attn_sample",
    )(qb, qib, wi, cache_k.reshape(n_b, past, hdm), cache_v.reshape(n_b, past, hdm), cache_ki, kb, vb, kib, x, w_o)


def kernel(x_prompt, x_sample, cache_mla_ckv, cache_mla_krope, cache_dsa_k, cache_dsa_v, cache_dsa_kidx, norm_mix, norm_ffn, norm_final, mla_w_dq, mla_g_q, mla_w_uq, mla_w_dkv, mla_g_kv, mla_w_ukv, mla_w_o, cmlp_w_in, cmlp_ln_g, cmlp_ln_b, cmlp_w_s, cmlp_b_s, cmlp_w_out, dsa_w_qkv, dsa_w_o, dsa_w_qidx, dsa_w_kidx, dsa_g_kidx, dsa_w_widx, ffn_w_in, ffn_w_out):
    batch, seq, d = x_prompt.shape
    n_b, n_q, _ = x_sample.shape
    past = cache_mla_ckv.shape[2]
    depth = norm_mix.shape[0]
    xp = x_prompt.reshape(batch * seq, d)
    xs = x_sample.reshape(n_b * n_q, d)
    tab_p = _rope_tables(jnp.arange(seq))
    tab_s = tuple(jnp.tile(a, (n_b, 1)) for a in _rope_tables(past + jnp.arange(n_q)))
    cast = lambda a: a.astype(BF16)
    outs = {k: [] for k in ("ckv_p", "kr_p", "ckv_s", "kr_s", "cv_s", "dk_p", "dv_p", "di_p", "dk_s", "dv_s", "di_s")}
    for i in range(depth):
        kind, j = i % 3, i // 3
        if kind == 0:
            wts = _mla_weights(mla_w_dq[j], mla_w_uq[j], mla_w_dkv[j], mla_w_ukv[j], mla_w_o[j])
            qt, kh, vt, ckv, kr, kn2 = _mla_proj_t(xp, norm_mix[i], wts, mla_g_q[j], mla_g_kv[j], tab_p, seq)
            xp = _attn_t(xp, qt, kh, vt, kn2, None, wts["wot"], batch, seq, 256, "mla_attn_prompt")
            outs["ckv_p"].append(ckv.reshape(batch, seq, -1)); outs["kr_p"].append(kr.reshape(batch, seq, -1))
            ql, qr, ckv, kr, ckvb, krb = _mla_proj(xs, norm_mix[i], wts, mla_g_q[j], mla_g_kv[j], tab_s)
            xs = _mla_attn_sample(xs, ql, qr, ckvb, krb, cache_mla_ckv[j], cache_mla_krope[j], wts, n_b, n_q)
            outs["ckv_s"].append(ckv.reshape(n_b, n_q, -1)); outs["kr_s"].append(kr.reshape(n_b, n_q, -1))
        elif kind == 1:
            w_in, w_out = cast(cmlp_w_in[j]), cast(cmlp_w_out[j])
            xp, _ = _cmlp(xp, norm_mix[i], w_in, cmlp_ln_g[j], cmlp_ln_b[j], cmlp_w_s[j], cmlp_b_s[j], w_out,
                          min(seq, CMLP_CHUNK), False)
            xs, v_s = _cmlp(xs, norm_mix[i], w_in, cmlp_ln_g[j], cmlp_ln_b[j], cmlp_w_s[j], cmlp_b_s[j], w_out,
                            min(n_q, CMLP_CHUNK), True)
            outs["cv_s"].append(v_s.reshape(n_b, n_q, -1))
        else:
            hdm = DSA_HEADS * DSA_HEAD_DIM
            w_qkv, w_qidx, w_o = cast(dsa_w_qkv[j]), cast(dsa_w_qidx[j]), cast(dsa_w_o[j])
            w_kidx, w_widx = cast(dsa_w_kidx[j]), cast(dsa_w_widx[j])
            zpad = lambda n: jnp.zeros((d, n), BF16)
            wt = dict(wkv=w_qkv[:, hdm:], wqt=w_qkv[:, :hdm].T, wvt=w_qkv[:, 2 * hdm:].T, wqit=w_qidx.T,
                      wkw=jnp.concatenate([w_kidx, zpad(LANES - IDX_DIM)], axis=1),
                      wwt=jnp.concatenate([w_widx, zpad(BF16_ROWS - IDX_HEADS)], axis=1).T)
            hshape = (DSA_HEADS, DSA_HEAD_DIM)
            k, v, ki, qt, kh, vt, qit, kib, wit, kn2 = _dsa_proj_t(xp, norm_mix[i], wt, dsa_g_kidx[j])
            bias = _dsa_select(qit, wit, kib, batch, seq, min(TOPK_MAX, seq // 4))
            xp = _attn_t(xp, qt, kh, vt, kn2, bias, w_o.T, batch, seq, 512, "dsa_attn_prompt")
            outs["dk_p"].append(k.reshape((batch, seq) + hshape)); outs["dv_p"].append(v.reshape((batch, seq) + hshape))
            outs["di_p"].append(ki.reshape(batch, seq, -1))
            w_kw = jnp.concatenate([w_kidx, w_widx, zpad(LANES - IDX_DIM - IDX_HEADS)], axis=1)
            k, v, ki, qb, kb, vb, qib, kib, wi = _dsa_proj(xs, norm_mix[i], w_qkv, w_qidx, w_kw, dsa_g_kidx[j])
            xs = _dsa_attn_sample(xs, qb, qib, wi, kb, vb, kib, cache_dsa_k[j], cache_dsa_v[j], cache_dsa_kidx[j], w_o,
                                  n_b, n_q, min(TOPK_MAX, (past + n_q) // 4))
            outs["dk_s"].append(k.reshape((n_b, n_q) + hshape)); outs["dv_s"].append(v.reshape((n_b, n_q) + hshape))
            outs["di_s"].append(ki.reshape(n_b, n_q, -1))
        w_in, w_out = cast(ffn_w_in[i]), cast(ffn_w_out[i])
        final = i == depth - 1
        xp = _ffn(xp, norm_ffn[i], w_in, w_out, norm_final, final)
        xs = _ffn(xs, norm_ffn[i], w_in, w_out, norm_final, final)
    st = lambda name: jnp.stack(outs[name])
    return (xp.reshape(batch, seq, d), xs.reshape(n_b, n_q, d),
            st("ckv_p"), st("kr_p"), st("ckv_s"), st("kr_s"), st("cv_s"),
            st("dk_p"), st("dv_p"), st("di_p"), st("dk_s"), st("dv_s"), st("di_s"))
```
